```python
import jax
import jax.numpy as jnp
from jax import lax
import numpy as np

D_MODEL = 1024
BATCH = 32
SEQ = 256
DEPTH = 4
DEC_BATCH = 4
DEC_SEQ = 4096
PAST_LEN = 512

GRID_W = 64
N_MIXERS = 3
N_MLA = (DEPTH + 2) // 3
N_REC = (DEPTH + 1) // 3
N_CONV = DEPTH // 3
N_DENSE = (DEPTH + 1) // 2
N_MOE = DEPTH // 2

N_HEADS = 16
QK_NOPE = 64
ROPE_DIM = 32
AXIS_DIM = ROPE_DIM // 2
V_DIM = 64
Q_RANK = 384
KV_RANK = 256
ROPE_THETA = 10000.0
ATTN_SCALE = (QK_NOPE + ROPE_DIM) ** -0.5
Q_BLOCK = 128

D_RNN = D_MODEL
RG_BLOCKS = 16
RG_BLOCK = D_RNN // RG_BLOCKS
RG_CONV = 4
RG_C = 8.0

SC_CONV = 3

D_FF = 2816
N_EXPERTS = 8
TOP_K = 2
D_FF_EXPERT = 1408

EPS = 1e-6

kernel_name = 'hybrid_mla_rglru_shortconv_diffusion_step'


def rmsnorm(x, g):
    xf = x.astype(jnp.float32)
    y = xf * lax.rsqrt(jnp.mean(xf * xf, axis=-1, keepdims=True) + EPS)
    return (y * g.astype(jnp.float32)).astype(x.dtype)


def modulation(cond, w, b):
    m = jax.nn.silu(cond) @ w + b
    m = m.reshape(cond.shape[0], 6, 1, D_MODEL)
    return tuple(m[:, k] for k in range(6))


def pre(x, g, shift, scale):
    return rmsnorm(x, g) * (1 + scale) + shift


def post(x, out, g, gate):
    return x + gate * rmsnorm(out, g)


def dwconv(x, w, pad_left):
    t = x.shape[1]
    k_w = w.shape[0]
    xp = jnp.pad(x, ((0, 0), (pad_left, k_w - 1 - pad_left), (0, 0)))
    acc = xp[:, 0:t] * w[0]
    for k in range(1, k_w):
        acc = acc + xp[:, k:k + t] * w[k]
    return acc


def axial_rope_tables(t):
    rows = t // GRID_W
    row = jnp.repeat(jnp.arange(rows, dtype=jnp.float32), GRID_W)
    col = jnp.tile(jnp.arange(GRID_W, dtype=jnp.float32), rows)
    inv = ROPE_THETA ** (-jnp.arange(0, AXIS_DIM, 2, dtype=jnp.float32) / AXIS_DIM)
    ang_r = row[:, None] * inv
    ang_c = col[:, None] * inv
    return jnp.cos(ang_r), jnp.sin(ang_r), jnp.cos(ang_c), jnp.sin(ang_c)


def rotate_pairs(x, cos, sin):
    f = x.shape[-1] // 2
    shp = (cos.shape[0],) + (1,) * (x.ndim - 3) + (f,)
    cos = cos.reshape(shp)
    sin = sin.reshape(shp)
    x1, x2 = x[..., :f], x[..., f:]
    return jnp.concatenate([x1 * cos - x2 * sin, x1 * sin + x2 * cos], axis=-1)


def axial_rope(x, tabs):
    cr, sr, cc, sc = tabs
    xf = x.astype(jnp.float32)
    y = jnp.concatenate([rotate_pairs(xf[..., :AXIS_DIM], cr, sr),
                         rotate_pairs(xf[..., AXIS_DIM:], cc, sc)], axis=-1)
    return y.astype(x.dtype)


def mla_queries(h, w_dq, g_q, w_uq):
    cq = rmsnorm(h @ w_dq, g_q)
    q = jnp.einsum('btr,rhd->bthd', cq, w_uq)
    return q[..., :QK_NOPE], q[..., QK_NOPE:]


def mla_compress(h, w_dkv, g_kv):
    kv = h @ w_dkv
    return rmsnorm(kv[..., :KV_RANK], g_kv), kv[..., KV_RANK:]


def mla_expand(ckv, w_uk, w_uv):
    k_nope = jnp.einsum('bsr,rhd->bshd', ckv, w_uk)
    v = jnp.einsum('bsr,rhd->bshd', ckv, w_uv)
    return k_nope, v


def attend(q_nope, q_rope, k_nope, k_rope, v):
    b, tq = q_nope.shape[:2]
    nblk = tq // Q_BLOCK
    qn = q_nope.reshape(b, nblk, Q_BLOCK, N_HEADS, QK_NOPE).swapaxes(0, 1)
    qr = q_rope.reshape(b, nblk, Q_BLOCK, N_HEADS, ROPE_DIM).swapaxes(0, 1)

    def block(args):
        qn_b, qr_b = args
        s = (jnp.einsum('bqhd,bkhd->bhqk', qn_b, k_nope)
             + jnp.einsum('bqhd,bkd->bhqk', qr_b, k_rope))
        p = jax.nn.softmax(s.astype(jnp.float32) * ATTN_SCALE, axis=-1).astype(v.dtype)
        return jnp.einsum('bhqk,bkhd->bqhd', p, v)

    o = lax.map(block, (qn, qr))
    return o.swapaxes(0, 1).reshape(b, tq, N_HEADS * V_DIM)


def mla_context(h, w_dq, g_q, w_uq, w_dkv, g_kv, w_uk, w_uv, w_o):
    qn, qr = mla_queries(h, w_dq, g_q, w_uq)
    ckv, kr = mla_compress(h, w_dkv, g_kv)
    kn, v = mla_expand(ckv, w_uk, w_uv)
    return attend(qn, qr, kn, kr, v) @ w_o, ckv, kr


def mla_latent(h, ctx_ckv, ctx_krope, tabs, w_dq, g_q, w_uq, w_dkv, g_kv, w_uk, w_uv, w_o):
    qn, qr = mla_queries(h, w_dq, g_q, w_uq)
    qr = axial_rope(qr, tabs)
    ckv, kr = mla_compress(h, w_dkv, g_kv)
    kr = axial_rope(kr, tabs)
    ckv_all = jnp.concatenate([ctx_ckv.astype(h.dtype), ckv], axis=1)
    kr_all = jnp.concatenate([ctx_krope.astype(h.dtype), kr], axis=1)
    kn, v = mla_expand(ckv_all, w_uk, w_uv)
    return attend(qn, qr, kn, kr_all, v) @ w_o


def linear_scan(a, b, h0):
    def combine(left, right):
        a_l, b_l = left
        a_r, b_r = right
        return a_l * a_r, a_r * b_l + b_r
    a_cum, b_cum = lax.associative_scan(combine, (a, b), axis=1)
    return a_cum * h0[:, None] + b_cum


def rglru_mixer(h, h0, w_x, w_y, conv_w, conv_b, w_a, b_a, w_i, b_i, lam, w_out):
    gate_branch = jax.nn.gelu(h @ w_y)
    u = dwconv(h @ w_x, conv_w, (RG_CONV - 1) // 2) + conv_b
    bsz, t = u.shape[:2]
    ys = []
    finals = []
    for d in range(2):
        ud = u if d == 0 else jnp.flip(u, axis=1)
        ub = ud.reshape(bsz, t, RG_BLOCKS, RG_BLOCK)
        r = jax.nn.sigmoid((jnp.einsum('btnk,nkj->btnj', ub, w_a[d]).reshape(bsz, t, D_RNN)
                            + b_a[d]).astype(jnp.float32))
        i = jax.nn.sigmoid((jnp.einsum('btnk,nkj->btnj', ub, w_i[d]).reshape(bsz, t, D_RNN)
                            + b_i[d]).astype(jnp.float32))
        log_a = -RG_C * r * jax.nn.softplus(-lam[d].astype(jnp.float32))
        a = jnp.exp(log_a)
        bx = jnp.sqrt(jnp.maximum(-jnp.expm1(2 * log_a), 0.0)) * (i * ud.astype(jnp.float32))
        hs = linear_scan(a, bx, h0[:, d].astype(jnp.float32))
        finals.append(hs[:, -1])
        ys.append(hs if d == 0 else jnp.flip(hs, axis=1))
    y = ((ys[0] + ys[1]).astype(h.dtype) * gate_branch) @ w_out
    return y, jnp.stack(finals, axis=1).astype(h.dtype)


def shortconv_mixer(h, w_in, conv_w, w_out):
    bg, cg, hv = jnp.split(h @ w_in, 3, axis=-1)
    z = dwconv(cg * hv, conv_w, (SC_CONV - 1) // 2)
    return (bg * z) @ w_out


def swiglu(h, w_gate, w_up, w_down):
    return (jax.nn.silu(h @ w_gate) * (h @ w_up)) @ w_down


def moe(h, w_router, b_router, w_gate, w_up, w_down):
    logits = (h @ w_router).astype(jnp.float32) + b_router.astype(jnp.float32)
    top_v, top_i = lax.top_k(logits, TOP_K)
    probs = jax.nn.softmax(top_v, axis=-1)
    combine = jnp.sum(jax.nn.one_hot(top_i, N_EXPERTS, dtype=jnp.float32) * probs[..., None],
                      axis=-2).astype(h.dtype)
    out = jnp.zeros_like(h)
    for e in range(N_EXPERTS):
        out = out + combine[..., e:e + 1] * swiglu(h, w_gate[e], w_up[e], w_down[e])
    return out


def setup_inputs(seed: int = 0) -> dict:
    key = jax.random.key(seed)
    keys = jax.random.split(key, 48)
    counter = [0]
    f32 = jnp.float32

    def nk():
        k = keys[counter[0]]
        counter[0] += 1
        return k

    def normal(shape, scale=1.0):
        return jax.random.normal(nk(), shape, f32) * scale

    def dense(shape, fan_in, gain=1.0):
        return normal(shape, gain * fan_in ** -0.5)

    def gains(shape):
        return 1.0 + normal(shape, 0.05)

    x_prompt = normal((BATCH, SEQ, D_MODEL))
    x_sample = normal((DEC_BATCH, DEC_SEQ, D_MODEL))
    cache_mla_ckv = normal((DEC_BATCH, N_MLA, PAST_LEN, KV_RANK))
    cache_mla_krope = normal((DEC_BATCH, N_MLA, PAST_LEN, ROPE_DIM))
    state_rglru = normal((DEC_BATCH, N_REC, 2, D_RNN), 0.3)
    c = normal((DEC_BATCH, D_MODEL))
    c_ctx = normal((D_MODEL,))

    mod_w = dense((DEPTH, D_MODEL, 6 * D_MODEL), D_MODEL, 0.5)
    mod_b = normal((DEPTH, 6 * D_MODEL), 0.02)
    norm_g = gains((DEPTH, 4, D_MODEL))

    mla_w_dq = dense((N_MLA, D_MODEL, Q_RANK), D_MODEL)
    mla_g_q = gains((N_MLA, Q_RANK))
    mla_w_uq = dense((N_MLA, Q_RANK, N_HEADS, QK_NOPE + ROPE_DIM), Q_RANK)
    mla_w_dkv = dense((N_MLA, D_MODEL, KV_RANK + ROPE_DIM), D_MODEL)
    mla_g_kv = gains((N_MLA, KV_RANK))
    mla_w_uk = dense((N_MLA, KV_RANK, N_HEADS, QK_NOPE), KV_RANK)
    mla_w_uv = dense((N_MLA, KV_RANK, N_HEADS, V_DIM), KV_RANK)
    mla_w_o = dense((N_MLA, N_HEADS * V_DIM, D_MODEL), N_HEADS * V_DIM)

    rg_w_x = dense((N_REC, D_MODEL, D_RNN), D_MODEL)
    rg_w_y = dense((N_REC, D_MODEL, D_RNN), D_MODEL)
    rg_conv_w = dense((N_REC, RG_CONV, D_RNN), RG_CONV)
    rg_conv_b = normal((N_REC, D_RNN), 0.02)
    rg_w_a = dense((N_REC, 2, RG_BLOCKS, RG_BLOCK, RG_BLOCK), RG_BLOCK)
    rg_b_a = normal((N_REC, 2, D_RNN), 0.02)
    rg_w_i = dense((N_REC, 2, RG_BLOCKS, RG_BLOCK, RG_BLOCK), RG_BLOCK)
    rg_b_i = normal((N_REC, 2, D_RNN), 0.02)
    a_c = jax.random.uniform(nk(), (N_REC, 2, D_RNN), f32, 0.9, 0.999)
    a_base = a_c ** (1.0 / RG_C)
    rg_lambda = jnp.log(a_base) - jnp.log1p(-a_base)
    rg_w_out = dense((N_REC, D_RNN, D_MODEL), D_RNN)

    sc_w_in = dense((N_CONV, D_MODEL, 3 * D_MODEL), D_MODEL)
    sc_conv_w = dense((N_CONV, SC_CONV, D_MODEL), SC_CONV)
    sc_w_out = dense((N_CONV, D_MODEL, D_MODEL), D_MODEL)

    ffn_w_gate = dense((N_DENSE, D_MODEL, D_FF), D_MODEL)
    ffn_w_up = dense((N_DENSE, D_MODEL, D_FF), D_MODEL)
    ffn_w_down = dense((N_DENSE, D_FF, D_MODEL), D_FF)

    moe_w_router = dense((N_MOE, D_MODEL, N_EXPERTS), D_MODEL)
    moe_b_router = normal((N_MOE, N_EXPERTS), 0.01)
    moe_w_gate = dense((N_MOE, N_EXPERTS, D_MODEL, D_FF_EXPERT), D_MODEL)
    moe_w_up = dense((N_MOE, N_EXPERTS, D_MODEL, D_FF_EXPERT), D_MODEL)
    moe_w_down = dense((N_MOE, N_EXPERTS, D_FF_EXPERT, D_MODEL), D_FF_EXPERT)

    return {
        'x_prompt': x_prompt, 'x_sample': x_sample,
        'cache_mla_ckv': cache_mla_ckv, 'cache_mla_krope': cache_mla_krope,
        'state_rglru': state_rglru, 'c': c, 'c_ctx': c_ctx,
        'mod_w': mod_w, 'mod_b': mod_b, 'norm_g': norm_g,
        'mla_w_dq': mla_w_dq, 'mla_g_q': mla_g_q, 'mla_w_uq': mla_w_uq,
        'mla_w_dkv': mla_w_dkv, 'mla_g_kv': mla_g_kv, 'mla_w_uk': mla_w_uk,
        'mla_w_uv': mla_w_uv, 'mla_w_o': mla_w_o,
        'rg_w_x': rg_w_x, 'rg_w_y': rg_w_y, 'rg_conv_w': rg_conv_w, 'rg_conv_b': rg_conv_b,
        'rg_w_a': rg_w_a, 'rg_b_a': rg_b_a, 'rg_w_i': rg_w_i, 'rg_b_i': rg_b_i,
        'rg_lambda': rg_lambda, 'rg_w_out': rg_w_out,
        'sc_w_in': sc_w_in, 'sc_conv_w': sc_conv_w, 'sc_w_out': sc_w_out,
        'ffn_w_gate': ffn_w_gate, 'ffn_w_up': ffn_w_up, 'ffn_w_down': ffn_w_down,
        'moe_w_router': moe_w_router, 'moe_b_router': moe_b_router,
        'moe_w_gate': moe_w_gate, 'moe_w_up': moe_w_up, 'moe_w_down': moe_w_down,
    }


def reference(x_prompt, x_sample, cache_mla_ckv, cache_mla_krope, state_rglru, c, c_ctx,
              mod_w, mod_b, norm_g,
              mla_w_dq, mla_g_q, mla_w_uq, mla_w_dkv, mla_g_kv, mla_w_uk, mla_w_uv, mla_w_o,
              rg_w_x, rg_w_y, rg_conv_w, rg_conv_b, rg_w_a, rg_b_a, rg_w_i, rg_b_i,
              rg_lambda, rg_w_out,
              sc_w_in, sc_conv_w, sc_w_out,
              ffn_w_gate, ffn_w_up, ffn_w_down,
              moe_w_router, moe_b_router, moe_w_gate, moe_w_up, moe_w_down):
    xp = x_prompt
    xs = x_sample
    tabs = axial_rope_tables(xs.shape[1])
    new_ckv = []
    new_krope = []
    new_rg = []
    for l in range(DEPTH):
        sh1c, sc1c, ga1c, sh2c, sc2c, ga2c = modulation(c_ctx[None], mod_w[l], mod_b[l])
        sh1, sc1, ga1, sh2, sc2, ga2 = modulation(c, mod_w[l], mod_b[l])
        hp = pre(xp, norm_g[l, 0], sh1c, sc1c)
        hs = pre(xs, norm_g[l, 0], sh1, sc1)
        kind = l % N_MIXERS
        j = l // N_MIXERS
        if kind == 0:
            mla_p = (mla_w_dq[j], mla_g_q[j], mla_w_uq[j], mla_w_dkv[j], mla_g_kv[j],
                     mla_w_uk[j], mla_w_uv[j], mla_w_o[j])
            op, ckv, kr = mla_context(hp, *mla_p)
            ol = mla_latent(hs, cache_mla_ckv[:, j], cache_mla_krope[:, j], tabs, *mla_p)
            new_ckv.append(ckv)
            new_krope.append(kr)
        elif kind == 1:
            rg_p = (rg_w_x[j], rg_w_y[j], rg_conv_w[j], rg_conv_b[j], rg_w_a[j], rg_b_a[j],
                    rg_w_i[j], rg_b_i[j], rg_lambda[j], rg_w_out[j])
            h0 = jnp.zeros((hp.shape[0], 2, D_RNN), hp.dtype)
            op, fin = rglru_mixer(hp, h0, *rg_p)
            ol, _ = rglru_mixer(hs, state_rglru[:, j], *rg_p)
            new_rg.append(fin)
        else:
            op = shortconv_mixer(hp, sc_w_in[j], sc_conv_w[j], sc_w_out[j])
            ol = shortconv_mixer(hs, sc_w_in[j], sc_conv_w[j], sc_w_out[j])
        xp = post(xp, op, norm_g[l, 1], ga1c)
        xs = post(xs, ol, norm_g[l, 1], ga1)
        hp = pre(xp, norm_g[l, 2], sh2c, sc2c)
        hs = pre(xs, norm_g[l, 2], sh2, sc2)
        m = l // 2
        if l % 2 == 0:
            fp = swiglu(hp, ffn_w_gate[m], ffn_w_up[m], ffn_w_down[m])
            fl = swiglu(hs, ffn_w_gate[m], ffn_w_up[m], ffn_w_down[m])
        else:
            moe_p = (moe_w_router[m], moe_b_router[m], moe_w_gate[m], moe_w_up[m], moe_w_down[m])
            fp = moe(hp, *moe_p)
            fl = moe(hs, *moe_p)
        xp = post(xp, fp, norm_g[l, 3], ga2c)
        xs = post(xs, fl, norm_g[l, 3], ga2)
    new_mla_ckv = jnp.stack(new_ckv, axis=1)
    new_mla_krope = jnp.stack(new_krope, axis=1)
    new_rglru = jnp.stack(new_rg, axis=1)
    return (xp, xs, new_mla_ckv, new_mla_krope, new_rglru)
```

```python
import functools

import numpy as np
import jax
import jax.numpy as jnp
from jax import lax
from jax.experimental import pallas as pl
from jax.experimental.pallas import tpu as pltpu

F32 = jnp.float32
BF16 = jnp.bfloat16

D = 1024
BATCH = 32
SEQ = 256
DEPTH = 4
DEC_BATCH = 4
DEC_SEQ = 4096
PAST = 512
GRID_W = 64
N_HEADS = 16
QK_NOPE = 64
ROPE_DIM = 32
AXIS_DIM = 16
V_DIM = 64
Q_RANK = 384
KV_RANK = 256
ROPE_THETA = 10000.0
ATTN_SCALE = (QK_NOPE + ROPE_DIM) ** -0.5
RG_BLOCKS = 16
RG_BLOCK = 64
RG_C = 8.0
D_FF = 2816
N_EXPERTS = 8
D_FF_EXPERT = 1408
EPS = 1e-6

P_ROWS = BATCH * SEQ
S_ROWS = DEC_BATCH * DEC_SEQ
T_ROWS = P_ROWS + S_ROWS
N_GROUPS = 8
HEAD_PAD = 128
HP = N_HEADS * HEAD_PAD
KVW = 384

TM = 512
NP_T = P_ROWS // TM
NS_T = DEC_SEQ // TM
N_T = T_ROWS // TM
SUBLANES = 8
LANES = 128
VMEM_LIMIT = 56 * 1024 * 1024


def _cparams(sem):
    return pltpu.CompilerParams(dimension_semantics=sem, vmem_limit_bytes=VMEM_LIMIT)


def _group_of_tile(i):
    return jnp.maximum(i - NP_T + NS_T, 0) // NS_T


def _pos_tile(i):
    return jnp.maximum(i - NP_T, 0) % NS_T


def _dot(a, b):
    return jnp.dot(a, b, preferred_element_type=F32)


def _rms(x, g):
    ms = jnp.mean(x * x, axis=-1, keepdims=True)
    return x * lax.rsqrt(ms + EPS) * g


def _silu(x):
    return x * jax.nn.sigmoid(x)


def _gelu_tanh(x):
    return x * (0.5 * (1.0 + jnp.tanh(np.sqrt(2.0 / np.pi).astype(np.float32)
                                      * (x + 0.044715 * (x * x * x)))))


def _rope(x, c, s_lo, s_hi):
    return (x * c + pltpu.roll(x, LANES - 8, 1) * s_lo + pltpu.roll(x, 8, 1) * s_hi)


def _mod_kernel(c_ref, w_ref, b_ref, o_ref):
    c = c_ref[...]
    s = _silu(c).astype(BF16)
    o_ref[...] = _dot(s, w_ref[...].astype(BF16)) + b_ref[...]


def _modulation(cond8, mod_w, mod_b):
    tn = 1536
    return pl.pallas_call(
        _mod_kernel,
        grid=(DEPTH, 6 * D // tn),
        in_specs=[pl.BlockSpec((N_GROUPS, D), lambda l, n: (0, 0)),
                  pl.BlockSpec((None, D, tn), lambda l, n: (l, 0, n)),
                  pl.BlockSpec((None, 1, tn), lambda l, n: (l, 0, n))],
        out_specs=pl.BlockSpec((None, N_GROUPS, tn), lambda l, n: (l, 0, n)),
        out_shape=jax.ShapeDtypeStruct((DEPTH, N_GROUPS, 6 * D), F32),
        compiler_params=_cparams(("parallel", "parallel")),
        name="modulation",
    )(cond8, mod_w, mod_b.reshape(DEPTH, 1, 6 * D))


def _pre0_kernel(x_ref, gv_ref, mv_ref, h_ref):
    h = _rms(x_ref[...], gv_ref[1:2, :]) * (1.0 + mv_ref[2:3, :]) + mv_ref[1:2, :]
    h_ref[...] = h.astype(BF16)


def _pre0(x, gv, mv):
    return pl.pallas_call(
        _pre0_kernel,
        grid=(N_T,),
        in_specs=[pl.BlockSpec((TM, D), lambda i: (i, 0)),
                  pl.BlockSpec((8, D), lambda i: (0, 0)),
                  pl.BlockSpec((None, 8, D), lambda i: (_group_of_tile(i), 0, 0))],
        out_specs=pl.BlockSpec((TM, D), lambda i: (i, 0)),
        out_shape=jax.ShapeDtypeStruct((T_ROWS, D), BF16),
        compiler_params=_cparams(("parallel",)),
        name="pre0",
    )(x, gv, mv)


def _router(hn, wr_ref, br_ref):
    logits = jnp.dot(hn, wr_ref[...], preferred_element_type=F32,
                     precision=lax.Precision.HIGHEST) + br_ref[...]
    lane = lax.broadcasted_iota(jnp.int32, logits.shape, 1)
    neg = jnp.float32(-jnp.inf)
    logits = jnp.where(lane < N_EXPERTS, logits, neg)
    m1 = jnp.max(logits, axis=-1, keepdims=True)
    i1 = jnp.min(jnp.where(logits == m1, lane, LANES), axis=-1, keepdims=True)
    rest = jnp.where(lane == i1, neg, logits)
    m2 = jnp.max(rest, axis=-1, keepdims=True)
    i2 = jnp.min(jnp.where(rest == m2, lane, LANES), axis=-1, keepdims=True)
    e = jnp.exp(m2 - m1)
    p1 = 1.0 / (1.0 + e)
    p2 = e / (1.0 + e)
    return jnp.where(lane == i1, p1, jnp.where(lane == i2, p2, 0.0))


def _post_pre(x, out, gv_ref, mv_ref, has_next):
    xn = x + mv_ref[0:1, :] * _rms(out, gv_ref[0:1, :])
    if not has_next:
        return xn, None
    hn = _rms(xn, gv_ref[1:2, :]) * (1.0 + mv_ref[2:3, :]) + mv_ref[1:2, :]
    return xn, hn


def _epilogue(out, x_ref, gv_ref, mv_ref, rest, has_next, want_router):
    rest = list(rest)
    if want_router:
        wr_ref, br_ref = rest[0], rest[1]
        rest = rest[2:]
    xn, hn = _post_pre(x_ref[...], out, gv_ref, mv_ref, has_next)
    rest[0][...] = xn
    if has_next:
        rest[1][...] = hn.astype(BF16)
    if want_router:
        rest[2][...] = _router(hn, wr_ref, br_ref)


def _epilogue_specs(has_next, want_router):
    in_specs = [pl.BlockSpec((TM, D), lambda i, *_: (i, 0)),
                pl.BlockSpec((8, D), lambda i, *_: (0, 0)),
                pl.BlockSpec((None, 8, D), lambda i, *_: (_group_of_tile(i), 0, 0))]
    out_specs = [pl.BlockSpec((TM, D), lambda i, *_: (i, 0))]
    out_shape = [jax.ShapeDtypeStruct((T_ROWS, D), F32)]
    if want_router:
        in_specs += [pl.BlockSpec((D, LANES), lambda i, *_: (0, 0)),
                     pl.BlockSpec((1, LANES), lambda i, *_: (0, 0))]
    if has_next:
        out_specs.append(pl.BlockSpec((TM, D), lambda i, *_: (i, 0)))
        out_shape.append(jax.ShapeDtypeStruct((T_ROWS, D), BF16))
    if want_router:
        out_specs.append(pl.BlockSpec((TM, LANES), lambda i, *_: (i, 0)))
        out_shape.append(jax.ShapeDtypeStruct((T_ROWS, LANES), F32))
    return in_specs, out_specs, out_shape


def _seq_pos(i, rows):
    seq_len = jnp.where(i < NP_T, SEQ, DEC_SEQ)
    r = lax.broadcasted_iota(jnp.int32, (rows, 1), 0)
    return (i * rows + r) & (seq_len - 1), seq_len, r


def _shifted(m, prev_ref, next_ref, k, pos, seq_len, r):
    rows = m.shape[0]
    if k < 0:
        y = pltpu.roll(m, -k, 0)
        y = jnp.where(r == 0, prev_ref[SUBLANES - 1:SUBLANES, :], y)
        return jnp.where(pos + k < 0, 0.0, y)
    y = pltpu.roll(m, rows - k, 0)
    for q in range(k):
        y = jnp.where(r == rows - k + q, next_ref[q:q + 1, :], y)
    return jnp.where(pos + k >= seq_len, 0.0, y)


def _halo_specs(rows, row_block_of):
    per = rows // SUBLANES
    last = T_ROWS // SUBLANES - 1
    return [pl.BlockSpec((rows, D), lambda *g: (row_block_of(*g), 0)),
            pl.BlockSpec((SUBLANES, D), lambda *g: (jnp.maximum(row_block_of(*g) * per - 1, 0), 0)),
            pl.BlockSpec((SUBLANES, D), lambda *g: (jnp.minimum((row_block_of(*g) + 1) * per, last), 0))]


def _proj_kernel(*refs, mode, has_next, want_router):
    i = pl.program_id(0)
    if mode == "plain":
        a_ref, w_ref = refs[:2]
        rest = refs[2:]
        a = a_ref[...]
    elif mode == "rg":
        yf_ref, yb_ref, g_ref, w_ref = refs[:4]
        rest = refs[4:]
        a = ((yf_ref[...] + yb_ref[...]) * g_ref[...]).astype(BF16)
    else:
        bg_ref, m_ref, mp_ref, mn_ref, cw_ref, w_ref = refs[:6]
        rest = refs[6:]
        m = m_ref[...]
        pos, seq_len, r = _seq_pos(i, TM)
        z = (cw_ref[0:1, :] * _shifted(m, mp_ref, mn_ref, -1, pos, seq_len, r)
             + cw_ref[1:2, :] * m
             + cw_ref[2:3, :] * _shifted(m, mp_ref, mn_ref, 1, pos, seq_len, r))
        a = (bg_ref[...] * z).astype(BF16)
    out = _dot(a, w_ref[...])
    _epilogue(out, rest[0], rest[1], rest[2], rest[3:], has_next, want_router)


def _proj(mode, ins, w, x, gv, mv, has_next=True, router=None):
    want_router = router is not None
    k = w.shape[0]
    row = lambda i: (i, 0)
    if mode == "plain":
        in_specs = [pl.BlockSpec((TM, k), row)]
    elif mode == "rg":
        in_specs = [pl.BlockSpec((TM, D), row)] * 3
    else:
        bg, m, cw = ins
        ins = (bg, m, m, m, cw)
        in_specs = ([pl.BlockSpec((TM, D), row)] + _halo_specs(TM, lambda i: i)
                    + [pl.BlockSpec((8, D), lambda i: (0, 0))])
    in_specs.append(pl.BlockSpec((k, D), lambda i: (0, 0)))
    e_in, out_specs, out_shape = _epilogue_specs(has_next, want_router)
    args = list(ins) + [w, x, gv, mv] + (list(router) if want_router else [])
    return pl.pallas_call(
        functools.partial(_proj_kernel, mode=mode, has_next=has_next, want_router=want_router),
        grid=(N_T,),
        in_specs=in_specs + e_in,
        out_specs=out_specs,
        out_shape=out_shape,
        compiler_params=_cparams(("parallel",)),
        name="proj_" + mode,
    )(*args)


def _ffn_kernel(*refs, n_e, use_comb, has_next, want_router):
    e = pl.program_id(1)
    h_ref, wg_ref, wu_ref, wd_ref = refs[:4]
    refs = refs[4:]
    if use_comb:
        comb_ref = refs[0]
        refs = refs[1:]
    acc_ref = refs[-1]
    refs = refs[:-1]

    @pl.when(e == 0)
    def _():
        acc_ref[...] = jnp.zeros(acc_ref.shape, F32)

    h = h_ref[...]
    g = _dot(h, wg_ref[...])
    u = _dot(h, wu_ref[...])
    a = _silu(g) * u
    if use_comb:
        comb = comb_ref[...]
        lane = lax.broadcasted_iota(jnp.int32, comb.shape, 1)
        a = a * jnp.sum(jnp.where(lane == e, comb, 0.0), axis=-1, keepdims=True)
    acc_ref[...] += _dot(a.astype(BF16), wd_ref[...])

    @pl.when(e == n_e - 1)
    def _():
        _epilogue(acc_ref[...], refs[0], refs[1], refs[2], refs[3:], has_next, want_router)


def _ffn(h, wg, wu, wd, comb, x, gv, mv, has_next=True, router=None):
    n_e, _, f = wg.shape
    use_comb = comb is not None
    want_router = router is not None
    in_specs = [pl.BlockSpec((TM, D), lambda i, e: (i, 0)),
                pl.BlockSpec((None, D, f), lambda i, e: (e, 0, 0)),
                pl.BlockSpec((None, D, f), lambda i, e: (e, 0, 0)),
                pl.BlockSpec((None, f, D), lambda i, e: (e, 0, 0))]
    args = [h, wg, wu, wd]
    if use_comb:
        in_specs.append(pl.BlockSpec((TM, LANES), lambda i, e: (i, 0)))
        args.append(comb)
    e_in, out_specs, out_shape = _epilogue_specs(has_next, want_router)
    args += [x, gv, mv] + (list(router) if want_router else [])
    return pl.pallas_call(
        functools.partial(_ffn_kernel, n_e=n_e, use_comb=use_comb, has_next=has_next,
                          want_router=want_router),
        grid=(N_T, n_e),
        in_specs=in_specs + e_in,
        out_specs=out_specs,
        out_shape=out_shape,
        scratch_shapes=[pltpu.VMEM((TM, D), F32)],
        compiler_params=_cparams(("parallel", "arbitrary")),
        name="ffn_moe" if use_comb else "ffn_dense",
    )(*args)


def _mla_down_kernel(h_ref, w_ref, gq_ref, gkv_ref, rc_ref, rlo_ref, rhi_ref,
                     cq_ref, ckv_ref, kr_ref, kvb_ref):
    i = pl.program_id(0)
    y = _dot(h_ref[...], w_ref[...])
    cq_ref[...] = _rms(y[:, :Q_RANK], gq_ref[...]).astype(BF16)
    ckv = _rms(y[:, Q_RANK:Q_RANK + KV_RANK], gkv_ref[...])
    kr = y[:, Q_RANK + KV_RANK:]
    kr = jnp.where(i >= NP_T, _rope(kr, rc_ref[...], rlo_ref[...], rhi_ref[...]), kr)
    ckv_ref[...] = ckv
    kr_ref[...] = kr
    kvb_ref[:, :KV_RANK] = ckv.astype(BF16)
    kvb_ref[:, KV_RANK:] = kr.astype(BF16)


def _mla_down(h, w, gq, gkv, tabs):
    n = Q_RANK + KVW
    row = lambda i: (i, 0)
    const = lambda i: (0, 0)
    tab = pl.BlockSpec((TM, LANES), lambda i: (_pos_tile(i), 0))
    return pl.pallas_call(
        _mla_down_kernel,
        grid=(N_T,),
        in_specs=[pl.BlockSpec((TM, D), row), pl.BlockSpec((D, n), const),
                  pl.BlockSpec((1, Q_RANK), const), pl.BlockSpec((1, KV_RANK), const),
                  tab, tab, tab],
        out_specs=[pl.BlockSpec((TM, Q_RANK), row), pl.BlockSpec((TM, KV_RANK), row),
                   pl.BlockSpec((TM, LANES), row), pl.BlockSpec((TM, KVW), row)],
        out_shape=[jax.ShapeDtypeStruct((T_ROWS, Q_RANK), BF16),
                   jax.ShapeDtypeStruct((T_ROWS, KV_RANK), F32),
                   jax.ShapeDtypeStruct((T_ROWS, LANES), F32),
                   jax.ShapeDtypeStruct((T_ROWS, KVW), BF16)],
        compiler_params=_cparams(("parallel",)),
        name="mla_down",
    )(h, w, gq, gkv, *tabs)


def _mla_q_kernel(cq_ref, w_ref, rc_ref, rlo_ref, rhi_ref, q_ref):
    i = pl.program_id(0)
    y = _dot(cq_ref[...], w_ref[...])
    is_latent = i >= NP_T
    c, lo, hi = rc_ref[...], rlo_ref[...], rhi_ref[...]
    for hd in range(N_HEADS):
        sl = slice(hd * HEAD_PAD, (hd + 1) * HEAD_PAD)
        q = y[:, sl]
        q_ref[:, sl] = jnp.where(is_latent, _rope(q, c, lo, hi), q).astype(BF16)


def _mla_q(cq, w, tabs):
    tab = pl.BlockSpec((TM, LANES), lambda i: (_pos_tile(i), 0))
    return pl.pallas_call(
        _mla_q_kernel,
        grid=(N_T,),
        in_specs=[pl.BlockSpec((TM, Q_RANK), lambda i: (i, 0)),
                  pl.BlockSpec((Q_RANK, HP), lambda i: (0, 0)), tab, tab, tab],
        out_specs=pl.BlockSpec((TM, HP), lambda i: (i, 0)),
        out_shape=jax.ShapeDtypeStruct((T_ROWS, HP), BF16),
        compiler_params=_cparams(("parallel",)),
        name="mla_q",
    )(cq, w, *tabs)


def _mla_kv_kernel(c_ref, w_ref, k_ref, v_ref):
    y = _dot(c_ref[...], w_ref[...])
    k_ref[...] = y[:, :HP].astype(BF16)
    v_ref[...] = y[:, HP:].astype(BF16)


def _mla_kv(ckvkr, w):
    rows = ckvkr.shape[0]
    return pl.pallas_call(
        _mla_kv_kernel,
        grid=(rows // TM,),
        in_specs=[pl.BlockSpec((TM, KVW), lambda i: (i, 0)),
                  pl.BlockSpec((KVW, 2 * HP), lambda i: (0, 0))],
        out_specs=[pl.BlockSpec((TM, HP), lambda i: (i, 0))] * 2,
        out_shape=[jax.ShapeDtypeStruct((rows, HP), BF16)] * 2,
        compiler_params=_cparams(("parallel",)),
        name="mla_kv",
    )(ckvkr, w)


def _attn_kernel(*refs, nk, aliased):
    if aliased:
        refs = refs[1:]
    q_ref, k_ref, v_ref, o_ref, m_scr, l_scr, acc_scr = refs
    ki = pl.program_id(2)

    @pl.when(ki == 0)
    def _():
        m_scr[...] = jnp.full(m_scr.shape, -jnp.inf, F32)
        l_scr[...] = jnp.zeros(l_scr.shape, F32)
        acc_scr[...] = jnp.zeros(acc_scr.shape, F32)

    for hd in range(N_HEADS):
        sl = slice(hd * HEAD_PAD, (hd + 1) * HEAD_PAD)
        s = lax.dot_general(q_ref[:, sl], k_ref[:, sl], (((1,), (1,)), ((), ())),
                            preferred_element_type=F32) * ATTN_SCALE
        m_prev = m_scr[hd]
        m_new = jnp.maximum(m_prev, jnp.max(s, axis=-1, keepdims=True))
        alpha = jnp.exp(m_prev - m_new)
        p = jnp.exp(s - m_new)
        l_scr[hd] = alpha * l_scr[hd] + jnp.sum(p, axis=-1, keepdims=True)
        acc_scr[:, sl] = alpha * acc_scr[:, sl] + _dot(p.astype(BF16), v_ref[:, sl])
        m_scr[hd] = m_new

    @pl.when(ki == nk - 1)
    def _():
        for hd in range(N_HEADS):
            sl = slice(hd * HEAD_PAD, (hd + 1) * HEAD_PAD)
            o_ref[:, sl] = (acc_scr[:, sl] / l_scr[hd]).astype(BF16)


def _attention(q, k, v, n_b, tq, tk, lq, lk, q_row0, prev_out=None):
    nq, nk = lq // tq, lk // tk
    qb0 = q_row0 // tq
    aliased = prev_out is not None
    in_specs = [pl.BlockSpec((tq, HP), lambda b, qi, ki: (qb0 + b * nq + qi, 0)),
                pl.BlockSpec((tk, HP), lambda b, qi, ki: (b * nk + ki, 0)),
                pl.BlockSpec((tk, HP), lambda b, qi, ki: (b * nk + ki, 0))]
    args = [q, k, v]
    if aliased:
        in_specs = [pl.BlockSpec(memory_space=pl.ANY)] + in_specs
        args = [prev_out] + args
    return pl.pallas_call(
        functools.partial(_attn_kernel, nk=nk, aliased=aliased),
        grid=(n_b, nq, nk),
        in_specs=in_specs,
        out_specs=pl.BlockSpec((tq, HP), lambda b, qi, ki: (qb0 + b * nq + qi, 0)),
        out_shape=jax.ShapeDtypeStruct((T_ROWS, HP), BF16),
        scratch_shapes=[pltpu.VMEM((N_HEADS, tq, 1), F32), pltpu.VMEM((N_HEADS, tq, 1), F32),
                        pltpu.VMEM((tq, HP), F32)],
        input_output_aliases={0: 0} if aliased else {},
        compiler_params=_cparams(("parallel", "parallel", "arbitrary")),
        name="attn_latent" if aliased else "attn_context",
    )(*args)


def _rg_in_kernel(h_ref, w_ref, u_ref, g_ref):
    y = _dot(h_ref[...], w_ref[...])
    u_ref[...] = y[:, :D]
    g_ref[...] = _gelu_tanh(y[:, D:])


def _rg_in(h, w):
    return pl.pallas_call(
        _rg_in_kernel,
        grid=(N_T,),
        in_specs=[pl.BlockSpec((TM, D), lambda i: (i, 0)), pl.BlockSpec((D, 2 * D), lambda i: (0, 0))],
        out_specs=[pl.BlockSpec((TM, D), lambda i: (i, 0))] * 2,
        out_shape=[jax.ShapeDtypeStruct((T_ROWS, D), F32)] * 2,
        compiler_params=_cparams(("parallel",)),
        name="rg_in",
    )(h, w)


RG_TC = 256
RG_GROUP = 256


def _rg_scan_kernel(*refs, nj, seq_len, aliased):
    if aliased:
        refs = refs[2:]
    (uf_ref, ufp_ref, ufn_ref, ub_ref, ubp_ref, ubn_ref, cw_ref, cb_ref, wai_ref, bai_ref,
     lam_ref, h0_ref, yf_ref, yb_ref, fin_ref, carry_ref) = refs
    j = pl.program_id(1)
    tc = RG_TC
    r = lax.broadcasted_iota(jnp.int32, (tc, 1), 0)
    sub = r & (SUBLANES - 1)

    @pl.when(j == 0)
    def _():
        carry_ref[0:2, :] = h0_ref[...]

    def gates(m_ref, p_ref, n_ref, d, chunk):
        m = m_ref[...]
        pos = chunk * tc + r
        u = (cw_ref[0:1, :] * _shifted(m, p_ref, n_ref, -1, pos, seq_len, r)
             + cw_ref[1:2, :] * m
             + cw_ref[2:3, :] * _shifted(m, p_ref, n_ref, 1, pos, seq_len, r)
             + cw_ref[3:4, :] * _shifted(m, p_ref, n_ref, 2, pos, seq_len, r)
             + cb_ref[...])
        ub = u.astype(BF16)
        ra, ri = [], []
        for q in range(D // RG_GROUP):
            y = _dot(ub[:, q * RG_GROUP:(q + 1) * RG_GROUP], wai_ref[d, q])
            ra.append(y[:, :RG_GROUP])
            ri.append(y[:, RG_GROUP:])
        rr = jax.nn.sigmoid(jnp.concatenate(ra, axis=1) + bai_ref[d, 0:1, :])
        ii = jax.nn.sigmoid(jnp.concatenate(ri, axis=1) + bai_ref[d, 1:2, :])
        nl = -lam_ref[d:d + 1, :]
        softplus = jnp.maximum(nl, 0.0) + jnp.log1p(jnp.exp(-jnp.abs(nl)))
        log_a = (-RG_C * softplus) * rr
        a = jnp.exp(log_a)
        bx = jnp.sqrt(jnp.maximum(-jnp.tanh(log_a) * (a * a + 1.0), 0.0)) * (ii * u)
        return a, bx

    a, b = gates(uf_ref, ufp_ref, ufn_ref, 0, j)
    for k in (1, 2, 4):
        ok = sub >= k
        b = jnp.where(ok, a * pltpu.roll(b, k, 0) + b, b)
        a = jnp.where(ok, a * pltpu.roll(a, k, 0), a)
    h = carry_ref[0:1, :]
    for g in range(tc // SUBLANES):
        sl = slice(g * SUBLANES, (g + 1) * SUBLANES)
        hg = a[sl] * h + b[sl]
        yf_ref[sl, :] = hg
        h = hg[SUBLANES - 1:SUBLANES, :]
    carry_ref[0:1, :] = h

    a, b = gates(ub_ref, ubp_ref, ubn_ref, 1, nj - 1 - j)
    for k in (1, 2, 4):
        ok = sub < SUBLANES - k
        b = jnp.where(ok, a * pltpu.roll(b, tc - k, 0) + b, b)
        a = jnp.where(ok, a * pltpu.roll(a, tc - k, 0), a)
    h = carry_ref[1:2, :]
    for g in reversed(range(tc // SUBLANES)):
        sl = slice(g * SUBLANES, (g + 1) * SUBLANES)
        hg = a[sl] * h + b[sl]
        yb_ref[sl, :] = hg
        h = hg[0:1, :]
    carry_ref[1:2, :] = h

    @pl.when(j == nj - 1)
    def _():
        fin_ref[...] = carry_ref[0:2, :]


def _rg_scan(u, cw, cb, wai, bai, lam, h0, n_seq, seq_len, row0, prev=None):
    nj = seq_len // RG_TC
    b0 = row0 // RG_TC
    aliased = prev is not None
    fwd = lambda s, j: b0 + s * nj + j
    bwd = lambda s, j: b0 + s * nj + (nj - 1 - j)
    const2 = lambda s, j: (0, 0)
    in_specs = (_halo_specs(RG_TC, fwd) + _halo_specs(RG_TC, bwd)
                + [pl.BlockSpec((8, D), const2), pl.BlockSpec((1, D), const2),
                   pl.BlockSpec((2, D // RG_GROUP, RG_GROUP, 2 * RG_GROUP), lambda s, j: (0, 0, 0, 0)),
                   pl.BlockSpec((2, 2, D), lambda s, j: (0, 0, 0)),
                   pl.BlockSpec((2, D), const2),
                   pl.BlockSpec((None, 2, D), lambda s, j: (s, 0, 0))])
    args = [u, u, u, u, u, u, cw, cb, wai, bai, lam, h0]
    if aliased:
        in_specs = [pl.BlockSpec(memory_space=pl.ANY)] * 2 + in_specs
        args = list(prev) + args
    return pl.pallas_call(
        functools.partial(_rg_scan_kernel, nj=nj, seq_len=seq_len, aliased=aliased),
        grid=(n_seq, nj),
        in_specs=in_specs,
        out_specs=[pl.BlockSpec((RG_TC, D), lambda s, j: (fwd(s, j), 0)),
                   pl.BlockSpec((RG_TC, D), lambda s, j: (bwd(s, j), 0)),
                   pl.BlockSpec((None, 2, D), lambda s, j: (s, 0, 0))],
        out_shape=[jax.ShapeDtypeStruct((T_ROWS, D), F32), jax.ShapeDtypeStruct((T_ROWS, D), F32),
                   jax.ShapeDtypeStruct((n_seq, 2, D), F32)],
        scratch_shapes=[pltpu.VMEM((8, D), F32)],
        input_output_aliases={0: 0, 1: 1} if aliased else {},
        compiler_params=_cparams(("parallel", "arbitrary")),
        name="rg_scan_latent" if aliased else "rg_scan_context",
    )(*args)


def _sc_in_kernel(h_ref, w_ref, bg_ref, m_ref):
    y = _dot(h_ref[...], w_ref[...])
    bg_ref[...] = y[:, :D]
    m_ref[...] = y[:, D:2 * D] * y[:, 2 * D:]


def _sc_in(h, w):
    return pl.pallas_call(
        _sc_in_kernel,
        grid=(N_T,),
        in_specs=[pl.BlockSpec((TM, D), lambda i: (i, 0)), pl.BlockSpec((D, 3 * D), lambda i: (0, 0))],
        out_specs=[pl.BlockSpec((TM, D), lambda i: (i, 0))] * 2,
        out_shape=[jax.ShapeDtypeStruct((T_ROWS, D), F32)] * 2,
        compiler_params=_cparams(("parallel",)),
        name="sc_in",
    )(h, w)


def _rope_tables():
    t = np.arange(DEC_SEQ)
    inv = ROPE_THETA ** (-jnp.arange(0, AXIS_DIM, 2, dtype=F32) / AXIS_DIM)
    ang_r = jnp.asarray((t // GRID_W).astype(np.float32))[:, None] * inv
    ang_c = jnp.asarray((t % GRID_W).astype(np.float32))[:, None] * inv
    cr, sr, cc, sc = jnp.cos(ang_r), jnp.sin(ang_r), jnp.cos(ang_c), jnp.sin(ang_c)
    z8 = jnp.zeros((DEC_SEQ, 8), F32)
    pad1 = jnp.ones((DEC_SEQ, LANES - ROPE_DIM), F32)
    pad0 = jnp.zeros((DEC_SEQ, LANES - ROPE_DIM), F32)
    c = jnp.concatenate([cr, cr, cc, cc, pad1], axis=1)
    lo = jnp.concatenate([-sr, z8, -sc, z8, pad0], axis=1)
    hi = jnp.concatenate([z8, sr, z8, sc, pad0], axis=1)
    return c, lo, hi


def _pad_heads(w, lo):
    r, h, d = w.shape
    return jnp.pad(w, ((0, 0), (0, 0), (lo, HEAD_PAD - lo - d))).reshape(r, h * HEAD_PAD)


def _mla_weights(w_dq, w_uq, w_dkv, w_uk, w_uv, w_o):
    w_down = jnp.concatenate([w_dq, w_dkv, jnp.zeros((D, KVW - KV_RANK - ROPE_DIM), F32)], axis=1)
    w_q = _pad_heads(w_uq[:, :, QK_NOPE:], 0) + _pad_heads(w_uq[:, :, :QK_NOPE], HEAD_PAD - QK_NOPE)
    place = jnp.broadcast_to(jnp.eye(ROPE_DIM, dtype=F32)[:, None, :], (ROPE_DIM, N_HEADS, ROPE_DIM))
    w_k = jnp.concatenate([_pad_heads(w_uk, HEAD_PAD - QK_NOPE), _pad_heads(place, 0),
                           jnp.zeros((KVW - KV_RANK - ROPE_DIM, HP), F32)], axis=0)
    w_v = jnp.concatenate([_pad_heads(w_uv, 0), jnp.zeros((KVW - KV_RANK, HP), F32)], axis=0)
    w_kv = jnp.concatenate([w_k, w_v], axis=1)
    w_op = jnp.pad(w_o.reshape(N_HEADS, V_DIM, D), ((0, 0), (0, HEAD_PAD - V_DIM), (0, 0))).reshape(HP, D)
    return w_down.astype(BF16), w_q.astype(BF16), w_kv.astype(BF16), w_op.astype(BF16)


def _block_diag_groups(w_a, w_i):
    per = RG_GROUP // RG_BLOCK

    def bd(w):
        w = w.reshape(2, D // RG_GROUP, per, RG_BLOCK, RG_BLOCK)
        eye = jnp.eye(per, dtype=F32)
        return jnp.einsum('dgpkj,pq->dgpkqj', w, eye).reshape(2, D // RG_GROUP, RG_GROUP, RG_GROUP)

    return jnp.concatenate([bd(w_a), bd(w_i)], axis=-1).astype(BF16)


def _pad_rows(w, rows=8):
    return jnp.pad(w, ((0, rows - w.shape[0]), (0, 0)))


def kernel(x_prompt, x_sample, cache_mla_ckv, cache_mla_krope, state_rglru, c, c_ctx, mod_w, mod_b, norm_g, mla_w_dq, mla_g_q, mla_w_uq, mla_w_dkv, mla_g_kv, mla_w_uk, mla_w_uv, mla_w_o, rg_w_x, rg_w_y, rg_conv_w, rg_conv_b, rg_w_a, rg_b_a, rg_w_i, rg_b_i, rg_lambda, rg_w_out, sc_w_in, sc_conv_w, sc_w_out, ffn_w_gate, ffn_w_up, ffn_w_down, moe_w_router, moe_b_router, moe_w_gate, moe_w_up, moe_w_down):
    x = jnp.concatenate([x_prompt.reshape(P_ROWS, D), x_sample.reshape(S_ROWS, D)], axis=0)
    cond8 = jnp.concatenate([c_ctx[None], c, jnp.zeros((N_GROUPS - 1 - DEC_BATCH, D), F32)], axis=0)
    mod = _modulation(cond8, mod_w, mod_b).reshape(DEPTH, N_GROUPS, 6, D)
    tabs = _rope_tables()
    zrow = jnp.zeros((N_GROUPS, 5, D), F32)

    def vecs(l, sub):
        if sub == 0:
            g_next, sh, sc = norm_g[l, 2], mod[l, :, 3], mod[l, :, 4]
        elif l + 1 < DEPTH:
            g_next, sh, sc = norm_g[l + 1, 0], mod[l + 1, :, 0], mod[l + 1, :, 1]
        else:
            g_next, sh, sc = jnp.zeros((D,), F32), zrow[:, 0], zrow[:, 0]
        gv = _pad_rows(jnp.stack([norm_g[l, 1 + 2 * sub], g_next]))
        mv = jnp.concatenate([jnp.stack([mod[l, :, 2 + 3 * sub], sh, sc], axis=1), zrow], axis=1)
        return gv, mv

    gv0 = _pad_rows(jnp.stack([jnp.zeros((D,), F32), norm_g[0, 0]]))
    mv0 = jnp.concatenate([jnp.stack([zrow[:, 0], mod[0, :, 0], mod[0, :, 1]], axis=1), zrow], axis=1)
    h = _pre0(x, gv0, mv0)

    new_ckv, new_krope, new_rg = [], [], []
    comb = None
    for l in range(DEPTH):
        kind, j = l % 3, l // 3
        gv, mv = vecs(l, 0)
        router = None
        if l % 2 == 1:
            m = l // 2
            router = (jnp.pad(moe_w_router[m], ((0, 0), (0, LANES - N_EXPERTS))),
                      jnp.pad(moe_b_router[m], (0, LANES - N_EXPERTS)).reshape(1, LANES))
        if kind == 0:
            w_down, w_q, w_kv, w_op = _mla_weights(mla_w_dq[j], mla_w_uq[j], mla_w_dkv[j],
                                                   mla_w_uk[j], mla_w_uv[j], mla_w_o[j])
            cq, ckv, kr, kvb = _mla_down(h, w_down, mla_g_q[j].reshape(1, Q_RANK),
                                         mla_g_kv[j].reshape(1, KV_RANK), tabs)
            q = _mla_q(cq, w_q, tabs)
            new_ckv.append(ckv[:P_ROWS].reshape(BATCH, SEQ, KV_RANK))
            new_krope.append(kr[:P_ROWS, :ROPE_DIM].reshape(BATCH, SEQ, ROPE_DIM))
            cache = jnp.concatenate(
                [cache_mla_ckv[:, j], cache_mla_krope[:, j],
                 jnp.zeros((DEC_BATCH, PAST, KVW - KV_RANK - ROPE_DIM), F32)], axis=-1).astype(BF16)
            kv_lat = jnp.concatenate([cache, kvb[P_ROWS:].reshape(DEC_BATCH, DEC_SEQ, KVW)], axis=1)
            k_c, v_c = _mla_kv(kvb[:P_ROWS], w_kv)
            k_l, v_l = _mla_kv(kv_lat.reshape(DEC_BATCH * (PAST + DEC_SEQ), KVW), w_kv)
            o = _attention(q, k_c, v_c, BATCH, SEQ, SEQ, SEQ, SEQ, 0)
            o = _attention(q, k_l, v_l, DEC_BATCH, 512, 512, DEC_SEQ, PAST + DEC_SEQ, P_ROWS, prev_out=o)
            outs = _proj("plain", (o,), w_op, x, gv, mv, router=router)
        elif kind == 1:
            w_xy = jnp.concatenate([rg_w_x[j], rg_w_y[j]], axis=1).astype(BF16)
            u, gate = _rg_in(h, w_xy)
            wai = _block_diag_groups(rg_w_a[j], rg_w_i[j])
            bai = jnp.stack([rg_b_a[j], rg_b_i[j]], axis=1)
            scan_args = (_pad_rows(rg_conv_w[j]), rg_conv_b[j].reshape(1, D), wai, bai, rg_lambda[j])
            yf, yb, fin = _rg_scan(u, *scan_args, jnp.zeros((BATCH, 2, D), F32), BATCH, SEQ, 0)
            yf, yb, _ = _rg_scan(u, *scan_args, state_rglru[:, j], DEC_BATCH, DEC_SEQ, P_ROWS,
                                 prev=(yf, yb))
            new_rg.append(fin)
            outs = _proj("rg", (yf, yb, gate), rg_w_out[j].astype(BF16), x, gv, mv, router=router)
        else:
            bg, mm = _sc_in(h, sc_w_in[j].astype(BF16))
            outs = _proj("sc", (bg, mm, _pad_rows(sc_conv_w[j])), sc_w_out[j].astype(BF16), x, gv, mv,
                         router=router)
        x, h = outs[0], outs[1]
        if router is not None:
            comb = outs[2]

        gv, mv = vecs(l, 1)
        has_next = l + 1 < DEPTH
        m = l // 2
        if l % 2 == 0:
            half = D_FF // 2
            wg = ffn_w_gate[m].reshape(D, 2, half).transpose(1, 0, 2).astype(BF16)
            wu = ffn_w_up[m].reshape(D, 2, half).transpose(1, 0, 2).astype(BF16)
            wd = ffn_w_down[m].reshape(2, half, D).astype(BF16)
            outs = _ffn(h, wg, wu, wd, None, x, gv, mv, has_next=has_next)
        else:
            outs = _ffn(h, moe_w_gate[m].astype(BF16), moe_w_up[m].astype(BF16),
                        moe_w_down[m].astype(BF16), comb, x, gv, mv, has_next=has_next)
        x = outs[0]
        h = outs[1] if has_next else None

    y_prompt = x[:P_ROWS].reshape(BATCH, SEQ, D)
    y_sample = x[P_ROWS:].reshape(DEC_BATCH, DEC_SEQ, D)
    return (y_prompt, y_sample, jnp.stack(new_ckv, axis=1), jnp.stack(new_krope, axis=1),
            jnp.stack(new_rg, axis=1))
```

```python
import functools

import numpy as np
import jax
import jax.numpy as jnp
from jax import lax
from jax.experimental import pallas as pl
from jax.experimental.pallas import tpu as pltpu

F32 = jnp.float32
BF16 = jnp.bfloat16

D = 1024
BATCH = 32
SEQ = 256
DEPTH = 4
DEC_BATCH = 4
DEC_SEQ = 4096
PAST = 512
GRID_W = 64
N_HEADS = 16
QK_NOPE = 64
ROPE_DIM = 32
AXIS_DIM = 16
V_DIM = 64
Q_RANK = 384
KV_RANK = 256
ROPE_THETA = 10000.0
ATTN_SCALE = (QK_NOPE + ROPE_DIM) ** -0.5
RG_BLOCKS = 16
RG_BLOCK = 64
RG_C = 8.0
D_FF = 2816
N_EXPERTS = 8
D_FF_EXPERT = 1408
EPS = 1e-6

P_ROWS = BATCH * SEQ
S_ROWS = DEC_BATCH * DEC_SEQ
T_ROWS = P_ROWS + S_ROWS
N_GROUPS = 8
HEAD_PAD = 128
HP = N_HEADS * HEAD_PAD
KVW = 384

TM = 512
NP_T = P_ROWS // TM
NS_T = DEC_SEQ // TM
N_T = T_ROWS // TM
SUBLANES = 8
LANES = 128
VMEM_LIMIT = 56 * 1024 * 1024


def _cparams(sem):
    return pltpu.CompilerParams(dimension_semantics=sem, vmem_limit_bytes=VMEM_LIMIT)


def _group_of_tile(i):
    return jnp.maximum(i - NP_T + NS_T, 0) // NS_T


def _pos_tile(i):
    return jnp.maximum(i - NP_T, 0) % NS_T


def _dot(a, b):
    return jnp.dot(a, b, preferred_element_type=F32)


def _rms(x, g):
    ms = jnp.mean(x * x, axis=-1, keepdims=True)
    return x * lax.rsqrt(ms + EPS) * g


def _silu(x):
    return x * jax.nn.sigmoid(x)


def _gelu_tanh(x):
    return x * (0.5 * (1.0 + jnp.tanh(np.sqrt(2.0 / np.pi).astype(np.float32)
                                      * (x + 0.044715 * (x * x * x)))))


def _rope(x, c, s_lo, s_hi):
    return (x * c + pltpu.roll(x, LANES - 8, 1) * s_lo + pltpu.roll(x, 8, 1) * s_hi)


def _mod_kernel(c_ref, w_ref, b_ref, o_ref):
    c = c_ref[...]
    s = _silu(c).astype(BF16)
    o_ref[...] = _dot(s, w_ref[...].astype(BF16)) + b_ref[...]


def _modulation(cond8, mod_w, mod_b):
    tn = 1536
    return pl.pallas_call(
        _mod_kernel,
        grid=(DEPTH, 6 * D // tn),
        in_specs=[pl.BlockSpec((N_GROUPS, D), lambda l, n: (0, 0)),
                  pl.BlockSpec((None, D, tn), lambda l, n: (l, 0, n)),
                  pl.BlockSpec((None, 1, tn), lambda l, n: (l, 0, n))],
        out_specs=pl.BlockSpec((None, N_GROUPS, tn), lambda l, n: (l, 0, n)),
        out_shape=jax.ShapeDtypeStruct((DEPTH, N_GROUPS, 6 * D), F32),
        compiler_params=_cparams(("parallel", "parallel")),
        name="modulation",
    )(cond8, mod_w, mod_b.reshape(DEPTH, 1, 6 * D))


def _pre0_kernel(x_ref, gv_ref, mv_ref, h_ref):
    h = _rms(x_ref[...], gv_ref[1:2, :]) * (1.0 + mv_ref[2:3, :]) + mv_ref[1:2, :]
    h_ref[...] = h.astype(BF16)


def _pre0(x, gv, mv):
    return pl.pallas_call(
        _pre0_kernel,
        grid=(N_T,),
        in_specs=[pl.BlockSpec((TM, D), lambda i: (i, 0)),
                  pl.BlockSpec((8, D), lambda i: (0, 0)),
                  pl.BlockSpec((None, 8, D), lambda i: (_group_of_tile(i), 0, 0))],
        out_specs=pl.BlockSpec((TM, D), lambda i: (i, 0)),
        out_shape=jax.ShapeDtypeStruct((T_ROWS, D), BF16),
        compiler_params=_cparams(("parallel",)),
        name="pre0",
    )(x, gv, mv)


def _router(hn, wr_ref, br_ref):
    logits = jnp.dot(hn, wr_ref[...], preferred_element_type=F32,
                     precision=lax.Precision.HIGHEST) + br_ref[...]
    lane = lax.broadcasted_iota(jnp.int32, logits.shape, 1)
    neg = jnp.float32(-jnp.inf)
    logits = jnp.where(lane < N_EXPERTS, logits, neg)
    m1 = jnp.max(logits, axis=-1, keepdims=True)
    i1 = jnp.min(jnp.where(logits == m1, lane, LANES), axis=-1, keepdims=True)
    rest = jnp.where(lane == i1, neg, logits)
    m2 = jnp.max(rest, axis=-1, keepdims=True)
    i2 = jnp.min(jnp.where(rest == m2, lane, LANES), axis=-1, keepdims=True)
    e = jnp.exp(m2 - m1)
    p1 = 1.0 / (1.0 + e)
    p2 = e / (1.0 + e)
    return jnp.where(lane == i1, p1, jnp.where(lane == i2, p2, 0.0))


def _post_pre(x, out, gv_ref, mv_ref, has_next):
    xn = x + mv_ref[0:1, :] * _rms(out, gv_ref[0:1, :])
    if not has_next:
        return xn, None
    hn = _rms(xn, gv_ref[1:2, :]) * (1.0 + mv_ref[2:3, :]) + mv_ref[1:2, :]
    return xn, hn


def _epilogue(out, x_ref, gv_ref, mv_ref, rest, has_next, want_router):
    rest = list(rest)
    if want_router:
        wr_ref, br_ref = rest[0], rest[1]
        rest = rest[2:]
    xn, hn = _post_pre(x_ref[...], out, gv_ref, mv_ref, has_next)
    rest[0][...] = xn
    if has_next:
        rest[1][...] = hn.astype(BF16)
    if want_router:
        rest[2][...] = _router(hn, wr_ref, br_ref)


def _epilogue_specs(has_next, want_router):
    in_specs = [pl.BlockSpec((TM, D), lambda i, *_: (i, 0)),
                pl.BlockSpec((8, D), lambda i, *_: (0, 0)),
                pl.BlockSpec((None, 8, D), lambda i, *_: (_group_of_tile(i), 0, 0))]
    out_specs = [pl.BlockSpec((TM, D), lambda i, *_: (i, 0))]
    out_shape = [jax.ShapeDtypeStruct((T_ROWS, D), F32)]
    if want_router:
        in_specs += [pl.BlockSpec((D, LANES), lambda i, *_: (0, 0)),
                     pl.BlockSpec((1, LANES), lambda i, *_: (0, 0))]
    if has_next:
        out_specs.append(pl.BlockSpec((TM, D), lambda i, *_: (i, 0)))
        out_shape.append(jax.ShapeDtypeStruct((T_ROWS, D), BF16))
    if want_router:
        out_specs.append(pl.BlockSpec((TM, LANES), lambda i, *_: (i, 0)))
        out_shape.append(jax.ShapeDtypeStruct((T_ROWS, LANES), F32))
    return in_specs, out_specs, out_shape


def _seq_pos(i, rows):
    seq_len = jnp.where(i < NP_T, SEQ, DEC_SEQ)
    r = lax.broadcasted_iota(jnp.int32, (rows, 1), 0)
    return (i * rows + r) & (seq_len - 1), seq_len, r


def _shifted(m, prev_ref, next_ref, k, pos, seq_len, r):
    rows = m.shape[0]
    if k < 0:
        y = pltpu.roll(m, -k, 0)
        y = jnp.where(r == 0, prev_ref[SUBLANES - 1:SUBLANES, :], y)
        return jnp.where(pos + k < 0, 0.0, y)
    y = pltpu.roll(m, rows - k, 0)
    for q in range(k):
        y = jnp.where(r == rows - k + q, next_ref[q:q + 1, :], y)
    return jnp.where(pos + k >= seq_len, 0.0, y)


def _halo_specs(rows, row_block_of):
    per = rows // SUBLANES
    last = T_ROWS // SUBLANES - 1
    return [pl.BlockSpec((rows, D), lambda *g: (row_block_of(*g), 0)),
            pl.BlockSpec((SUBLANES, D), lambda *g: (jnp.maximum(row_block_of(*g) * per - 1, 0), 0)),
            pl.BlockSpec((SUBLANES, D), lambda *g: (jnp.minimum((row_block_of(*g) + 1) * per, last), 0))]


def _proj_kernel(*refs, mode, has_next, want_router):
    i = pl.program_id(0)
    if mode == "plain":
        a_ref, w_ref = refs[:2]
        rest = refs[2:]
        a = a_ref[...]
    elif mode == "rg":
        yf_ref, yb_ref, g_ref, w_ref = refs[:4]
        rest = refs[4:]
        a = ((yf_ref[...] + yb_ref[...]) * g_ref[...]).astype(BF16)
    else:
        bg_ref, m_ref, mp_ref, mn_ref, cw_ref, w_ref = refs[:6]
        rest = refs[6:]
        m = m_ref[...]
        pos, seq_len, r = _seq_pos(i, TM)
        z = (cw_ref[0:1, :] * _shifted(m, mp_ref, mn_ref, -1, pos, seq_len, r)
             + cw_ref[1:2, :] * m
             + cw_ref[2:3, :] * _shifted(m, mp_ref, mn_ref, 1, pos, seq_len, r))
        a = (bg_ref[...] * z).astype(BF16)
    out = _dot(a, w_ref[...])
    _epilogue(out, rest[0], rest[1], rest[2], rest[3:], has_next, want_router)


def _proj(mode, ins, w, x, gv, mv, has_next=True, router=None):
    want_router = router is not None
    k = w.shape[0]
    row = lambda i: (i, 0)
    if mode == "plain":
        in_specs = [pl.BlockSpec((TM, k), row)]
    elif mode == "rg":
        in_specs = [pl.BlockSpec((TM, D), row)] * 3
    else:
        bg, m, cw = ins
        ins = (bg, m, m, m, cw)
        in_specs = ([pl.BlockSpec((TM, D), row)] + _halo_specs(TM, lambda i: i)
                    + [pl.BlockSpec((8, D), lambda i: (0, 0))])
    in_specs.append(pl.BlockSpec((k, D), lambda i: (0, 0)))
    e_in, out_specs, out_shape = _epilogue_specs(has_next, want_router)
    args = list(ins) + [w, x, gv, mv] + (list(router) if want_router else [])
    return pl.pallas_call(
        functools.partial(_proj_kernel, mode=mode, has_next=has_next, want_router=want_router),
        grid=(N_T,),
        in_specs=in_specs + e_in,
        out_specs=out_specs,
        out_shape=out_shape,
        compiler_params=_cparams(("parallel",)),
        name="proj_" + mode,
    )(*args)


def _ffn_kernel(*refs, n_e, use_comb, has_next, want_router):
    e = pl.program_id(1)
    h_ref, wg_ref, wu_ref, wd_ref = refs[:4]
    refs = refs[4:]
    if use_comb:
        comb_ref = refs[0]
        refs = refs[1:]
    acc_ref = refs[-1]
    refs = refs[:-1]

    @pl.when(e == 0)
    def _():
        acc_ref[...] = jnp.zeros(acc_ref.shape, F32)

    h = h_ref[...]
    g = _dot(h, wg_ref[...])
    u = _dot(h, wu_ref[...])
    a = _silu(g) * u
    if use_comb:
        comb = comb_ref[...]
        lane = lax.broadcasted_iota(jnp.int32, comb.shape, 1)
        a = a * jnp.sum(jnp.where(lane == e, comb, 0.0), axis=-1, keepdims=True)
    acc_ref[...] += _dot(a.astype(BF16), wd_ref[...])

    @pl.when(e == n_e - 1)
    def _():
        _epilogue(acc_ref[...], refs[0], refs[1], refs[2], refs[3:], has_next, want_router)


def _ffn(h, wg, wu, wd, comb, x, gv, mv, has_next=True, router=None):
    n_e, _, f = wg.shape
    use_comb = comb is not None
    want_router = router is not None
    in_specs = [pl.BlockSpec((TM, D), lambda i, e: (i, 0)),
                pl.BlockSpec((None, D, f), lambda i, e: (e, 0, 0)),
                pl.BlockSpec((None, D, f), lambda i, e: (e, 0, 0)),
                pl.BlockSpec((None, f, D), lambda i, e: (e, 0, 0))]
    args = [h, wg, wu, wd]
    if use_comb:
        in_specs.append(pl.BlockSpec((TM, LANES), lambda i, e: (i, 0)))
        args.append(comb)
    e_in, out_specs, out_shape = _epilogue_specs(has_next, want_router)
    args += [x, gv, mv] + (list(router) if want_router else [])
    return pl.pallas_call(
        functools.partial(_ffn_kernel, n_e=n_e, use_comb=use_comb, has_next=has_next,
                          want_router=want_router),
        grid=(N_T, n_e),
        in_specs=in_specs + e_in,
        out_specs=out_specs,
        out_shape=out_shape,
        scratch_shapes=[pltpu.VMEM((TM, D), F32)],
        compiler_params=_cparams(("parallel", "arbitrary")),
        name="ffn_moe" if use_comb else "ffn_dense",
    )(*args)


def _mla_down_kernel(h_ref, w_ref, gq_ref, gkv_ref, rc_ref, rlo_ref, rhi_ref,
                     cq_ref, ckv_ref, kr_ref, kvb_ref):
    i = pl.program_id(0)
    y = _dot(h_ref[...], w_ref[...])
    cq_ref[...] = _rms(y[:, :Q_RANK], gq_ref[...]).astype(BF16)
    ckv = _rms(y[:, Q_RANK:Q_RANK + KV_RANK], gkv_ref[...])
    kr = y[:, Q_RANK + KV_RANK:]
    kr = jnp.where(i >= NP_T, _rope(kr, rc_ref[...], rlo_ref[...], rhi_ref[...]), kr)
    ckv_ref[...] = ckv
    kr_ref[...] = kr
    kvb_ref[:, :KV_RANK] = ckv.astype(BF16)
    kvb_ref[:, KV_RANK:] = kr.astype(BF16)


def _mla_down(h, w, gq, gkv, tabs):
    n = Q_RANK + KVW
    row = lambda i: (i, 0)
    const = lambda i: (0, 0)
    tab = pl.BlockSpec((TM, LANES), lambda i: (_pos_tile(i), 0))
    return pl.pallas_call(
        _mla_down_kernel,
        grid=(N_T,),
        in_specs=[pl.BlockSpec((TM, D), row), pl.BlockSpec((D, n), const),
                  pl.BlockSpec((1, Q_RANK), const), pl.BlockSpec((1, KV_RANK), const),
                  tab, tab, tab],
        out_specs=[pl.BlockSpec((TM, Q_RANK), row), pl.BlockSpec((TM, KV_RANK), row),
                   pl.BlockSpec((TM, LANES), row), pl.BlockSpec((TM, KVW), row)],
        out_shape=[jax.ShapeDtypeStruct((T_ROWS, Q_RANK), BF16),
                   jax.ShapeDtypeStruct((T_ROWS, KV_RANK), F32),
                   jax.ShapeDtypeStruct((T_ROWS, LANES), F32),
                   jax.ShapeDtypeStruct((T_ROWS, KVW), BF16)],
        compiler_params=_cparams(("parallel",)),
        name="mla_down",
    )(h, w, gq, gkv, *tabs)


def _dot_nt(a, b):
    return lax.dot_general(a, b, (((1,), (1,)), ((), ())), preferred_element_type=F32)


def _mla_q_kernel(cq_ref, w_ref, cr_ref, sr_ref, cc_ref, sc_ref, q_ref):
    i = pl.program_id(0)
    y = _dot_nt(w_ref[...], cq_ref[...])
    is_latent = i >= NP_T
    cr = jnp.where(is_latent, cr_ref[...], 1.0)
    sr = jnp.where(is_latent, sr_ref[...], 0.0)
    cc = jnp.where(is_latent, cc_ref[...], 1.0)
    sc = jnp.where(is_latent, sc_ref[...], 0.0)
    for hd in range(N_HEADS):
        r0 = hd * HEAD_PAD
        x0, x1, x2, x3 = (y[r0 + 8 * a:r0 + 8 * (a + 1), :] for a in range(4))
        rot = jnp.concatenate([x0 * cr - x1 * sr, x0 * sr + x1 * cr,
                               x2 * cc - x3 * sc, x2 * sc + x3 * cc], axis=0)
        q_ref[r0:r0 + ROPE_DIM, :] = rot.astype(BF16)
        q_ref[r0 + ROPE_DIM:r0 + HEAD_PAD, :] = y[r0 + ROPE_DIM:r0 + HEAD_PAD, :].astype(BF16)


def _mla_q(cq, w_t, tabs_t):
    tab = pl.BlockSpec((8, TM), lambda i: (0, _pos_tile(i)))
    return pl.pallas_call(
        _mla_q_kernel,
        grid=(N_T,),
        in_specs=[pl.BlockSpec((TM, Q_RANK), lambda i: (i, 0)),
                  pl.BlockSpec((HP, Q_RANK), lambda i: (0, 0)), tab, tab, tab, tab],
        out_specs=pl.BlockSpec((HP, TM), lambda i: (0, i)),
        out_shape=jax.ShapeDtypeStruct((HP, T_ROWS), BF16),
        compiler_params=_cparams(("parallel",)),
        name="mla_q",
    )(cq, w_t, *tabs_t)


def _mla_kv_kernel(c_ref, wk_ref, wvt_ref, k_ref, vt_ref):
    c = c_ref[...]
    k_ref[...] = _dot(c, wk_ref[...]).astype(BF16)
    vt_ref[...] = _dot_nt(wvt_ref[...], c).astype(BF16)


def _mla_kv(ckvkr, w_k, w_vt):
    rows = ckvkr.shape[0]
    hv = N_HEADS * V_DIM
    return pl.pallas_call(
        _mla_kv_kernel,
        grid=(rows // TM,),
        in_specs=[pl.BlockSpec((TM, KVW), lambda i: (i, 0)),
                  pl.BlockSpec((KVW, HP), lambda i: (0, 0)),
                  pl.BlockSpec((hv, KVW), lambda i: (0, 0))],
        out_specs=[pl.BlockSpec((TM, HP), lambda i: (i, 0)), pl.BlockSpec((hv, TM), lambda i: (0, i))],
        out_shape=[jax.ShapeDtypeStruct((rows, HP), BF16), jax.ShapeDtypeStruct((hv, rows), BF16)],
        compiler_params=_cparams(("parallel",)),
        name="mla_kv",
    )(ckvkr, w_k, w_vt)


ATTN_C2 = ATTN_SCALE * float(np.log2(np.e))


def _attn_kernel(*refs, nk, aliased):
    if aliased:
        refs = refs[1:]
    qt_ref, k_ref, vt_ref, o_ref, m_scr, l_scr, acc_scr = refs
    ki = pl.program_id(2)

    @pl.when(ki == 0)
    def _():
        m_scr[...] = jnp.full(m_scr.shape, -jnp.inf, F32)
        l_scr[...] = jnp.zeros(l_scr.shape, F32)
        acc_scr[...] = jnp.zeros(acc_scr.shape, F32)

    for hd in range(N_HEADS):
        qs = slice(hd * HEAD_PAD, (hd + 1) * HEAD_PAD)
        vs = slice(hd * V_DIM, (hd + 1) * V_DIM)
        st = slice(hd, hd + 1)
        s = _dot(k_ref[:, qs], qt_ref[qs, :]) * ATTN_C2
        m_prev = m_scr[st, :]
        m_new = jnp.maximum(m_prev, jnp.max(s, axis=0, keepdims=True))
        alpha = jnp.exp2(m_prev - m_new)
        p = jnp.exp2(s - m_new)
        l_scr[st, :] = alpha * l_scr[st, :] + jnp.sum(p, axis=0, keepdims=True)
        acc_scr[vs, :] = alpha * acc_scr[vs, :] + _dot(vt_ref[vs, :], p.astype(BF16))
        m_scr[st, :] = m_new

    @pl.when(ki == nk - 1)
    def _():
        for hd in range(N_HEADS):
            vs = slice(hd * V_DIM, (hd + 1) * V_DIM)
            acc_scr[vs, :] = acc_scr[vs, :] / l_scr[hd:hd + 1, :]
        o_ref[...] = acc_scr[...].T.astype(BF16)


def _attention(qt, k, vt, n_b, tq, tk, lq, lk, q_row0, prev_out=None):
    nq, nk = lq // tq, lk // tk
    qb0 = q_row0 // tq
    hv = N_HEADS * V_DIM
    aliased = prev_out is not None
    in_specs = [pl.BlockSpec((HP, tq), lambda b, qi, ki: (0, qb0 + b * nq + qi)),
                pl.BlockSpec((tk, HP), lambda b, qi, ki: (b * nk + ki, 0)),
                pl.BlockSpec((hv, tk), lambda b, qi, ki: (0, b * nk + ki))]
    args = [qt, k, vt]
    if aliased:
        in_specs = [pl.BlockSpec(memory_space=pl.ANY)] + in_specs
        args = [prev_out] + args
    return pl.pallas_call(
        functools.partial(_attn_kernel, nk=nk, aliased=aliased),
        grid=(n_b, nq, nk),
        in_specs=in_specs,
        out_specs=pl.BlockSpec((tq, hv), lambda b, qi, ki: (qb0 + b * nq + qi, 0)),
        out_shape=jax.ShapeDtypeStruct((T_ROWS, hv), BF16),
        scratch_shapes=[pltpu.VMEM((N_HEADS, tq), F32), pltpu.VMEM((N_HEADS, tq), F32),
                        pltpu.VMEM((hv, tq), F32)],
        input_output_aliases={0: 0} if aliased else {},
        compiler_params=_cparams(("parallel", "parallel", "arbitrary")),
        name="attn_latent" if aliased else "attn_context",
    )(*args)


def _rg_in_kernel(h_ref, w_ref, u_ref, g_ref):
    y = _dot(h_ref[...], w_ref[...])
    u_ref[...] = y[:, :D]
    g_ref[...] = _gelu_tanh(y[:, D:])


def _rg_in(h, w):
    return pl.pallas_call(
        _rg_in_kernel,
        grid=(N_T,),
        in_specs=[pl.BlockSpec((TM, D), lambda i: (i, 0)), pl.BlockSpec((D, 2 * D), lambda i: (0, 0))],
        out_specs=[pl.BlockSpec((TM, D), lambda i: (i, 0))] * 2,
        out_shape=[jax.ShapeDtypeStruct((T_ROWS, D), F32)] * 2,
        compiler_params=_cparams(("parallel",)),
        name="rg_in",
    )(h, w)


RG_TC = 256
RG_GROUP = 256


def _rg_scan_kernel(*refs, nj, seq_len, aliased):
    if aliased:
        refs = refs[2:]
    (uf_ref, ufp_ref, ufn_ref, ub_ref, ubp_ref, ubn_ref, cw_ref, cb_ref, wai_ref, bai_ref,
     lam_ref, h0_ref, yf_ref, yb_ref, fin_ref, carry_ref) = refs
    j = pl.program_id(1)
    tc = RG_TC
    r = lax.broadcasted_iota(jnp.int32, (tc, 1), 0)
    sub = r & (SUBLANES - 1)

    @pl.when(j == 0)
    def _():
        carry_ref[0:2, :] = h0_ref[...]

    def gates(m_ref, p_ref, n_ref, d, chunk):
        m = m_ref[...]
        pos = chunk * tc + r
        u = (cw_ref[0:1, :] * _shifted(m, p_ref, n_ref, -1, pos, seq_len, r)
             + cw_ref[1:2, :] * m
             + cw_ref[2:3, :] * _shifted(m, p_ref, n_ref, 1, pos, seq_len, r)
             + cw_ref[3:4, :] * _shifted(m, p_ref, n_ref, 2, pos, seq_len, r)
             + cb_ref[...])
        ub = u.astype(BF16)
        ra, ri = [], []
        for q in range(D // RG_GROUP):
            y = _dot(ub[:, q * RG_GROUP:(q + 1) * RG_GROUP], wai_ref[d, q])
            ra.append(y[:, :RG_GROUP])
            ri.append(y[:, RG_GROUP:])
        rr = jax.nn.sigmoid(jnp.concatenate(ra, axis=1) + bai_ref[d, 0:1, :])
        ii = jax.nn.sigmoid(jnp.concatenate(ri, axis=1) + bai_ref[d, 1:2, :])
        nl = -lam_ref[d:d + 1, :]
        softplus = jnp.maximum(nl, 0.0) + jnp.log1p(jnp.exp(-jnp.abs(nl)))
        log_a = (-RG_C * softplus) * rr
        a = jnp.exp(log_a)
        bx = jnp.sqrt(jnp.maximum(-jnp.tanh(log_a) * (a * a + 1.0), 0.0)) * (ii * u)
        return a, bx

    a, b = gates(uf_ref, ufp_ref, ufn_ref, 0, j)
    for k in (1, 2, 4):
        ok = sub >= k
        b = jnp.where(ok, a * pltpu.roll(b, k, 0) + b, b)
        a = jnp.where(ok, a * pltpu.roll(a, k, 0), a)
    h = carry_ref[0:1, :]
    for g in range(tc // SUBLANES):
        sl = slice(g * SUBLANES, (g + 1) * SUBLANES)
        hg = a[sl] * h + b[sl]
        yf_ref[sl, :] = hg
        h = hg[SUBLANES - 1:SUBLANES, :]
    carry_ref[0:1, :] = h

    a, b = gates(ub_ref, ubp_ref, ubn_ref, 1, nj - 1 - j)
    for k in (1, 2, 4):
        ok = sub < SUBLANES - k
        b = jnp.where(ok, a * pltpu.roll(b, tc - k, 0) + b, b)
        a = jnp.where(ok, a * pltpu.roll(a, tc - k, 0), a)
    h = carry_ref[1:2, :]
    for g in reversed(range(tc // SUBLANES)):
        sl = slice(g * SUBLANES, (g + 1) * SUBLANES)
        hg = a[sl] * h + b[sl]
        yb_ref[sl, :] = hg
        h = hg[0:1, :]
    carry_ref[1:2, :] = h

    @pl.when(j == nj - 1)
    def _():
        fin_ref[...] = carry_ref[0:2, :]


def _rg_scan(u, cw, cb, wai, bai, lam, h0, n_seq, seq_len, row0, prev=None):
    nj = seq_len // RG_TC
    b0 = row0 // RG_TC
    aliased = prev is not None
    fwd = lambda s, j: b0 + s * nj + j
    bwd = lambda s, j: b0 + s * nj + (nj - 1 - j)
    const2 = lambda s, j: (0, 0)
    in_specs = (_halo_specs(RG_TC, fwd) + _halo_specs(RG_TC, bwd)
                + [pl.BlockSpec((8, D), const2), pl.BlockSpec((1, D), const2),
                   pl.BlockSpec((2, D // RG_GROUP, RG_GROUP, 2 * RG_GROUP), lambda s, j: (0, 0, 0, 0)),
                   pl.BlockSpec((2, 2, D), lambda s, j: (0, 0, 0)),
                   pl.BlockSpec((2, D), const2),
                   pl.BlockSpec((None, 2, D), lambda s, j: (s, 0, 0))])
    args = [u, u, u, u, u, u, cw, cb, wai, bai, lam, h0]
    if aliased:
        in_specs = [pl.BlockSpec(memory_space=pl.ANY)] * 2 + in_specs
        args = list(prev) + args
    return pl.pallas_call(
        functools.partial(_rg_scan_kernel, nj=nj, seq_len=seq_len, aliased=aliased),
        grid=(n_seq, nj),
        in_specs=in_specs,
        out_specs=[pl.BlockSpec((RG_TC, D), lambda s, j: (fwd(s, j), 0)),
                   pl.BlockSpec((RG_TC, D), lambda s, j: (bwd(s, j), 0)),
                   pl.BlockSpec((None, 2, D), lambda s, j: (s, 0, 0))],
        out_shape=[jax.ShapeDtypeStruct((T_ROWS, D), F32), jax.ShapeDtypeStruct((T_ROWS, D), F32),
                   jax.ShapeDtypeStruct((n_seq, 2, D), F32)],
        scratch_shapes=[pltpu.VMEM((8, D), F32)],
        input_output_aliases={0: 0, 1: 1} if aliased else {},
        compiler_params=_cparams(("parallel", "arbitrary")),
        name="rg_scan_latent" if aliased else "rg_scan_context",
    )(*args)


def _sc_in_kernel(h_ref, w_ref, bg_ref, m_ref):
    y = _dot(h_ref[...], w_ref[...])
    bg_ref[...] = y[:, :D]
    m_ref[...] = y[:, D:2 * D] * y[:, 2 * D:]


def _sc_in(h, w):
    return pl.pallas_call(
        _sc_in_kernel,
        grid=(N_T,),
        in_specs=[pl.BlockSpec((TM, D), lambda i: (i, 0)), pl.BlockSpec((D, 3 * D), lambda i: (0, 0))],
        out_specs=[pl.BlockSpec((TM, D), lambda i: (i, 0))] * 2,
        out_shape=[jax.ShapeDtypeStruct((T_ROWS, D), F32)] * 2,
        compiler_params=_cparams(("parallel",)),
        name="sc_in",
    )(h, w)


def _rope_tables():
    t = np.arange(DEC_SEQ)
    inv = ROPE_THETA ** (-jnp.arange(0, AXIS_DIM, 2, dtype=F32) / AXIS_DIM)
    ang_r = jnp.asarray((t // GRID_W).astype(np.float32))[:, None] * inv
    ang_c = jnp.asarray((t % GRID_W).astype(np.float32))[:, None] * inv
    cr, sr, cc, sc = jnp.cos(ang_r), jnp.sin(ang_r), jnp.cos(ang_c), jnp.sin(ang_c)
    z8 = jnp.zeros((DEC_SEQ, 8), F32)
    pad1 = jnp.ones((DEC_SEQ, LANES - ROPE_DIM), F32)
    pad0 = jnp.zeros((DEC_SEQ, LANES - ROPE_DIM), F32)
    c = jnp.concatenate([cr, cr, cc, cc, pad1], axis=1)
    lo = jnp.concatenate([-sr, z8, -sc, z8, pad0], axis=1)
    hi = jnp.concatenate([z8, sr, z8, sc, pad0], axis=1)
    return (c, lo, hi), (cr.T, sr.T, cc.T, sc.T)


def _pad_heads(w, lo):
    r, h, d = w.shape
    return jnp.pad(w, ((0, 0), (0, 0), (lo, HEAD_PAD - lo - d))).reshape(r, h * HEAD_PAD)


def _mla_weights(w_dq, w_uq, w_dkv, w_uk, w_uv, w_o):
    w_down = jnp.concatenate([w_dq, w_dkv, jnp.zeros((D, KVW - KV_RANK - ROPE_DIM), F32)], axis=1)
    w_q = _pad_heads(w_uq[:, :, QK_NOPE:], 0) + _pad_heads(w_uq[:, :, :QK_NOPE], HEAD_PAD - QK_NOPE)
    place = jnp.broadcast_to(jnp.eye(ROPE_DIM, dtype=F32)[:, None, :], (ROPE_DIM, N_HEADS, ROPE_DIM))
    w_k = jnp.concatenate([_pad_heads(w_uk, HEAD_PAD - QK_NOPE), _pad_heads(place, 0),
                           jnp.zeros((KVW - KV_RANK - ROPE_DIM, HP), F32)], axis=0)
    w_vt = jnp.pad(w_uv.reshape(KV_RANK, N_HEADS * V_DIM).T, ((0, 0), (0, KVW - KV_RANK)))
    return (w_down.astype(BF16), w_q.T.astype(BF16), w_k.astype(BF16), w_vt.astype(BF16),
            w_o.astype(BF16))


def _block_diag_groups(w_a, w_i):
    per = RG_GROUP // RG_BLOCK

    def bd(w):
        w = w.reshape(2, D // RG_GROUP, per, RG_BLOCK, RG_BLOCK)
        eye = jnp.eye(per, dtype=F32)
        return jnp.einsum('dgpkj,pq->dgpkqj', w, eye).reshape(2, D // RG_GROUP, RG_GROUP, RG_GROUP)

    return jnp.concatenate([bd(w_a), bd(w_i)], axis=-1).astype(BF16)


def _pad_rows(w, rows=8):
    return jnp.pad(w, ((0, rows - w.shape[0]), (0, 0)))


def kernel(x_prompt, x_sample, cache_mla_ckv, cache_mla_krope, state_rglru, c, c_ctx, mod_w, mod_b, norm_g, mla_w_dq, mla_g_q, mla_w_uq, mla_w_dkv, mla_g_kv, mla_w_uk, mla_w_uv, mla_w_o, rg_w_x, rg_w_y, rg_conv_w, rg_conv_b, rg_w_a, rg_b_a, rg_w_i, rg_b_i, rg_lambda, rg_w_out, sc_w_in, sc_conv_w, sc_w_out, ffn_w_gate, ffn_w_up, ffn_w_down, moe_w_router, moe_b_router, moe_w_gate, moe_w_up, moe_w_down):
    x = jnp.concatenate([x_prompt.reshape(P_ROWS, D), x_sample.reshape(S_ROWS, D)], axis=0)
    cond8 = jnp.concatenate([c_ctx[None], c, jnp.zeros((N_GROUPS - 1 - DEC_BATCH, D), F32)], axis=0)
    mod = _modulation(cond8, mod_w, mod_b).reshape(DEPTH, N_GROUPS, 6, D)
    tabs, tabs_t = _rope_tables()
    zrow = jnp.zeros((N_GROUPS, 5, D), F32)

    def vecs(l, sub):
        if sub == 0:
            g_next, sh, sc = norm_g[l, 2], mod[l, :, 3], mod[l, :, 4]
        elif l + 1 < DEPTH:
            g_next, sh, sc = norm_g[l + 1, 0], mod[l + 1, :, 0], mod[l + 1, :, 1]
        else:
            g_next, sh, sc = jnp.zeros((D,), F32), zrow[:, 0], zrow[:, 0]
        gv = _pad_rows(jnp.stack([norm_g[l, 1 + 2 * sub], g_next]))
        mv = jnp.concatenate([jnp.stack([mod[l, :, 2 + 3 * sub], sh, sc], axis=1), zrow], axis=1)
        return gv, mv

    gv0 = _pad_rows(jnp.stack([jnp.zeros((D,), F32), norm_g[0, 0]]))
    mv0 = jnp.concatenate([jnp.stack([zrow[:, 0], mod[0, :, 0], mod[0, :, 1]], axis=1), zrow], axis=1)
    h = _pre0(x, gv0, mv0)

    new_ckv, new_krope, new_rg = [], [], []
    comb = None
    for l in range(DEPTH):
        kind, j = l % 3, l // 3
        gv, mv = vecs(l, 0)
        router = None
        if l % 2 == 1:
            m = l // 2
            router = (jnp.pad(moe_w_router[m], ((0, 0), (0, LANES - N_EXPERTS))),
                      jnp.pad(moe_b_router[m], (0, LANES - N_EXPERTS)).reshape(1, LANES))
        if kind == 0:
            w_down, w_qt, w_k, w_vt, w_op = _mla_weights(mla_w_dq[j], mla_w_uq[j], mla_w_dkv[j],
                                                         mla_w_uk[j], mla_w_uv[j], mla_w_o[j])
            cq, ckv, kr, kvb = _mla_down(h, w_down, mla_g_q[j].reshape(1, Q_RANK),
                                         mla_g_kv[j].reshape(1, KV_RANK), tabs)
            q = _mla_q(cq, w_qt, tabs_t)
            new_ckv.append(ckv[:P_ROWS].reshape(BATCH, SEQ, KV_RANK))
            new_krope.append(kr[:P_ROWS, :ROPE_DIM].reshape(BATCH, SEQ, ROPE_DIM))
            cache = jnp.concatenate(
                [cache_mla_ckv[:, j], cache_mla_krope[:, j],
                 jnp.zeros((DEC_BATCH, PAST, KVW - KV_RANK - ROPE_DIM), F32)], axis=-1).astype(BF16)
            kv_lat = jnp.concatenate([cache, kvb[P_ROWS:].reshape(DEC_BATCH, DEC_SEQ, KVW)], axis=1)
            k_c, v_c = _mla_kv(kvb[:P_ROWS], w_k, w_vt)
            k_l, v_l = _mla_kv(kv_lat.reshape(DEC_BATCH * (PAST + DEC_SEQ), KVW), w_k, w_vt)
            o = _attention(q, k_c, v_c, BATCH, SEQ, SEQ, SEQ, SEQ, 0)
            o = _attention(q, k_l, v_l, DEC_BATCH, 512, 512, DEC_SEQ, PAST + DEC_SEQ, P_ROWS, prev_out=o)
            outs = _proj("plain", (o,), w_op, x, gv, mv, router=router)
        elif kind == 1:
            w_xy = jnp.concatenate([rg_w_x[j], rg_w_y[j]], axis=1).astype(BF16)
            u, gate = _rg_in(h, w_xy)
            wai = _block_diag_groups(rg_w_a[j], rg_w_i[j])
            bai = jnp.stack([rg_b_a[j], rg_b_i[j]], axis=1)
            scan_args = (_pad_rows(rg_conv_w[j]), rg_conv_b[j].reshape(1, D), wai, bai, rg_lambda[j])
            yf, yb, fin = _rg_scan(u, *scan_args, jnp.zeros((BATCH, 2, D), F32), BATCH, SEQ, 0)
            yf, yb, _ = _rg_scan(u, *scan_args, state_rglru[:, j], DEC_BATCH, DEC_SEQ, P_ROWS,
                                 prev=(yf, yb))
            new_rg.append(fin)
            outs = _proj("rg", (yf, yb, gate), rg_w_out[j].astype(BF16), x, gv, mv, router=router)
        else:
            bg, mm = _sc_in(h, sc_w_in[j].astype(BF16))
            outs = _proj("sc", (bg, mm, _pad_rows(sc_conv_w[j])), sc_w_out[j].astype(BF16), x, gv, mv,
                         router=router)
        x, h = outs[0], outs[1]
        if router is not None:
            comb = outs[2]

        gv, mv = vecs(l, 1)
        has_next = l + 1 < DEPTH
        m = l // 2
        if l % 2 == 0:
            half = D_FF // 2
            wg = ffn_w_gate[m].reshape(D, 2, half).transpose(1, 0, 2).astype(BF16)
            wu = ffn_w_up[m].reshape(D, 2, half).transpose(1, 0, 2).astype(BF16)
            wd = ffn_w_down[m].reshape(2, half, D).astype(BF16)
            outs = _ffn(h, wg, wu, wd, None, x, gv, mv, has_next=has_next)
        else:
            outs = _ffn(h, moe_w_gate[m].astype(BF16), moe_w_up[m].astype(BF16),
                        moe_w_down[m].astype(BF16), comb, x, gv, mv, has_next=has_next)
        x = outs[0]
        h = outs[1] if has_next else None

    y_prompt = x[:P_ROWS].reshape(BATCH, SEQ, D)
    y_sample = x[P_ROWS:].reshape(DEC_BATCH, DEC_SEQ, D)
    return (y_prompt, y_sample, jnp.stack(new_ckv, axis=1), jnp.stack(new_krope, axis=1),
            jnp.stack(new_rg, axis=1))
```

```python
import functools

import numpy as np
import jax
import jax.numpy as jnp
from jax import lax
from jax.experimental import pallas as pl
from jax.experimental.pallas import tpu as pltpu

F32 = jnp.float32
BF16 = jnp.bfloat16

D = 1024
BATCH = 32
SEQ = 256
DEPTH = 4
DEC_BATCH = 4
DEC_SEQ = 4096
PAST = 512
GRID_W = 64
N_HEADS = 16
QK_NOPE = 64
ROPE_DIM = 32
AXIS_DIM = 16
V_DIM = 64
Q_RANK = 384
KV_RANK = 256
ROPE_THETA = 10000.0
ATTN_SCALE = (QK_NOPE + ROPE_DIM) ** -0.5
RG_BLOCKS = 16
RG_BLOCK = 64
RG_C = 8.0
D_FF = 2816
N_EXPERTS = 8
D_FF_EXPERT = 1408
EPS = 1e-6

P_ROWS = BATCH * SEQ
S_ROWS = DEC_BATCH * DEC_SEQ
T_ROWS = P_ROWS + S_ROWS
N_GROUPS = 8
HEAD_PAD = 128
HP = N_HEADS * HEAD_PAD
KVW = 384

TM = 512
NP_T = P_ROWS // TM
NS_T = DEC_SEQ // TM
N_T = T_ROWS // TM
SUBLANES = 8
LANES = 128
VMEM_LIMIT = 56 * 1024 * 1024


def _cparams(sem):
    return pltpu.CompilerParams(dimension_semantics=sem, vmem_limit_bytes=VMEM_LIMIT)


def _group_of_tile(i):
    return jnp.maximum(i - NP_T + NS_T, 0) // NS_T


def _pos_tile(i):
    return jnp.maximum(i - NP_T, 0) % NS_T


def _dot(a, b):
    return jnp.dot(a, b, preferred_element_type=F32)


def _rms(x, g):
    ms = jnp.mean(x * x, axis=-1, keepdims=True)
    return x * lax.rsqrt(ms + EPS) * g


def _silu(x):
    return x * jax.nn.sigmoid(x)


def _gelu_tanh(x):
    return x * (0.5 * (1.0 + jnp.tanh(np.sqrt(2.0 / np.pi).astype(np.float32)
                                      * (x + 0.044715 * (x * x * x)))))


def _rope(x, c, s_lo, s_hi):
    return (x * c + pltpu.roll(x, LANES - 8, 1) * s_lo + pltpu.roll(x, 8, 1) * s_hi)


def _mod_kernel(c_ref, w_ref, b_ref, o_ref):
    c = c_ref[...]
    s = _silu(c).astype(BF16)
    o_ref[...] = _dot(s, w_ref[...].astype(BF16)) + b_ref[...]


def _modulation(cond8, mod_w, mod_b):
    tn = 1536
    return pl.pallas_call(
        _mod_kernel,
        grid=(DEPTH, 6 * D // tn),
        in_specs=[pl.BlockSpec((N_GROUPS, D), lambda l, n: (0, 0)),
                  pl.BlockSpec((None, D, tn), lambda l, n: (l, 0, n)),
                  pl.BlockSpec((None, 1, tn), lambda l, n: (l, 0, n))],
        out_specs=pl.BlockSpec((None, N_GROUPS, tn), lambda l, n: (l, 0, n)),
        out_shape=jax.ShapeDtypeStruct((DEPTH, N_GROUPS, 6 * D), F32),
        compiler_params=_cparams(("parallel", "parallel")),
        name="modulation",
    )(cond8, mod_w, mod_b.reshape(DEPTH, 1, 6 * D))


def _pre0_kernel(x_ref, gv_ref, mv_ref, h_ref):
    h = _rms(x_ref[...], gv_ref[1:2, :]) * (1.0 + mv_ref[2:3, :]) + mv_ref[1:2, :]
    h_ref[...] = h.astype(BF16)


def _pre0(x, gv, mv):
    return pl.pallas_call(
        _pre0_kernel,
        grid=(N_T,),
        in_specs=[pl.BlockSpec((TM, D), lambda i: (i, 0)),
                  pl.BlockSpec((8, D), lambda i: (0, 0)),
                  pl.BlockSpec((None, 8, D), lambda i: (_group_of_tile(i), 0, 0))],
        out_specs=pl.BlockSpec((TM, D), lambda i: (i, 0)),
        out_shape=jax.ShapeDtypeStruct((T_ROWS, D), BF16),
        compiler_params=_cparams(("parallel",)),
        name="pre0",
    )(x, gv, mv)


def _router(hn, wr_ref, br_ref):
    logits = jnp.dot(hn, wr_ref[...], preferred_element_type=F32,
                     precision=lax.Precision.HIGHEST) + br_ref[...]
    lane = lax.broadcasted_iota(jnp.int32, logits.shape, 1)
    neg = jnp.float32(-jnp.inf)
    logits = jnp.where(lane < N_EXPERTS, logits, neg)
    m1 = jnp.max(logits, axis=-1, keepdims=True)
    i1 = jnp.min(jnp.where(logits == m1, lane, LANES), axis=-1, keepdims=True)
    rest = jnp.where(lane == i1, neg, logits)
    m2 = jnp.max(rest, axis=-1, keepdims=True)
    i2 = jnp.min(jnp.where(rest == m2, lane, LANES), axis=-1, keepdims=True)
    e = jnp.exp(m2 - m1)
    p1 = 1.0 / (1.0 + e)
    p2 = e / (1.0 + e)
    return jnp.where(lane == 0, i1.astype(F32),
                     jnp.where(lane == 1, i2.astype(F32),
                               jnp.where(lane == 2, p1, jnp.where(lane == 3, p2, 0.0))))


def _post_pre(x, out, gv_ref, mv_ref, has_next):
    xn = x + mv_ref[0:1, :] * _rms(out, gv_ref[0:1, :])
    if not has_next:
        return xn, None
    hn = _rms(xn, gv_ref[1:2, :]) * (1.0 + mv_ref[2:3, :]) + mv_ref[1:2, :]
    return xn, hn


def _epilogue(out, x_ref, gv_ref, mv_ref, rest, has_next, want_router):
    rest = list(rest)
    if want_router:
        wr_ref, br_ref = rest[0], rest[1]
        rest = rest[2:]
    xn, hn = _post_pre(x_ref[...], out, gv_ref, mv_ref, has_next)
    rest[0][...] = xn
    if has_next:
        rest[1][...] = hn.astype(rest[1].dtype)
    if want_router:
        rest[2][...] = _router(hn, wr_ref, br_ref)


def _epilogue_specs(has_next, want_router):
    in_specs = [pl.BlockSpec((TM, D), lambda i, *_: (i, 0)),
                pl.BlockSpec((8, D), lambda i, *_: (0, 0)),
                pl.BlockSpec((None, 8, D), lambda i, *_: (_group_of_tile(i), 0, 0))]
    out_specs = [pl.BlockSpec((TM, D), lambda i, *_: (i, 0))]
    out_shape = [jax.ShapeDtypeStruct((T_ROWS, D), F32)]
    if want_router:
        in_specs += [pl.BlockSpec((D, LANES), lambda i, *_: (0, 0)),
                     pl.BlockSpec((1, LANES), lambda i, *_: (0, 0))]
    if has_next:
        out_specs.append(pl.BlockSpec((TM, D), lambda i, *_: (i, 0)))
        out_shape.append(jax.ShapeDtypeStruct((T_ROWS, D), F32 if want_router else BF16))
    if want_router:
        out_specs.append(pl.BlockSpec((TM, LANES), lambda i, *_: (i, 0)))
        out_shape.append(jax.ShapeDtypeStruct((T_ROWS, LANES), F32))
    return in_specs, out_specs, out_shape


def _seq_pos(i, rows):
    seq_len = jnp.where(i < NP_T, SEQ, DEC_SEQ)
    r = lax.broadcasted_iota(jnp.int32, (rows, 1), 0)
    return (i * rows + r) & (seq_len - 1), seq_len, r


def _shifted(m, prev_ref, next_ref, k, pos, seq_len, r):
    rows = m.shape[0]
    if k < 0:
        y = pltpu.roll(m, -k, 0)
        y = jnp.where(r == 0, prev_ref[SUBLANES - 1:SUBLANES, :], y)
        return jnp.where(pos + k < 0, 0.0, y)
    y = pltpu.roll(m, rows - k, 0)
    for q in range(k):
        y = jnp.where(r == rows - k + q, next_ref[q:q + 1, :], y)
    return jnp.where(pos + k >= seq_len, 0.0, y)


def _halo_specs(rows, row_block_of):
    per = rows // SUBLANES
    last = T_ROWS // SUBLANES - 1
    return [pl.BlockSpec((rows, D), lambda *g: (row_block_of(*g), 0)),
            pl.BlockSpec((SUBLANES, D), lambda *g: (jnp.maximum(row_block_of(*g) * per - 1, 0), 0)),
            pl.BlockSpec((SUBLANES, D), lambda *g: (jnp.minimum((row_block_of(*g) + 1) * per, last), 0))]


def _proj_kernel(*refs, mode, has_next, want_router):
    i = pl.program_id(0)
    if mode == "plain":
        a_ref, w_ref = refs[:2]
        rest = refs[2:]
        a = a_ref[...]
    elif mode == "rg":
        yf_ref, yb_ref, g_ref, w_ref = refs[:4]
        rest = refs[4:]
        a = ((yf_ref[...] + yb_ref[...]) * g_ref[...]).astype(BF16)
    else:
        bg_ref, m_ref, mp_ref, mn_ref, cw_ref, w_ref = refs[:6]
        rest = refs[6:]
        m = m_ref[...]
        pos, seq_len, r = _seq_pos(i, TM)
        z = (cw_ref[0:1, :] * _shifted(m, mp_ref, mn_ref, -1, pos, seq_len, r)
             + cw_ref[1:2, :] * m
             + cw_ref[2:3, :] * _shifted(m, mp_ref, mn_ref, 1, pos, seq_len, r))
        a = (bg_ref[...] * z).astype(BF16)
    out = _dot(a, w_ref[...])
    _epilogue(out, rest[0], rest[1], rest[2], rest[3:], has_next, want_router)


def _proj(mode, ins, w, x, gv, mv, has_next=True, router=None):
    want_router = router is not None
    k = w.shape[0]
    row = lambda i: (i, 0)
    if mode == "plain":
        in_specs = [pl.BlockSpec((TM, k), row)]
    elif mode == "rg":
        in_specs = [pl.BlockSpec((TM, D), row)] * 3
    else:
        bg, m, cw = ins
        ins = (bg, m, m, m, cw)
        in_specs = ([pl.BlockSpec((TM, D), row)] + _halo_specs(TM, lambda i: i)
                    + [pl.BlockSpec((8, D), lambda i: (0, 0))])
    in_specs.append(pl.BlockSpec((k, D), lambda i: (0, 0)))
    e_in, out_specs, out_shape = _epilogue_specs(has_next, want_router)
    args = list(ins) + [w, x, gv, mv] + (list(router) if want_router else [])
    return pl.pallas_call(
        functools.partial(_proj_kernel, mode=mode, has_next=has_next, want_router=want_router),
        grid=(N_T,),
        in_specs=in_specs + e_in,
        out_specs=out_specs,
        out_shape=out_shape,
        compiler_params=_cparams(("parallel",)),
        name="proj_" + mode,
    )(*args)


def _ffn_kernel(*refs, n_f, has_next):
    f = pl.program_id(1)
    h_ref, wg_ref, wu_ref, wd_ref = refs[:4]
    acc_ref = refs[-1]
    refs = refs[4:-1]

    @pl.when(f == 0)
    def _():
        acc_ref[...] = jnp.zeros(acc_ref.shape, F32)

    h = h_ref[...]
    a = _silu(_dot(h, wg_ref[...])) * _dot(h, wu_ref[...])
    acc_ref[...] += _dot(a.astype(BF16), wd_ref[...])

    @pl.when(f == n_f - 1)
    def _():
        _epilogue(acc_ref[...], refs[0], refs[1], refs[2], refs[3:], has_next, False)


def _ffn(h, wg, wu, wd, x, gv, mv, has_next=True):
    n_f, _, f = wg.shape
    in_specs = [pl.BlockSpec((TM, D), lambda i, e: (i, 0)),
                pl.BlockSpec((None, D, f), lambda i, e: (e, 0, 0)),
                pl.BlockSpec((None, D, f), lambda i, e: (e, 0, 0)),
                pl.BlockSpec((None, f, D), lambda i, e: (e, 0, 0))]
    e_in, out_specs, out_shape = _epilogue_specs(has_next, False)
    return pl.pallas_call(
        functools.partial(_ffn_kernel, n_f=n_f, has_next=has_next),
        grid=(N_T, n_f),
        in_specs=in_specs + e_in,
        out_specs=out_specs,
        out_shape=out_shape,
        scratch_shapes=[pltpu.VMEM((TM, D), F32)],
        compiler_params=_cparams(("parallel", "arbitrary")),
        name="ffn_dense",
    )(h, wg, wu, wd, x, gv, mv)


TE = 256
R_ROWS = 2 * T_ROWS + N_EXPERTS * TE
N_TE = R_ROWS // TE


def _route_meta(route):
    e1 = route[:, 0].astype(jnp.int32)
    e2 = route[:, 1].astype(jnp.int32)
    ids = jnp.arange(N_EXPERTS, dtype=jnp.int32)
    hit = ((e1[:, None] == ids) | (e2[:, None] == ids)).astype(jnp.int32)
    csum = jnp.cumsum(hit, axis=0)
    counts = csum[-1]
    padded = (counts + TE - 1) // TE * TE
    ends = jnp.cumsum(padded)
    offs = ends - padded
    rank = csum - 1
    pos1 = offs[e1] + jnp.take_along_axis(rank, e1[:, None], axis=1)[:, 0]
    pos2 = offs[e2] + jnp.take_along_axis(rank, e2[:, None], axis=1)[:, 0]
    n_used = ends[-1] // TE
    tile_row = jnp.minimum(jnp.arange(N_TE, dtype=jnp.int32), n_used - 1) * TE
    tile_e = jnp.minimum(jnp.searchsorted(ends, tile_row, side='right'), N_EXPERTS - 1).astype(jnp.int32)
    pos = jnp.concatenate([pos1, pos2]).astype(jnp.int32)
    pad = jnp.concatenate([offs + counts, padded - counts]).astype(jnp.int32)
    return pos, pad, tile_e, n_used.astype(jnp.int32).reshape(1)


def _row_copy(src, s, dst, d, sem):
    return pltpu.make_async_copy(src.at[pl.ds(s, 1), :], dst.at[pl.ds(d, 1), :], sem)


def _dispatch_kernel(pos_ref, pad_ref, h_ref, xs_ref, zero_scr, sem):
    i = pl.program_id(0)
    base = i * TM

    def issue(r, c):
        _row_copy(h_ref, r, xs_ref, pos_ref[base + r], sem).start()
        _row_copy(h_ref, r, xs_ref, pos_ref[T_ROWS + base + r], sem).start()
        return c

    lax.fori_loop(0, TM, issue, 0, unroll=8)
    for _ in range(2):
        pltpu.make_async_copy(h_ref, xs_ref.at[pl.ds(0, TM), :], sem).wait()

    @pl.when(i == N_T - 1)
    def _():
        zero_scr[...] = jnp.zeros(zero_scr.shape, F32)
        for e in range(N_EXPERTS):
            start, n = pad_ref[e], pad_ref[N_EXPERTS + e]

            def fill(r, c):
                _row_copy(zero_scr, 0, xs_ref, start + r, sem).start()
                return c

            def drain(r, c):
                _row_copy(zero_scr, 0, xs_ref, start, sem).wait()
                return c

            lax.fori_loop(0, n, fill, 0)
            lax.fori_loop(0, n, drain, 0)


def _dispatch(pos, pad, h32):
    return pl.pallas_call(
        _dispatch_kernel,
        grid_spec=pltpu.PrefetchScalarGridSpec(
            num_scalar_prefetch=2,
            grid=(N_T,),
            in_specs=[pl.BlockSpec((TM, D), lambda i, *_: (i, 0))],
            out_specs=pl.BlockSpec(memory_space=pl.ANY),
            scratch_shapes=[pltpu.VMEM((SUBLANES, D), F32), pltpu.SemaphoreType.DMA]),
        out_shape=jax.ShapeDtypeStruct((R_ROWS, D), F32),
        compiler_params=_cparams(("arbitrary",)),
        name="moe_dispatch",
    )(pos, pad, h32)


def _moe_ffn_kernel(te_ref, nu_ref, xs_ref, wg_ref, wu_ref, wd_ref, ys_ref):
    @pl.when(pl.program_id(0) < nu_ref[0])
    def _():
        x = xs_ref[...].astype(BF16)
        a = _silu(_dot(x, wg_ref[...])) * _dot(x, wu_ref[...])
        ys_ref[...] = _dot(a.astype(BF16), wd_ref[...])


def _moe_ffn(tile_e, n_used, xs, wg, wu, wd):
    f = wg.shape[2]
    row = lambda i, te, nu: (jnp.minimum(i, nu[0] - 1), 0)
    return pl.pallas_call(
        _moe_ffn_kernel,
        grid_spec=pltpu.PrefetchScalarGridSpec(
            num_scalar_prefetch=2,
            grid=(N_TE,),
            in_specs=[pl.BlockSpec((TE, D), row),
                      pl.BlockSpec((None, D, f), lambda i, te, nu: (te[i], 0, 0)),
                      pl.BlockSpec((None, D, f), lambda i, te, nu: (te[i], 0, 0)),
                      pl.BlockSpec((None, f, D), lambda i, te, nu: (te[i], 0, 0))],
            out_specs=pl.BlockSpec((TE, D), row)),
        out_shape=jax.ShapeDtypeStruct((R_ROWS, D), F32),
        compiler_params=_cparams(("arbitrary",)),
        name="moe_ffn",
    )(tile_e, n_used, xs, wg, wu, wd)


def _combine_kernel(*refs, has_next):
    pos_ref, ys_ref, route_ref = refs[:3]
    ybuf, sem = refs[-2:]
    refs = refs[3:-2]
    i = pl.program_id(0)
    slot = i % 2

    def gather(tile, s):
        base = tile * TM

        def issue(r, c):
            _row_copy(ys_ref, pos_ref[base + r], ybuf.at[s, 0], r, sem.at[s]).start()
            _row_copy(ys_ref, pos_ref[T_ROWS + base + r], ybuf.at[s, 1], r, sem.at[s]).start()
            return c

        lax.fori_loop(0, TM, issue, 0, unroll=8)

    @pl.when(i == 0)
    def _():
        gather(0, 0)

    @pl.when(i + 1 < N_T)
    def _():
        gather(i + 1, 1 - slot)

    for c in range(2):
        pltpu.make_async_copy(ys_ref.at[pl.ds(0, TM), :], ybuf.at[slot, c], sem.at[slot]).wait()
    route = route_ref[...]
    y = route[:, 2:3] * ybuf[slot, 0] + route[:, 3:4] * ybuf[slot, 1]
    _epilogue(y, refs[0], refs[1], refs[2], refs[3:], has_next, False)


def _combine(pos, ys, route, x, gv, mv, has_next=True):
    e_in, out_specs, out_shape = _epilogue_specs(has_next, False)
    return pl.pallas_call(
        functools.partial(_combine_kernel, has_next=has_next),
        grid_spec=pltpu.PrefetchScalarGridSpec(
            num_scalar_prefetch=1,
            grid=(N_T,),
            in_specs=[pl.BlockSpec(memory_space=pl.ANY),
                      pl.BlockSpec((TM, LANES), lambda i, *_: (i, 0))] + e_in,
            out_specs=out_specs,
            scratch_shapes=[pltpu.VMEM((2, 2, TM, D), F32), pltpu.SemaphoreType.DMA((2,))]),
        out_shape=out_shape,
        compiler_params=_cparams(("arbitrary",)),
        name="moe_combine",
    )(pos, ys, route, x, gv, mv)


def _mla_down_kernel(h_ref, w_ref, gq_ref, gkv_ref, rc_ref, rlo_ref, rhi_ref,
                     cq_ref, ckv_ref, kr_ref, kvb_ref):
    i = pl.program_id(0)
    y = _dot(h_ref[...], w_ref[...])
    cq_ref[...] = _rms(y[:, :Q_RANK], gq_ref[...]).astype(BF16)
    ckv = _rms(y[:, Q_RANK:Q_RANK + KV_RANK], gkv_ref[...])
    kr = y[:, Q_RANK + KV_RANK:]
    kr = jnp.where(i >= NP_T, _rope(kr, rc_ref[...], rlo_ref[...], rhi_ref[...]), kr)
    ckv_ref[...] = ckv
    kr_ref[...] = kr
    kvb_ref[:, :KV_RANK] = ckv.astype(BF16)
    kvb_ref[:, KV_RANK:] = kr.astype(BF16)


def _mla_down(h, w, gq, gkv, tabs):
    n = Q_RANK + KVW
    row = lambda i: (i, 0)
    const = lambda i: (0, 0)
    tab = pl.BlockSpec((TM, LANES), lambda i: (_pos_tile(i), 0))
    return pl.pallas_call(
        _mla_down_kernel,
        grid=(N_T,),
        in_specs=[pl.BlockSpec((TM, D), row), pl.BlockSpec((D, n), const),
                  pl.BlockSpec((1, Q_RANK), const), pl.BlockSpec((1, KV_RANK), const),
                  tab, tab, tab],
        out_specs=[pl.BlockSpec((TM, Q_RANK), row), pl.BlockSpec((TM, KV_RANK), row),
                   pl.BlockSpec((TM, LANES), row), pl.BlockSpec((TM, KVW), row)],
        out_shape=[jax.ShapeDtypeStruct((T_ROWS, Q_RANK), BF16),
                   jax.ShapeDtypeStruct((T_ROWS, KV_RANK), F32),
                   jax.ShapeDtypeStruct((T_ROWS, LANES), F32),
                   jax.ShapeDtypeStruct((T_ROWS, KVW), BF16)],
        compiler_params=_cparams(("parallel",)),
        name="mla_down",
    )(h, w, gq, gkv, *tabs)


def _dot_nt(a, b):
    return lax.dot_general(a, b, (((1,), (1,)), ((), ())), preferred_element_type=F32)


def _mla_q_kernel(cq_ref, w_ref, cr_ref, sr_ref, cc_ref, sc_ref, q_ref):
    i = pl.program_id(0)
    y = _dot_nt(w_ref[...], cq_ref[...])
    is_latent = i >= NP_T
    cr = jnp.where(is_latent, cr_ref[...], 1.0)
    sr = jnp.where(is_latent, sr_ref[...], 0.0)
    cc = jnp.where(is_latent, cc_ref[...], 1.0)
    sc = jnp.where(is_latent, sc_ref[...], 0.0)
    for hd in range(N_HEADS):
        r0 = hd * HEAD_PAD
        x0, x1, x2, x3 = (y[r0 + 8 * a:r0 + 8 * (a + 1), :] for a in range(4))
        rot = jnp.concatenate([x0 * cr - x1 * sr, x0 * sr + x1 * cr,
                               x2 * cc - x3 * sc, x2 * sc + x3 * cc], axis=0)
        q_ref[r0:r0 + ROPE_DIM, :] = rot.astype(BF16)
        q_ref[r0 + ROPE_DIM:r0 + HEAD_PAD, :] = y[r0 + ROPE_DIM:r0 + HEAD_PAD, :].astype(BF16)


def _mla_q(cq, w_t, tabs_t):
    tab = pl.BlockSpec((8, TM), lambda i: (0, _pos_tile(i)))
    return pl.pallas_call(
        _mla_q_kernel,
        grid=(N_T,),
        in_specs=[pl.BlockSpec((TM, Q_RANK), lambda i: (i, 0)),
                  pl.BlockSpec((HP, Q_RANK), lambda i: (0, 0)), tab, tab, tab, tab],
        out_specs=pl.BlockSpec((HP, TM), lambda i: (0, i)),
        out_shape=jax.ShapeDtypeStruct((HP, T_ROWS), BF16),
        compiler_params=_cparams(("parallel",)),
        name="mla_q",
    )(cq, w_t, *tabs_t)


def _mla_kv_kernel(c_ref, wk_ref, wvt_ref, k_ref, vt_ref):
    c = c_ref[...]
    k_ref[...] = _dot(c, wk_ref[...]).astype(BF16)
    vt_ref[...] = _dot_nt(wvt_ref[...], c).astype(BF16)


def _mla_kv(ckvkr, w_k, w_vt):
    rows = ckvkr.shape[0]
    hv = N_HEADS * V_DIM
    return pl.pallas_call(
        _mla_kv_kernel,
        grid=(rows // TM,),
        in_specs=[pl.BlockSpec((TM, KVW), lambda i: (i, 0)),
                  pl.BlockSpec((KVW, HP), lambda i: (0, 0)),
                  pl.BlockSpec((hv, KVW), lambda i: (0, 0))],
        out_specs=[pl.BlockSpec((TM, HP), lambda i: (i, 0)), pl.BlockSpec((hv, TM), lambda i: (0, i))],
        out_shape=[jax.ShapeDtypeStruct((rows, HP), BF16), jax.ShapeDtypeStruct((hv, rows), BF16)],
        compiler_params=_cparams(("parallel",)),
        name="mla_kv",
    )(ckvkr, w_k, w_vt)


ATTN_C2 = ATTN_SCALE * float(np.log2(np.e))


def _attn_kernel(*refs, nk, aliased):
    if aliased:
        refs = refs[1:]
    qt_ref, k_ref, vt_ref, o_ref, m_scr, l_scr, acc_scr = refs
    ki = pl.program_id(2)

    @pl.when(ki == 0)
    def _():
        m_scr[...] = jnp.full(m_scr.shape, -jnp.inf, F32)
        l_scr[...] = jnp.zeros(l_scr.shape, F32)
        acc_scr[...] = jnp.zeros(acc_scr.shape, F32)

    for hd in range(N_HEADS):
        qs = slice(hd * HEAD_PAD, (hd + 1) * HEAD_PAD)
        vs = slice(hd * V_DIM, (hd + 1) * V_DIM)
        st = slice(hd, hd + 1)
        s = _dot(k_ref[:, qs], qt_ref[qs, :]) * ATTN_C2
        m_prev = m_scr[st, :]
        m_new = jnp.maximum(m_prev, jnp.max(s, axis=0, keepdims=True))
        alpha = jnp.exp2(m_prev - m_new)
        p = jnp.exp2(s - m_new)
        l_scr[st, :] = alpha * l_scr[st, :] + jnp.sum(p, axis=0, keepdims=True)
        acc_scr[vs, :] = alpha * acc_scr[vs, :] + _dot(vt_ref[vs, :], p.astype(BF16))
        m_scr[st, :] = m_new

    @pl.when(ki == nk - 1)
    def _():
        for hd in range(N_HEADS):
            vs = slice(hd * V_DIM, (hd + 1) * V_DIM)
            acc_scr[vs, :] = acc_scr[vs, :] / l_scr[hd:hd + 1, :]
        o_ref[...] = acc_scr[...].T.astype(BF16)


def _attention(qt, k, vt, n_b, tq, tk, lq, lk, q_row0, prev_out=None):
    nq, nk = lq // tq, lk // tk
    qb0 = q_row0 // tq
    hv = N_HEADS * V_DIM
    aliased = prev_out is not None
    in_specs = [pl.BlockSpec((HP, tq), lambda b, qi, ki: (0, qb0 + b * nq + qi)),
                pl.BlockSpec((tk, HP), lambda b, qi, ki: (b * nk + ki, 0)),
                pl.BlockSpec((hv, tk), lambda b, qi, ki: (0, b * nk + ki))]
    args = [qt, k, vt]
    if aliased:
        in_specs = [pl.BlockSpec(memory_space=pl.ANY)] + in_specs
        args = [prev_out] + args
    return pl.pallas_call(
        functools.partial(_attn_kernel, nk=nk, aliased=aliased),
        grid=(n_b, nq, nk),
        in_specs=in_specs,
        out_specs=pl.BlockSpec((tq, hv), lambda b, qi, ki: (qb0 + b * nq + qi, 0)),
        out_shape=jax.ShapeDtypeStruct((T_ROWS, hv), BF16),
        scratch_shapes=[pltpu.VMEM((N_HEADS, tq), F32), pltpu.VMEM((N_HEADS, tq), F32),
                        pltpu.VMEM((hv, tq), F32)],
        input_output_aliases={0: 0} if aliased else {},
        compiler_params=_cparams(("parallel", "parallel", "arbitrary")),
        name="attn_latent" if aliased else "attn_context",
    )(*args)


def _rg_in_kernel(h_ref, w_ref, u_ref, g_ref):
    y = _dot(h_ref[...], w_ref[...])
    u_ref[...] = y[:, :D]
    g_ref[...] = _gelu_tanh(y[:, D:])


def _rg_in(h, w):
    return pl.pallas_call(
        _rg_in_kernel,
        grid=(N_T,),
        in_specs=[pl.BlockSpec((TM, D), lambda i: (i, 0)), pl.BlockSpec((D, 2 * D), lambda i: (0, 0))],
        out_specs=[pl.BlockSpec((TM, D), lambda i: (i, 0))] * 2,
        out_shape=[jax.ShapeDtypeStruct((T_ROWS, D), F32)] * 2,
        compiler_params=_cparams(("parallel",)),
        name="rg_in",
    )(h, w)


RG_TC = 256
RG_GROUP = 256


def _rg_scan_kernel(*refs, nj, seq_len, aliased):
    if aliased:
        refs = refs[2:]
    (uf_ref, ufp_ref, ufn_ref, ub_ref, ubp_ref, ubn_ref, cw_ref, cb_ref, wai_ref, bai_ref,
     lam_ref, h0_ref, yf_ref, yb_ref, fin_ref, carry_ref) = refs
    j = pl.program_id(1)
    tc = RG_TC
    r = lax.broadcasted_iota(jnp.int32, (tc, 1), 0)
    sub = r & (SUBLANES - 1)

    @pl.when(j == 0)
    def _():
        carry_ref[0:2, :] = h0_ref[...]

    def gates(m_ref, p_ref, n_ref, d, chunk):
        m = m_ref[...]
        pos = chunk * tc + r
        u = (cw_ref[0:1, :] * _shifted(m, p_ref, n_ref, -1, pos, seq_len, r)
             + cw_ref[1:2, :] * m
             + cw_ref[2:3, :] * _shifted(m, p_ref, n_ref, 1, pos, seq_len, r)
             + cw_ref[3:4, :] * _shifted(m, p_ref, n_ref, 2, pos, seq_len, r)
             + cb_ref[...])
        ub = u.astype(BF16)
        ra, ri = [], []
        for q in range(D // RG_GROUP):
            y = _dot(ub[:, q * RG_GROUP:(q + 1) * RG_GROUP], wai_ref[d, q])
            ra.append(y[:, :RG_GROUP])
            ri.append(y[:, RG_GROUP:])
        rr = jax.nn.sigmoid(jnp.concatenate(ra, axis=1) + bai_ref[d, 0:1, :])
        ii = jax.nn.sigmoid(jnp.concatenate(ri, axis=1) + bai_ref[d, 1:2, :])
        nl = -lam_ref[d:d + 1, :]
        softplus = jnp.maximum(nl, 0.0) + jnp.log1p(jnp.exp(-jnp.abs(nl)))
        log_a = (-RG_C * softplus) * rr
        a = jnp.exp(log_a)
        bx = jnp.sqrt(jnp.maximum(-jnp.tanh(log_a) * (a * a + 1.0), 0.0)) * (ii * u)
        return a, bx

    a, b = gates(uf_ref, ufp_ref, ufn_ref, 0, j)
    for k in (1, 2, 4):
        ok = sub >= k
        b = jnp.where(ok, a * pltpu.roll(b, k, 0) + b, b)
        a = jnp.where(ok, a * pltpu.roll(a, k, 0), a)
    h = carry_ref[0:1, :]
    for g in range(tc // SUBLANES):
        sl = slice(g * SUBLANES, (g + 1) * SUBLANES)
        hg = a[sl] * h + b[sl]
        yf_ref[sl, :] = hg
        h = hg[SUBLANES - 1:SUBLANES, :]
    carry_ref[0:1, :] = h

    a, b = gates(ub_ref, ubp_ref, ubn_ref, 1, nj - 1 - j)
    for k in (1, 2, 4):
        ok = sub < SUBLANES - k
        b = jnp.where(ok, a * pltpu.roll(b, tc - k, 0) + b, b)
        a = jnp.where(ok, a * pltpu.roll(a, tc - k, 0), a)
    h = carry_ref[1:2, :]
    for g in reversed(range(tc // SUBLANES)):
        sl = slice(g * SUBLANES, (g + 1) * SUBLANES)
        hg = a[sl] * h + b[sl]
        yb_ref[sl, :] = hg
        h = hg[0:1, :]
    carry_ref[1:2, :] = h

    @pl.when(j == nj - 1)
    def _():
        fin_ref[...] = carry_ref[0:2, :]


def _rg_scan(u, cw, cb, wai, bai, lam, h0, n_seq, seq_len, row0, prev=None):
    nj = seq_len // RG_TC
    b0 = row0 // RG_TC
    aliased = prev is not None
    fwd = lambda s, j: b0 + s * nj + j
    bwd = lambda s, j: b0 + s * nj + (nj - 1 - j)
    const2 = lambda s, j: (0, 0)
    in_specs = (_halo_specs(RG_TC, fwd) + _halo_specs(RG_TC, bwd)
                + [pl.BlockSpec((8, D), const2), pl.BlockSpec((1, D), const2),
                   pl.BlockSpec((2, D // RG_GROUP, RG_GROUP, 2 * RG_GROUP), lambda s, j: (0, 0, 0, 0)),
                   pl.BlockSpec((2, 2, D), lambda s, j: (0, 0, 0)),
                   pl.BlockSpec((2, D), const2),
                   pl.BlockSpec((None, 2, D), lambda s, j: (s, 0, 0))])
    args = [u, u, u, u, u, u, cw, cb, wai, bai, lam, h0]
    if aliased:
        in_specs = [pl.BlockSpec(memory_space=pl.ANY)] * 2 + in_specs
        args = list(prev) + args
    return pl.pallas_call(
        functools.partial(_rg_scan_kernel, nj=nj, seq_len=seq_len, aliased=aliased),
        grid=(n_seq, nj),
        in_specs=in_specs,
        out_specs=[pl.BlockSpec((RG_TC, D), lambda s, j: (fwd(s, j), 0)),
                   pl.BlockSpec((RG_TC, D), lambda s, j: (bwd(s, j), 0)),
                   pl.BlockSpec((None, 2, D), lambda s, j: (s, 0, 0))],
        out_shape=[jax.ShapeDtypeStruct((T_ROWS, D), F32), jax.ShapeDtypeStruct((T_ROWS, D), F32),
                   jax.ShapeDtypeStruct((n_seq, 2, D), F32)],
        scratch_shapes=[pltpu.VMEM((8, D), F32)],
        input_output_aliases={0: 0, 1: 1} if aliased else {},
        compiler_params=_cparams(("parallel", "arbitrary")),
        name="rg_scan_latent" if aliased else "rg_scan_context",
    )(*args)


def _sc_in_kernel(h_ref, w_ref, bg_ref, m_ref):
    y = _dot(h_ref[...], w_ref[...])
    bg_ref[...] = y[:, :D]
    m_ref[...] = y[:, D:2 * D] * y[:, 2 * D:]


def _sc_in(h, w):
    return pl.pallas_call(
        _sc_in_kernel,
        grid=(N_T,),
        in_specs=[pl.BlockSpec((TM, D), lambda i: (i, 0)), pl.BlockSpec((D, 3 * D), lambda i: (0, 0))],
        out_specs=[pl.BlockSpec((TM, D), lambda i: (i, 0))] * 2,
        out_shape=[jax.ShapeDtypeStruct((T_ROWS, D), F32)] * 2,
        compiler_params=_cparams(("parallel",)),
        name="sc_in",
    )(h, w)


def _rope_tables():
    t = np.arange(DEC_SEQ)
    inv = ROPE_THETA ** (-jnp.arange(0, AXIS_DIM, 2, dtype=F32) / AXIS_DIM)
    ang_r = jnp.asarray((t // GRID_W).astype(np.float32))[:, None] * inv
    ang_c = jnp.asarray((t % GRID_W).astype(np.float32))[:, None] * inv
    cr, sr, cc, sc = jnp.cos(ang_r), jnp.sin(ang_r), jnp.cos(ang_c), jnp.sin(ang_c)
    z8 = jnp.zeros((DEC_SEQ, 8), F32)
    pad1 = jnp.ones((DEC_SEQ, LANES - ROPE_DIM), F32)
    pad0 = jnp.zeros((DEC_SEQ, LANES - ROPE_DIM), F32)
    c = jnp.concatenate([cr, cr, cc, cc, pad1], axis=1)
    lo = jnp.concatenate([-sr, z8, -sc, z8, pad0], axis=1)
    hi = jnp.concatenate([z8, sr, z8, sc, pad0], axis=1)
    return (c, lo, hi), (cr.T, sr.T, cc.T, sc.T)


def _pad_heads(w, lo):
    r, h, d = w.shape
    return jnp.pad(w, ((0, 0), (0, 0), (lo, HEAD_PAD - lo - d))).reshape(r, h * HEAD_PAD)


def _mla_weights(w_dq, w_uq, w_dkv, w_uk, w_uv, w_o):
    w_down = jnp.concatenate([w_dq, w_dkv, jnp.zeros((D, KVW - KV_RANK - ROPE_DIM), F32)], axis=1)
    w_q = _pad_heads(w_uq[:, :, QK_NOPE:], 0) + _pad_heads(w_uq[:, :, :QK_NOPE], HEAD_PAD - QK_NOPE)
    place = jnp.broadcast_to(jnp.eye(ROPE_DIM, dtype=F32)[:, None, :], (ROPE_DIM, N_HEADS, ROPE_DIM))
    w_k = jnp.concatenate([_pad_heads(w_uk, HEAD_PAD - QK_NOPE), _pad_heads(place, 0),
                           jnp.zeros((KVW - KV_RANK - ROPE_DIM, HP), F32)], axis=0)
    w_vt = jnp.pad(w_uv.reshape(KV_RANK, N_HEADS * V_DIM).T, ((0, 0), (0, KVW - KV_RANK)))
    return (w_down.astype(BF16), w_q.T.astype(BF16), w_k.astype(BF16), w_vt.astype(BF16),
            w_o.astype(BF16))


def _block_diag_groups(w_a, w_i):
    per = RG_GROUP // RG_BLOCK

    def bd(w):
        w = w.reshape(2, D // RG_GROUP, per, RG_BLOCK, RG_BLOCK)
        eye = jnp.eye(per, dtype=F32)
        return jnp.einsum('dgpkj,pq->dgpkqj', w, eye).reshape(2, D // RG_GROUP, RG_GROUP, RG_GROUP)

    return jnp.concatenate([bd(w_a), bd(w_i)], axis=-1).astype(BF16)


def _pad_rows(w, rows=8):
    return jnp.pad(w, ((0, rows - w.shape[0]), (0, 0)))


def kernel(x_prompt, x_sample, cache_mla_ckv, cache_mla_krope, state_rglru, c, c_ctx, mod_w, mod_b, norm_g, mla_w_dq, mla_g_q, mla_w_uq, mla_w_dkv, mla_g_kv, mla_w_uk, mla_w_uv, mla_w_o, rg_w_x, rg_w_y, rg_conv_w, rg_conv_b, rg_w_a, rg_b_a, rg_w_i, rg_b_i, rg_lambda, rg_w_out, sc_w_in, sc_conv_w, sc_w_out, ffn_w_gate, ffn_w_up, ffn_w_down, moe_w_router, moe_b_router, moe_w_gate, moe_w_up, moe_w_down):
    x = jnp.concatenate([x_prompt.reshape(P_ROWS, D), x_sample.reshape(S_ROWS, D)], axis=0)
    cond8 = jnp.concatenate([c_ctx[None], c, jnp.zeros((N_GROUPS - 1 - DEC_BATCH, D), F32)], axis=0)
    mod = _modulation(cond8, mod_w, mod_b).reshape(DEPTH, N_GROUPS, 6, D)
    tabs, tabs_t = _rope_tables()
    zrow = jnp.zeros((N_GROUPS, 5, D), F32)

    def vecs(l, sub):
        if sub == 0:
            g_next, sh, sc = norm_g[l, 2], mod[l, :, 3], mod[l, :, 4]
        elif l + 1 < DEPTH:
            g_next, sh, sc = norm_g[l + 1, 0], mod[l + 1, :, 0], mod[l + 1, :, 1]
        else:
            g_next, sh, sc = jnp.zeros((D,), F32), zrow[:, 0], zrow[:, 0]
        gv = _pad_rows(jnp.stack([norm_g[l, 1 + 2 * sub], g_next]))
        mv = jnp.concatenate([jnp.stack([mod[l, :, 2 + 3 * sub], sh, sc], axis=1), zrow], axis=1)
        return gv, mv

    gv0 = _pad_rows(jnp.stack([jnp.zeros((D,), F32), norm_g[0, 0]]))
    mv0 = jnp.concatenate([jnp.stack([zrow[:, 0], mod[0, :, 0], mod[0, :, 1]], axis=1), zrow], axis=1)
    h = _pre0(x, gv0, mv0)

    new_ckv, new_krope, new_rg = [], [], []
    for l in range(DEPTH):
        kind, j = l % 3, l // 3
        gv, mv = vecs(l, 0)
        router = None
        if l % 2 == 1:
            m = l // 2
            router = (jnp.pad(moe_w_router[m], ((0, 0), (0, LANES - N_EXPERTS))),
                      jnp.pad(moe_b_router[m], (0, LANES - N_EXPERTS)).reshape(1, LANES))
        if kind == 0:
            w_down, w_qt, w_k, w_vt, w_op = _mla_weights(mla_w_dq[j], mla_w_uq[j], mla_w_dkv[j],
                                                         mla_w_uk[j], mla_w_uv[j], mla_w_o[j])
            cq, ckv, kr, kvb = _mla_down(h, w_down, mla_g_q[j].reshape(1, Q_RANK),
                                         mla_g_kv[j].reshape(1, KV_RANK), tabs)
            q = _mla_q(cq, w_qt, tabs_t)
            new_ckv.append(ckv[:P_ROWS].reshape(BATCH, SEQ, KV_RANK))
            new_krope.append(kr[:P_ROWS, :ROPE_DIM].reshape(BATCH, SEQ, ROPE_DIM))
            cache = jnp.concatenate(
                [cache_mla_ckv[:, j], cache_mla_krope[:, j],
                 jnp.zeros((DEC_BATCH, PAST, KVW - KV_RANK - ROPE_DIM), F32)], axis=-1).astype(BF16)
            kv_lat = jnp.concatenate([cache, kvb[P_ROWS:].reshape(DEC_BATCH, DEC_SEQ, KVW)], axis=1)
            k_c, v_c = _mla_kv(kvb[:P_ROWS], w_k, w_vt)
            k_l, v_l = _mla_kv(kv_lat.reshape(DEC_BATCH * (PAST + DEC_SEQ), KVW), w_k, w_vt)
            o = _attention(q, k_c, v_c, BATCH, SEQ, SEQ, SEQ, SEQ, 0)
            o = _attention(q, k_l, v_l, DEC_BATCH, 512, 512, DEC_SEQ, PAST + DEC_SEQ, P_ROWS, prev_out=o)
            outs = _proj("plain", (o,), w_op, x, gv, mv, router=router)
        elif kind == 1:
            w_xy = jnp.concatenate([rg_w_x[j], rg_w_y[j]], axis=1).astype(BF16)
            u, gate = _rg_in(h, w_xy)
            wai = _block_diag_groups(rg_w_a[j], rg_w_i[j])
            bai = jnp.stack([rg_b_a[j], rg_b_i[j]], axis=1)
            scan_args = (_pad_rows(rg_conv_w[j]), rg_conv_b[j].reshape(1, D), wai, bai, rg_lambda[j])
            yf, yb, fin = _rg_scan(u, *scan_args, jnp.zeros((BATCH, 2, D), F32), BATCH, SEQ, 0)
            yf, yb, _ = _rg_scan(u, *scan_args, state_rglru[:, j], DEC_BATCH, DEC_SEQ, P_ROWS,
                                 prev=(yf, yb))
            new_rg.append(fin)
            outs = _proj("rg", (yf, yb, gate), rg_w_out[j].astype(BF16), x, gv, mv, router=router)
        else:
            bg, mm = _sc_in(h, sc_w_in[j].astype(BF16))
            outs = _proj("sc", (bg, mm, _pad_rows(sc_conv_w[j])), sc_w_out[j].astype(BF16), x, gv, mv,
                         router=router)
        x, h = outs[0], outs[1]

        gv, mv = vecs(l, 1)
        has_next = l + 1 < DEPTH
        m = l // 2
        if l % 2 == 0:
            half = D_FF // 2
            wg = ffn_w_gate[m].reshape(D, 2, half).transpose(1, 0, 2).astype(BF16)
            wu = ffn_w_up[m].reshape(D, 2, half).transpose(1, 0, 2).astype(BF16)
            wd = ffn_w_down[m].reshape(2, half, D).astype(BF16)
            outs = _ffn(h, wg, wu, wd, x, gv, mv, has_next=has_next)
        else:
            route = outs[2]
            pos, pad, tile_e, n_used = _route_meta(route)
            xs = _dispatch(pos, pad, h)
            ys = _moe_ffn(tile_e, n_used, xs, moe_w_gate[m].astype(BF16), moe_w_up[m].astype(BF16),
                          moe_w_down[m].astype(BF16))
            outs = _combine(pos, ys, route, x, gv, mv, has_next=has_next)
        x = outs[0]
        h = outs[1] if has_next else None

    y_prompt = x[:P_ROWS].reshape(BATCH, SEQ, D)
    y_sample = x[P_ROWS:].reshape(DEC_BATCH, DEC_SEQ, D)
    return (y_prompt, y_sample, jnp.stack(new_ckv, axis=1), jnp.stack(new_krope, axis=1),
            jnp.stack(new_rg, axis=1))
```

```python
import functools

import numpy as np
import jax
import jax.numpy as jnp
from jax import lax
from jax.experimental import pallas as pl
from jax.experimental.pallas import tpu as pltpu

F32 = jnp.float32
BF16 = jnp.bfloat16

D = 1024
BATCH = 32
SEQ = 256
DEPTH = 4
DEC_BATCH = 4
DEC_SEQ = 4096
PAST = 512
GRID_W = 64
N_HEADS = 16
QK_NOPE = 64
ROPE_DIM = 32
AXIS_DIM = 16
V_DIM = 64
Q_RANK = 384
KV_RANK = 256
ROPE_THETA = 10000.0
ATTN_SCALE = (QK_NOPE + ROPE_DIM) ** -0.5
RG_BLOCKS = 16
RG_BLOCK = 64
RG_C = 8.0
D_FF = 2816
N_EXPERTS = 8
D_FF_EXPERT = 1408
EPS = 1e-6

P_ROWS = BATCH * SEQ
S_ROWS = DEC_BATCH * DEC_SEQ
T_ROWS = P_ROWS + S_ROWS
N_GROUPS = 8
HEAD_PAD = 128
HP = N_HEADS * HEAD_PAD
KVW = 384
ONE_COL = KV_RANK + ROPE_DIM
V_EXT = 80
HVX = N_HEADS * V_EXT
ATTN_C2 = ATTN_SCALE * float(np.log2(np.e))

TM = 512
NP_T = P_ROWS // TM
NS_T = DEC_SEQ // TM
N_T = T_ROWS // TM
SUBLANES = 8
LANES = 128
VMEM_LIMIT = 56 * 1024 * 1024


def _cparams(sem):
    return pltpu.CompilerParams(dimension_semantics=sem, vmem_limit_bytes=VMEM_LIMIT)


def _group_of_tile(i):
    return jnp.maximum(i - NP_T + NS_T, 0) // NS_T


def _pos_tile(i):
    return jnp.maximum(i - NP_T, 0) % NS_T


def _dot(a, b):
    return jnp.dot(a, b, preferred_element_type=F32)


def _rms(x, g):
    ms = jnp.mean(x * x, axis=-1, keepdims=True)
    return x * lax.rsqrt(ms + EPS) * g


def _silu(x):
    return x * jax.nn.sigmoid(x)


def _gelu_tanh(x):
    return x * (0.5 * (1.0 + jnp.tanh(np.sqrt(2.0 / np.pi).astype(np.float32)
                                      * (x + 0.044715 * (x * x * x)))))


def _rope(x, c, s_lo, s_hi):
    return (x * c + pltpu.roll(x, LANES - 8, 1) * s_lo + pltpu.roll(x, 8, 1) * s_hi)


def _mod_kernel(c_ref, w_ref, b_ref, o_ref):
    c = c_ref[...]
    s = _silu(c).astype(BF16)
    o_ref[...] = _dot(s, w_ref[...].astype(BF16)) + b_ref[...]


def _modulation(cond8, mod_w, mod_b):
    tn = 1536
    return pl.pallas_call(
        _mod_kernel,
        grid=(DEPTH, 6 * D // tn),
        in_specs=[pl.BlockSpec((N_GROUPS, D), lambda l, n: (0, 0)),
                  pl.BlockSpec((None, D, tn), lambda l, n: (l, 0, n)),
                  pl.BlockSpec((None, 1, tn), lambda l, n: (l, 0, n))],
        out_specs=pl.BlockSpec((None, N_GROUPS, tn), lambda l, n: (l, 0, n)),
        out_shape=jax.ShapeDtypeStruct((DEPTH, N_GROUPS, 6 * D), F32),
        compiler_params=_cparams(("parallel", "parallel")),
        name="modulation",
    )(cond8, mod_w, mod_b.reshape(DEPTH, 1, 6 * D))


def _pre0_kernel(x_ref, gv_ref, mv_ref, h_ref):
    h = _rms(x_ref[...], gv_ref[1:2, :]) * (1.0 + mv_ref[2:3, :]) + mv_ref[1:2, :]
    h_ref[...] = h.astype(BF16)


def _pre0(x, gv, mv):
    return pl.pallas_call(
        _pre0_kernel,
        grid=(N_T,),
        in_specs=[pl.BlockSpec((TM, D), lambda i: (i, 0)),
                  pl.BlockSpec((8, D), lambda i: (0, 0)),
                  pl.BlockSpec((None, 8, D), lambda i: (_group_of_tile(i), 0, 0))],
        out_specs=pl.BlockSpec((TM, D), lambda i: (i, 0)),
        out_shape=jax.ShapeDtypeStruct((T_ROWS, D), BF16),
        compiler_params=_cparams(("parallel",)),
        name="pre0",
    )(x, gv, mv)


def _router(hn, wr_ref, br_ref):
    logits = jnp.dot(hn, wr_ref[...], preferred_element_type=F32,
                     precision=lax.Precision.HIGHEST) + br_ref[...]
    lane = lax.broadcasted_iota(jnp.int32, logits.shape, 1)
    neg = jnp.float32(-jnp.inf)
    logits = jnp.where(lane < N_EXPERTS, logits, neg)
    m1 = jnp.max(logits, axis=-1, keepdims=True)
    i1 = jnp.min(jnp.where(logits == m1, lane, LANES), axis=-1, keepdims=True)
    rest = jnp.where(lane == i1, neg, logits)
    m2 = jnp.max(rest, axis=-1, keepdims=True)
    i2 = jnp.min(jnp.where(rest == m2, lane, LANES), axis=-1, keepdims=True)
    e = jnp.exp(m2 - m1)
    p1 = 1.0 / (1.0 + e)
    p2 = e / (1.0 + e)
    return jnp.where(lane == 0, i1.astype(F32),
                     jnp.where(lane == 1, i2.astype(F32),
                               jnp.where(lane == 2, p1, jnp.where(lane == 3, p2, 0.0))))


def _post_pre(x, out, gv_ref, mv_ref, has_next):
    xn = x + mv_ref[0:1, :] * _rms(out, gv_ref[0:1, :])
    if not has_next:
        return xn, None
    hn = _rms(xn, gv_ref[1:2, :]) * (1.0 + mv_ref[2:3, :]) + mv_ref[1:2, :]
    return xn, hn


def _epilogue(out, x_ref, gv_ref, mv_ref, rest, has_next, want_router):
    rest = list(rest)
    if want_router:
        wr_ref, br_ref = rest[0], rest[1]
        rest = rest[2:]
    xn, hn = _post_pre(x_ref[...], out, gv_ref, mv_ref, has_next)
    rest[0][...] = xn
    if has_next:
        rest[1][...] = hn.astype(rest[1].dtype)
    if want_router:
        rest[2][...] = _router(hn, wr_ref, br_ref)


def _epilogue_specs(has_next, want_router):
    in_specs = [pl.BlockSpec((TM, D), lambda i, *_: (i, 0)),
                pl.BlockSpec((8, D), lambda i, *_: (0, 0)),
                pl.BlockSpec((None, 8, D), lambda i, *_: (_group_of_tile(i), 0, 0))]
    out_specs = [pl.BlockSpec((TM, D), lambda i, *_: (i, 0))]
    out_shape = [jax.ShapeDtypeStruct((T_ROWS, D), F32)]
    if want_router:
        in_specs += [pl.BlockSpec((D, LANES), lambda i, *_: (0, 0)),
                     pl.BlockSpec((1, LANES), lambda i, *_: (0, 0))]
    if has_next:
        out_specs.append(pl.BlockSpec((TM, D), lambda i, *_: (i, 0)))
        out_shape.append(jax.ShapeDtypeStruct((T_ROWS, D), F32 if want_router else BF16))
    if want_router:
        out_specs.append(pl.BlockSpec((TM, LANES), lambda i, *_: (i, 0)))
        out_shape.append(jax.ShapeDtypeStruct((T_ROWS, LANES), F32))
    return in_specs, out_specs, out_shape


def _seq_pos(i, rows):
    seq_len = jnp.where(i < NP_T, SEQ, DEC_SEQ)
    r = lax.broadcasted_iota(jnp.int32, (rows, 1), 0)
    return (i * rows + r) & (seq_len - 1), seq_len, r


def _shifted(m, prev_ref, next_ref, k, pos, seq_len, r):
    rows = m.shape[0]
    if k < 0:
        y = pltpu.roll(m, -k, 0)
        y = jnp.where(r == 0, prev_ref[SUBLANES - 1:SUBLANES, :], y)
        return jnp.where(pos + k < 0, 0.0, y)
    y = pltpu.roll(m, rows - k, 0)
    for q in range(k):
        y = jnp.where(r == rows - k + q, next_ref[q:q + 1, :], y)
    return jnp.where(pos + k >= seq_len, 0.0, y)


def _halo_specs(rows, row_block_of):
    per = rows // SUBLANES
    last = T_ROWS // SUBLANES - 1
    return [pl.BlockSpec((rows, D), lambda *g: (row_block_of(*g), 0)),
            pl.BlockSpec((SUBLANES, D), lambda *g: (jnp.maximum(row_block_of(*g) * per - 1, 0), 0)),
            pl.BlockSpec((SUBLANES, D), lambda *g: (jnp.minimum((row_block_of(*g) + 1) * per, last), 0))]


def _proj_kernel(*refs, mode, has_next, want_router):
    i = pl.program_id(0)
    if mode == "plain":
        a_ref, w_ref = refs[:2]
        rest = refs[2:]
        a = a_ref[...]
    elif mode == "rg":
        yf_ref, yb_ref, g_ref, w_ref = refs[:4]
        rest = refs[4:]
        a = ((yf_ref[...] + yb_ref[...]) * g_ref[...]).astype(BF16)
    else:
        bg_ref, m_ref, mp_ref, mn_ref, cw_ref, w_ref = refs[:6]
        rest = refs[6:]
        m = m_ref[...]
        pos, seq_len, r = _seq_pos(i, TM)
        z = (cw_ref[0:1, :] * _shifted(m, mp_ref, mn_ref, -1, pos, seq_len, r)
             + cw_ref[1:2, :] * m
             + cw_ref[2:3, :] * _shifted(m, mp_ref, mn_ref, 1, pos, seq_len, r))
        a = (bg_ref[...] * z).astype(BF16)
    out = _dot(a, w_ref[...])
    _epilogue(out, rest[0], rest[1], rest[2], rest[3:], has_next, want_router)


def _proj(mode, ins, w, x, gv, mv, has_next=True, router=None):
    want_router = router is not None
    k = w.shape[0]
    row = lambda i: (i, 0)
    if mode == "plain":
        in_specs = [pl.BlockSpec((TM, k), row)]
    elif mode == "rg":
        in_specs = [pl.BlockSpec((TM, D), row)] * 3
    else:
        bg, m, cw = ins
        ins = (bg, m, m, m, cw)
        in_specs = ([pl.BlockSpec((TM, D), row)] + _halo_specs(TM, lambda i: i)
                    + [pl.BlockSpec((8, D), lambda i: (0, 0))])
    in_specs.append(pl.BlockSpec((k, D), lambda i: (0, 0)))
    e_in, out_specs, out_shape = _epilogue_specs(has_next, want_router)
    args = list(ins) + [w, x, gv, mv] + (list(router) if want_router else [])
    return pl.pallas_call(
        functools.partial(_proj_kernel, mode=mode, has_next=has_next, want_router=want_router),
        grid=(N_T,),
        in_specs=in_specs + e_in,
        out_specs=out_specs,
        out_shape=out_shape,
        compiler_params=_cparams(("parallel",)),
        name="proj_" + mode,
    )(*args)


def _ffn_kernel(*refs, n_f, has_next):
    f = pl.program_id(1)
    h_ref, wg_ref, wu_ref, wd_ref = refs[:4]
    acc_ref = refs[-1]
    refs = refs[4:-1]

    @pl.when(f == 0)
    def _():
        acc_ref[...] = jnp.zeros(acc_ref.shape, F32)

    h = h_ref[...]
    a = _silu(_dot(h, wg_ref[...])) * _dot(h, wu_ref[...])
    acc_ref[...] += _dot(a.astype(BF16), wd_ref[...])

    @pl.when(f == n_f - 1)
    def _():
        _epilogue(acc_ref[...], refs[0], refs[1], refs[2], refs[3:], has_next, False)


def _ffn(h, wg, wu, wd, x, gv, mv, has_next=True):
    n_f, _, f = wg.shape
    in_specs = [pl.BlockSpec((TM, D), lambda i, e: (i, 0)),
                pl.BlockSpec((None, D, f), lambda i, e: (e, 0, 0)),
                pl.BlockSpec((None, D, f), lambda i, e: (e, 0, 0)),
                pl.BlockSpec((None, f, D), lambda i, e: (e, 0, 0))]
    e_in, out_specs, out_shape = _epilogue_specs(has_next, False)
    return pl.pallas_call(
        functools.partial(_ffn_kernel, n_f=n_f, has_next=has_next),
        grid=(N_T, n_f),
        in_specs=in_specs + e_in,
        out_specs=out_specs,
        out_shape=out_shape,
        scratch_shapes=[pltpu.VMEM((TM, D), F32)],
        compiler_params=_cparams(("parallel", "arbitrary")),
        name="ffn_dense",
    )(h, wg, wu, wd, x, gv, mv)


TE = 256
R_ROWS = 2 * T_ROWS + N_EXPERTS * TE
N_TE = R_ROWS // TE


def _route_meta(route):
    e1 = route[:, 0].astype(jnp.int32)
    e2 = route[:, 1].astype(jnp.int32)
    ids = jnp.arange(N_EXPERTS, dtype=jnp.int32)
    hit = ((e1[:, None] == ids) | (e2[:, None] == ids)).astype(jnp.int32)
    csum = jnp.cumsum(hit, axis=0)
    counts = csum[-1]
    padded = (counts + TE - 1) // TE * TE
    ends = jnp.cumsum(padded)
    offs = ends - padded
    rank = csum - 1
    pos1 = offs[e1] + jnp.take_along_axis(rank, e1[:, None], axis=1)[:, 0]
    pos2 = offs[e2] + jnp.take_along_axis(rank, e2[:, None], axis=1)[:, 0]
    n_used = ends[-1] // TE
    tile_row = jnp.minimum(jnp.arange(N_TE, dtype=jnp.int32), n_used - 1) * TE
    tile_e = jnp.minimum(jnp.searchsorted(ends, tile_row, side='right'), N_EXPERTS - 1).astype(jnp.int32)
    pos = jnp.concatenate([pos1, pos2]).astype(jnp.int32)
    pad = jnp.concatenate([offs + counts, padded - counts]).astype(jnp.int32)
    return pos, pad, tile_e, n_used.astype(jnp.int32).reshape(1)


def _row_copy(src, s, dst, d, sem):
    return pltpu.make_async_copy(src.at[pl.ds(s, 1), :], dst.at[pl.ds(d, 1), :], sem)


def _dispatch_kernel(pos_ref, pad_ref, h_ref, xs_ref, zero_scr, sem):
    i = pl.program_id(0)
    base = i * TM

    def issue(r, c):
        _row_copy(h_ref, r, xs_ref, pos_ref[base + r], sem).start()
        _row_copy(h_ref, r, xs_ref, pos_ref[T_ROWS + base + r], sem).start()
        return c

    lax.fori_loop(0, TM, issue, 0, unroll=8)
    for _ in range(2):
        pltpu.make_async_copy(h_ref, xs_ref.at[pl.ds(0, TM), :], sem).wait()

    @pl.when(i == N_T - 1)
    def _():
        zero_scr[...] = jnp.zeros(zero_scr.shape, F32)
        for e in range(N_EXPERTS):
            start, n = pad_ref[e], pad_ref[N_EXPERTS + e]

            def fill(r, c):
                _row_copy(zero_scr, 0, xs_ref, start + r, sem).start()
                return c

            def drain(r, c):
                _row_copy(zero_scr, 0, xs_ref, start, sem).wait()
                return c

            lax.fori_loop(0, n, fill, 0)
            lax.fori_loop(0, n, drain, 0)


def _dispatch(pos, pad, h32):
    return pl.pallas_call(
        _dispatch_kernel,
        grid_spec=pltpu.PrefetchScalarGridSpec(
            num_scalar_prefetch=2,
            grid=(N_T,),
            in_specs=[pl.BlockSpec((TM, D), lambda i, *_: (i, 0))],
            out_specs=pl.BlockSpec(memory_space=pl.ANY),
            scratch_shapes=[pltpu.VMEM((SUBLANES, D), F32), pltpu.SemaphoreType.DMA]),
        out_shape=jax.ShapeDtypeStruct((R_ROWS, D), F32),
        compiler_params=_cparams(("arbitrary",)),
        name="moe_dispatch",
    )(pos, pad, h32)


def _moe_ffn_kernel(te_ref, nu_ref, xs_ref, wg_ref, wu_ref, wd_ref, ys_ref):
    @pl.when(pl.program_id(0) < nu_ref[0])
    def _():
        x = xs_ref[...].astype(BF16)
        a = _silu(_dot(x, wg_ref[...])) * _dot(x, wu_ref[...])
        ys_ref[...] = _dot(a.astype(BF16), wd_ref[...])


def _moe_ffn(tile_e, n_used, xs, wg, wu, wd):
    f = wg.shape[2]
    row = lambda i, te, nu: (jnp.minimum(i, nu[0] - 1), 0)
    return pl.pallas_call(
        _moe_ffn_kernel,
        grid_spec=pltpu.PrefetchScalarGridSpec(
            num_scalar_prefetch=2,
            grid=(N_TE,),
            in_specs=[pl.BlockSpec((TE, D), row),
                      pl.BlockSpec((None, D, f), lambda i, te, nu: (te[i], 0, 0)),
                      pl.BlockSpec((None, D, f), lambda i, te, nu: (te[i], 0, 0)),
                      pl.BlockSpec((None, f, D), lambda i, te, nu: (te[i], 0, 0))],
            out_specs=pl.BlockSpec((TE, D), row)),
        out_shape=jax.ShapeDtypeStruct((R_ROWS, D), F32),
        compiler_params=_cparams(("arbitrary",)),
        name="moe_ffn",
    )(tile_e, n_used, xs, wg, wu, wd)


def _combine_kernel(*refs, has_next):
    pos_ref, ys_ref, route_ref = refs[:3]
    ybuf, sem = refs[-2:]
    refs = refs[3:-2]
    i = pl.program_id(0)
    slot = i % 2

    def gather(tile, s):
        base = tile * TM

        def issue(r, c):
            _row_copy(ys_ref, pos_ref[base + r], ybuf.at[s, 0], r, sem.at[s]).start()
            _row_copy(ys_ref, pos_ref[T_ROWS + base + r], ybuf.at[s, 1], r, sem.at[s]).start()
            return c

        lax.fori_loop(0, TM, issue, 0, unroll=8)

    @pl.when(i == 0)
    def _():
        gather(0, 0)

    @pl.when(i + 1 < N_T)
    def _():
        gather(i + 1, 1 - slot)

    for c in range(2):
        pltpu.make_async_copy(ys_ref.at[pl.ds(0, TM), :], ybuf.at[slot, c], sem.at[slot]).wait()
    route = route_ref[...]
    y = route[:, 2:3] * ybuf[slot, 0] + route[:, 3:4] * ybuf[slot, 1]
    _epilogue(y, refs[0], refs[1], refs[2], refs[3:], has_next, False)


def _combine(pos, ys, route, x, gv, mv, has_next=True):
    e_in, out_specs, out_shape = _epilogue_specs(has_next, False)
    return pl.pallas_call(
        functools.partial(_combine_kernel, has_next=has_next),
        grid_spec=pltpu.PrefetchScalarGridSpec(
            num_scalar_prefetch=1,
            grid=(N_T,),
            in_specs=[pl.BlockSpec(memory_space=pl.ANY),
                      pl.BlockSpec((TM, LANES), lambda i, *_: (i, 0))] + e_in,
            out_specs=out_specs,
            scratch_shapes=[pltpu.VMEM((2, 2, TM, D), F32), pltpu.SemaphoreType.DMA((2,))]),
        out_shape=out_shape,
        compiler_params=_cparams(("arbitrary",)),
        name="moe_combine",
    )(pos, ys, route, x, gv, mv)


def _mla_down_kernel(h_ref, w_ref, gq_ref, gkv_ref, rc_ref, rlo_ref, rhi_ref,
                     cq_ref, ckv_ref, kr_ref, kvb_ref):
    i = pl.program_id(0)
    y = _dot(h_ref[...], w_ref[...])
    cq_ref[...] = _rms(y[:, :Q_RANK], gq_ref[...]).astype(BF16)
    ckv = _rms(y[:, Q_RANK:Q_RANK + KV_RANK], gkv_ref[...])
    kr = y[:, Q_RANK + KV_RANK:]
    kr = jnp.where(i >= NP_T, _rope(kr, rc_ref[...], rlo_ref[...], rhi_ref[...]), kr)
    ckv_ref[...] = ckv
    kr_ref[...] = kr
    kvb_ref[:, :KV_RANK] = ckv.astype(BF16)
    lane = lax.broadcasted_iota(jnp.int32, kr.shape, 1)
    kvb_ref[:, KV_RANK:] = jnp.where(lane == ROPE_DIM, 1.0, kr).astype(BF16)


def _mla_down(h, w, gq, gkv, tabs):
    n = Q_RANK + KVW
    row = lambda i: (i, 0)
    const = lambda i: (0, 0)
    tab = pl.BlockSpec((TM, LANES), lambda i: (_pos_tile(i), 0))
    return pl.pallas_call(
        _mla_down_kernel,
        grid=(N_T,),
        in_specs=[pl.BlockSpec((TM, D), row), pl.BlockSpec((D, n), const),
                  pl.BlockSpec((1, Q_RANK), const), pl.BlockSpec((1, KV_RANK), const),
                  tab, tab, tab],
        out_specs=[pl.BlockSpec((TM, Q_RANK), row), pl.BlockSpec((TM, KV_RANK), row),
                   pl.BlockSpec((TM, LANES), row), pl.BlockSpec((TM, KVW), row)],
        out_shape=[jax.ShapeDtypeStruct((T_ROWS, Q_RANK), BF16),
                   jax.ShapeDtypeStruct((T_ROWS, KV_RANK), F32),
                   jax.ShapeDtypeStruct((T_ROWS, LANES), F32),
                   jax.ShapeDtypeStruct((T_ROWS, KVW), BF16)],
        compiler_params=_cparams(("parallel",)),
        name="mla_down",
    )(h, w, gq, gkv, *tabs)


def _dot_nt(a, b):
    return lax.dot_general(a, b, (((1,), (1,)), ((), ())), preferred_element_type=F32)


def _mla_q_kernel(cq_ref, w_ref, cr_ref, sr_ref, cc_ref, sc_ref, q_ref):
    i = pl.program_id(0)
    y = _dot_nt(w_ref[...], cq_ref[...]) * ATTN_C2
    is_latent = i >= NP_T
    cr = jnp.where(is_latent, cr_ref[...], 1.0)
    sr = jnp.where(is_latent, sr_ref[...], 0.0)
    cc = jnp.where(is_latent, cc_ref[...], 1.0)
    sc = jnp.where(is_latent, sc_ref[...], 0.0)
    for hd in range(N_HEADS):
        r0 = hd * HEAD_PAD
        x0, x1, x2, x3 = (y[r0 + 8 * a:r0 + 8 * (a + 1), :] for a in range(4))
        rot = jnp.concatenate([x0 * cr - x1 * sr, x0 * sr + x1 * cr,
                               x2 * cc - x3 * sc, x2 * sc + x3 * cc], axis=0)
        q_ref[r0:r0 + ROPE_DIM, :] = rot.astype(BF16)
        q_ref[r0 + ROPE_DIM:r0 + HEAD_PAD, :] = y[r0 + ROPE_DIM:r0 + HEAD_PAD, :].astype(BF16)


def _mla_q(cq, w_t, tabs_t):
    tab = pl.BlockSpec((8, TM), lambda i: (0, _pos_tile(i)))
    return pl.pallas_call(
        _mla_q_kernel,
        grid=(N_T,),
        in_specs=[pl.BlockSpec((TM, Q_RANK), lambda i: (i, 0)),
                  pl.BlockSpec((HP, Q_RANK), lambda i: (0, 0)), tab, tab, tab, tab],
        out_specs=pl.BlockSpec((HP, TM), lambda i: (0, i)),
        out_shape=jax.ShapeDtypeStruct((HP, T_ROWS), BF16),
        compiler_params=_cparams(("parallel",)),
        name="mla_q",
    )(cq, w_t, *tabs_t)


def _mla_kv_kernel(c_ref, wk_ref, wvt_ref, k_ref, vt_ref):
    c = c_ref[...]
    k_ref[...] = _dot(c, wk_ref[...]).astype(BF16)
    vt_ref[...] = _dot_nt(wvt_ref[...], c).astype(BF16)


def _mla_kv(ckvkr, w_k, w_vt):
    rows = ckvkr.shape[0]
    hv = HVX
    return pl.pallas_call(
        _mla_kv_kernel,
        grid=(rows // TM,),
        in_specs=[pl.BlockSpec((TM, KVW), lambda i: (i, 0)),
                  pl.BlockSpec((KVW, HP), lambda i: (0, 0)),
                  pl.BlockSpec((hv, KVW), lambda i: (0, 0))],
        out_specs=[pl.BlockSpec((TM, HP), lambda i: (i, 0)), pl.BlockSpec((hv, TM), lambda i: (0, i))],
        out_shape=[jax.ShapeDtypeStruct((rows, HP), BF16), jax.ShapeDtypeStruct((hv, rows), BF16)],
        compiler_params=_cparams(("parallel",)),
        name="mla_kv",
    )(ckvkr, w_k, w_vt)


def _attn_kernel(*refs, nk, aliased):
    if aliased:
        refs = refs[1:]
    qt_ref, k_ref, vt_ref, o_ref, m_scr, acc_scr, ot_scr = refs
    ki = pl.program_id(2)

    @pl.when(ki == 0)
    def _():
        m_scr[...] = jnp.full(m_scr.shape, -jnp.inf, F32)
        acc_scr[...] = jnp.zeros(acc_scr.shape, F32)

    tk, tq = k_ref.shape[0], qt_ref.shape[1]
    vg = V_EXT // SUBLANES

    def across_sublanes(x, op):
        for k in (4, 2, 1):
            x = op(x, pltpu.roll(x, k, 0))
        return x

    def scores(hd):
        qs = slice(hd * HEAD_PAD, (hd + 1) * HEAD_PAD)
        return _dot(k_ref[:, qs], qt_ref[qs, :]).reshape(tk // SUBLANES, SUBLANES, tq)

    s_next = scores(0)
    for hd in range(N_HEADS):
        vs = slice(hd * V_EXT, (hd + 1) * V_EXT)
        s = s_next
        if hd + 1 < N_HEADS:
            s_next = scores(hd + 1)
        m_prev = m_scr[hd]
        m_new = jnp.maximum(m_prev, across_sublanes(jnp.max(s, axis=0), jnp.maximum))
        alpha = jnp.exp2(m_prev - m_new)
        p = jnp.exp2(s - m_new[None])
        pv = _dot(vt_ref[vs, :], p.reshape(tk, tq).astype(BF16))
        acc = acc_scr[vs, :].reshape(vg, SUBLANES, tq)
        acc_scr[vs, :] = (alpha[None] * acc).reshape(V_EXT, tq) + pv
        m_scr[hd] = m_new

    @pl.when(ki == nk - 1)
    def _():
        for hd in range(N_HEADS):
            r0 = hd * V_EXT
            den = across_sublanes(acc_scr[r0 + V_DIM:r0 + V_DIM + SUBLANES, :], jnp.add)
            acc = acc_scr[r0:r0 + V_DIM, :].reshape(V_DIM // SUBLANES, SUBLANES, tq)
            ot_scr[hd * V_DIM:(hd + 1) * V_DIM, :] = (acc / den[None]).reshape(V_DIM, tq)
        o_ref[...] = ot_scr[...].T.astype(BF16)


def _attention(qt, k, vt, n_b, tq, tk, lq, lk, q_row0, prev_out=None):
    nq, nk = lq // tq, lk // tk
    qb0 = q_row0 // tq
    hv = N_HEADS * V_DIM
    aliased = prev_out is not None
    in_specs = [pl.BlockSpec((HP, tq), lambda b, qi, ki: (0, qb0 + b * nq + qi)),
                pl.BlockSpec((tk, HP), lambda b, qi, ki: (b * nk + ki, 0)),
                pl.BlockSpec((HVX, tk), lambda b, qi, ki: (0, b * nk + ki))]
    args = [qt, k, vt]
    if aliased:
        in_specs = [pl.BlockSpec(memory_space=pl.ANY)] + in_specs
        args = [prev_out] + args
    return pl.pallas_call(
        functools.partial(_attn_kernel, nk=nk, aliased=aliased),
        grid=(n_b, nq, nk),
        in_specs=in_specs,
        out_specs=pl.BlockSpec((tq, hv), lambda b, qi, ki: (qb0 + b * nq + qi, 0)),
        out_shape=jax.ShapeDtypeStruct((T_ROWS, hv), BF16),
        scratch_shapes=[pltpu.VMEM((N_HEADS, SUBLANES, tq), F32), pltpu.VMEM((HVX, tq), F32),
                        pltpu.VMEM((hv, tq), F32)],
        input_output_aliases={0: 0} if aliased else {},
        compiler_params=_cparams(("parallel", "parallel", "arbitrary")),
        name="attn_latent" if aliased else "attn_context",
    )(*args)


def _rg_in_kernel(h_ref, w_ref, u_ref, g_ref):
    y = _dot(h_ref[...], w_ref[...])
    u_ref[...] = y[:, :D]
    g_ref[...] = _gelu_tanh(y[:, D:])


def _rg_in(h, w):
    return pl.pallas_call(
        _rg_in_kernel,
        grid=(N_T,),
        in_specs=[pl.BlockSpec((TM, D), lambda i: (i, 0)), pl.BlockSpec((D, 2 * D), lambda i: (0, 0))],
        out_specs=[pl.BlockSpec((TM, D), lambda i: (i, 0))] * 2,
        out_shape=[jax.ShapeDtypeStruct((T_ROWS, D), F32)] * 2,
        compiler_params=_cparams(("parallel",)),
        name="rg_in",
    )(h, w)


RG_TC = 256
RG_GROUP = 256


def _rg_scan_kernel(*refs, nj, seq_len, aliased):
    if aliased:
        refs = refs[2:]
    (uf_ref, ufp_ref, ufn_ref, ub_ref, ubp_ref, ubn_ref, cw_ref, cb_ref, wai_ref, bai_ref,
     lam_ref, h0_ref, yf_ref, yb_ref, fin_ref, carry_ref) = refs
    j = pl.program_id(1)
    tc = RG_TC
    r = lax.broadcasted_iota(jnp.int32, (tc, 1), 0)
    sub = r & (SUBLANES - 1)

    @pl.when(j == 0)
    def _():
        carry_ref[0:2, :] = h0_ref[...]

    def gates(m_ref, p_ref, n_ref, d, chunk):
        m = m_ref[...]
        pos = chunk * tc + r
        u = (cw_ref[0:1, :] * _shifted(m, p_ref, n_ref, -1, pos, seq_len, r)
             + cw_ref[1:2, :] * m
             + cw_ref[2:3, :] * _shifted(m, p_ref, n_ref, 1, pos, seq_len, r)
             + cw_ref[3:4, :] * _shifted(m, p_ref, n_ref, 2, pos, seq_len, r)
             + cb_ref[...])
        ub = u.astype(BF16)
        ra, ri = [], []
        for q in range(D // RG_GROUP):
            y = _dot(ub[:, q * RG_GROUP:(q + 1) * RG_GROUP], wai_ref[d, q])
            ra.append(y[:, :RG_GROUP])
            ri.append(y[:, RG_GROUP:])
        rr = jax.nn.sigmoid(jnp.concatenate(ra, axis=1) + bai_ref[d, 0:1, :])
        ii = jax.nn.sigmoid(jnp.concatenate(ri, axis=1) + bai_ref[d, 1:2, :])
        nl = -lam_ref[d:d + 1, :]
        softplus = jnp.maximum(nl, 0.0) + jnp.log1p(jnp.exp(-jnp.abs(nl)))
        log_a = (-RG_C * softplus) * rr
        a = jnp.exp(log_a)
        bx = jnp.sqrt(jnp.maximum(-jnp.tanh(log_a) * (a * a + 1.0), 0.0)) * (ii * u)
        return a, bx

    a, b = gates(uf_ref, ufp_ref, ufn_ref, 0, j)
    for k in (1, 2, 4):
        ok = sub >= k
        b = jnp.where(ok, a * pltpu.roll(b, k, 0) + b, b)
        a = jnp.where(ok, a * pltpu.roll(a, k, 0), a)
    h = carry_ref[0:1, :]
    for g in range(tc // SUBLANES):
        sl = slice(g * SUBLANES, (g + 1) * SUBLANES)
        hg = a[sl] * h + b[sl]
        yf_ref[sl, :] = hg
        h = hg[SUBLANES - 1:SUBLANES, :]
    carry_ref[0:1, :] = h

    a, b = gates(ub_ref, ubp_ref, ubn_ref, 1, nj - 1 - j)
    for k in (1, 2, 4):
        ok = sub < SUBLANES - k
        b = jnp.where(ok, a * pltpu.roll(b, tc - k, 0) + b, b)
        a = jnp.where(ok, a * pltpu.roll(a, tc - k, 0), a)
    h = carry_ref[1:2, :]
    for g in reversed(range(tc // SUBLANES)):
        sl = slice(g * SUBLANES, (g + 1) * SUBLANES)
        hg = a[sl] * h + b[sl]
        yb_ref[sl, :] = hg
        h = hg[0:1, :]
    carry_ref[1:2, :] = h

    @pl.when(j == nj - 1)
    def _():
        fin_ref[...] = carry_ref[0:2, :]


def _rg_scan(u, cw, cb, wai, bai, lam, h0, n_seq, seq_len, row0, prev=None):
    nj = seq_len // RG_TC
    b0 = row0 // RG_TC
    aliased = prev is not None
    fwd = lambda s, j: b0 + s * nj + j
    bwd = lambda s, j: b0 + s * nj + (nj - 1 - j)
    const2 = lambda s, j: (0, 0)
    in_specs = (_halo_specs(RG_TC, fwd) + _halo_specs(RG_TC, bwd)
                + [pl.BlockSpec((8, D), const2), pl.BlockSpec((1, D), const2),
                   pl.BlockSpec((2, D // RG_GROUP, RG_GROUP, 2 * RG_GROUP), lambda s, j: (0, 0, 0, 0)),
                   pl.BlockSpec((2, 2, D), lambda s, j: (0, 0, 0)),
                   pl.BlockSpec((2, D), const2),
                   pl.BlockSpec((None, 2, D), lambda s, j: (s, 0, 0))])
    args = [u, u, u, u, u, u, cw, cb, wai, bai, lam, h0]
    if aliased:
        in_specs = [pl.BlockSpec(memory_space=pl.ANY)] * 2 + in_specs
        args = list(prev) + args
    return pl.pallas_call(
        functools.partial(_rg_scan_kernel, nj=nj, seq_len=seq_len, aliased=aliased),
        grid=(n_seq, nj),
        in_specs=in_specs,
        out_specs=[pl.BlockSpec((RG_TC, D), lambda s, j: (fwd(s, j), 0)),
                   pl.BlockSpec((RG_TC, D), lambda s, j: (bwd(s, j), 0)),
                   pl.BlockSpec((None, 2, D), lambda s, j: (s, 0, 0))],
        out_shape=[jax.ShapeDtypeStruct((T_ROWS, D), F32), jax.ShapeDtypeStruct((T_ROWS, D), F32),
                   jax.ShapeDtypeStruct((n_seq, 2, D), F32)],
        scratch_shapes=[pltpu.VMEM((8, D), F32)],
        input_output_aliases={0: 0, 1: 1} if aliased else {},
        compiler_params=_cparams(("parallel", "arbitrary")),
        name="rg_scan_latent" if aliased else "rg_scan_context",
    )(*args)


def _sc_in_kernel(h_ref, w_ref, bg_ref, m_ref):
    y = _dot(h_ref[...], w_ref[...])
    bg_ref[...] = y[:, :D]
    m_ref[...] = y[:, D:2 * D] * y[:, 2 * D:]


def _sc_in(h, w):
    return pl.pallas_call(
        _sc_in_kernel,
        grid=(N_T,),
        in_specs=[pl.BlockSpec((TM, D), lambda i: (i, 0)), pl.BlockSpec((D, 3 * D), lambda i: (0, 0))],
        out_specs=[pl.BlockSpec((TM, D), lambda i: (i, 0))] * 2,
        out_shape=[jax.ShapeDtypeStruct((T_ROWS, D), F32)] * 2,
        compiler_params=_cparams(("parallel",)),
        name="sc_in",
    )(h, w)


def _rope_tables():
    t = np.arange(DEC_SEQ)
    inv = ROPE_THETA ** (-jnp.arange(0, AXIS_DIM, 2, dtype=F32) / AXIS_DIM)
    ang_r = jnp.asarray((t // GRID_W).astype(np.float32))[:, None] * inv
    ang_c = jnp.asarray((t % GRID_W).astype(np.float32))[:, None] * inv
    cr, sr, cc, sc = jnp.cos(ang_r), jnp.sin(ang_r), jnp.cos(ang_c), jnp.sin(ang_c)
    z8 = jnp.zeros((DEC_SEQ, 8), F32)
    pad1 = jnp.ones((DEC_SEQ, LANES - ROPE_DIM), F32)
    pad0 = jnp.zeros((DEC_SEQ, LANES - ROPE_DIM), F32)
    c = jnp.concatenate([cr, cr, cc, cc, pad1], axis=1)
    lo = jnp.concatenate([-sr, z8, -sc, z8, pad0], axis=1)
    hi = jnp.concatenate([z8, sr, z8, sc, pad0], axis=1)
    return (c, lo, hi), (cr.T, sr.T, cc.T, sc.T)


def _pad_heads(w, lo):
    r, h, d = w.shape
    return jnp.pad(w, ((0, 0), (0, 0), (lo, HEAD_PAD - lo - d))).reshape(r, h * HEAD_PAD)


def _mla_weights(w_dq, w_uq, w_dkv, w_uk, w_uv, w_o):
    w_down = jnp.concatenate([w_dq, w_dkv, jnp.zeros((D, KVW - KV_RANK - ROPE_DIM), F32)], axis=1)
    w_q = _pad_heads(w_uq[:, :, QK_NOPE:], 0) + _pad_heads(w_uq[:, :, :QK_NOPE], HEAD_PAD - QK_NOPE)
    place = jnp.broadcast_to(jnp.eye(ROPE_DIM, dtype=F32)[:, None, :], (ROPE_DIM, N_HEADS, ROPE_DIM))
    w_k = jnp.concatenate([_pad_heads(w_uk, HEAD_PAD - QK_NOPE), _pad_heads(place, 0),
                           jnp.zeros((KVW - KV_RANK - ROPE_DIM, HP), F32)], axis=0)
    w_vt = jnp.pad(jnp.transpose(w_uv, (1, 2, 0)), ((0, 0), (0, V_EXT - V_DIM), (0, KVW - KV_RANK)))
    w_vt = w_vt.at[:, V_DIM, ONE_COL].set(1.0).reshape(HVX, KVW)
    return (w_down.astype(BF16), w_q.T.astype(BF16), w_k.astype(BF16), w_vt.astype(BF16),
            w_o.astype(BF16))


def _block_diag_groups(w_a, w_i):
    per = RG_GROUP // RG_BLOCK

    def bd(w):
        w = w.reshape(2, D // RG_GROUP, per, RG_BLOCK, RG_BLOCK)
        eye = jnp.eye(per, dtype=F32)
        return jnp.einsum('dgpkj,pq->dgpkqj', w, eye).reshape(2, D // RG_GROUP, RG_GROUP, RG_GROUP)

    return jnp.concatenate([bd(w_a), bd(w_i)], axis=-1).astype(BF16)


def _pad_rows(w, rows=8):
    return jnp.pad(w, ((0, rows - w.shape[0]), (0, 0)))


def kernel(x_prompt, x_sample, cache_mla_ckv, cache_mla_krope, state_rglru, c, c_ctx, mod_w, mod_b, norm_g, mla_w_dq, mla_g_q, mla_w_uq, mla_w_dkv, mla_g_kv, mla_w_uk, mla_w_uv, mla_w_o, rg_w_x, rg_w_y, rg_conv_w, rg_conv_b, rg_w_a, rg_b_a, rg_w_i, rg_b_i, rg_lambda, rg_w_out, sc_w_in, sc_conv_w, sc_w_out, ffn_w_gate, ffn_w_up, ffn_w_down, moe_w_router, moe_b_router, moe_w_gate, moe_w_up, moe_w_down):
    x = jnp.concatenate([x_prompt.reshape(P_ROWS, D), x_sample.reshape(S_ROWS, D)], axis=0)
    cond8 = jnp.concatenate([c_ctx[None], c, jnp.zeros((N_GROUPS - 1 - DEC_BATCH, D), F32)], axis=0)
    mod = _modulation(cond8, mod_w, mod_b).reshape(DEPTH, N_GROUPS, 6, D)
    tabs, tabs_t = _rope_tables()
    zrow = jnp.zeros((N_GROUPS, 5, D), F32)

    def vecs(l, sub):
        if sub == 0:
            g_next, sh, sc = norm_g[l, 2], mod[l, :, 3], mod[l, :, 4]
        elif l + 1 < DEPTH:
            g_next, sh, sc = norm_g[l + 1, 0], mod[l + 1, :, 0], mod[l + 1, :, 1]
        else:
            g_next, sh, sc = jnp.zeros((D,), F32), zrow[:, 0], zrow[:, 0]
        gv = _pad_rows(jnp.stack([norm_g[l, 1 + 2 * sub], g_next]))
        mv = jnp.concatenate([jnp.stack([mod[l, :, 2 + 3 * sub], sh, sc], axis=1), zrow], axis=1)
        return gv, mv

    gv0 = _pad_rows(jnp.stack([jnp.zeros((D,), F32), norm_g[0, 0]]))
    mv0 = jnp.concatenate([jnp.stack([zrow[:, 0], mod[0, :, 0], mod[0, :, 1]], axis=1), zrow], axis=1)
    h = _pre0(x, gv0, mv0)

    new_ckv, new_krope, new_rg = [], [], []
    for l in range(DEPTH):
        kind, j = l % 3, l // 3
        gv, mv = vecs(l, 0)
        router = None
        if l % 2 == 1:
            m = l // 2
            router = (jnp.pad(moe_w_router[m], ((0, 0), (0, LANES - N_EXPERTS))),
                      jnp.pad(moe_b_router[m], (0, LANES - N_EXPERTS)).reshape(1, LANES))
        if kind == 0:
            w_down, w_qt, w_k, w_vt, w_op = _mla_weights(mla_w_dq[j], mla_w_uq[j], mla_w_dkv[j],
                                                         mla_w_uk[j], mla_w_uv[j], mla_w_o[j])
            cq, ckv, kr, kvb = _mla_down(h, w_down, mla_g_q[j].reshape(1, Q_RANK),
                                         mla_g_kv[j].reshape(1, KV_RANK), tabs)
            q = _mla_q(cq, w_qt, tabs_t)
            new_ckv.append(ckv[:P_ROWS].reshape(BATCH, SEQ, KV_RANK))
            new_krope.append(kr[:P_ROWS, :ROPE_DIM].reshape(BATCH, SEQ, ROPE_DIM))
            cache = jnp.concatenate(
                [cache_mla_ckv[:, j], cache_mla_krope[:, j], jnp.ones((DEC_BATCH, PAST, 1), F32),
                 jnp.zeros((DEC_BATCH, PAST, KVW - ONE_COL - 1), F32)], axis=-1).astype(BF16)
            kv_lat = jnp.concatenate([cache, kvb[P_ROWS:].reshape(DEC_BATCH, DEC_SEQ, KVW)], axis=1)
            k_c, v_c = _mla_kv(kvb[:P_ROWS], w_k, w_vt)
            k_l, v_l = _mla_kv(kv_lat.reshape(DEC_BATCH * (PAST + DEC_SEQ), KVW), w_k, w_vt)
            o = _attention(q, k_c, v_c, BATCH, SEQ, SEQ, SEQ, SEQ, 0)
            o = _attention(q, k_l, v_l, DEC_BATCH, 512, 512, DEC_SEQ, PAST + DEC_SEQ, P_ROWS, prev_out=o)
            outs = _proj("plain", (o,), w_op, x, gv, mv, router=router)
        elif kind == 1:
            w_xy = jnp.concatenate([rg_w_x[j], rg_w_y[j]], axis=1).astype(BF16)
            u, gate = _rg_in(h, w_xy)
            wai = _block_diag_groups(rg_w_a[j], rg_w_i[j])
            bai = jnp.stack([rg_b_a[j], rg_b_i[j]], axis=1)
            scan_args = (_pad_rows(rg_conv_w[j]), rg_conv_b[j].reshape(1, D), wai, bai, rg_lambda[j])
            yf, yb, fin = _rg_scan(u, *scan_args, jnp.zeros((BATCH, 2, D), F32), BATCH, SEQ, 0)
            yf, yb, _ = _rg_scan(u, *scan_args, state_rglru[:, j], DEC_BATCH, DEC_SEQ, P_ROWS,
                                 prev=(yf, yb))
            new_rg.append(fin)
            outs = _proj("rg", (yf, yb, gate), rg_w_out[j].astype(BF16), x, gv, mv, router=router)
        else:
            bg, mm = _sc_in(h, sc_w_in[j].astype(BF16))
            outs = _proj("sc", (bg, mm, _pad_rows(sc_conv_w[j])), sc_w_out[j].astype(BF16), x, gv, mv,
                         router=router)
        x, h = outs[0], outs[1]

        gv, mv = vecs(l, 1)
        has_next = l + 1 < DEPTH
        m = l // 2
        if l % 2 == 0:
            half = D_FF // 2
            wg = ffn_w_gate[m].reshape(D, 2, half).transpose(1, 0, 2).astype(BF16)
            wu = ffn_w_up[m].reshape(D, 2, half).transpose(1, 0, 2).astype(BF16)
            wd = ffn_w_down[m].reshape(2, half, D).astype(BF16)
            outs = _ffn(h, wg, wu, wd, x, gv, mv, has_next=has_next)
        else:
            route = outs[2]
            pos, pad, tile_e, n_used = _route_meta(route)
            xs = _dispatch(pos, pad, h)
            ys = _moe_ffn(tile_e, n_used, xs, moe_w_gate[m].astype(BF16), moe_w_up[m].astype(BF16),
                          moe_w_down[m].astype(BF16))
            outs = _combine(pos, ys, route, x, gv, mv, has_next=has_next)
        x = outs[0]
        h = outs[1] if has_next else None

    y_prompt = x[:P_ROWS].reshape(BATCH, SEQ, D)
    y_sample = x[P_ROWS:].reshape(DEC_BATCH, DEC_SEQ, D)
    return (y_prompt, y_sample, jnp.stack(new_ckv, axis=1), jnp.stack(new_krope, axis=1),
            jnp.stack(new_rg, axis=1))
```

```python
import functools

import numpy as np
import jax
import jax.numpy as jnp
from jax import lax
from jax.experimental import pallas as pl
from jax.experimental.pallas import tpu as pltpu

F32 = jnp.float32
BF16 = jnp.bfloat16

D = 1024
BATCH = 32
SEQ = 256
DEPTH = 4
DEC_BATCH = 4
DEC_SEQ = 4096
PAST = 512
GRID_W = 64
N_HEADS = 16
QK_NOPE = 64
ROPE_DIM = 32
AXIS_DIM = 16
V_DIM = 64
Q_RANK = 384
KV_RANK = 256
ROPE_THETA = 10000.0
ATTN_SCALE = (QK_NOPE + ROPE_DIM) ** -0.5
RG_BLOCKS = 16
RG_BLOCK = 64
RG_C = 8.0
D_FF = 2816
N_EXPERTS = 8
D_FF_EXPERT = 1408
EPS = 1e-6

P_ROWS = BATCH * SEQ
S_ROWS = DEC_BATCH * DEC_SEQ
T_ROWS = P_ROWS + S_ROWS
N_GROUPS = 8
HEAD_PAD = 128
HP = N_HEADS * HEAD_PAD
KVW = 384
ONE_COL = KV_RANK + ROPE_DIM
V_EXT = 80
HVX = N_HEADS * V_EXT
ATTN_C2 = ATTN_SCALE * float(np.log2(np.e))

TM = 512
NP_T = P_ROWS // TM
NS_T = DEC_SEQ // TM
N_T = T_ROWS // TM
SUBLANES = 8
LANES = 128
VMEM_LIMIT = 56 * 1024 * 1024


def _cparams(sem):
    return pltpu.CompilerParams(dimension_semantics=sem, vmem_limit_bytes=VMEM_LIMIT)


def _group_of_tile(i):
    return jnp.maximum(i - NP_T + NS_T, 0) // NS_T


def _pos_tile(i):
    return jnp.maximum(i - NP_T, 0) % NS_T


def _dot(a, b):
    return jnp.dot(a, b, preferred_element_type=F32)


def _rms(x, g):
    ms = jnp.mean(x * x, axis=-1, keepdims=True)
    return x * lax.rsqrt(ms + EPS) * g


def _silu(x):
    return x * jax.nn.sigmoid(x)


def _gelu_tanh(x):
    return x * (0.5 * (1.0 + jnp.tanh(np.sqrt(2.0 / np.pi).astype(np.float32)
                                      * (x + 0.044715 * (x * x * x)))))


def _rope(x, c, s_lo, s_hi):
    return (x * c + pltpu.roll(x, LANES - 8, 1) * s_lo + pltpu.roll(x, 8, 1) * s_hi)


def _mod_kernel(c_ref, w_ref, b_ref, o_ref):
    c = c_ref[...]
    s = _silu(c).astype(BF16)
    o_ref[...] = _dot(s, w_ref[...].astype(BF16)) + b_ref[...]


def _modulation(cond8, mod_w, mod_b):
    tn = 1536
    return pl.pallas_call(
        _mod_kernel,
        grid=(DEPTH, 6 * D // tn),
        in_specs=[pl.BlockSpec((N_GROUPS, D), lambda l, n: (0, 0)),
                  pl.BlockSpec((None, D, tn), lambda l, n: (l, 0, n)),
                  pl.BlockSpec((None, 1, tn), lambda l, n: (l, 0, n))],
        out_specs=pl.BlockSpec((None, N_GROUPS, tn), lambda l, n: (l, 0, n)),
        out_shape=jax.ShapeDtypeStruct((DEPTH, N_GROUPS, 6 * D), F32),
        compiler_params=_cparams(("parallel", "parallel")),
        name="modulation",
    )(cond8, mod_w, mod_b.reshape(DEPTH, 1, 6 * D))


def _ctx_block(i):
    return jnp.minimum(i, NP_T - 1)


def _lat_block(i):
    return jnp.maximum(i - NP_T, 0)


def _pre0_kernel(xp_ref, xs_ref, gv_ref, mv_ref, x_ref, h_ref):
    x = jnp.where(pl.program_id(0) < NP_T, xp_ref[...], xs_ref[...])
    x_ref[...] = x
    h = _rms(x, gv_ref[1:2, :]) * (1.0 + mv_ref[2:3, :]) + mv_ref[1:2, :]
    h_ref[...] = h.astype(BF16)


def _pre0(xp, xs, gv, mv):
    return pl.pallas_call(
        _pre0_kernel,
        grid=(N_T,),
        in_specs=[pl.BlockSpec((TM, D), lambda i: (_ctx_block(i), 0)),
                  pl.BlockSpec((TM, D), lambda i: (_lat_block(i), 0)),
                  pl.BlockSpec((8, D), lambda i: (0, 0)),
                  pl.BlockSpec((None, 8, D), lambda i: (_group_of_tile(i), 0, 0))],
        out_specs=[pl.BlockSpec((TM, D), lambda i: (i, 0))] * 2,
        out_shape=[jax.ShapeDtypeStruct((T_ROWS, D), F32), jax.ShapeDtypeStruct((T_ROWS, D), BF16)],
        compiler_params=_cparams(("arbitrary",)),
        name="pre0",
    )(xp, xs, gv, mv)


def _router(hn, wh_ref, wl_ref, br_ref):
    hh = hn.astype(BF16)
    hl = (hn - hh.astype(F32)).astype(BF16)
    logits = (_dot(hh, wh_ref[...]) + (_dot(hl, wh_ref[...]) + _dot(hh, wl_ref[...]))) + br_ref[...]
    lane = lax.broadcasted_iota(jnp.int32, logits.shape, 1)
    neg = jnp.float32(-jnp.inf)
    logits = jnp.where(lane < N_EXPERTS, logits, neg)
    m1 = jnp.max(logits, axis=-1, keepdims=True)
    i1 = jnp.min(jnp.where(logits == m1, lane, LANES), axis=-1, keepdims=True)
    rest = jnp.where(lane == i1, neg, logits)
    m2 = jnp.max(rest, axis=-1, keepdims=True)
    i2 = jnp.min(jnp.where(rest == m2, lane, LANES), axis=-1, keepdims=True)
    e = jnp.exp(m2 - m1)
    p1 = 1.0 / (1.0 + e)
    p2 = e / (1.0 + e)
    return jnp.where(lane == 0, i1.astype(F32),
                     jnp.where(lane == 1, i2.astype(F32),
                               jnp.where(lane == 2, p1, jnp.where(lane == 3, p2, 0.0))))


def _post_pre(x, out, gv_ref, mv_ref, has_next):
    xn = x + mv_ref[0:1, :] * _rms(out, gv_ref[0:1, :])
    if not has_next:
        return xn, None
    hn = _rms(xn, gv_ref[1:2, :]) * (1.0 + mv_ref[2:3, :]) + mv_ref[1:2, :]
    return xn, hn


def _epilogue(out, x_ref, gv_ref, mv_ref, rest, has_next, want_router):
    rest = list(rest)
    if want_router:
        router_refs = rest[:3]
        rest = rest[3:]
    xn, hn = _post_pre(x_ref[...], out, gv_ref, mv_ref, has_next)
    if not has_next:
        i = pl.program_id(0)

        @pl.when(i < NP_T)
        def _():
            rest[0][...] = xn

        @pl.when(i >= NP_T)
        def _():
            rest[1][...] = xn
        return
    rest[0][...] = xn
    rest[1][...] = hn.astype(rest[1].dtype)
    if want_router:
        rest[2][...] = _router(hn, *router_refs)


def _epilogue_specs(has_next, want_router):
    in_specs = [pl.BlockSpec((TM, D), lambda i, *_: (i, 0)),
                pl.BlockSpec((8, D), lambda i, *_: (0, 0)),
                pl.BlockSpec((None, 8, D), lambda i, *_: (_group_of_tile(i), 0, 0))]
    if not has_next:
        out_specs = [pl.BlockSpec((TM, D), lambda i, *_: (_ctx_block(i), 0)),
                     pl.BlockSpec((TM, D), lambda i, *_: (_lat_block(i), 0))]
        out_shape = [jax.ShapeDtypeStruct((P_ROWS, D), F32), jax.ShapeDtypeStruct((S_ROWS, D), F32)]
        return in_specs, out_specs, out_shape
    out_specs = [pl.BlockSpec((TM, D), lambda i, *_: (i, 0))]
    out_shape = [jax.ShapeDtypeStruct((T_ROWS, D), F32)]
    if want_router:
        in_specs += [pl.BlockSpec((D, LANES), lambda i, *_: (0, 0)),
                     pl.BlockSpec((D, LANES), lambda i, *_: (0, 0)),
                     pl.BlockSpec((1, LANES), lambda i, *_: (0, 0))]
    if has_next:
        out_specs.append(pl.BlockSpec((TM, D), lambda i, *_: (i, 0)))
        out_shape.append(jax.ShapeDtypeStruct((T_ROWS, D), F32 if want_router else BF16))
    if want_router:
        out_specs.append(pl.BlockSpec((TM, LANES), lambda i, *_: (i, 0)))
        out_shape.append(jax.ShapeDtypeStruct((T_ROWS, LANES), F32))
    return in_specs, out_specs, out_shape


def _seq_pos(i, rows):
    seq_len = jnp.where(i < NP_T, SEQ, DEC_SEQ)
    r = lax.broadcasted_iota(jnp.int32, (rows, 1), 0)
    return (i * rows + r) & (seq_len - 1), seq_len, r


def _shifted(m, prev_ref, next_ref, k, pos, seq_len, r):
    rows = m.shape[0]
    if k < 0:
        y = pltpu.roll(m, -k, 0)
        y = jnp.where(r == 0, prev_ref[SUBLANES - 1:SUBLANES, :], y)
        return jnp.where(pos + k < 0, 0.0, y)
    y = pltpu.roll(m, rows - k, 0)
    for q in range(k):
        y = jnp.where(r == rows - k + q, next_ref[q:q + 1, :], y)
    return jnp.where(pos + k >= seq_len, 0.0, y)


def _halo_specs(rows, row_block_of):
    per = rows // SUBLANES
    last = T_ROWS // SUBLANES - 1
    return [pl.BlockSpec((rows, D), lambda *g: (row_block_of(*g), 0)),
            pl.BlockSpec((SUBLANES, D), lambda *g: (jnp.maximum(row_block_of(*g) * per - 1, 0), 0)),
            pl.BlockSpec((SUBLANES, D), lambda *g: (jnp.minimum((row_block_of(*g) + 1) * per, last), 0))]


def _proj_kernel(*refs, mode, has_next, want_router):
    i = pl.program_id(0)
    if mode == "plain":
        a_ref, w_ref = refs[:2]
        rest = refs[2:]
        a = a_ref[...]
    elif mode == "rg":
        yf_ref, yb_ref, g_ref, w_ref = refs[:4]
        rest = refs[4:]
        a = ((yf_ref[...] + yb_ref[...]) * g_ref[...]).astype(BF16)
    else:
        bg_ref, m_ref, mp_ref, mn_ref, cw_ref, w_ref = refs[:6]
        rest = refs[6:]
        m = m_ref[...]
        pos, seq_len, r = _seq_pos(i, TM)
        z = (cw_ref[0:1, :] * _shifted(m, mp_ref, mn_ref, -1, pos, seq_len, r)
             + cw_ref[1:2, :] * m
             + cw_ref[2:3, :] * _shifted(m, mp_ref, mn_ref, 1, pos, seq_len, r))
        a = (bg_ref[...] * z).astype(BF16)
    out = _dot(a, w_ref[...])
    _epilogue(out, rest[0], rest[1], rest[2], rest[3:], has_next, want_router)


def _proj(mode, ins, w, x, gv, mv, has_next=True, router=None):
    want_router = router is not None
    k = w.shape[0]
    row = lambda i: (i, 0)
    if mode == "plain":
        in_specs = [pl.BlockSpec((TM, k), row)]
    elif mode == "rg":
        in_specs = [pl.BlockSpec((TM, D), row)] * 3
    else:
        bg, m, cw = ins
        ins = (bg, m, m, m, cw)
        in_specs = ([pl.BlockSpec((TM, D), row)] + _halo_specs(TM, lambda i: i)
                    + [pl.BlockSpec((8, D), lambda i: (0, 0))])
    in_specs.append(pl.BlockSpec((k, D), lambda i: (0, 0)))
    e_in, out_specs, out_shape = _epilogue_specs(has_next, want_router)
    args = list(ins) + [w, x, gv, mv] + (list(router) if want_router else [])
    return pl.pallas_call(
        functools.partial(_proj_kernel, mode=mode, has_next=has_next, want_router=want_router),
        grid=(N_T,),
        in_specs=in_specs + e_in,
        out_specs=out_specs,
        out_shape=out_shape,
        compiler_params=_cparams(("parallel",)),
        name="proj_" + mode,
    )(*args)


def _ffn_kernel(*refs, n_f, has_next):
    f = pl.program_id(1)
    h_ref, wg_ref, wu_ref, wd_ref = refs[:4]
    acc_ref = refs[-1]
    refs = refs[4:-1]

    @pl.when(f == 0)
    def _():
        acc_ref[...] = jnp.zeros(acc_ref.shape, F32)

    h = h_ref[...]
    a = _silu(_dot(h, wg_ref[...])) * _dot(h, wu_ref[...])
    acc_ref[...] += _dot(a.astype(BF16), wd_ref[...])

    @pl.when(f == n_f - 1)
    def _():
        _epilogue(acc_ref[...], refs[0], refs[1], refs[2], refs[3:], has_next, False)


def _ffn(h, wg, wu, wd, x, gv, mv, has_next=True):
    n_f, _, f = wg.shape
    in_specs = [pl.BlockSpec((TM, D), lambda i, e: (i, 0)),
                pl.BlockSpec((None, D, f), lambda i, e: (e, 0, 0)),
                pl.BlockSpec((None, D, f), lambda i, e: (e, 0, 0)),
                pl.BlockSpec((None, f, D), lambda i, e: (e, 0, 0))]
    e_in, out_specs, out_shape = _epilogue_specs(has_next, False)
    return pl.pallas_call(
        functools.partial(_ffn_kernel, n_f=n_f, has_next=has_next),
        grid=(N_T, n_f),
        in_specs=in_specs + e_in,
        out_specs=out_specs,
        out_shape=out_shape,
        scratch_shapes=[pltpu.VMEM((TM, D), F32)],
        compiler_params=_cparams(("parallel", "arbitrary")),
        name="ffn_dense",
    )(h, wg, wu, wd, x, gv, mv)


TE = 256
R_ROWS = 2 * T_ROWS + N_EXPERTS * TE
N_TE = R_ROWS // TE


def _route_meta(route):
    e1 = route[:, 0].astype(jnp.int32)
    e2 = route[:, 1].astype(jnp.int32)
    ids = jnp.arange(N_EXPERTS, dtype=jnp.int32)
    hit = ((e1[:, None] == ids) | (e2[:, None] == ids)).astype(jnp.int32)
    csum = jnp.cumsum(hit, axis=0)
    counts = csum[-1]
    padded = (counts + TE - 1) // TE * TE
    ends = jnp.cumsum(padded)
    offs = ends - padded
    rank = csum - 1
    pos1 = offs[e1] + jnp.take_along_axis(rank, e1[:, None], axis=1)[:, 0]
    pos2 = offs[e2] + jnp.take_along_axis(rank, e2[:, None], axis=1)[:, 0]
    n_used = ends[-1] // TE
    tile_row = jnp.minimum(jnp.arange(N_TE, dtype=jnp.int32), n_used - 1) * TE
    tile_e = jnp.minimum(jnp.searchsorted(ends, tile_row, side='right'), N_EXPERTS - 1).astype(jnp.int32)
    pos = jnp.concatenate([pos1, pos2]).astype(jnp.int32)
    pad = jnp.concatenate([offs + counts, padded - counts]).astype(jnp.int32)
    return pos, pad, tile_e, n_used.astype(jnp.int32).reshape(1)


def _row_copy(src, s, dst, d, sem):
    return pltpu.make_async_copy(src.at[pl.ds(s, 1), :], dst.at[pl.ds(d, 1), :], sem)


def _dispatch_kernel(pos_ref, pad_ref, h_ref, xs_ref, zero_scr, sem):
    i = pl.program_id(0)
    base = i * TM

    def issue(r, c):
        _row_copy(h_ref, r, xs_ref, pos_ref[base + r], sem).start()
        _row_copy(h_ref, r, xs_ref, pos_ref[T_ROWS + base + r], sem).start()
        return c

    lax.fori_loop(0, TM, issue, 0, unroll=8)
    for _ in range(2):
        pltpu.make_async_copy(h_ref, xs_ref.at[pl.ds(0, TM), :], sem).wait()

    @pl.when(i == N_T - 1)
    def _():
        zero_scr[...] = jnp.zeros(zero_scr.shape, F32)
        for e in range(N_EXPERTS):
            start, n = pad_ref[e], pad_ref[N_EXPERTS + e]

            def fill(r, c):
                _row_copy(zero_scr, 0, xs_ref, start + r, sem).start()
                return c

            def drain(r, c):
                _row_copy(zero_scr, 0, xs_ref, start, sem).wait()
                return c

            lax.fori_loop(0, n, fill, 0)
            lax.fori_loop(0, n, drain, 0)


def _dispatch(pos, pad, h32):
    return pl.pallas_call(
        _dispatch_kernel,
        grid_spec=pltpu.PrefetchScalarGridSpec(
            num_scalar_prefetch=2,
            grid=(N_T,),
            in_specs=[pl.BlockSpec((TM, D), lambda i, *_: (i, 0))],
            out_specs=pl.BlockSpec(memory_space=pl.ANY),
            scratch_shapes=[pltpu.VMEM((SUBLANES, D), F32), pltpu.SemaphoreType.DMA]),
        out_shape=jax.ShapeDtypeStruct((R_ROWS, D), F32),
        compiler_params=_cparams(("arbitrary",)),
        name="moe_dispatch",
    )(pos, pad, h32)


def _moe_ffn_kernel(te_ref, nu_ref, xs_ref, wg_ref, wu_ref, wd_ref, ys_ref, wg_b, wu_b, wd_b):
    i = pl.program_id(0)

    @pl.when((i == 0) | (te_ref[i] != te_ref[jnp.maximum(i - 1, 0)]))
    def _():
        wg_b[...] = wg_ref[...].astype(BF16)
        wu_b[...] = wu_ref[...].astype(BF16)
        wd_b[...] = wd_ref[...].astype(BF16)

    @pl.when(i < nu_ref[0])
    def _():
        x = xs_ref[...].astype(BF16)
        a = _silu(_dot(x, wg_b[...])) * _dot(x, wu_b[...])
        ys_ref[...] = _dot(a.astype(BF16), wd_b[...])


def _moe_ffn(tile_e, n_used, xs, wg, wu, wd):
    f = wg.shape[2]
    row = lambda i, te, nu: (jnp.minimum(i, nu[0] - 1), 0)
    return pl.pallas_call(
        _moe_ffn_kernel,
        grid_spec=pltpu.PrefetchScalarGridSpec(
            num_scalar_prefetch=2,
            grid=(N_TE,),
            in_specs=[pl.BlockSpec((TE, D), row),
                      pl.BlockSpec((None, D, f), lambda i, te, nu: (te[i], 0, 0)),
                      pl.BlockSpec((None, D, f), lambda i, te, nu: (te[i], 0, 0)),
                      pl.BlockSpec((None, f, D), lambda i, te, nu: (te[i], 0, 0))],
            out_specs=pl.BlockSpec((TE, D), row),
            scratch_shapes=[pltpu.VMEM((D, f), BF16), pltpu.VMEM((D, f), BF16), pltpu.VMEM((f, D), BF16)]),
        out_shape=jax.ShapeDtypeStruct((R_ROWS, D), F32),
        compiler_params=_cparams(("arbitrary",)),
        name="moe_ffn",
    )(tile_e, n_used, xs, wg, wu, wd)


def _combine_kernel(*refs, has_next):
    pos_ref, ys_ref, route_ref = refs[:3]
    ybuf, sem = refs[-2:]
    refs = refs[3:-2]
    i = pl.program_id(0)
    slot = i % 2

    def gather(tile, s):
        base = tile * TM

        def issue(r, c):
            _row_copy(ys_ref, pos_ref[base + r], ybuf.at[s, 0], r, sem.at[s]).start()
            _row_copy(ys_ref, pos_ref[T_ROWS + base + r], ybuf.at[s, 1], r, sem.at[s]).start()
            return c

        lax.fori_loop(0, TM, issue, 0, unroll=8)

    @pl.when(i == 0)
    def _():
        gather(0, 0)

    @pl.when(i + 1 < N_T)
    def _():
        gather(i + 1, 1 - slot)

    for c in range(2):
        pltpu.make_async_copy(ys_ref.at[pl.ds(0, TM), :], ybuf.at[slot, c], sem.at[slot]).wait()
    route = route_ref[...]
    y = route[:, 2:3] * ybuf[slot, 0] + route[:, 3:4] * ybuf[slot, 1]
    _epilogue(y, refs[0], refs[1], refs[2], refs[3:], has_next, False)


def _combine(pos, ys, route, x, gv, mv, has_next=True):
    e_in, out_specs, out_shape = _epilogue_specs(has_next, False)
    return pl.pallas_call(
        functools.partial(_combine_kernel, has_next=has_next),
        grid_spec=pltpu.PrefetchScalarGridSpec(
            num_scalar_prefetch=1,
            grid=(N_T,),
            in_specs=[pl.BlockSpec(memory_space=pl.ANY),
                      pl.BlockSpec((TM, LANES), lambda i, *_: (i, 0))] + e_in,
            out_specs=out_specs,
            scratch_shapes=[pltpu.VMEM((2, 2, TM, D), F32), pltpu.SemaphoreType.DMA((2,))]),
        out_shape=out_shape,
        compiler_params=_cparams(("arbitrary",)),
        name="moe_combine",
    )(pos, ys, route, x, gv, mv)


def _mla_down_kernel(h_ref, w_ref, gq_ref, gkv_ref, rc_ref, rlo_ref, rhi_ref,
                     cq_ref, ckv_ref, kr_ref, kvb_ref):
    i = pl.program_id(0)
    y = _dot(h_ref[...], w_ref[...])
    cq_ref[...] = _rms(y[:, :Q_RANK], gq_ref[...]).astype(BF16)
    ckv = _rms(y[:, Q_RANK:Q_RANK + KV_RANK], gkv_ref[...])
    kr = y[:, Q_RANK + KV_RANK:]
    kr = jnp.where(i >= NP_T, _rope(kr, rc_ref[...], rlo_ref[...], rhi_ref[...]), kr)
    ckv_ref[...] = ckv
    kr_ref[...] = kr
    kvb_ref[:, :KV_RANK] = ckv.astype(BF16)
    lane = lax.broadcasted_iota(jnp.int32, kr.shape, 1)
    kvb_ref[:, KV_RANK:] = jnp.where(lane == ROPE_DIM, 1.0, kr).astype(BF16)


def _mla_down(h, w, gq, gkv, tabs):
    n = Q_RANK + KVW
    row = lambda i: (i, 0)
    const = lambda i: (0, 0)
    tab = pl.BlockSpec((TM, LANES), lambda i: (_pos_tile(i), 0))
    return pl.pallas_call(
        _mla_down_kernel,
        grid=(N_T,),
        in_specs=[pl.BlockSpec((TM, D), row), pl.BlockSpec((D, n), const),
                  pl.BlockSpec((1, Q_RANK), const), pl.BlockSpec((1, KV_RANK), const),
                  tab, tab, tab],
        out_specs=[pl.BlockSpec((TM, Q_RANK), row), pl.BlockSpec((TM, KV_RANK), row),
                   pl.BlockSpec((TM, LANES), row), pl.BlockSpec((TM, KVW), row)],
        out_shape=[jax.ShapeDtypeStruct((T_ROWS, Q_RANK), BF16),
                   jax.ShapeDtypeStruct((T_ROWS, KV_RANK), F32),
                   jax.ShapeDtypeStruct((T_ROWS, LANES), F32),
                   jax.ShapeDtypeStruct((T_ROWS, KVW), BF16)],
        compiler_params=_cparams(("parallel",)),
        name="mla_down",
    )(h, w, gq, gkv, *tabs)


def _dot_nt(a, b):
    return lax.dot_general(a, b, (((1,), (1,)), ((), ())), preferred_element_type=F32)


def _mla_q_kernel(cq_ref, w_ref, cr_ref, sr_ref, cc_ref, sc_ref, q_ref):
    i = pl.program_id(0)
    y = _dot_nt(w_ref[...], cq_ref[...]) * ATTN_C2
    is_latent = i >= NP_T
    cr = jnp.where(is_latent, cr_ref[...], 1.0)
    sr = jnp.where(is_latent, sr_ref[...], 0.0)
    cc = jnp.where(is_latent, cc_ref[...], 1.0)
    sc = jnp.where(is_latent, sc_ref[...], 0.0)
    for hd in range(N_HEADS):
        r0 = hd * HEAD_PAD
        x0, x1, x2, x3 = (y[r0 + 8 * a:r0 + 8 * (a + 1), :] for a in range(4))
        rot = jnp.concatenate([x0 * cr - x1 * sr, x0 * sr + x1 * cr,
                               x2 * cc - x3 * sc, x2 * sc + x3 * cc], axis=0)
        q_ref[r0:r0 + ROPE_DIM, :] = rot.astype(BF16)
        q_ref[r0 + ROPE_DIM:r0 + HEAD_PAD, :] = y[r0 + ROPE_DIM:r0 + HEAD_PAD, :].astype(BF16)


def _mla_q(cq, w_t, tabs_t):
    tab = pl.BlockSpec((8, TM), lambda i: (0, _pos_tile(i)))
    return pl.pallas_call(
        _mla_q_kernel,
        grid=(N_T,),
        in_specs=[pl.BlockSpec((TM, Q_RANK), lambda i: (i, 0)),
                  pl.BlockSpec((HP, Q_RANK), lambda i: (0, 0)), tab, tab, tab, tab],
        out_specs=pl.BlockSpec((HP, TM), lambda i: (0, i)),
        out_shape=jax.ShapeDtypeStruct((HP, T_ROWS), BF16),
        compiler_params=_cparams(("parallel",)),
        name="mla_q",
    )(cq, w_t, *tabs_t)


def _mla_kv_kernel(c_ref, wk_ref, wvt_ref, k_ref, vt_ref):
    c = c_ref[...]
    k_ref[...] = _dot(c, wk_ref[...]).astype(BF16)
    vt_ref[...] = _dot_nt(wvt_ref[...], c).astype(BF16)


def _mla_kv(ckvkr, w_k, w_vt):
    rows = ckvkr.shape[0]
    hv = HVX
    return pl.pallas_call(
        _mla_kv_kernel,
        grid=(rows // TM,),
        in_specs=[pl.BlockSpec((TM, KVW), lambda i: (i, 0)),
                  pl.BlockSpec((KVW, HP), lambda i: (0, 0)),
                  pl.BlockSpec((hv, KVW), lambda i: (0, 0))],
        out_specs=[pl.BlockSpec((TM, HP), lambda i: (i, 0)), pl.BlockSpec((hv, TM), lambda i: (0, i))],
        out_shape=[jax.ShapeDtypeStruct((rows, HP), BF16), jax.ShapeDtypeStruct((hv, rows), BF16)],
        compiler_params=_cparams(("parallel",)),
        name="mla_kv",
    )(ckvkr, w_k, w_vt)


def _attn_kernel(*refs, nk, aliased):
    if aliased:
        refs = refs[1:]
    qt_ref, k_ref, vt_ref, o_ref, m_scr, acc_scr, ot_scr = refs
    ki = pl.program_id(2)

    @pl.when(ki == 0)
    def _():
        m_scr[...] = jnp.full(m_scr.shape, -jnp.inf, F32)
        acc_scr[...] = jnp.zeros(acc_scr.shape, F32)

    tk, tq = k_ref.shape[0], qt_ref.shape[1]
    vg = V_EXT // SUBLANES

    def across_sublanes(x, op):
        for k in (4, 2, 1):
            x = op(x, pltpu.roll(x, k, 0))
        return x

    def scores(hd):
        qs = slice(hd * HEAD_PAD, (hd + 1) * HEAD_PAD)
        return _dot(k_ref[:, qs], qt_ref[qs, :]).reshape(tk // SUBLANES, SUBLANES, tq)

    s_next = scores(0)
    for hd in range(N_HEADS):
        vs = slice(hd * V_EXT, (hd + 1) * V_EXT)
        s = s_next
        if hd + 1 < N_HEADS:
            s_next = scores(hd + 1)
        m_prev = m_scr[hd]
        m_new = jnp.maximum(m_prev, across_sublanes(jnp.max(s, axis=0), jnp.maximum))
        alpha = jnp.exp2(m_prev - m_new)
        p = jnp.exp2(s - m_new[None])
        pv = _dot(vt_ref[vs, :], p.reshape(tk, tq).astype(BF16))
        acc = acc_scr[vs, :].reshape(vg, SUBLANES, tq)
        acc_scr[vs, :] = (alpha[None] * acc).reshape(V_EXT, tq) + pv
        m_scr[hd] = m_new

    @pl.when(ki == nk - 1)
    def _():
        for hd in range(N_HEADS):
            r0 = hd * V_EXT
            den = across_sublanes(acc_scr[r0 + V_DIM:r0 + V_DIM + SUBLANES, :], jnp.add)
            acc = acc_scr[r0:r0 + V_DIM, :].reshape(V_DIM // SUBLANES, SUBLANES, tq)
            ot_scr[hd * V_DIM:(hd + 1) * V_DIM, :] = (acc / den[None]).reshape(V_DIM, tq)
        o_ref[...] = ot_scr[...].T.astype(BF16)


def _attention(qt, k, vt, n_b, tq, tk, lq, lk, q_row0, prev_out=None):
    nq, nk = lq // tq, lk // tk
    qb0 = q_row0 // tq
    hv = N_HEADS * V_DIM
    aliased = prev_out is not None
    in_specs = [pl.BlockSpec((HP, tq), lambda b, qi, ki: (0, qb0 + b * nq + qi)),
                pl.BlockSpec((tk, HP), lambda b, qi, ki: (b * nk + ki, 0)),
                pl.BlockSpec((HVX, tk), lambda b, qi, ki: (0, b * nk + ki))]
    args = [qt, k, vt]
    if aliased:
        in_specs = [pl.BlockSpec(memory_space=pl.ANY)] + in_specs
        args = [prev_out] + args
    return pl.pallas_call(
        functools.partial(_attn_kernel, nk=nk, aliased=aliased),
        grid=(n_b, nq, nk),
        in_specs=in_specs,
        out_specs=pl.BlockSpec((tq, hv), lambda b, qi, ki: (qb0 + b * nq + qi, 0)),
        out_shape=jax.ShapeDtypeStruct((T_ROWS, hv), BF16),
        scratch_shapes=[pltpu.VMEM((N_HEADS, SUBLANES, tq), F32), pltpu.VMEM((HVX, tq), F32),
                        pltpu.VMEM((hv, tq), F32)],
        input_output_aliases={0: 0} if aliased else {},
        compiler_params=_cparams(("parallel", "parallel", "arbitrary")),
        name="attn_latent" if aliased else "attn_context",
    )(*args)


def _rg_in_kernel(h_ref, w_ref, u_ref, g_ref):
    y = _dot(h_ref[...], w_ref[...])
    u_ref[...] = y[:, :D]
    g_ref[...] = _gelu_tanh(y[:, D:])


def _rg_in(h, w):
    return pl.pallas_call(
        _rg_in_kernel,
        grid=(N_T,),
        in_specs=[pl.BlockSpec((TM, D), lambda i: (i, 0)), pl.BlockSpec((D, 2 * D), lambda i: (0, 0))],
        out_specs=[pl.BlockSpec((TM, D), lambda i: (i, 0))] * 2,
        out_shape=[jax.ShapeDtypeStruct((T_ROWS, D), F32)] * 2,
        compiler_params=_cparams(("parallel",)),
        name="rg_in",
    )(h, w)


RG_TC = 256
RG_GROUP = 256


def _rg_scan_kernel(*refs, nj, seq_len, aliased):
    if aliased:
        refs = refs[2:]
    (uf_ref, ufp_ref, ufn_ref, ub_ref, ubp_ref, ubn_ref, cw_ref, cb_ref, wai_ref, bai_ref,
     lam_ref, h0_ref, yf_ref, yb_ref, fin_ref, carry_ref) = refs
    j = pl.program_id(1)
    tc = RG_TC
    r = lax.broadcasted_iota(jnp.int32, (tc, 1), 0)
    sub = r & (SUBLANES - 1)

    @pl.when(j == 0)
    def _():
        carry_ref[0:2, :] = h0_ref[...]

    def gates(m_ref, p_ref, n_ref, d, chunk):
        m = m_ref[...]
        pos = chunk * tc + r
        u = (cw_ref[0:1, :] * _shifted(m, p_ref, n_ref, -1, pos, seq_len, r)
             + cw_ref[1:2, :] * m
             + cw_ref[2:3, :] * _shifted(m, p_ref, n_ref, 1, pos, seq_len, r)
             + cw_ref[3:4, :] * _shifted(m, p_ref, n_ref, 2, pos, seq_len, r)
             + cb_ref[...])
        ub = u.astype(BF16)
        ra, ri = [], []
        for q in range(D // RG_GROUP):
            y = _dot(ub[:, q * RG_GROUP:(q + 1) * RG_GROUP], wai_ref[d, q])
            ra.append(y[:, :RG_GROUP])
            ri.append(y[:, RG_GROUP:])
        rr = jax.nn.sigmoid(jnp.concatenate(ra, axis=1) + bai_ref[d, 0:1, :])
        ii = jax.nn.sigmoid(jnp.concatenate(ri, axis=1) + bai_ref[d, 1:2, :])
        nl = -lam_ref[d:d + 1, :]
        softplus = jnp.maximum(nl, 0.0) + jnp.log1p(jnp.exp(-jnp.abs(nl)))
        log_a = (-RG_C * softplus) * rr
        a = jnp.exp(log_a)
        bx = jnp.sqrt(jnp.maximum(-jnp.tanh(log_a) * (a * a + 1.0), 0.0)) * (ii * u)
        return a, bx

    a, b = gates(uf_ref, ufp_ref, ufn_ref, 0, j)
    for k in (1, 2, 4):
        ok = sub >= k
        b = jnp.where(ok, a * pltpu.roll(b, k, 0) + b, b)
        a = jnp.where(ok, a * pltpu.roll(a, k, 0), a)
    h = carry_ref[0:1, :]
    for g in range(tc // SUBLANES):
        sl = slice(g * SUBLANES, (g + 1) * SUBLANES)
        hg = a[sl] * h + b[sl]
        yf_ref[sl, :] = hg
        h = hg[SUBLANES - 1:SUBLANES, :]
    carry_ref[0:1, :] = h

    a, b = gates(ub_ref, ubp_ref, ubn_ref, 1, nj - 1 - j)
    for k in (1, 2, 4):
        ok = sub < SUBLANES - k
        b = jnp.where(ok, a * pltpu.roll(b, tc - k, 0) + b, b)
        a = jnp.where(ok, a * pltpu.roll(a, tc - k, 0), a)
    h = carry_ref[1:2, :]
    for g in reversed(range(tc // SUBLANES)):
        sl = slice(g * SUBLANES, (g + 1) * SUBLANES)
        hg = a[sl] * h + b[sl]
        yb_ref[sl, :] = hg
        h = hg[0:1, :]
    carry_ref[1:2, :] = h

    @pl.when(j == nj - 1)
    def _():
        fin_ref[...] = carry_ref[0:2, :]


def _rg_scan(u, cw, cb, wai, bai, lam, h0, n_seq, seq_len, row0, prev=None):
    nj = seq_len // RG_TC
    b0 = row0 // RG_TC
    aliased = prev is not None
    fwd = lambda s, j: b0 + s * nj + j
    bwd = lambda s, j: b0 + s * nj + (nj - 1 - j)
    const2 = lambda s, j: (0, 0)
    in_specs = (_halo_specs(RG_TC, fwd) + _halo_specs(RG_TC, bwd)
                + [pl.BlockSpec((8, D), const2), pl.BlockSpec((1, D), const2),
                   pl.BlockSpec((2, D // RG_GROUP, RG_GROUP, 2 * RG_GROUP), lambda s, j: (0, 0, 0, 0)),
                   pl.BlockSpec((2, 2, D), lambda s, j: (0, 0, 0)),
                   pl.BlockSpec((2, D), const2),
                   pl.BlockSpec((None, 2, D), lambda s, j: (s, 0, 0))])
    args = [u, u, u, u, u, u, cw, cb, wai, bai, lam, h0]
    if aliased:
        in_specs = [pl.BlockSpec(memory_space=pl.ANY)] * 2 + in_specs
        args = list(prev) + args
    return pl.pallas_call(
        functools.partial(_rg_scan_kernel, nj=nj, seq_len=seq_len, aliased=aliased),
        grid=(n_seq, nj),
        in_specs=in_specs,
        out_specs=[pl.BlockSpec((RG_TC, D), lambda s, j: (fwd(s, j), 0)),
                   pl.BlockSpec((RG_TC, D), lambda s, j: (bwd(s, j), 0)),
                   pl.BlockSpec((None, 2, D), lambda s, j: (s, 0, 0))],
        out_shape=[jax.ShapeDtypeStruct((T_ROWS, D), F32), jax.ShapeDtypeStruct((T_ROWS, D), F32),
                   jax.ShapeDtypeStruct((n_seq, 2, D), F32)],
        scratch_shapes=[pltpu.VMEM((8, D), F32)],
        input_output_aliases={0: 0, 1: 1} if aliased else {},
        compiler_params=_cparams(("parallel", "arbitrary")),
        name="rg_scan_latent" if aliased else "rg_scan_context",
    )(*args)


def _sc_in_kernel(h_ref, w_ref, bg_ref, m_ref):
    y = _dot(h_ref[...], w_ref[...])
    bg_ref[...] = y[:, :D]
    m_ref[...] = y[:, D:2 * D] * y[:, 2 * D:]


def _sc_in(h, w):
    return pl.pallas_call(
        _sc_in_kernel,
        grid=(N_T,),
        in_specs=[pl.BlockSpec((TM, D), lambda i: (i, 0)), pl.BlockSpec((D, 3 * D), lambda i: (0, 0))],
        out_specs=[pl.BlockSpec((TM, D), lambda i: (i, 0))] * 2,
        out_shape=[jax.ShapeDtypeStruct((T_ROWS, D), F32)] * 2,
        compiler_params=_cparams(("parallel",)),
        name="sc_in",
    )(h, w)


def _rope_tables():
    t = np.arange(DEC_SEQ)
    inv = ROPE_THETA ** (-jnp.arange(0, AXIS_DIM, 2, dtype=F32) / AXIS_DIM)
    ang_r = jnp.asarray((t // GRID_W).astype(np.float32))[:, None] * inv
    ang_c = jnp.asarray((t % GRID_W).astype(np.float32))[:, None] * inv
    cr, sr, cc, sc = jnp.cos(ang_r), jnp.sin(ang_r), jnp.cos(ang_c), jnp.sin(ang_c)
    z8 = jnp.zeros((DEC_SEQ, 8), F32)
    pad1 = jnp.ones((DEC_SEQ, LANES - ROPE_DIM), F32)
    pad0 = jnp.zeros((DEC_SEQ, LANES - ROPE_DIM), F32)
    c = jnp.concatenate([cr, cr, cc, cc, pad1], axis=1)
    lo = jnp.concatenate([-sr, z8, -sc, z8, pad0], axis=1)
    hi = jnp.concatenate([z8, sr, z8, sc, pad0], axis=1)
    return (c, lo, hi), (cr.T, sr.T, cc.T, sc.T)


def _pad_heads(w, lo):
    r, h, d = w.shape
    return jnp.pad(w, ((0, 0), (0, 0), (lo, HEAD_PAD - lo - d))).reshape(r, h * HEAD_PAD)


def _mla_weights(w_dq, w_uq, w_dkv, w_uk, w_uv, w_o):
    w_down = jnp.concatenate([w_dq, w_dkv, jnp.zeros((D, KVW - KV_RANK - ROPE_DIM), F32)], axis=1)
    w_q = _pad_heads(w_uq[:, :, QK_NOPE:], 0) + _pad_heads(w_uq[:, :, :QK_NOPE], HEAD_PAD - QK_NOPE)
    place = jnp.broadcast_to(jnp.eye(ROPE_DIM, dtype=F32)[:, None, :], (ROPE_DIM, N_HEADS, ROPE_DIM))
    w_k = jnp.concatenate([_pad_heads(w_uk, HEAD_PAD - QK_NOPE), _pad_heads(place, 0),
                           jnp.zeros((KVW - KV_RANK - ROPE_DIM, HP), F32)], axis=0)
    w_vt = jnp.pad(jnp.transpose(w_uv, (1, 2, 0)), ((0, 0), (0, V_EXT - V_DIM), (0, KVW - KV_RANK)))
    w_vt = w_vt.at[:, V_DIM, ONE_COL].set(1.0).reshape(HVX, KVW)
    return (w_down.astype(BF16), w_q.T.astype(BF16), w_k.astype(BF16), w_vt.astype(BF16),
            w_o.astype(BF16))


def _block_diag_groups(w_a, w_i):
    per = RG_GROUP // RG_BLOCK

    def bd(w):
        w = w.reshape(2, D // RG_GROUP, per, RG_BLOCK, RG_BLOCK)
        eye = jnp.eye(per, dtype=F32)
        return jnp.einsum('dgpkj,pq->dgpkqj', w, eye).reshape(2, D // RG_GROUP, RG_GROUP, RG_GROUP)

    return jnp.concatenate([bd(w_a), bd(w_i)], axis=-1).astype(BF16)


def _pad_rows(w, rows=8):
    return jnp.pad(w, ((0, rows - w.shape[0]), (0, 0)))


def kernel(x_prompt, x_sample, cache_mla_ckv, cache_mla_krope, state_rglru, c, c_ctx, mod_w, mod_b, norm_g, mla_w_dq, mla_g_q, mla_w_uq, mla_w_dkv, mla_g_kv, mla_w_uk, mla_w_uv, mla_w_o, rg_w_x, rg_w_y, rg_conv_w, rg_conv_b, rg_w_a, rg_b_a, rg_w_i, rg_b_i, rg_lambda, rg_w_out, sc_w_in, sc_conv_w, sc_w_out, ffn_w_gate, ffn_w_up, ffn_w_down, moe_w_router, moe_b_router, moe_w_gate, moe_w_up, moe_w_down):
    cond8 = jnp.concatenate([c_ctx[None], c, jnp.zeros((N_GROUPS - 1 - DEC_BATCH, D), F32)], axis=0)
    mod = _modulation(cond8, mod_w, mod_b).reshape(DEPTH, N_GROUPS, 6, D)
    tabs, tabs_t = _rope_tables()
    zrow = jnp.zeros((N_GROUPS, 5, D), F32)

    def vecs(l, sub):
        if sub == 0:
            g_next, sh, sc = norm_g[l, 2], mod[l, :, 3], mod[l, :, 4]
        elif l + 1 < DEPTH:
            g_next, sh, sc = norm_g[l + 1, 0], mod[l + 1, :, 0], mod[l + 1, :, 1]
        else:
            g_next, sh, sc = jnp.zeros((D,), F32), zrow[:, 0], zrow[:, 0]
        gv = _pad_rows(jnp.stack([norm_g[l, 1 + 2 * sub], g_next]))
        mv = jnp.concatenate([jnp.stack([mod[l, :, 2 + 3 * sub], sh, sc], axis=1), zrow], axis=1)
        return gv, mv

    gv0 = _pad_rows(jnp.stack([jnp.zeros((D,), F32), norm_g[0, 0]]))
    mv0 = jnp.concatenate([jnp.stack([zrow[:, 0], mod[0, :, 0], mod[0, :, 1]], axis=1), zrow], axis=1)
    x, h = _pre0(x_prompt.reshape(P_ROWS, D), x_sample.reshape(S_ROWS, D), gv0, mv0)

    new_ckv, new_krope, new_rg = [], [], []
    for l in range(DEPTH):
        kind, j = l % 3, l // 3
        gv, mv = vecs(l, 0)
        router = None
        if l % 2 == 1:
            m = l // 2
            w_r = jnp.pad(moe_w_router[m], ((0, 0), (0, LANES - N_EXPERTS)))
            w_r_hi = w_r.astype(BF16)
            router = (w_r_hi, (w_r - w_r_hi.astype(F32)).astype(BF16),
                      jnp.pad(moe_b_router[m], (0, LANES - N_EXPERTS)).reshape(1, LANES))
        if kind == 0:
            w_down, w_qt, w_k, w_vt, w_op = _mla_weights(mla_w_dq[j], mla_w_uq[j], mla_w_dkv[j],
                                                         mla_w_uk[j], mla_w_uv[j], mla_w_o[j])
            cq, ckv, kr, kvb = _mla_down(h, w_down, mla_g_q[j].reshape(1, Q_RANK),
                                         mla_g_kv[j].reshape(1, KV_RANK), tabs)
            q = _mla_q(cq, w_qt, tabs_t)
            new_ckv.append(ckv[:P_ROWS].reshape(BATCH, SEQ, KV_RANK))
            new_krope.append(kr[:P_ROWS, :ROPE_DIM].reshape(BATCH, SEQ, ROPE_DIM))
            cache = jnp.concatenate(
                [cache_mla_ckv[:, j], cache_mla_krope[:, j], jnp.ones((DEC_BATCH, PAST, 1), F32),
                 jnp.zeros((DEC_BATCH, PAST, KVW - ONE_COL - 1), F32)], axis=-1).astype(BF16)
            kv_lat = jnp.concatenate([cache, kvb[P_ROWS:].reshape(DEC_BATCH, DEC_SEQ, KVW)], axis=1)
            k_c, v_c = _mla_kv(kvb[:P_ROWS], w_k, w_vt)
            k_l, v_l = _mla_kv(kv_lat.reshape(DEC_BATCH * (PAST + DEC_SEQ), KVW), w_k, w_vt)
            o = _attention(q, k_c, v_c, BATCH, SEQ, SEQ, SEQ, SEQ, 0)
            o = _attention(q, k_l, v_l, DEC_BATCH, 512, 512, DEC_SEQ, PAST + DEC_SEQ, P_ROWS, prev_out=o)
            outs = _proj("plain", (o,), w_op, x, gv, mv, router=router)
        elif kind == 1:
            w_xy = jnp.concatenate([rg_w_x[j], rg_w_y[j]], axis=1).astype(BF16)
            u, gate = _rg_in(h, w_xy)
            wai = _block_diag_groups(rg_w_a[j], rg_w_i[j])
            bai = jnp.stack([rg_b_a[j], rg_b_i[j]], axis=1)
            scan_args = (_pad_rows(rg_conv_w[j]), rg_conv_b[j].reshape(1, D), wai, bai, rg_lambda[j])
            yf, yb, fin = _rg_scan(u, *scan_args, jnp.zeros((BATCH, 2, D), F32), BATCH, SEQ, 0)
            yf, yb, _ = _rg_scan(u, *scan_args, state_rglru[:, j], DEC_BATCH, DEC_SEQ, P_ROWS,
                                 prev=(yf, yb))
            new_rg.append(fin)
            outs = _proj("rg", (yf, yb, gate), rg_w_out[j].astype(BF16), x, gv, mv, router=router)
        else:
            bg, mm = _sc_in(h, sc_w_in[j].astype(BF16))
            outs = _proj("sc", (bg, mm, _pad_rows(sc_conv_w[j])), sc_w_out[j].astype(BF16), x, gv, mv,
                         router=router)
        x, h = outs[0], outs[1]

        gv, mv = vecs(l, 1)
        has_next = l + 1 < DEPTH
        m = l // 2
        if l % 2 == 0:
            half = D_FF // 2
            wg = ffn_w_gate[m].reshape(D, 2, half).transpose(1, 0, 2).astype(BF16)
            wu = ffn_w_up[m].reshape(D, 2, half).transpose(1, 0, 2).astype(BF16)
            wd = ffn_w_down[m].reshape(2, half, D).astype(BF16)
            outs = _ffn(h, wg, wu, wd, x, gv, mv, has_next=has_next)
        else:
            route = outs[2]
            pos, pad, tile_e, n_used = _route_meta(route)
            xs = _dispatch(pos, pad, h)
            ys = _moe_ffn(tile_e, n_used, xs, moe_w_gate[m], moe_w_up[m], moe_w_down[m])
            outs = _combine(pos, ys, route, x, gv, mv, has_next=has_next)
        x, h = outs[0], outs[1]

    y_prompt = x.reshape(BATCH, SEQ, D)
    y_sample = h.reshape(DEC_BATCH, DEC_SEQ, D)
    return (y_prompt, y_sample, jnp.stack(new_ckv, axis=1), jnp.stack(new_krope, axis=1),
            jnp.stack(new_rg, axis=1))
```

```python
import functools

import numpy as np
import jax
import jax.numpy as jnp
from jax import lax
from jax.experimental import pallas as pl
from jax.experimental.pallas import tpu as pltpu

F32 = jnp.float32
BF16 = jnp.bfloat16

D = 1024
BATCH = 32
SEQ = 256
DEPTH = 4
DEC_BATCH = 4
DEC_SEQ = 4096
PAST = 512
GRID_W = 64
N_HEADS = 16
QK_NOPE = 64
ROPE_DIM = 32
AXIS_DIM = 16
V_DIM = 64
Q_RANK = 384
KV_RANK = 256
ROPE_THETA = 10000.0
ATTN_SCALE = (QK_NOPE + ROPE_DIM) ** -0.5
RG_BLOCKS = 16
RG_BLOCK = 64
RG_C = 8.0
D_FF = 2816
N_EXPERTS = 8
D_FF_EXPERT = 1408
EPS = 1e-6

P_ROWS = BATCH * SEQ
S_ROWS = DEC_BATCH * DEC_SEQ
T_ROWS = P_ROWS + S_ROWS
N_GROUPS = 8
HEAD_PAD = 128
HP = N_HEADS * HEAD_PAD
KVW = 384
ONE_COL = KV_RANK + ROPE_DIM
V_EXT = 80
HVX = N_HEADS * V_EXT
ATTN_C2 = ATTN_SCALE * float(np.log2(np.e))

TM = 512
NP_T = P_ROWS // TM
NS_T = DEC_SEQ // TM
N_T = T_ROWS // TM
SUBLANES = 8
LANES = 128
VMEM_LIMIT = 56 * 1024 * 1024


def _cparams(sem):
    return pltpu.CompilerParams(dimension_semantics=sem, vmem_limit_bytes=VMEM_LIMIT)


def _group_of_tile(i):
    return jnp.maximum(i - NP_T + NS_T, 0) // NS_T


def _pos_tile(i):
    return jnp.maximum(i - NP_T, 0) % NS_T


def _dot(a, b):
    return jnp.dot(a, b, preferred_element_type=F32)


def _rms(x, g):
    ms = jnp.mean(x * x, axis=-1, keepdims=True)
    return x * lax.rsqrt(ms + EPS) * g


def _silu(x):
    return x * jax.nn.sigmoid(x)


def _gelu_tanh(x):
    return x * (0.5 * (1.0 + jnp.tanh(np.sqrt(2.0 / np.pi).astype(np.float32)
                                      * (x + 0.044715 * (x * x * x)))))


def _rope(x, c, s_lo, s_hi):
    return (x * c + pltpu.roll(x, LANES - 8, 1) * s_lo + pltpu.roll(x, 8, 1) * s_hi)


def _mod_kernel(c_ref, w_ref, b_ref, o_ref):
    c = c_ref[...]
    s = _silu(c).astype(BF16)
    o_ref[...] = _dot(s, w_ref[...].astype(BF16)) + b_ref[...]


def _modulation(cond8, mod_w, mod_b):
    tn = 1536
    return pl.pallas_call(
        _mod_kernel,
        grid=(DEPTH, 6 * D // tn),
        in_specs=[pl.BlockSpec((N_GROUPS, D), lambda l, n: (0, 0)),
                  pl.BlockSpec((None, D, tn), lambda l, n: (l, 0, n)),
                  pl.BlockSpec((None, 1, tn), lambda l, n: (l, 0, n))],
        out_specs=pl.BlockSpec((None, N_GROUPS, tn), lambda l, n: (l, 0, n)),
        out_shape=jax.ShapeDtypeStruct((DEPTH, N_GROUPS, 6 * D), F32),
        compiler_params=_cparams(("parallel", "parallel")),
        name="modulation",
    )(cond8, mod_w, mod_b.reshape(DEPTH, 1, 6 * D))


def _ctx_block(i):
    return jnp.minimum(i, NP_T - 1)


def _lat_block(i):
    return jnp.maximum(i - NP_T, 0)


def _pre0_kernel(xp_ref, xs_ref, gv_ref, mv_ref, x_ref, h_ref):
    x = jnp.where(pl.program_id(0) < NP_T, xp_ref[...], xs_ref[...])
    x_ref[...] = x
    h = _rms(x, gv_ref[1:2, :]) * (1.0 + mv_ref[2:3, :]) + mv_ref[1:2, :]
    h_ref[...] = h.astype(BF16)


def _pre0(xp, xs, gv, mv):
    return pl.pallas_call(
        _pre0_kernel,
        grid=(N_T,),
        in_specs=[pl.BlockSpec((TM, D), lambda i: (_ctx_block(i), 0)),
                  pl.BlockSpec((TM, D), lambda i: (_lat_block(i), 0)),
                  pl.BlockSpec((8, D), lambda i: (0, 0)),
                  pl.BlockSpec((None, 8, D), lambda i: (_group_of_tile(i), 0, 0))],
        out_specs=[pl.BlockSpec((TM, D), lambda i: (i, 0))] * 2,
        out_shape=[jax.ShapeDtypeStruct((T_ROWS, D), F32), jax.ShapeDtypeStruct((T_ROWS, D), BF16)],
        compiler_params=_cparams(("arbitrary",)),
        name="pre0",
    )(xp, xs, gv, mv)


def _router(hn, wh_ref, wl_ref, br_ref):
    hh = hn.astype(BF16)
    hl = (hn - hh.astype(F32)).astype(BF16)
    logits = (_dot(hh, wh_ref[...]) + (_dot(hl, wh_ref[...]) + _dot(hh, wl_ref[...]))) + br_ref[...]
    lane = lax.broadcasted_iota(jnp.int32, logits.shape, 1)
    neg = jnp.float32(-jnp.inf)
    logits = jnp.where(lane < N_EXPERTS, logits, neg)
    m1 = jnp.max(logits, axis=-1, keepdims=True)
    i1 = jnp.min(jnp.where(logits == m1, lane, LANES), axis=-1, keepdims=True)
    rest = jnp.where(lane == i1, neg, logits)
    m2 = jnp.max(rest, axis=-1, keepdims=True)
    i2 = jnp.min(jnp.where(rest == m2, lane, LANES), axis=-1, keepdims=True)
    e = jnp.exp(m2 - m1)
    p1 = 1.0 / (1.0 + e)
    p2 = e / (1.0 + e)
    return jnp.where(lane == 0, i1.astype(F32),
                     jnp.where(lane == 1, i2.astype(F32),
                               jnp.where(lane == 2, p1, jnp.where(lane == 3, p2, 0.0))))


def _post_pre(x, out, gv_ref, mv_ref, has_next):
    xn = x + mv_ref[0:1, :] * _rms(out, gv_ref[0:1, :])
    if not has_next:
        return xn, None
    hn = _rms(xn, gv_ref[1:2, :]) * (1.0 + mv_ref[2:3, :]) + mv_ref[1:2, :]
    return xn, hn


def _epilogue(out, x_ref, gv_ref, mv_ref, rest, has_next, want_router):
    rest = list(rest)
    if want_router:
        router_refs = rest[:3]
        rest = rest[3:]
    xn, hn = _post_pre(x_ref[...], out, gv_ref, mv_ref, has_next)
    if not has_next:
        i = pl.program_id(0)

        @pl.when(i < NP_T)
        def _():
            rest[0][...] = xn

        @pl.when(i >= NP_T)
        def _():
            rest[1][...] = xn
        return
    rest[0][...] = xn
    rest[1][...] = hn.astype(rest[1].dtype)
    if want_router:
        rest[2][...] = _router(hn, *router_refs)


def _epilogue_specs(has_next, want_router):
    in_specs = [pl.BlockSpec((TM, D), lambda i, *_: (i, 0)),
                pl.BlockSpec((8, D), lambda i, *_: (0, 0)),
                pl.BlockSpec((None, 8, D), lambda i, *_: (_group_of_tile(i), 0, 0))]
    if not has_next:
        out_specs = [pl.BlockSpec((TM, D), lambda i, *_: (_ctx_block(i), 0)),
                     pl.BlockSpec((TM, D), lambda i, *_: (_lat_block(i), 0))]
        out_shape = [jax.ShapeDtypeStruct((P_ROWS, D), F32), jax.ShapeDtypeStruct((S_ROWS, D), F32)]
        return in_specs, out_specs, out_shape
    out_specs = [pl.BlockSpec((TM, D), lambda i, *_: (i, 0))]
    out_shape = [jax.ShapeDtypeStruct((T_ROWS, D), F32)]
    if want_router:
        in_specs += [pl.BlockSpec((D, LANES), lambda i, *_: (0, 0)),
                     pl.BlockSpec((D, LANES), lambda i, *_: (0, 0)),
                     pl.BlockSpec((1, LANES), lambda i, *_: (0, 0))]
    if has_next:
        out_specs.append(pl.BlockSpec((TM, D), lambda i, *_: (i, 0)))
        out_shape.append(jax.ShapeDtypeStruct((T_ROWS, D), F32 if want_router else BF16))
    if want_router:
        out_specs.append(pl.BlockSpec((TM, LANES), lambda i, *_: (i, 0)))
        out_shape.append(jax.ShapeDtypeStruct((T_ROWS, LANES), F32))
    return in_specs, out_specs, out_shape


def _seq_pos(i, rows):
    seq_len = jnp.where(i < NP_T, SEQ, DEC_SEQ)
    r = lax.broadcasted_iota(jnp.int32, (rows, 1), 0)
    return (i * rows + r) & (seq_len - 1), seq_len, r


def _shifted(m, prev_ref, next_ref, k, pos, seq_len, r):
    rows = m.shape[0]
    if k < 0:
        y = pltpu.roll(m, -k, 0)
        y = jnp.where(r == 0, prev_ref[SUBLANES - 1:SUBLANES, :], y)
        return jnp.where(pos + k < 0, 0.0, y)
    y = pltpu.roll(m, rows - k, 0)
    for q in range(k):
        y = jnp.where(r == rows - k + q, next_ref[q:q + 1, :], y)
    return jnp.where(pos + k >= seq_len, 0.0, y)


def _halo_specs(rows, row_block_of):
    per = rows // SUBLANES
    last = T_ROWS // SUBLANES - 1
    return [pl.BlockSpec((rows, D), lambda *g: (row_block_of(*g), 0)),
            pl.BlockSpec((SUBLANES, D), lambda *g: (jnp.maximum(row_block_of(*g) * per - 1, 0), 0)),
            pl.BlockSpec((SUBLANES, D), lambda *g: (jnp.minimum((row_block_of(*g) + 1) * per, last), 0))]


def _proj_kernel(*refs, mode, has_next, want_router):
    i = pl.program_id(0)
    if mode == "plain":
        a_ref, w_ref = refs[:2]
        rest = refs[2:]
        a = a_ref[...]
    elif mode == "rg":
        yf_ref, yb_ref, g_ref, w_ref = refs[:4]
        rest = refs[4:]
        a = ((yf_ref[...] + yb_ref[...]) * g_ref[...]).astype(BF16)
    else:
        bg_ref, m_ref, mp_ref, mn_ref, cw_ref, w_ref = refs[:6]
        rest = refs[6:]
        m = m_ref[...]
        pos, seq_len, r = _seq_pos(i, TM)
        z = (cw_ref[0:1, :] * _shifted(m, mp_ref, mn_ref, -1, pos, seq_len, r)
             + cw_ref[1:2, :] * m
             + cw_ref[2:3, :] * _shifted(m, mp_ref, mn_ref, 1, pos, seq_len, r))
        a = (bg_ref[...] * z).astype(BF16)
    out = _dot(a, w_ref[...])
    _epilogue(out, rest[0], rest[1], rest[2], rest[3:], has_next, want_router)


def _proj(mode, ins, w, x, gv, mv, has_next=True, router=None):
    want_router = router is not None
    k = w.shape[0]
    row = lambda i: (i, 0)
    if mode == "plain":
        in_specs = [pl.BlockSpec((TM, k), row)]
    elif mode == "rg":
        in_specs = [pl.BlockSpec((TM, D), row)] * 3
    else:
        bg, m, cw = ins
        ins = (bg, m, m, m, cw)
        in_specs = ([pl.BlockSpec((TM, D), row)] + _halo_specs(TM, lambda i: i)
                    + [pl.BlockSpec((8, D), lambda i: (0, 0))])
    in_specs.append(pl.BlockSpec((k, D), lambda i: (0, 0)))
    e_in, out_specs, out_shape = _epilogue_specs(has_next, want_router)
    args = list(ins) + [w, x, gv, mv] + (list(router) if want_router else [])
    return pl.pallas_call(
        functools.partial(_proj_kernel, mode=mode, has_next=has_next, want_router=want_router),
        grid=(N_T,),
        in_specs=in_specs + e_in,
        out_specs=out_specs,
        out_shape=out_shape,
        compiler_params=_cparams(("parallel",)),
        name="proj_" + mode,
    )(*args)


MXU_N = 256
FF_SPLIT = (D_FF // MXU_N + 1) // 2 * MXU_N


def _ffn_kernel(h_ref, wg_ref, wu_ref, wd_ref, *refs, has_next):
    h = h_ref[...]
    out = None
    for lo, hi in ((0, FF_SPLIT), (FF_SPLIT, D_FF)):
        a = _silu(_dot(h, wg_ref[:, lo:hi])) * _dot(h, wu_ref[:, lo:hi])
        part = _dot(a.astype(BF16), wd_ref[lo:hi, :])
        out = part if out is None else out + part
    _epilogue(out, refs[0], refs[1], refs[2], refs[3:], has_next, False)


def _ffn(h, wg, wu, wd, x, gv, mv, has_next=True):
    f = wg.shape[1]
    once = pl.Buffered(1)
    in_specs = [pl.BlockSpec((TM, D), lambda i: (i, 0)),
                pl.BlockSpec((D, f), lambda i: (0, 0), pipeline_mode=once),
                pl.BlockSpec((D, f), lambda i: (0, 0), pipeline_mode=once),
                pl.BlockSpec((f, D), lambda i: (0, 0), pipeline_mode=once)]
    e_in, out_specs, out_shape = _epilogue_specs(has_next, False)
    return pl.pallas_call(
        functools.partial(_ffn_kernel, has_next=has_next),
        grid=(N_T,),
        in_specs=in_specs + e_in,
        out_specs=out_specs,
        out_shape=out_shape,
        compiler_params=_cparams(("parallel",)),
        name="ffn_dense",
    )(h, wg, wu, wd, x, gv, mv)


TE = 256
R_ROWS = 2 * T_ROWS + N_EXPERTS * TE
N_TE = R_ROWS // TE


def _route_meta(route):
    e1 = route[:, 0].astype(jnp.int32)
    e2 = route[:, 1].astype(jnp.int32)
    ids = jnp.arange(N_EXPERTS, dtype=jnp.int32)
    hit = ((e1[:, None] == ids) | (e2[:, None] == ids)).astype(jnp.int32)
    csum = jnp.cumsum(hit, axis=0)
    counts = csum[-1]
    padded = (counts + TE - 1) // TE * TE
    ends = jnp.cumsum(padded)
    offs = ends - padded
    rank = csum - 1
    pos1 = offs[e1] + jnp.take_along_axis(rank, e1[:, None], axis=1)[:, 0]
    pos2 = offs[e2] + jnp.take_along_axis(rank, e2[:, None], axis=1)[:, 0]
    n_used = ends[-1] // TE
    tile_row = jnp.minimum(jnp.arange(N_TE, dtype=jnp.int32), n_used - 1) * TE
    tile_e = jnp.minimum(jnp.searchsorted(ends, tile_row, side='right'), N_EXPERTS - 1).astype(jnp.int32)
    pos = jnp.concatenate([pos1, pos2]).astype(jnp.int32)
    pad = jnp.concatenate([offs + counts, padded - counts]).astype(jnp.int32)
    return pos, pad, tile_e, n_used.astype(jnp.int32).reshape(1)


def _row_copy(src, s, dst, d, sem):
    return pltpu.make_async_copy(src.at[pl.ds(s, 1), :], dst.at[pl.ds(d, 1), :], sem)


def _dispatch_kernel(pos_ref, pad_ref, h_ref, xs_ref, zero_scr, sem):
    i = pl.program_id(0)
    base = i * TM

    for r in range(TM):
        _row_copy(h_ref, r, xs_ref, pos_ref[base + r], sem).start()
        _row_copy(h_ref, r, xs_ref, pos_ref[T_ROWS + base + r], sem).start()
    for _ in range(2):
        pltpu.make_async_copy(h_ref, xs_ref.at[pl.ds(0, TM), :], sem).wait()

    @pl.when(i == N_T - 1)
    def _():
        zero_scr[...] = jnp.zeros(zero_scr.shape, F32)
        for e in range(N_EXPERTS):
            start, n = pad_ref[e], pad_ref[N_EXPERTS + e]

            def fill(r, c):
                _row_copy(zero_scr, 0, xs_ref, start + r, sem).start()
                return c

            def drain(r, c):
                _row_copy(zero_scr, 0, xs_ref, start, sem).wait()
                return c

            lax.fori_loop(0, n, fill, 0)
            lax.fori_loop(0, n, drain, 0)


def _dispatch(pos, pad, h32):
    return pl.pallas_call(
        _dispatch_kernel,
        grid_spec=pltpu.PrefetchScalarGridSpec(
            num_scalar_prefetch=2,
            grid=(N_T,),
            in_specs=[pl.BlockSpec((TM, D), lambda i, *_: (i, 0))],
            out_specs=pl.BlockSpec(memory_space=pl.ANY),
            scratch_shapes=[pltpu.VMEM((SUBLANES, D), F32), pltpu.SemaphoreType.DMA]),
        out_shape=jax.ShapeDtypeStruct((R_ROWS, D), F32),
        compiler_params=_cparams(("arbitrary",)),
        name="moe_dispatch",
    )(pos, pad, h32)


def _moe_ffn_kernel(te_ref, nu_ref, xs_ref, wg_ref, wu_ref, wd_ref, ys_ref, wg_b, wu_b, wd_b):
    i = pl.program_id(0)

    @pl.when((i == 0) | (te_ref[i] != te_ref[jnp.maximum(i - 1, 0)]))
    def _():
        wg_b[...] = wg_ref[...].astype(BF16)
        wu_b[...] = wu_ref[...].astype(BF16)
        wd_b[...] = wd_ref[...].astype(BF16)

    @pl.when(i < nu_ref[0])
    def _():
        x = xs_ref[...].astype(BF16)
        a = _silu(_dot(x, wg_b[...])) * _dot(x, wu_b[...])
        ys_ref[...] = _dot(a.astype(BF16), wd_b[...])


def _moe_ffn(tile_e, n_used, xs, wg, wu, wd, m):
    f = wg.shape[3]
    row = lambda i, te, nu: (jnp.minimum(i, nu[0] - 1), 0)
    expert = lambda i, te, nu: (m, te[i], 0, 0)
    return pl.pallas_call(
        _moe_ffn_kernel,
        grid_spec=pltpu.PrefetchScalarGridSpec(
            num_scalar_prefetch=2,
            grid=(N_TE,),
            in_specs=[pl.BlockSpec((TE, D), row),
                      pl.BlockSpec((None, None, D, f), expert),
                      pl.BlockSpec((None, None, D, f), expert),
                      pl.BlockSpec((None, None, f, D), expert)],
            out_specs=pl.BlockSpec((TE, D), row),
            scratch_shapes=[pltpu.VMEM((D, f), BF16), pltpu.VMEM((D, f), BF16), pltpu.VMEM((f, D), BF16)]),
        out_shape=jax.ShapeDtypeStruct((R_ROWS, D), F32),
        compiler_params=_cparams(("arbitrary",)),
        name="moe_ffn",
    )(tile_e, n_used, xs, wg, wu, wd)


def _combine_kernel(*refs, has_next):
    pos_ref, ys_ref, route_ref = refs[:3]
    ybuf, sem = refs[-2:]
    refs = refs[3:-2]
    i = pl.program_id(0)
    slot = i % 2

    def issue(base, s, r):
        _row_copy(ys_ref, pos_ref[base + r], ybuf.at[s, 0], r, sem.at[s]).start()
        _row_copy(ys_ref, pos_ref[T_ROWS + base + r], ybuf.at[s, 1], r, sem.at[s]).start()

    @pl.when(i == 0)
    def _():
        lax.fori_loop(0, TM, lambda r, c: issue(0, 0, r), None)

    @pl.when(i + 1 < N_T)
    def _():
        for r in range(TM):
            issue((i + 1) * TM, 1 - slot, r)

    for c in range(2):
        pltpu.make_async_copy(ys_ref.at[pl.ds(0, TM), :], ybuf.at[slot, c], sem.at[slot]).wait()
    route = route_ref[...]
    y = route[:, 2:3] * ybuf[slot, 0] + route[:, 3:4] * ybuf[slot, 1]
    _epilogue(y, refs[0], refs[1], refs[2], refs[3:], has_next, False)


def _combine(pos, ys, route, x, gv, mv, has_next=True):
    e_in, out_specs, out_shape = _epilogue_specs(has_next, False)
    return pl.pallas_call(
        functools.partial(_combine_kernel, has_next=has_next),
        grid_spec=pltpu.PrefetchScalarGridSpec(
            num_scalar_prefetch=1,
            grid=(N_T,),
            in_specs=[pl.BlockSpec(memory_space=pl.ANY),
                      pl.BlockSpec((TM, LANES), lambda i, *_: (i, 0))] + e_in,
            out_specs=out_specs,
            scratch_shapes=[pltpu.VMEM((2, 2, TM, D), F32), pltpu.SemaphoreType.DMA((2,))]),
        out_shape=out_shape,
        compiler_params=_cparams(("arbitrary",)),
        name="moe_combine",
    )(pos, ys, route, x, gv, mv)


def _mla_down_kernel(h_ref, w_ref, gq_ref, gkv_ref, rc_ref, rlo_ref, rhi_ref,
                     cq_ref, ckv_ref, kr_ref, kvb_ref):
    i = pl.program_id(0)
    y = _dot(h_ref[...], w_ref[...])
    cq_ref[...] = _rms(y[:, :Q_RANK], gq_ref[...]).astype(BF16)
    ckv = _rms(y[:, Q_RANK:Q_RANK + KV_RANK], gkv_ref[...])
    kr = y[:, Q_RANK + KV_RANK:]
    kr = jnp.where(i >= NP_T, _rope(kr, rc_ref[...], rlo_ref[...], rhi_ref[...]), kr)
    ckv_ref[...] = ckv
    kr_ref[...] = kr
    kvb_ref[:, :KV_RANK] = ckv.astype(BF16)
    lane = lax.broadcasted_iota(jnp.int32, kr.shape, 1)
    kvb_ref[:, KV_RANK:] = jnp.where(lane == ROPE_DIM, 1.0, kr).astype(BF16)


def _mla_down(h, w, gq, gkv, tabs):
    n = Q_RANK + KVW
    row = lambda i: (i, 0)
    const = lambda i: (0, 0)
    tab = pl.BlockSpec((TM, LANES), lambda i: (_pos_tile(i), 0))
    return pl.pallas_call(
        _mla_down_kernel,
        grid=(N_T,),
        in_specs=[pl.BlockSpec((TM, D), row), pl.BlockSpec((D, n), const),
                  pl.BlockSpec((1, Q_RANK), const), pl.BlockSpec((1, KV_RANK), const),
                  tab, tab, tab],
        out_specs=[pl.BlockSpec((TM, Q_RANK), row), pl.BlockSpec((TM, KV_RANK), row),
                   pl.BlockSpec((TM, LANES), row), pl.BlockSpec((TM, KVW), row)],
        out_shape=[jax.ShapeDtypeStruct((T_ROWS, Q_RANK), BF16),
                   jax.ShapeDtypeStruct((T_ROWS, KV_RANK), F32),
                   jax.ShapeDtypeStruct((T_ROWS, LANES), F32),
                   jax.ShapeDtypeStruct((T_ROWS, KVW), BF16)],
        compiler_params=_cparams(("parallel",)),
        name="mla_down",
    )(h, w, gq, gkv, *tabs)


def _dot_nt(a, b):
    return lax.dot_general(a, b, (((1,), (1,)), ((), ())), preferred_element_type=F32)


def _mla_q_kernel(cq_ref, w_ref, cr_ref, sr_ref, cc_ref, sc_ref, q_ref):
    i = pl.program_id(0)
    y = _dot_nt(w_ref[...], cq_ref[...]) * ATTN_C2
    is_latent = i >= NP_T
    cr = jnp.where(is_latent, cr_ref[...], 1.0)
    sr = jnp.where(is_latent, sr_ref[...], 0.0)
    cc = jnp.where(is_latent, cc_ref[...], 1.0)
    sc = jnp.where(is_latent, sc_ref[...], 0.0)
    for hd in range(N_HEADS):
        r0 = hd * HEAD_PAD
        x0, x1, x2, x3 = (y[r0 + 8 * a:r0 + 8 * (a + 1), :] for a in range(4))
        rot = jnp.concatenate([x0 * cr - x1 * sr, x0 * sr + x1 * cr,
                               x2 * cc - x3 * sc, x2 * sc + x3 * cc], axis=0)
        q_ref[r0:r0 + ROPE_DIM, :] = rot.astype(BF16)
        q_ref[r0 + ROPE_DIM:r0 + HEAD_PAD, :] = y[r0 + ROPE_DIM:r0 + HEAD_PAD, :].astype(BF16)


def _mla_q(cq, w_t, tabs_t):
    tab = pl.BlockSpec((8, TM), lambda i: (0, _pos_tile(i)))
    return pl.pallas_call(
        _mla_q_kernel,
        grid=(N_T,),
        in_specs=[pl.BlockSpec((TM, Q_RANK), lambda i: (i, 0)),
                  pl.BlockSpec((HP, Q_RANK), lambda i: (0, 0)), tab, tab, tab, tab],
        out_specs=pl.BlockSpec((HP, TM), lambda i: (0, i)),
        out_shape=jax.ShapeDtypeStruct((HP, T_ROWS), BF16),
        compiler_params=_cparams(("parallel",)),
        name="mla_q",
    )(cq, w_t, *tabs_t)


def _mla_kv_kernel(c_ref, wk_ref, wvt_ref, k_ref, vt_ref):
    c = c_ref[...]
    k_ref[...] = _dot(c, wk_ref[...]).astype(BF16)
    vt_ref[...] = _dot_nt(wvt_ref[...], c).astype(BF16)


def _mla_kv(ckvkr, w_k, w_vt):
    rows = ckvkr.shape[0]
    hv = HVX
    return pl.pallas_call(
        _mla_kv_kernel,
        grid=(rows // TM,),
        in_specs=[pl.BlockSpec((TM, KVW), lambda i: (i, 0)),
                  pl.BlockSpec((KVW, HP), lambda i: (0, 0)),
                  pl.BlockSpec((hv, KVW), lambda i: (0, 0))],
        out_specs=[pl.BlockSpec((TM, HP), lambda i: (i, 0)), pl.BlockSpec((hv, TM), lambda i: (0, i))],
        out_shape=[jax.ShapeDtypeStruct((rows, HP), BF16), jax.ShapeDtypeStruct((hv, rows), BF16)],
        compiler_params=_cparams(("parallel",)),
        name="mla_kv",
    )(ckvkr, w_k, w_vt)


def _attn_kernel(*refs, nk, aliased):
    if aliased:
        refs = refs[1:]
    qt_ref, k_ref, vt_ref, o_ref, m_scr, acc_scr, ot_scr = refs
    ki = pl.program_id(2)

    @pl.when(ki == 0)
    def _():
        m_scr[...] = jnp.full(m_scr.shape, -jnp.inf, F32)
        acc_scr[...] = jnp.zeros(acc_scr.shape, F32)

    tk, tq = k_ref.shape[0], qt_ref.shape[1]
    vg = V_EXT // SUBLANES

    def across_sublanes(x, op):
        for k in (4, 2, 1):
            x = op(x, pltpu.roll(x, k, 0))
        return x

    def scores(hd):
        qs = slice(hd * HEAD_PAD, (hd + 1) * HEAD_PAD)
        return _dot(k_ref[:, qs], qt_ref[qs, :]).reshape(tk // SUBLANES, SUBLANES, tq)

    s_next = scores(0)
    for hd in range(N_HEADS):
        vs = slice(hd * V_EXT, (hd + 1) * V_EXT)
        s = s_next
        if hd + 1 < N_HEADS:
            s_next = scores(hd + 1)
        m_prev = m_scr[hd]
        m_new = jnp.maximum(m_prev, across_sublanes(jnp.max(s, axis=0), jnp.maximum))
        alpha = jnp.exp2(m_prev - m_new)
        p = jnp.exp2(s - m_new[None])
        pv = _dot(vt_ref[vs, :], p.reshape(tk, tq).astype(BF16))
        acc = acc_scr[vs, :].reshape(vg, SUBLANES, tq)
        acc_scr[vs, :] = (alpha[None] * acc).reshape(V_EXT, tq) + pv
        m_scr[hd] = m_new

    @pl.when(ki == nk - 1)
    def _():
        for hd in range(N_HEADS):
            r0 = hd * V_EXT
            den = across_sublanes(acc_scr[r0 + V_DIM:r0 + V_DIM + SUBLANES, :], jnp.add)
            acc = acc_scr[r0:r0 + V_DIM, :].reshape(V_DIM // SUBLANES, SUBLANES, tq)
            ot_scr[hd * V_DIM:(hd + 1) * V_DIM, :] = (acc / den[None]).reshape(V_DIM, tq)
        o_ref[...] = ot_scr[...].T.astype(BF16)


def _attention(qt, k, vt, n_b, tq, tk, lq, lk, q_row0, prev_out=None):
    nq, nk = lq // tq, lk // tk
    qb0 = q_row0 // tq
    hv = N_HEADS * V_DIM
    aliased = prev_out is not None
    in_specs = [pl.BlockSpec((HP, tq), lambda b, qi, ki: (0, qb0 + b * nq + qi)),
                pl.BlockSpec((tk, HP), lambda b, qi, ki: (b * nk + ki, 0)),
                pl.BlockSpec((HVX, tk), lambda b, qi, ki: (0, b * nk + ki))]
    args = [qt, k, vt]
    if aliased:
        in_specs = [pl.BlockSpec(memory_space=pl.ANY)] + in_specs
        args = [prev_out] + args
    return pl.pallas_call(
        functools.partial(_attn_kernel, nk=nk, aliased=aliased),
        grid=(n_b, nq, nk),
        in_specs=in_specs,
        out_specs=pl.BlockSpec((tq, hv), lambda b, qi, ki: (qb0 + b * nq + qi, 0)),
        out_shape=jax.ShapeDtypeStruct((T_ROWS, hv), BF16),
        scratch_shapes=[pltpu.VMEM((N_HEADS, SUBLANES, tq), F32), pltpu.VMEM((HVX, tq), F32),
                        pltpu.VMEM((hv, tq), F32)],
        input_output_aliases={0: 0} if aliased else {},
        compiler_params=_cparams(("parallel", "parallel", "arbitrary")),
        name="attn_latent" if aliased else "attn_context",
    )(*args)


def _rg_in_kernel(h_ref, w_ref, u_ref, g_ref):
    y = _dot(h_ref[...], w_ref[...])
    u_ref[...] = y[:, :D]
    g_ref[...] = _gelu_tanh(y[:, D:])


def _rg_in(h, w):
    return pl.pallas_call(
        _rg_in_kernel,
        grid=(N_T,),
        in_specs=[pl.BlockSpec((TM, D), lambda i: (i, 0)), pl.BlockSpec((D, 2 * D), lambda i: (0, 0))],
        out_specs=[pl.BlockSpec((TM, D), lambda i: (i, 0))] * 2,
        out_shape=[jax.ShapeDtypeStruct((T_ROWS, D), F32)] * 2,
        compiler_params=_cparams(("parallel",)),
        name="rg_in",
    )(h, w)


RG_TC = 256
RG_GROUP = 256


def _rg_scan_kernel(*refs, nj, seq_len, aliased):
    if aliased:
        refs = refs[2:]
    (uf_ref, ufp_ref, ufn_ref, ub_ref, ubp_ref, ubn_ref, cw_ref, cb_ref, wai_ref, bai_ref,
     lam_ref, h0_ref, yf_ref, yb_ref, fin_ref, carry_ref) = refs
    j = pl.program_id(1)
    tc = RG_TC
    r = lax.broadcasted_iota(jnp.int32, (tc, 1), 0)
    sub = r & (SUBLANES - 1)

    @pl.when(j == 0)
    def _():
        carry_ref[0:2, :] = h0_ref[...]

    def gates(m_ref, p_ref, n_ref, d, chunk):
        m = m_ref[...]
        pos = chunk * tc + r
        u = (cw_ref[0:1, :] * _shifted(m, p_ref, n_ref, -1, pos, seq_len, r)
             + cw_ref[1:2, :] * m
             + cw_ref[2:3, :] * _shifted(m, p_ref, n_ref, 1, pos, seq_len, r)
             + cw_ref[3:4, :] * _shifted(m, p_ref, n_ref, 2, pos, seq_len, r)
             + cb_ref[...])
        ub = u.astype(BF16)
        ra, ri = [], []
        for q in range(D // RG_GROUP):
            y = _dot(ub[:, q * RG_GROUP:(q + 1) * RG_GROUP], wai_ref[d, q])
            ra.append(y[:, :RG_GROUP])
            ri.append(y[:, RG_GROUP:])
        rr = jax.nn.sigmoid(jnp.concatenate(ra, axis=1) + bai_ref[d, 0:1, :])
        ii = jax.nn.sigmoid(jnp.concatenate(ri, axis=1) + bai_ref[d, 1:2, :])
        nl = -lam_ref[d:d + 1, :]
        softplus = jnp.maximum(nl, 0.0) + jnp.log1p(jnp.exp(-jnp.abs(nl)))
        log_a = (-RG_C * softplus) * rr
        a = jnp.exp(log_a)
        bx = jnp.sqrt(jnp.maximum(-jnp.tanh(log_a) * (a * a + 1.0), 0.0)) * (ii * u)
        return a, bx

    a, b = gates(uf_ref, ufp_ref, ufn_ref, 0, j)
    for k in (1, 2, 4):
        ok = sub >= k
        b = jnp.where(ok, a * pltpu.roll(b, k, 0) + b, b)
        a = jnp.where(ok, a * pltpu.roll(a, k, 0), a)
    h = carry_ref[0:1, :]
    for g in range(tc // SUBLANES):
        sl = slice(g * SUBLANES, (g + 1) * SUBLANES)
        hg = a[sl] * h + b[sl]
        yf_ref[sl, :] = hg
        h = hg[SUBLANES - 1:SUBLANES, :]
    carry_ref[0:1, :] = h

    a, b = gates(ub_ref, ubp_ref, ubn_ref, 1, nj - 1 - j)
    for k in (1, 2, 4):
        ok = sub < SUBLANES - k
        b = jnp.where(ok, a * pltpu.roll(b, tc - k, 0) + b, b)
        a = jnp.where(ok, a * pltpu.roll(a, tc - k, 0), a)
    h = carry_ref[1:2, :]
    for g in reversed(range(tc // SUBLANES)):
        sl = slice(g * SUBLANES, (g + 1) * SUBLANES)
        hg = a[sl] * h + b[sl]
        yb_ref[sl, :] = hg
        h = hg[0:1, :]
    carry_ref[1:2, :] = h

    @pl.when(j == nj - 1)
    def _():
        fin_ref[...] = carry_ref[0:2, :]


def _rg_scan(u, cw, cb, wai, bai, lam, h0, n_seq, seq_len, row0, prev=None):
    nj = seq_len // RG_TC
    b0 = row0 // RG_TC
    aliased = prev is not None
    fwd = lambda s, j: b0 + s * nj + j
    bwd = lambda s, j: b0 + s * nj + (nj - 1 - j)
    const2 = lambda s, j: (0, 0)
    in_specs = (_halo_specs(RG_TC, fwd) + _halo_specs(RG_TC, bwd)
                + [pl.BlockSpec((8, D), const2), pl.BlockSpec((1, D), const2),
                   pl.BlockSpec((2, D // RG_GROUP, RG_GROUP, 2 * RG_GROUP), lambda s, j: (0, 0, 0, 0)),
                   pl.BlockSpec((2, 2, D), lambda s, j: (0, 0, 0)),
                   pl.BlockSpec((2, D), const2),
                   pl.BlockSpec((None, 2, D), lambda s, j: (s, 0, 0))])
    args = [u, u, u, u, u, u, cw, cb, wai, bai, lam, h0]
    if aliased:
        in_specs = [pl.BlockSpec(memory_space=pl.ANY)] * 2 + in_specs
        args = list(prev) + args
    return pl.pallas_call(
        functools.partial(_rg_scan_kernel, nj=nj, seq_len=seq_len, aliased=aliased),
        grid=(n_seq, nj),
        in_specs=in_specs,
        out_specs=[pl.BlockSpec((RG_TC, D), lambda s, j: (fwd(s, j), 0)),
                   pl.BlockSpec((RG_TC, D), lambda s, j: (bwd(s, j), 0)),
                   pl.BlockSpec((None, 2, D), lambda s, j: (s, 0, 0))],
        out_shape=[jax.ShapeDtypeStruct((T_ROWS, D), F32), jax.ShapeDtypeStruct((T_ROWS, D), F32),
                   jax.ShapeDtypeStruct((n_seq, 2, D), F32)],
        scratch_shapes=[pltpu.VMEM((8, D), F32)],
        input_output_aliases={0: 0, 1: 1} if aliased else {},
        compiler_params=_cparams(("parallel", "arbitrary")),
        name="rg_scan_latent" if aliased else "rg_scan_context",
    )(*args)


def _sc_in_kernel(h_ref, w_ref, bg_ref, m_ref):
    y = _dot(h_ref[...], w_ref[...])
    bg_ref[...] = y[:, :D]
    m_ref[...] = y[:, D:2 * D] * y[:, 2 * D:]


def _sc_in(h, w):
    return pl.pallas_call(
        _sc_in_kernel,
        grid=(N_T,),
        in_specs=[pl.BlockSpec((TM, D), lambda i: (i, 0)), pl.BlockSpec((D, 3 * D), lambda i: (0, 0))],
        out_specs=[pl.BlockSpec((TM, D), lambda i: (i, 0))] * 2,
        out_shape=[jax.ShapeDtypeStruct((T_ROWS, D), F32)] * 2,
        compiler_params=_cparams(("parallel",)),
        name="sc_in",
    )(h, w)


def _rope_tables():
    t = np.arange(DEC_SEQ)
    inv = ROPE_THETA ** (-jnp.arange(0, AXIS_DIM, 2, dtype=F32) / AXIS_DIM)
    ang_r = jnp.asarray((t // GRID_W).astype(np.float32))[:, None] * inv
    ang_c = jnp.asarray((t % GRID_W).astype(np.float32))[:, None] * inv
    cr, sr, cc, sc = jnp.cos(ang_r), jnp.sin(ang_r), jnp.cos(ang_c), jnp.sin(ang_c)
    z8 = jnp.zeros((DEC_SEQ, 8), F32)
    pad1 = jnp.ones((DEC_SEQ, LANES - ROPE_DIM), F32)
    pad0 = jnp.zeros((DEC_SEQ, LANES - ROPE_DIM), F32)
    c = jnp.concatenate([cr, cr, cc, cc, pad1], axis=1)
    lo = jnp.concatenate([-sr, z8, -sc, z8, pad0], axis=1)
    hi = jnp.concatenate([z8, sr, z8, sc, pad0], axis=1)
    return (c, lo, hi), (cr.T, sr.T, cc.T, sc.T)


def _pad_heads(w, lo):
    r, h, d = w.shape
    return jnp.pad(w, ((0, 0), (0, 0), (lo, HEAD_PAD - lo - d))).reshape(r, h * HEAD_PAD)


def _mla_weights(w_dq, w_uq, w_dkv, w_uk, w_uv, w_o):
    w_down = jnp.concatenate([w_dq, w_dkv, jnp.zeros((D, KVW - KV_RANK - ROPE_DIM), F32)], axis=1)
    w_q = _pad_heads(w_uq[:, :, QK_NOPE:], 0) + _pad_heads(w_uq[:, :, :QK_NOPE], HEAD_PAD - QK_NOPE)
    place = jnp.broadcast_to(jnp.eye(ROPE_DIM, dtype=F32)[:, None, :], (ROPE_DIM, N_HEADS, ROPE_DIM))
    w_k = jnp.concatenate([_pad_heads(w_uk, HEAD_PAD - QK_NOPE), _pad_heads(place, 0),
                           jnp.zeros((KVW - KV_RANK - ROPE_DIM, HP), F32)], axis=0)
    w_vt = jnp.pad(jnp.transpose(w_uv, (1, 2, 0)), ((0, 0), (0, V_EXT - V_DIM), (0, KVW - KV_RANK)))
    w_vt = w_vt.at[:, V_DIM, ONE_COL].set(1.0).reshape(HVX, KVW)
    return (w_down.astype(BF16), w_q.T.astype(BF16), w_k.astype(BF16), w_vt.astype(BF16),
            w_o.astype(BF16))


def _block_diag_groups(w_a, w_i):
    per = RG_GROUP // RG_BLOCK

    def bd(w):
        w = w.reshape(2, D // RG_GROUP, per, RG_BLOCK, RG_BLOCK)
        eye = jnp.eye(per, dtype=F32)
        return jnp.einsum('dgpkj,pq->dgpkqj', w, eye).reshape(2, D // RG_GROUP, RG_GROUP, RG_GROUP)

    return jnp.concatenate([bd(w_a), bd(w_i)], axis=-1).astype(BF16)


def _pad_rows(w, rows=8):
    return jnp.pad(w, ((0, rows - w.shape[0]), (0, 0)))


def kernel(x_prompt, x_sample, cache_mla_ckv, cache_mla_krope, state_rglru, c, c_ctx, mod_w, mod_b, norm_g, mla_w_dq, mla_g_q, mla_w_uq, mla_w_dkv, mla_g_kv, mla_w_uk, mla_w_uv, mla_w_o, rg_w_x, rg_w_y, rg_conv_w, rg_conv_b, rg_w_a, rg_b_a, rg_w_i, rg_b_i, rg_lambda, rg_w_out, sc_w_in, sc_conv_w, sc_w_out, ffn_w_gate, ffn_w_up, ffn_w_down, moe_w_router, moe_b_router, moe_w_gate, moe_w_up, moe_w_down):
    cond8 = jnp.concatenate([c_ctx[None], c, jnp.zeros((N_GROUPS - 1 - DEC_BATCH, D), F32)], axis=0)
    mod = _modulation(cond8, mod_w, mod_b).reshape(DEPTH, N_GROUPS, 6, D)
    tabs, tabs_t = _rope_tables()
    zrow = jnp.zeros((N_GROUPS, 5, D), F32)

    def vecs(l, sub):
        if sub == 0:
            g_next, sh, sc = norm_g[l, 2], mod[l, :, 3], mod[l, :, 4]
        elif l + 1 < DEPTH:
            g_next, sh, sc = norm_g[l + 1, 0], mod[l + 1, :, 0], mod[l + 1, :, 1]
        else:
            g_next, sh, sc = jnp.zeros((D,), F32), zrow[:, 0], zrow[:, 0]
        gv = _pad_rows(jnp.stack([norm_g[l, 1 + 2 * sub], g_next]))
        mv = jnp.concatenate([jnp.stack([mod[l, :, 2 + 3 * sub], sh, sc], axis=1), zrow], axis=1)
        return gv, mv

    gv0 = _pad_rows(jnp.stack([jnp.zeros((D,), F32), norm_g[0, 0]]))
    mv0 = jnp.concatenate([jnp.stack([zrow[:, 0], mod[0, :, 0], mod[0, :, 1]], axis=1), zrow], axis=1)
    x, h = _pre0(x_prompt.reshape(P_ROWS, D), x_sample.reshape(S_ROWS, D), gv0, mv0)

    new_ckv, new_krope, new_rg = [], [], []
    for l in range(DEPTH):
        kind, j = l % 3, l // 3
        gv, mv = vecs(l, 0)
        router = None
        if l % 2 == 1:
            m = l // 2
            w_r = jnp.pad(moe_w_router[m], ((0, 0), (0, LANES - N_EXPERTS)))
            w_r_hi = w_r.astype(BF16)
            router = (w_r_hi, (w_r - w_r_hi.astype(F32)).astype(BF16),
                      jnp.pad(moe_b_router[m], (0, LANES - N_EXPERTS)).reshape(1, LANES))
        if kind == 0:
            w_down, w_qt, w_k, w_vt, w_op = _mla_weights(mla_w_dq[j], mla_w_uq[j], mla_w_dkv[j],
                                                         mla_w_uk[j], mla_w_uv[j], mla_w_o[j])
            cq, ckv, kr, kvb = _mla_down(h, w_down, mla_g_q[j].reshape(1, Q_RANK),
                                         mla_g_kv[j].reshape(1, KV_RANK), tabs)
            q = _mla_q(cq, w_qt, tabs_t)
            new_ckv.append(ckv[:P_ROWS].reshape(BATCH, SEQ, KV_RANK))
            new_krope.append(kr[:P_ROWS, :ROPE_DIM].reshape(BATCH, SEQ, ROPE_DIM))
            cache = jnp.concatenate(
                [cache_mla_ckv[:, j], cache_mla_krope[:, j], jnp.ones((DEC_BATCH, PAST, 1), F32),
                 jnp.zeros((DEC_BATCH, PAST, KVW - ONE_COL - 1), F32)], axis=-1).astype(BF16)
            kv_lat = jnp.concatenate([cache, kvb[P_ROWS:].reshape(DEC_BATCH, DEC_SEQ, KVW)], axis=1)
            k_c, v_c = _mla_kv(kvb[:P_ROWS], w_k, w_vt)
            k_l, v_l = _mla_kv(kv_lat.reshape(DEC_BATCH * (PAST + DEC_SEQ), KVW), w_k, w_vt)
            o = _attention(q, k_c, v_c, BATCH, SEQ, SEQ, SEQ, SEQ, 0)
            o = _attention(q, k_l, v_l, DEC_BATCH, 512, 512, DEC_SEQ, PAST + DEC_SEQ, P_ROWS, prev_out=o)
            outs = _proj("plain", (o,), w_op, x, gv, mv, router=router)
        elif kind == 1:
            w_xy = jnp.concatenate([rg_w_x[j], rg_w_y[j]], axis=1).astype(BF16)
            u, gate = _rg_in(h, w_xy)
            wai = _block_diag_groups(rg_w_a[j], rg_w_i[j])
            bai = jnp.stack([rg_b_a[j], rg_b_i[j]], axis=1)
            scan_args = (_pad_rows(rg_conv_w[j]), rg_conv_b[j].reshape(1, D), wai, bai, rg_lambda[j])
            yf, yb, fin = _rg_scan(u, *scan_args, jnp.zeros((BATCH, 2, D), F32), BATCH, SEQ, 0)
            yf, yb, _ = _rg_scan(u, *scan_args, state_rglru[:, j], DEC_BATCH, DEC_SEQ, P_ROWS,
                                 prev=(yf, yb))
            new_rg.append(fin)
            outs = _proj("rg", (yf, yb, gate), rg_w_out[j].astype(BF16), x, gv, mv, router=router)
        else:
            bg, mm = _sc_in(h, sc_w_in[j].astype(BF16))
            outs = _proj("sc", (bg, mm, _pad_rows(sc_conv_w[j])), sc_w_out[j].astype(BF16), x, gv, mv,
                         router=router)
        x, h = outs[0], outs[1]

        gv, mv = vecs(l, 1)
        has_next = l + 1 < DEPTH
        m = l // 2
        if l % 2 == 0:
            outs = _ffn(h, ffn_w_gate[m].astype(BF16), ffn_w_up[m].astype(BF16),
                        ffn_w_down[m].astype(BF16), x, gv, mv, has_next=has_next)
        else:
            route = outs[2]
            pos, pad, tile_e, n_used = _route_meta(route)
            xs = _dispatch(pos, pad, h)
            ys = _moe_ffn(tile_e, n_used, xs, moe_w_gate, moe_w_up, moe_w_down, m)
            outs = _combine(pos, ys, route, x, gv, mv, has_next=has_next)
        x, h = outs[0], outs[1]

    y_prompt = x.reshape(BATCH, SEQ, D)
    y_sample = h.reshape(DEC_BATCH, DEC_SEQ, D)
    return (y_prompt, y_sample, jnp.stack(new_ckv, axis=1), jnp.stack(new_krope, axis=1),
            jnp.stack(new_rg, axis=1))
```

```python
import functools

import numpy as np
import jax
import jax.numpy as jnp
from jax import lax
from jax.experimental import pallas as pl
from jax.experimental.pallas import tpu as pltpu

F32 = jnp.float32
BF16 = jnp.bfloat16

D = 1024
BATCH = 32
SEQ = 256
DEPTH = 4
DEC_BATCH = 4
DEC_SEQ = 4096
PAST = 512
GRID_W = 64
N_HEADS = 16
QK_NOPE = 64
ROPE_DIM = 32
AXIS_DIM = 16
V_DIM = 64
Q_RANK = 384
KV_RANK = 256
ROPE_THETA = 10000.0
ATTN_SCALE = (QK_NOPE + ROPE_DIM) ** -0.5
RG_BLOCKS = 16
RG_BLOCK = 64
RG_C = 8.0
D_FF = 2816
N_EXPERTS = 8
D_FF_EXPERT = 1408
EPS = 1e-6

P_ROWS = BATCH * SEQ
S_ROWS = DEC_BATCH * DEC_SEQ
T_ROWS = P_ROWS + S_ROWS
N_GROUPS = 8
HEAD_PAD = 128
HP = N_HEADS * HEAD_PAD
KVW = 384
ONE_COL = KV_RANK + ROPE_DIM
V_EXT = 80
HVX = N_HEADS * V_EXT
ATTN_C2 = ATTN_SCALE * float(np.log2(np.e))

TM = 512
NP_T = P_ROWS // TM
NS_T = DEC_SEQ // TM
N_T = T_ROWS // TM
SUBLANES = 8
LANES = 128
VMEM_LIMIT = 56 * 1024 * 1024


def _cparams(sem):
    return pltpu.CompilerParams(dimension_semantics=sem, vmem_limit_bytes=VMEM_LIMIT)


def _group_of_tile(i):
    return jnp.maximum(i - NP_T + NS_T, 0) // NS_T


def _pos_tile(i):
    return jnp.maximum(i - NP_T, 0) % NS_T


def _dot(a, b):
    return jnp.dot(a, b, preferred_element_type=F32)


def _rms(x, g):
    ms = jnp.mean(x * x, axis=-1, keepdims=True)
    return x * lax.rsqrt(ms + EPS) * g


def _silu(x):
    return x * jax.nn.sigmoid(x)


def _gelu_tanh(x):
    return x * (0.5 * (1.0 + jnp.tanh(np.sqrt(2.0 / np.pi).astype(np.float32)
                                      * (x + 0.044715 * (x * x * x)))))


def _rope(x, c, s_lo, s_hi):
    return (x * c + pltpu.roll(x, LANES - 8, 1) * s_lo + pltpu.roll(x, 8, 1) * s_hi)


def _mod_kernel(c_ref, w_ref, b_ref, o_ref):
    c = c_ref[...]
    s = _silu(c).astype(BF16)
    o_ref[...] = _dot(s, w_ref[...].astype(BF16)) + b_ref[...]


def _modulation(cond8, mod_w, mod_b):
    tn = 1536
    return pl.pallas_call(
        _mod_kernel,
        grid=(DEPTH, 6 * D // tn),
        in_specs=[pl.BlockSpec((N_GROUPS, D), lambda l, n: (0, 0)),
                  pl.BlockSpec((None, D, tn), lambda l, n: (l, 0, n)),
                  pl.BlockSpec((None, 1, tn), lambda l, n: (l, 0, n))],
        out_specs=pl.BlockSpec((None, N_GROUPS, tn), lambda l, n: (l, 0, n)),
        out_shape=jax.ShapeDtypeStruct((DEPTH, N_GROUPS, 6 * D), F32),
        compiler_params=_cparams(("parallel", "parallel")),
        name="modulation",
    )(cond8, mod_w, mod_b.reshape(DEPTH, 1, 6 * D))


def _ctx_block(i):
    return jnp.minimum(i, NP_T - 1)


def _lat_block(i):
    return jnp.maximum(i - NP_T, 0)


def _pre0_kernel(xp_ref, xs_ref, gv_ref, mv_ref, x_ref, h_ref):
    x = jnp.where(pl.program_id(0) < NP_T, xp_ref[...], xs_ref[...])
    x_ref[...] = x
    h = _rms(x, gv_ref[1:2, :]) * (1.0 + mv_ref[2:3, :]) + mv_ref[1:2, :]
    h_ref[...] = h.astype(BF16)


def _pre0(xp, xs, gv, mv):
    return pl.pallas_call(
        _pre0_kernel,
        grid=(N_T,),
        in_specs=[pl.BlockSpec((TM, D), lambda i: (_ctx_block(i), 0)),
                  pl.BlockSpec((TM, D), lambda i: (_lat_block(i), 0)),
                  pl.BlockSpec((8, D), lambda i: (0, 0)),
                  pl.BlockSpec((None, 8, D), lambda i: (_group_of_tile(i), 0, 0))],
        out_specs=[pl.BlockSpec((TM, D), lambda i: (i, 0))] * 2,
        out_shape=[jax.ShapeDtypeStruct((T_ROWS, D), F32), jax.ShapeDtypeStruct((T_ROWS, D), BF16)],
        compiler_params=_cparams(("arbitrary",)),
        name="pre0",
    )(xp, xs, gv, mv)


def _router(hn, wh_ref, wl_ref, br_ref):
    hh = hn.astype(BF16)
    hl = (hn - hh.astype(F32)).astype(BF16)
    logits = (_dot(hh, wh_ref[...]) + (_dot(hl, wh_ref[...]) + _dot(hh, wl_ref[...]))) + br_ref[...]
    lane = lax.broadcasted_iota(jnp.int32, logits.shape, 1)
    neg = jnp.float32(-jnp.inf)
    logits = jnp.where(lane < N_EXPERTS, logits, neg)
    m1 = jnp.max(logits, axis=-1, keepdims=True)
    i1 = jnp.min(jnp.where(logits == m1, lane, LANES), axis=-1, keepdims=True)
    rest = jnp.where(lane == i1, neg, logits)
    m2 = jnp.max(rest, axis=-1, keepdims=True)
    i2 = jnp.min(jnp.where(rest == m2, lane, LANES), axis=-1, keepdims=True)
    e = jnp.exp(m2 - m1)
    p1 = 1.0 / (1.0 + e)
    p2 = e / (1.0 + e)
    return jnp.where(lane == 0, i1.astype(F32),
                     jnp.where(lane == 1, i2.astype(F32),
                               jnp.where(lane == 2, p1, jnp.where(lane == 3, p2, 0.0))))


def _post_pre(x, out, gv_ref, mv_ref, has_next):
    xn = x + mv_ref[0:1, :] * _rms(out, gv_ref[0:1, :])
    if not has_next:
        return xn, None
    hn = _rms(xn, gv_ref[1:2, :]) * (1.0 + mv_ref[2:3, :]) + mv_ref[1:2, :]
    return xn, hn


def _epilogue(out, x_ref, gv_ref, mv_ref, rest, has_next, want_router):
    rest = list(rest)
    if want_router:
        router_refs = rest[:3]
        rest = rest[3:]
    xn, hn = _post_pre(x_ref[...], out, gv_ref, mv_ref, has_next)
    if not has_next:
        i = pl.program_id(0)

        @pl.when(i < NP_T)
        def _():
            rest[0][...] = xn

        @pl.when(i >= NP_T)
        def _():
            rest[1][...] = xn
        return
    rest[0][...] = xn
    rest[1][...] = hn.astype(rest[1].dtype)
    if want_router:
        rest[2][...] = _router(hn, *router_refs)


def _epilogue_specs(has_next, want_router):
    in_specs = [pl.BlockSpec((TM, D), lambda i, *_: (i, 0)),
                pl.BlockSpec((8, D), lambda i, *_: (0, 0)),
                pl.BlockSpec((None, 8, D), lambda i, *_: (_group_of_tile(i), 0, 0))]
    if not has_next:
        out_specs = [pl.BlockSpec((TM, D), lambda i, *_: (_ctx_block(i), 0)),
                     pl.BlockSpec((TM, D), lambda i, *_: (_lat_block(i), 0))]
        out_shape = [jax.ShapeDtypeStruct((P_ROWS, D), F32), jax.ShapeDtypeStruct((S_ROWS, D), F32)]
        return in_specs, out_specs, out_shape
    out_specs = [pl.BlockSpec((TM, D), lambda i, *_: (i, 0))]
    out_shape = [jax.ShapeDtypeStruct((T_ROWS, D), F32)]
    if want_router:
        in_specs += [pl.BlockSpec((D, LANES), lambda i, *_: (0, 0)),
                     pl.BlockSpec((D, LANES), lambda i, *_: (0, 0)),
                     pl.BlockSpec((1, LANES), lambda i, *_: (0, 0))]
    if has_next:
        out_specs.append(pl.BlockSpec((TM, D), lambda i, *_: (i, 0)))
        out_shape.append(jax.ShapeDtypeStruct((T_ROWS, D), F32 if want_router else BF16))
    if want_router:
        out_specs.append(pl.BlockSpec((TM, LANES), lambda i, *_: (i, 0)))
        out_shape.append(jax.ShapeDtypeStruct((T_ROWS, LANES), F32))
    return in_specs, out_specs, out_shape


def _seq_pos(i, rows):
    seq_len = jnp.where(i < NP_T, SEQ, DEC_SEQ)
    r = lax.broadcasted_iota(jnp.int32, (rows, 1), 0)
    return (i * rows + r) & (seq_len - 1), seq_len, r


def _shifted(m, prev_ref, next_ref, k, pos, seq_len, r):
    rows = m.shape[0]
    if k < 0:
        y = pltpu.roll(m, -k, 0)
        y = jnp.where(r == 0, prev_ref[SUBLANES - 1:SUBLANES, :], y)
        return jnp.where(pos + k < 0, 0.0, y)
    y = pltpu.roll(m, rows - k, 0)
    for q in range(k):
        y = jnp.where(r == rows - k + q, next_ref[q:q + 1, :], y)
    return jnp.where(pos + k >= seq_len, 0.0, y)


def _halo_specs(rows, row_block_of):
    per = rows // SUBLANES
    last = T_ROWS // SUBLANES - 1
    return [pl.BlockSpec((rows, D), lambda *g: (row_block_of(*g), 0)),
            pl.BlockSpec((SUBLANES, D), lambda *g: (jnp.maximum(row_block_of(*g) * per - 1, 0), 0)),
            pl.BlockSpec((SUBLANES, D), lambda *g: (jnp.minimum((row_block_of(*g) + 1) * per, last), 0))]


def _proj_kernel(*refs, mode, has_next, want_router):
    i = pl.program_id(0)
    if mode == "plain":
        a_ref, w_ref = refs[:2]
        rest = refs[2:]
        a = a_ref[...]
    elif mode == "rg":
        yf_ref, yb_ref, g_ref, w_ref = refs[:4]
        rest = refs[4:]
        a = ((yf_ref[...] + yb_ref[...]) * g_ref[...]).astype(BF16)
    else:
        bg_ref, m_ref, mp_ref, mn_ref, cw_ref, w_ref = refs[:6]
        rest = refs[6:]
        m = m_ref[...]
        pos, seq_len, r = _seq_pos(i, TM)
        z = (cw_ref[0:1, :] * _shifted(m, mp_ref, mn_ref, -1, pos, seq_len, r)
             + cw_ref[1:2, :] * m
             + cw_ref[2:3, :] * _shifted(m, mp_ref, mn_ref, 1, pos, seq_len, r))
        a = (bg_ref[...] * z).astype(BF16)
    out = _dot(a, w_ref[...])
    _epilogue(out, rest[0], rest[1], rest[2], rest[3:], has_next, want_router)


def _proj(mode, ins, w, x, gv, mv, has_next=True, router=None):
    want_router = router is not None
    k = w.shape[0]
    row = lambda i: (i, 0)
    if mode == "plain":
        in_specs = [pl.BlockSpec((TM, k), row)]
    elif mode == "rg":
        in_specs = [pl.BlockSpec((TM, D), row)] * 3
    else:
        bg, m, cw = ins
        ins = (bg, m, m, m, cw)
        in_specs = ([pl.BlockSpec((TM, D), row)] + _halo_specs(TM, lambda i: i)
                    + [pl.BlockSpec((8, D), lambda i: (0, 0))])
    in_specs.append(pl.BlockSpec((k, D), lambda i: (0, 0)))
    e_in, out_specs, out_shape = _epilogue_specs(has_next, want_router)
    args = list(ins) + [w, x, gv, mv] + (list(router) if want_router else [])
    return pl.pallas_call(
        functools.partial(_proj_kernel, mode=mode, has_next=has_next, want_router=want_router),
        grid=(N_T,),
        in_specs=in_specs + e_in,
        out_specs=out_specs,
        out_shape=out_shape,
        compiler_params=_cparams(("parallel",)),
        name="proj_" + mode,
    )(*args)


MXU_N = 256
FF_SPLIT = (D_FF // MXU_N + 1) // 2 * MXU_N


def _ffn_kernel(h_ref, wg_ref, wu_ref, wd_ref, *refs, has_next):
    h = h_ref[...]
    out = None
    for lo, hi in ((0, FF_SPLIT), (FF_SPLIT, D_FF)):
        a = _silu(_dot(h, wg_ref[:, lo:hi])) * _dot(h, wu_ref[:, lo:hi])
        part = _dot(a.astype(BF16), wd_ref[lo:hi, :])
        out = part if out is None else out + part
    _epilogue(out, refs[0], refs[1], refs[2], refs[3:], has_next, False)


def _ffn(h, wg, wu, wd, x, gv, mv, has_next=True):
    f = wg.shape[1]
    once = pl.Buffered(1)
    in_specs = [pl.BlockSpec((TM, D), lambda i: (i, 0)),
                pl.BlockSpec((D, f), lambda i: (0, 0), pipeline_mode=once),
                pl.BlockSpec((D, f), lambda i: (0, 0), pipeline_mode=once),
                pl.BlockSpec((f, D), lambda i: (0, 0), pipeline_mode=once)]
    e_in, out_specs, out_shape = _epilogue_specs(has_next, False)
    return pl.pallas_call(
        functools.partial(_ffn_kernel, has_next=has_next),
        grid=(N_T,),
        in_specs=in_specs + e_in,
        out_specs=out_specs,
        out_shape=out_shape,
        compiler_params=_cparams(("parallel",)),
        name="ffn_dense",
    )(h, wg, wu, wd, x, gv, mv)


TE = 256
R_ROWS = 2 * T_ROWS + N_EXPERTS * TE
N_TE = R_ROWS // TE


def _route_meta(route):
    e1 = route[:, 0].astype(jnp.int32)
    e2 = route[:, 1].astype(jnp.int32)
    ids = jnp.arange(N_EXPERTS, dtype=jnp.int32)
    hit = ((e1[:, None] == ids) | (e2[:, None] == ids)).astype(jnp.int32)
    csum = jnp.cumsum(hit, axis=0)
    counts = csum[-1]
    padded = (counts + TE - 1) // TE * TE
    ends = jnp.cumsum(padded)
    offs = ends - padded
    rank = csum - 1
    pos1 = offs[e1] + jnp.take_along_axis(rank, e1[:, None], axis=1)[:, 0]
    pos2 = offs[e2] + jnp.take_along_axis(rank, e2[:, None], axis=1)[:, 0]
    n_used = ends[-1] // TE
    tile_row = jnp.minimum(jnp.arange(N_TE, dtype=jnp.int32), n_used - 1) * TE
    tile_e = jnp.minimum(jnp.searchsorted(ends, tile_row, side='right'), N_EXPERTS - 1).astype(jnp.int32)
    pos = jnp.concatenate([pos1, pos2]).astype(jnp.int32)
    pad = jnp.concatenate([offs + counts, padded - counts]).astype(jnp.int32)
    return pos, pad, tile_e, n_used.astype(jnp.int32).reshape(1)


def _row_copy(src, s, dst, d, sem):
    return pltpu.make_async_copy(src.at[pl.ds(s, 1), :], dst.at[pl.ds(d, 1), :], sem)


def _dispatch_kernel(pos_ref, pad_ref, h_ref, xs_ref, zero_scr, sem):
    i = pl.program_id(0)
    base = i * TM

    for r in range(TM):
        _row_copy(h_ref, r, xs_ref, pos_ref[base + r], sem).start()
        _row_copy(h_ref, r, xs_ref, pos_ref[T_ROWS + base + r], sem).start()
    for _ in range(2):
        pltpu.make_async_copy(h_ref, xs_ref.at[pl.ds(0, TM), :], sem).wait()

    @pl.when(i == N_T - 1)
    def _():
        zero_scr[...] = jnp.zeros(zero_scr.shape, F32)
        for e in range(N_EXPERTS):
            start, n = pad_ref[e], pad_ref[N_EXPERTS + e]

            def fill(r, c):
                _row_copy(zero_scr, 0, xs_ref, start + r, sem).start()
                return c

            def drain(r, c):
                _row_copy(zero_scr, 0, xs_ref, start, sem).wait()
                return c

            lax.fori_loop(0, n, fill, 0)
            lax.fori_loop(0, n, drain, 0)


def _dispatch(pos, pad, h32):
    return pl.pallas_call(
        _dispatch_kernel,
        grid_spec=pltpu.PrefetchScalarGridSpec(
            num_scalar_prefetch=2,
            grid=(N_T,),
            in_specs=[pl.BlockSpec((TM, D), lambda i, *_: (i, 0))],
            out_specs=pl.BlockSpec(memory_space=pl.ANY),
            scratch_shapes=[pltpu.VMEM((SUBLANES, D), F32), pltpu.SemaphoreType.DMA]),
        out_shape=jax.ShapeDtypeStruct((R_ROWS, D), F32),
        compiler_params=_cparams(("arbitrary",)),
        name="moe_dispatch",
    )(pos, pad, h32)


def _moe_ffn_kernel(te_ref, nu_ref, xs_ref, wg_ref, wu_ref, wd_ref, ys_ref, wg_b, wu_b, wd_b):
    i = pl.program_id(0)

    @pl.when((i == 0) | (te_ref[i] != te_ref[jnp.maximum(i - 1, 0)]))
    def _():
        wg_b[...] = wg_ref[...].astype(BF16)
        wu_b[...] = wu_ref[...].astype(BF16)
        wd_b[...] = wd_ref[...].astype(BF16)

    @pl.when(i < nu_ref[0])
    def _():
        x = xs_ref[...].astype(BF16)
        a = _silu(_dot(x, wg_b[...])) * _dot(x, wu_b[...])
        ys_ref[...] = _dot(a.astype(BF16), wd_b[...])


def _moe_ffn(tile_e, n_used, xs, wg, wu, wd, m):
    f = wg.shape[3]
    row = lambda i, te, nu: (jnp.minimum(i, nu[0] - 1), 0)
    expert = lambda i, te, nu: (m, te[i], 0, 0)
    return pl.pallas_call(
        _moe_ffn_kernel,
        grid_spec=pltpu.PrefetchScalarGridSpec(
            num_scalar_prefetch=2,
            grid=(N_TE,),
            in_specs=[pl.BlockSpec((TE, D), row),
                      pl.BlockSpec((None, None, D, f), expert),
                      pl.BlockSpec((None, None, D, f), expert),
                      pl.BlockSpec((None, None, f, D), expert)],
            out_specs=pl.BlockSpec((TE, D), row),
            scratch_shapes=[pltpu.VMEM((D, f), BF16), pltpu.VMEM((D, f), BF16), pltpu.VMEM((f, D), BF16)]),
        out_shape=jax.ShapeDtypeStruct((R_ROWS, D), F32),
        compiler_params=_cparams(("arbitrary",)),
        name="moe_ffn",
    )(tile_e, n_used, xs, wg, wu, wd)


def _combine_kernel(*refs, has_next):
    pos_ref, ys_ref, route_ref = refs[:3]
    ybuf, sem = refs[-2:]
    refs = refs[3:-2]
    i = pl.program_id(0)
    slot = i % 2

    def issue(base, s, r):
        _row_copy(ys_ref, pos_ref[base + r], ybuf.at[s, 0], r, sem.at[s]).start()
        _row_copy(ys_ref, pos_ref[T_ROWS + base + r], ybuf.at[s, 1], r, sem.at[s]).start()

    @pl.when(i == 0)
    def _():
        lax.fori_loop(0, TM, lambda r, c: issue(0, 0, r), None)

    @pl.when(i + 1 < N_T)
    def _():
        for r in range(TM):
            issue((i + 1) * TM, 1 - slot, r)

    for c in range(2):
        pltpu.make_async_copy(ys_ref.at[pl.ds(0, TM), :], ybuf.at[slot, c], sem.at[slot]).wait()
    route = route_ref[...]
    y = route[:, 2:3] * ybuf[slot, 0] + route[:, 3:4] * ybuf[slot, 1]
    _epilogue(y, refs[0], refs[1], refs[2], refs[3:], has_next, False)


def _combine(pos, ys, route, x, gv, mv, has_next=True):
    e_in, out_specs, out_shape = _epilogue_specs(has_next, False)
    return pl.pallas_call(
        functools.partial(_combine_kernel, has_next=has_next),
        grid_spec=pltpu.PrefetchScalarGridSpec(
            num_scalar_prefetch=1,
            grid=(N_T,),
            in_specs=[pl.BlockSpec(memory_space=pl.ANY),
                      pl.BlockSpec((TM, LANES), lambda i, *_: (i, 0))] + e_in,
            out_specs=out_specs,
            scratch_shapes=[pltpu.VMEM((2, 2, TM, D), F32), pltpu.SemaphoreType.DMA((2,))]),
        out_shape=out_shape,
        compiler_params=_cparams(("arbitrary",)),
        name="moe_combine",
    )(pos, ys, route, x, gv, mv)


def _mla_down_kernel(h_ref, w_ref, gq_ref, gkv_ref, rc_ref, rlo_ref, rhi_ref,
                     cq_ref, ckv_ref, kr_ref, kvb_ref):
    i = pl.program_id(0)
    y = _dot(h_ref[...], w_ref[...])
    cq_ref[...] = _rms(y[:, :Q_RANK], gq_ref[...]).astype(BF16)
    ckv = _rms(y[:, Q_RANK:Q_RANK + KV_RANK], gkv_ref[...])
    kr = y[:, Q_RANK + KV_RANK:]
    kr = jnp.where(i >= NP_T, _rope(kr, rc_ref[...], rlo_ref[...], rhi_ref[...]), kr)
    ckv_ref[...] = ckv
    kr_ref[...] = kr
    kvb_ref[:, :KV_RANK] = ckv.astype(BF16)
    lane = lax.broadcasted_iota(jnp.int32, kr.shape, 1)
    kvb_ref[:, KV_RANK:] = jnp.where(lane == ROPE_DIM, 1.0, kr).astype(BF16)


def _mla_down(h, w, gq, gkv, tabs):
    n = Q_RANK + KVW
    row = lambda i: (i, 0)
    const = lambda i: (0, 0)
    tab = pl.BlockSpec((TM, LANES), lambda i: (_pos_tile(i), 0))
    return pl.pallas_call(
        _mla_down_kernel,
        grid=(N_T,),
        in_specs=[pl.BlockSpec((TM, D), row), pl.BlockSpec((D, n), const),
                  pl.BlockSpec((1, Q_RANK), const), pl.BlockSpec((1, KV_RANK), const),
                  tab, tab, tab],
        out_specs=[pl.BlockSpec((TM, Q_RANK), row), pl.BlockSpec((TM, KV_RANK), row),
                   pl.BlockSpec((TM, LANES), row), pl.BlockSpec((TM, KVW), row)],
        out_shape=[jax.ShapeDtypeStruct((T_ROWS, Q_RANK), BF16),
                   jax.ShapeDtypeStruct((T_ROWS, KV_RANK), F32),
                   jax.ShapeDtypeStruct((T_ROWS, LANES), F32),
                   jax.ShapeDtypeStruct((T_ROWS, KVW), BF16)],
        compiler_params=_cparams(("parallel",)),
        name="mla_down",
    )(h, w, gq, gkv, *tabs)


def _dot_nt(a, b):
    return lax.dot_general(a, b, (((1,), (1,)), ((), ())), preferred_element_type=F32)


def _mla_q_kernel(cq_ref, w_ref, cr_ref, sr_ref, cc_ref, sc_ref, q_ref):
    i = pl.program_id(0)
    y = _dot_nt(w_ref[...], cq_ref[...]) * ATTN_C2
    is_latent = i >= NP_T
    cr = jnp.where(is_latent, cr_ref[...], 1.0)
    sr = jnp.where(is_latent, sr_ref[...], 0.0)
    cc = jnp.where(is_latent, cc_ref[...], 1.0)
    sc = jnp.where(is_latent, sc_ref[...], 0.0)
    for hd in range(N_HEADS):
        r0 = hd * HEAD_PAD
        x0, x1, x2, x3 = (y[r0 + 8 * a:r0 + 8 * (a + 1), :] for a in range(4))
        rot = jnp.concatenate([x0 * cr - x1 * sr, x0 * sr + x1 * cr,
                               x2 * cc - x3 * sc, x2 * sc + x3 * cc], axis=0)
        q_ref[r0:r0 + ROPE_DIM, :] = rot.astype(BF16)
        q_ref[r0 + ROPE_DIM:r0 + HEAD_PAD, :] = y[r0 + ROPE_DIM:r0 + HEAD_PAD, :].astype(BF16)


def _mla_q(cq, w_t, tabs_t):
    tab = pl.BlockSpec((8, TM), lambda i: (0, _pos_tile(i)))
    return pl.pallas_call(
        _mla_q_kernel,
        grid=(N_T,),
        in_specs=[pl.BlockSpec((TM, Q_RANK), lambda i: (i, 0)),
                  pl.BlockSpec((HP, Q_RANK), lambda i: (0, 0)), tab, tab, tab, tab],
        out_specs=pl.BlockSpec((HP, TM), lambda i: (0, i)),
        out_shape=jax.ShapeDtypeStruct((HP, T_ROWS), BF16),
        compiler_params=_cparams(("parallel",)),
        name="mla_q",
    )(cq, w_t, *tabs_t)


def _mla_kv_kernel(c_ref, wk_ref, wvt_ref, k_ref, vt_ref):
    c = c_ref[...]
    k_ref[...] = _dot(c, wk_ref[...]).astype(BF16)
    vt_ref[...] = _dot_nt(wvt_ref[...], c).astype(BF16)


def _mla_kv(ckvkr, w_k, w_vt):
    rows = ckvkr.shape[0]
    hv = HVX
    return pl.pallas_call(
        _mla_kv_kernel,
        grid=(rows // TM,),
        in_specs=[pl.BlockSpec((TM, KVW), lambda i: (i, 0)),
                  pl.BlockSpec((KVW, HP), lambda i: (0, 0)),
                  pl.BlockSpec((hv, KVW), lambda i: (0, 0))],
        out_specs=[pl.BlockSpec((TM, HP), lambda i: (i, 0)), pl.BlockSpec((hv, TM), lambda i: (0, i))],
        out_shape=[jax.ShapeDtypeStruct((rows, HP), BF16), jax.ShapeDtypeStruct((hv, rows), BF16)],
        compiler_params=_cparams(("parallel",)),
        name="mla_kv",
    )(ckvkr, w_k, w_vt)


def _attn_kernel(*refs, nk, aliased):
    if aliased:
        refs = refs[1:]
    qt_ref, k_ref, vt_ref, o_ref, m_scr, acc_scr, ot_scr = refs
    ki = pl.program_id(2)

    @pl.when(ki == 0)
    def _():
        m_scr[...] = jnp.full(m_scr.shape, -jnp.inf, F32)
        acc_scr[...] = jnp.zeros(acc_scr.shape, F32)

    tk, tq = k_ref.shape[0], qt_ref.shape[1]
    vg = V_EXT // SUBLANES

    def across_sublanes(x, op):
        for k in (4, 2, 1):
            x = op(x, pltpu.roll(x, k, 0))
        return x

    def scores(hd):
        qs = slice(hd * HEAD_PAD, (hd + 1) * HEAD_PAD)
        return _dot(k_ref[:, qs], qt_ref[qs, :]).reshape(tk // SUBLANES, SUBLANES, tq)

    s_next = scores(0)
    for hd in range(N_HEADS):
        vs = slice(hd * V_EXT, (hd + 1) * V_EXT)
        s = s_next
        if hd + 1 < N_HEADS:
            s_next = scores(hd + 1)
        m_prev = m_scr[hd]
        m_new = jnp.maximum(m_prev, across_sublanes(jnp.max(s, axis=0), jnp.maximum))
        alpha = jnp.exp2(m_prev - m_new)
        p = jnp.exp2(s - m_new[None])
        pv = _dot(vt_ref[vs, :], p.reshape(tk, tq).astype(BF16))
        acc = acc_scr[vs, :].reshape(vg, SUBLANES, tq)
        acc_scr[vs, :] = (alpha[None] * acc).reshape(V_EXT, tq) + pv
        m_scr[hd] = m_new

    @pl.when(ki == nk - 1)
    def _():
        for hd in range(N_HEADS):
            r0 = hd * V_EXT
            den = across_sublanes(acc_scr[r0 + V_DIM:r0 + V_DIM + SUBLANES, :], jnp.add)
            acc = acc_scr[r0:r0 + V_DIM, :].reshape(V_DIM // SUBLANES, SUBLANES, tq)
            ot_scr[hd * V_DIM:(hd + 1) * V_DIM, :] = (acc / den[None]).reshape(V_DIM, tq)
        o_ref[...] = ot_scr[...].T.astype(BF16)


def _attention(qt, k, vt, n_b, tq, tk, lq, lk, q_row0, prev_out=None):
    nq, nk = lq // tq, lk // tk
    qb0 = q_row0 // tq
    hv = N_HEADS * V_DIM
    aliased = prev_out is not None
    in_specs = [pl.BlockSpec((HP, tq), lambda b, qi, ki: (0, qb0 + b * nq + qi)),
                pl.BlockSpec((tk, HP), lambda b, qi, ki: (b * nk + ki, 0)),
                pl.BlockSpec((HVX, tk), lambda b, qi, ki: (0, b * nk + ki))]
    args = [qt, k, vt]
    if aliased:
        in_specs = [pl.BlockSpec(memory_space=pl.ANY)] + in_specs
        args = [prev_out] + args
    return pl.pallas_call(
        functools.partial(_attn_kernel, nk=nk, aliased=aliased),
        grid=(n_b, nq, nk),
        in_specs=in_specs,
        out_specs=pl.BlockSpec((tq, hv), lambda b, qi, ki: (qb0 + b * nq + qi, 0)),
        out_shape=jax.ShapeDtypeStruct((T_ROWS, hv), BF16),
        scratch_shapes=[pltpu.VMEM((N_HEADS, SUBLANES, tq), F32), pltpu.VMEM((HVX, tq), F32),
                        pltpu.VMEM((hv, tq), F32)],
        input_output_aliases={0: 0} if aliased else {},
        compiler_params=_cparams(("parallel", "parallel", "arbitrary")),
        name="attn_latent" if aliased else "attn_context",
    )(*args)


def _rg_in_kernel(h_ref, w_ref, u_ref, g_ref):
    y = _dot(h_ref[...], w_ref[...])
    u_ref[...] = y[:, :D]
    g_ref[...] = _gelu_tanh(y[:, D:])


def _rg_in(h, w):
    return pl.pallas_call(
        _rg_in_kernel,
        grid=(N_T,),
        in_specs=[pl.BlockSpec((TM, D), lambda i: (i, 0)), pl.BlockSpec((D, 2 * D), lambda i: (0, 0))],
        out_specs=[pl.BlockSpec((TM, D), lambda i: (i, 0))] * 2,
        out_shape=[jax.ShapeDtypeStruct((T_ROWS, D), F32)] * 2,
        compiler_params=_cparams(("parallel",)),
        name="rg_in",
    )(h, w)


RG_TC = 256
RG_GROUP = 256


def _rg_scan_kernel(*refs, nj, seq_len, aliased):
    if aliased:
        refs = refs[2:]
    (uf_ref, ufp_ref, ufn_ref, ub_ref, ubp_ref, ubn_ref, cw_ref, cb_ref, wai_ref, bai_ref,
     lam_ref, h0_ref, yf_ref, yb_ref, fin_ref, carry_ref) = refs
    j = pl.program_id(1)
    tc = RG_TC
    ng = tc // SUBLANES
    sub = lax.broadcasted_iota(jnp.int32, (1, SUBLANES, 1), 1)

    @pl.when(j == 0)
    def _():
        carry_ref[0:2, :] = h0_ref[...]

    def gates(m_ref, p_ref, n_ref, d, chunk):
        m = m_ref[...]
        ext = jnp.concatenate([jnp.where(chunk == 0, 0.0, p_ref[...]), m,
                               jnp.where(chunk == nj - 1, 0.0, n_ref[...])], axis=0)

        def tap(k):
            return pltpu.roll(ext, (-k) % (tc + 2 * SUBLANES), 0)[SUBLANES:SUBLANES + tc]

        u = (cw_ref[0:1, :] * tap(-1) + cw_ref[1:2, :] * m + cw_ref[2:3, :] * tap(1)
             + cw_ref[3:4, :] * tap(2) + cb_ref[...])
        ub = u.astype(BF16)
        ra, ri = [], []
        for q in range(D // RG_GROUP):
            y = _dot(ub[:, q * RG_GROUP:(q + 1) * RG_GROUP], wai_ref[d, q])
            ra.append(y[:, :RG_GROUP])
            ri.append(y[:, RG_GROUP:])
        rr = jax.nn.sigmoid(jnp.concatenate(ra, axis=1) + bai_ref[d, 0:1, :])
        ii = jax.nn.sigmoid(jnp.concatenate(ri, axis=1) + bai_ref[d, 1:2, :])
        nl = -lam_ref[d:d + 1, :]
        softplus = jnp.maximum(nl, 0.0) + jnp.log1p(jnp.exp(-jnp.abs(nl)))
        log_a = (-RG_C * softplus) * rr
        a = jnp.exp(log_a)
        y = jnp.maximum(-jnp.tanh(log_a) * (a * a + 1.0), 0.0)
        bx = jnp.where(y > 0.0, y * lax.rsqrt(y), 0.0) * (ii * u)
        return a.reshape(ng, SUBLANES, D), bx.reshape(ng, SUBLANES, D)

    a, b = gates(uf_ref, ufp_ref, ufn_ref, 0, j)
    for k in (1, 2, 4):
        ok = sub >= k
        b = jnp.where(ok, a * pltpu.roll(b, k, 1) + b, b)
        a = jnp.where(ok, a * pltpu.roll(a, k, 1), a)
    h = carry_ref[0:1, :]
    for g in range(ng):
        hg = a[g] * h + b[g]
        yf_ref[g * SUBLANES:(g + 1) * SUBLANES, :] = hg
        h = hg[SUBLANES - 1:SUBLANES, :]
    carry_ref[0:1, :] = h

    a, b = gates(ub_ref, ubp_ref, ubn_ref, 1, nj - 1 - j)
    for k in (1, 2, 4):
        ok = sub < SUBLANES - k
        b = jnp.where(ok, a * pltpu.roll(b, SUBLANES - k, 1) + b, b)
        a = jnp.where(ok, a * pltpu.roll(a, SUBLANES - k, 1), a)
    h = carry_ref[1:2, :]
    for g in reversed(range(ng)):
        hg = a[g] * h + b[g]
        yb_ref[g * SUBLANES:(g + 1) * SUBLANES, :] = hg
        h = hg[0:1, :]
    carry_ref[1:2, :] = h

    @pl.when(j == nj - 1)
    def _():
        fin_ref[...] = carry_ref[0:2, :]


def _rg_scan(u, cw, cb, wai, bai, lam, h0, n_seq, seq_len, row0, prev=None):
    nj = seq_len // RG_TC
    b0 = row0 // RG_TC
    aliased = prev is not None
    fwd = lambda s, j: b0 + s * nj + j
    bwd = lambda s, j: b0 + s * nj + (nj - 1 - j)
    const2 = lambda s, j: (0, 0)
    in_specs = (_halo_specs(RG_TC, fwd) + _halo_specs(RG_TC, bwd)
                + [pl.BlockSpec((8, D), const2), pl.BlockSpec((1, D), const2),
                   pl.BlockSpec((2, D // RG_GROUP, RG_GROUP, 2 * RG_GROUP), lambda s, j: (0, 0, 0, 0)),
                   pl.BlockSpec((2, 2, D), lambda s, j: (0, 0, 0)),
                   pl.BlockSpec((2, D), const2),
                   pl.BlockSpec((None, 2, D), lambda s, j: (s, 0, 0))])
    args = [u, u, u, u, u, u, cw, cb, wai, bai, lam, h0]
    if aliased:
        in_specs = [pl.BlockSpec(memory_space=pl.ANY)] * 2 + in_specs
        args = list(prev) + args
    return pl.pallas_call(
        functools.partial(_rg_scan_kernel, nj=nj, seq_len=seq_len, aliased=aliased),
        grid=(n_seq, nj),
        in_specs=in_specs,
        out_specs=[pl.BlockSpec((RG_TC, D), lambda s, j: (fwd(s, j), 0)),
                   pl.BlockSpec((RG_TC, D), lambda s, j: (bwd(s, j), 0)),
                   pl.BlockSpec((None, 2, D), lambda s, j: (s, 0, 0))],
        out_shape=[jax.ShapeDtypeStruct((T_ROWS, D), F32), jax.ShapeDtypeStruct((T_ROWS, D), F32),
                   jax.ShapeDtypeStruct((n_seq, 2, D), F32)],
        scratch_shapes=[pltpu.VMEM((8, D), F32)],
        input_output_aliases={0: 0, 1: 1} if aliased else {},
        compiler_params=_cparams(("parallel", "arbitrary")),
        name="rg_scan_latent" if aliased else "rg_scan_context",
    )(*args)


def _sc_in_kernel(h_ref, w_ref, bg_ref, m_ref):
    y = _dot(h_ref[...], w_ref[...])
    bg_ref[...] = y[:, :D]
    m_ref[...] = y[:, D:2 * D] * y[:, 2 * D:]


def _sc_in(h, w):
    return pl.pallas_call(
        _sc_in_kernel,
        grid=(N_T,),
        in_specs=[pl.BlockSpec((TM, D), lambda i: (i, 0)), pl.BlockSpec((D, 3 * D), lambda i: (0, 0))],
        out_specs=[pl.BlockSpec((TM, D), lambda i: (i, 0))] * 2,
        out_shape=[jax.ShapeDtypeStruct((T_ROWS, D), F32)] * 2,
        compiler_params=_cparams(("parallel",)),
        name="sc_in",
    )(h, w)


def _rope_tables():
    t = np.arange(DEC_SEQ)
    inv = ROPE_THETA ** (-jnp.arange(0, AXIS_DIM, 2, dtype=F32) / AXIS_DIM)
    ang_r = jnp.asarray((t // GRID_W).astype(np.float32))[:, None] * inv
    ang_c = jnp.asarray((t % GRID_W).astype(np.float32))[:, None] * inv
    cr, sr, cc, sc = jnp.cos(ang_r), jnp.sin(ang_r), jnp.cos(ang_c), jnp.sin(ang_c)
    z8 = jnp.zeros((DEC_SEQ, 8), F32)
    pad1 = jnp.ones((DEC_SEQ, LANES - ROPE_DIM), F32)
    pad0 = jnp.zeros((DEC_SEQ, LANES - ROPE_DIM), F32)
    c = jnp.concatenate([cr, cr, cc, cc, pad1], axis=1)
    lo = jnp.concatenate([-sr, z8, -sc, z8, pad0], axis=1)
    hi = jnp.concatenate([z8, sr, z8, sc, pad0], axis=1)
    return (c, lo, hi), (cr.T, sr.T, cc.T, sc.T)


def _pad_heads(w, lo):
    r, h, d = w.shape
    return jnp.pad(w, ((0, 0), (0, 0), (lo, HEAD_PAD - lo - d))).reshape(r, h * HEAD_PAD)


def _mla_weights(w_dq, w_uq, w_dkv, w_uk, w_uv, w_o):
    w_down = jnp.concatenate([w_dq, w_dkv, jnp.zeros((D, KVW - KV_RANK - ROPE_DIM), F32)], axis=1)
    w_q = _pad_heads(w_uq[:, :, QK_NOPE:], 0) + _pad_heads(w_uq[:, :, :QK_NOPE], HEAD_PAD - QK_NOPE)
    place = jnp.broadcast_to(jnp.eye(ROPE_DIM, dtype=F32)[:, None, :], (ROPE_DIM, N_HEADS, ROPE_DIM))
    w_k = jnp.concatenate([_pad_heads(w_uk, HEAD_PAD - QK_NOPE), _pad_heads(place, 0),
                           jnp.zeros((KVW - KV_RANK - ROPE_DIM, HP), F32)], axis=0)
    w_vt = jnp.pad(jnp.transpose(w_uv, (1, 2, 0)), ((0, 0), (0, V_EXT - V_DIM), (0, KVW - KV_RANK)))
    w_vt = w_vt.at[:, V_DIM, ONE_COL].set(1.0).reshape(HVX, KVW)
    return (w_down.astype(BF16), w_q.T.astype(BF16), w_k.astype(BF16), w_vt.astype(BF16),
            w_o.astype(BF16))


def _block_diag_groups(w_a, w_i):
    per = RG_GROUP // RG_BLOCK

    def bd(w):
        w = w.reshape(2, D // RG_GROUP, per, RG_BLOCK, RG_BLOCK)
        eye = jnp.eye(per, dtype=F32)
        return jnp.einsum('dgpkj,pq->dgpkqj', w, eye).reshape(2, D // RG_GROUP, RG_GROUP, RG_GROUP)

    return jnp.concatenate([bd(w_a), bd(w_i)], axis=-1).astype(BF16)


def _pad_rows(w, rows=8):
    return jnp.pad(w, ((0, rows - w.shape[0]), (0, 0)))


def kernel(x_prompt, x_sample, cache_mla_ckv, cache_mla_krope, state_rglru, c, c_ctx, mod_w, mod_b, norm_g, mla_w_dq, mla_g_q, mla_w_uq, mla_w_dkv, mla_g_kv, mla_w_uk, mla_w_uv, mla_w_o, rg_w_x, rg_w_y, rg_conv_w, rg_conv_b, rg_w_a, rg_b_a, rg_w_i, rg_b_i, rg_lambda, rg_w_out, sc_w_in, sc_conv_w, sc_w_out, ffn_w_gate, ffn_w_up, ffn_w_down, moe_w_router, moe_b_router, moe_w_gate, moe_w_up, moe_w_down):
    cond8 = jnp.concatenate([c_ctx[None], c, jnp.zeros((N_GROUPS - 1 - DEC_BATCH, D), F32)], axis=0)
    mod = _modulation(cond8, mod_w, mod_b).reshape(DEPTH, N_GROUPS, 6, D)
    tabs, tabs_t = _rope_tables()
    zrow = jnp.zeros((N_GROUPS, 5, D), F32)

    def vecs(l, sub):
        if sub == 0:
            g_next, sh, sc = norm_g[l, 2], mod[l, :, 3], mod[l, :, 4]
        elif l + 1 < DEPTH:
            g_next, sh, sc = norm_g[l + 1, 0], mod[l + 1, :, 0], mod[l + 1, :, 1]
        else:
            g_next, sh, sc = jnp.zeros((D,), F32), zrow[:, 0], zrow[:, 0]
        gv = _pad_rows(jnp.stack([norm_g[l, 1 + 2 * sub], g_next]))
        mv = jnp.concatenate([jnp.stack([mod[l, :, 2 + 3 * sub], sh, sc], axis=1), zrow], axis=1)
        return gv, mv

    gv0 = _pad_rows(jnp.stack([jnp.zeros((D,), F32), norm_g[0, 0]]))
    mv0 = jnp.concatenate([jnp.stack([zrow[:, 0], mod[0, :, 0], mod[0, :, 1]], axis=1), zrow], axis=1)
    x, h = _pre0(x_prompt.reshape(P_ROWS, D), x_sample.reshape(S_ROWS, D), gv0, mv0)

    new_ckv, new_krope, new_rg = [], [], []
    for l in range(DEPTH):
        kind, j = l % 3, l // 3
        gv, mv = vecs(l, 0)
        router = None
        if l % 2 == 1:
            m = l // 2
            w_r = jnp.pad(moe_w_router[m], ((0, 0), (0, LANES - N_EXPERTS)))
            w_r_hi = w_r.astype(BF16)
            router = (w_r_hi, (w_r - w_r_hi.astype(F32)).astype(BF16),
                      jnp.pad(moe_b_router[m], (0, LANES - N_EXPERTS)).reshape(1, LANES))
        if kind == 0:
            w_down, w_qt, w_k, w_vt, w_op = _mla_weights(mla_w_dq[j], mla_w_uq[j], mla_w_dkv[j],
                                                         mla_w_uk[j], mla_w_uv[j], mla_w_o[j])
            cq, ckv, kr, kvb = _mla_down(h, w_down, mla_g_q[j].reshape(1, Q_RANK),
                                         mla_g_kv[j].reshape(1, KV_RANK), tabs)
            q = _mla_q(cq, w_qt, tabs_t)
            new_ckv.append(ckv[:P_ROWS].reshape(BATCH, SEQ, KV_RANK))
            new_krope.append(kr[:P_ROWS, :ROPE_DIM].reshape(BATCH, SEQ, ROPE_DIM))
            cache = jnp.concatenate(
                [cache_mla_ckv[:, j], cache_mla_krope[:, j], jnp.ones((DEC_BATCH, PAST, 1), F32),
                 jnp.zeros((DEC_BATCH, PAST, KVW - ONE_COL - 1), F32)], axis=-1).astype(BF16)
            kv_lat = jnp.concatenate([cache, kvb[P_ROWS:].reshape(DEC_BATCH, DEC_SEQ, KVW)], axis=1)
            k_c, v_c = _mla_kv(kvb[:P_ROWS], w_k, w_vt)
            k_l, v_l = _mla_kv(kv_lat.reshape(DEC_BATCH * (PAST + DEC_SEQ), KVW), w_k, w_vt)
            o = _attention(q, k_c, v_c, BATCH, SEQ, SEQ, SEQ, SEQ, 0)
            o = _attention(q, k_l, v_l, DEC_BATCH, 1024, 512, DEC_SEQ, PAST + DEC_SEQ, P_ROWS, prev_out=o)
            outs = _proj("plain", (o,), w_op, x, gv, mv, router=router)
        elif kind == 1:
            w_xy = jnp.concatenate([rg_w_x[j], rg_w_y[j]], axis=1).astype(BF16)
            u, gate = _rg_in(h, w_xy)
            wai = _block_diag_groups(rg_w_a[j], rg_w_i[j])
            bai = jnp.stack([rg_b_a[j], rg_b_i[j]], axis=1)
            scan_args = (_pad_rows(rg_conv_w[j]), rg_conv_b[j].reshape(1, D), wai, bai, rg_lambda[j])
            yf, yb, fin = _rg_scan(u, *scan_args, jnp.zeros((BATCH, 2, D), F32), BATCH, SEQ, 0)
            yf, yb, _ = _rg_scan(u, *scan_args, state_rglru[:, j], DEC_BATCH, DEC_SEQ, P_ROWS,
                                 prev=(yf, yb))
            new_rg.append(fin)
            outs = _proj("rg", (yf, yb, gate), rg_w_out[j].astype(BF16), x, gv, mv, router=router)
        else:
            bg, mm = _sc_in(h, sc_w_in[j].astype(BF16))
            outs = _proj("sc", (bg, mm, _pad_rows(sc_conv_w[j])), sc_w_out[j].astype(BF16), x, gv, mv,
                         router=router)
        x, h = outs[0], outs[1]

        gv, mv = vecs(l, 1)
        has_next = l + 1 < DEPTH
        m = l // 2
        if l % 2 == 0:
            outs = _ffn(h, ffn_w_gate[m].astype(BF16), ffn_w_up[m].astype(BF16),
                        ffn_w_down[m].astype(BF16), x, gv, mv, has_next=has_next)
        else:
            route = outs[2]
            pos, pad, tile_e, n_used = _route_meta(route)
            xs = _dispatch(pos, pad, h)
            ys = _moe_ffn(tile_e, n_used, xs, moe_w_gate, moe_w_up, moe_w_down, m)
            outs = _combine(pos, ys, route, x, gv, mv, has_next=has_next)
        x, h = outs[0], outs[1]

    y_prompt = x.reshape(BATCH, SEQ, D)
    y_sample = h.reshape(DEC_BATCH, DEC_SEQ, D)
    return (y_prompt, y_sample, jnp.stack(new_ckv, axis=1), jnp.stack(new_krope, axis=1),
            jnp.stack(new_rg, axis=1))
```

```python
import functools

import numpy as np
import jax
import jax.numpy as jnp
from jax import lax
from jax.experimental import pallas as pl
from jax.experimental.pallas import tpu as pltpu

F32 = jnp.float32
BF16 = jnp.bfloat16

D = 1024
BATCH = 32
SEQ = 256
DEPTH = 4
DEC_BATCH = 4
DEC_SEQ = 4096
PAST = 512
GRID_W = 64
N_HEADS = 16
QK_NOPE = 64
ROPE_DIM = 32
AXIS_DIM = 16
V_DIM = 64
Q_RANK = 384
KV_RANK = 256
ROPE_THETA = 10000.0
ATTN_SCALE = (QK_NOPE + ROPE_DIM) ** -0.5
RG_BLOCKS = 16
RG_BLOCK = 64
RG_C = 8.0
D_FF = 2816
N_EXPERTS = 8
D_FF_EXPERT = 1408
EPS = 1e-6

P_ROWS = BATCH * SEQ
S_ROWS = DEC_BATCH * DEC_SEQ
T_ROWS = P_ROWS + S_ROWS
N_GROUPS = 8
HEAD_PAD = 128
HP = N_HEADS * HEAD_PAD
KVW = 384
ONE_COL = KV_RANK + ROPE_DIM
V_EXT = 80
HVX = N_HEADS * V_EXT
ATTN_C2 = ATTN_SCALE * float(np.log2(np.e))

TM = 512
NP_T = P_ROWS // TM
NS_T = DEC_SEQ // TM
N_T = T_ROWS // TM
SUBLANES = 8
LANES = 128
VMEM_LIMIT = 56 * 1024 * 1024


def _cparams(sem):
    return pltpu.CompilerParams(dimension_semantics=sem, vmem_limit_bytes=VMEM_LIMIT)


def _group_of_tile(i):
    return jnp.maximum(i - NP_T + NS_T, 0) // NS_T


def _pos_tile(i):
    return jnp.maximum(i - NP_T, 0) % NS_T


def _dot(a, b):
    return jnp.dot(a, b, preferred_element_type=F32)


def _rms(x, g):
    ms = jnp.mean(x * x, axis=-1, keepdims=True)
    return x * lax.rsqrt(ms + EPS) * g


def _silu(x):
    return x * jax.nn.sigmoid(x)


def _gelu_tanh(x):
    return x * (0.5 * (1.0 + jnp.tanh(np.sqrt(2.0 / np.pi).astype(np.float32)
                                      * (x + 0.044715 * (x * x * x)))))


def _rope(x, c, s_lo, s_hi):
    return (x * c + pltpu.roll(x, LANES - 8, 1) * s_lo + pltpu.roll(x, 8, 1) * s_hi)


def _mod_kernel(c_ref, w_ref, b_ref, o_ref):
    c = c_ref[...]
    s = _silu(c).astype(BF16)
    o_ref[...] = _dot(s, w_ref[...].astype(BF16)) + b_ref[...]


def _modulation(cond8, mod_w, mod_b):
    tn = 1536
    return pl.pallas_call(
        _mod_kernel,
        grid=(DEPTH, 6 * D // tn),
        in_specs=[pl.BlockSpec((N_GROUPS, D), lambda l, n: (0, 0)),
                  pl.BlockSpec((None, D, tn), lambda l, n: (l, 0, n)),
                  pl.BlockSpec((None, 1, tn), lambda l, n: (l, 0, n))],
        out_specs=pl.BlockSpec((None, N_GROUPS, tn), lambda l, n: (l, 0, n)),
        out_shape=jax.ShapeDtypeStruct((DEPTH, N_GROUPS, 6 * D), F32),
        compiler_params=_cparams(("parallel", "parallel")),
        name="modulation",
    )(cond8, mod_w, mod_b.reshape(DEPTH, 1, 6 * D))


def _ctx_block(i):
    return jnp.minimum(i, NP_T - 1)


def _lat_block(i):
    return jnp.maximum(i - NP_T, 0)


def _pre0_kernel(xp_ref, xs_ref, gv_ref, mv_ref, x_ref, h_ref):
    x = jnp.where(pl.program_id(0) < NP_T, xp_ref[...], xs_ref[...])
    x_ref[...] = x
    h = _rms(x, gv_ref[1:2, :]) * (1.0 + mv_ref[2:3, :]) + mv_ref[1:2, :]
    h_ref[...] = h.astype(BF16)


def _pre0(xp, xs, gv, mv):
    return pl.pallas_call(
        _pre0_kernel,
        grid=(N_T,),
        in_specs=[pl.BlockSpec((TM, D), lambda i: (_ctx_block(i), 0)),
                  pl.BlockSpec((TM, D), lambda i: (_lat_block(i), 0)),
                  pl.BlockSpec((8, D), lambda i: (0, 0)),
                  pl.BlockSpec((None, 8, D), lambda i: (_group_of_tile(i), 0, 0))],
        out_specs=[pl.BlockSpec((TM, D), lambda i: (i, 0))] * 2,
        out_shape=[jax.ShapeDtypeStruct((T_ROWS, D), F32), jax.ShapeDtypeStruct((T_ROWS, D), BF16)],
        compiler_params=_cparams(("arbitrary",)),
        name="pre0",
    )(xp, xs, gv, mv)


def _router(hn, wh_ref, wl_ref, br_ref):
    hh = hn.astype(BF16)
    hl = (hn - hh.astype(F32)).astype(BF16)
    logits = (_dot(hh, wh_ref[...]) + (_dot(hl, wh_ref[...]) + _dot(hh, wl_ref[...]))) + br_ref[...]
    lane = lax.broadcasted_iota(jnp.int32, logits.shape, 1)
    neg = jnp.float32(-jnp.inf)
    logits = jnp.where(lane < N_EXPERTS, logits, neg)
    m1 = jnp.max(logits, axis=-1, keepdims=True)
    i1 = jnp.min(jnp.where(logits == m1, lane, LANES), axis=-1, keepdims=True)
    rest = jnp.where(lane == i1, neg, logits)
    m2 = jnp.max(rest, axis=-1, keepdims=True)
    i2 = jnp.min(jnp.where(rest == m2, lane, LANES), axis=-1, keepdims=True)
    e = jnp.exp(m2 - m1)
    p1 = 1.0 / (1.0 + e)
    p2 = e / (1.0 + e)
    return jnp.where(lane == 0, i1.astype(F32),
                     jnp.where(lane == 1, i2.astype(F32),
                               jnp.where(lane == 2, p1, jnp.where(lane == 3, p2, 0.0))))


def _post_pre(x, out, gv_ref, mv_ref, has_next):
    xn = x + mv_ref[0:1, :] * _rms(out, gv_ref[0:1, :])
    if not has_next:
        return xn, None
    hn = _rms(xn, gv_ref[1:2, :]) * (1.0 + mv_ref[2:3, :]) + mv_ref[1:2, :]
    return xn, hn


def _epilogue(out, x_ref, gv_ref, mv_ref, rest, has_next, want_router):
    rest = list(rest)
    if want_router:
        router_refs = rest[:3]
        rest = rest[3:]
    xn, hn = _post_pre(x_ref[...], out, gv_ref, mv_ref, has_next)
    if not has_next:
        i = pl.program_id(0)

        @pl.when(i < NP_T)
        def _():
            rest[0][...] = xn

        @pl.when(i >= NP_T)
        def _():
            rest[1][...] = xn
        return
    rest[0][...] = xn
    rest[1][...] = hn.astype(rest[1].dtype)
    if want_router:
        rest[2][...] = _router(hn, *router_refs)


def _epilogue_specs(has_next, want_router):
    in_specs = [pl.BlockSpec((TM, D), lambda i, *_: (i, 0)),
                pl.BlockSpec((8, D), lambda i, *_: (0, 0)),
                pl.BlockSpec((None, 8, D), lambda i, *_: (_group_of_tile(i), 0, 0))]
    if not has_next:
        out_specs = [pl.BlockSpec((TM, D), lambda i, *_: (_ctx_block(i), 0)),
                     pl.BlockSpec((TM, D), lambda i, *_: (_lat_block(i), 0))]
        out_shape = [jax.ShapeDtypeStruct((P_ROWS, D), F32), jax.ShapeDtypeStruct((S_ROWS, D), F32)]
        return in_specs, out_specs, out_shape
    out_specs = [pl.BlockSpec((TM, D), lambda i, *_: (i, 0))]
    out_shape = [jax.ShapeDtypeStruct((T_ROWS, D), F32)]
    if want_router:
        in_specs += [pl.BlockSpec((D, LANES), lambda i, *_: (0, 0)),
                     pl.BlockSpec((D, LANES), lambda i, *_: (0, 0)),
                     pl.BlockSpec((1, LANES), lambda i, *_: (0, 0))]
    if has_next:
        out_specs.append(pl.BlockSpec((TM, D), lambda i, *_: (i, 0)))
        out_shape.append(jax.ShapeDtypeStruct((T_ROWS, D), F32 if want_router else BF16))
    if want_router:
        out_specs.append(pl.BlockSpec((TM, LANES), lambda i, *_: (i, 0)))
        out_shape.append(jax.ShapeDtypeStruct((T_ROWS, LANES), F32))
    return in_specs, out_specs, out_shape


def _seq_pos(i, rows):
    seq_len = jnp.where(i < NP_T, SEQ, DEC_SEQ)
    r = lax.broadcasted_iota(jnp.int32, (rows, 1), 0)
    return (i * rows + r) & (seq_len - 1), seq_len, r


def _shifted(m, prev_ref, next_ref, k, pos, seq_len, r):
    rows = m.shape[0]
    if k < 0:
        y = pltpu.roll(m, -k, 0)
        y = jnp.where(r == 0, prev_ref[SUBLANES - 1:SUBLANES, :], y)
        return jnp.where(pos + k < 0, 0.0, y)
    y = pltpu.roll(m, rows - k, 0)
    for q in range(k):
        y = jnp.where(r == rows - k + q, next_ref[q:q + 1, :], y)
    return jnp.where(pos + k >= seq_len, 0.0, y)


def _halo_specs(rows, row_block_of):
    per = rows // SUBLANES
    last = T_ROWS // SUBLANES - 1
    return [pl.BlockSpec((rows, D), lambda *g: (row_block_of(*g), 0)),
            pl.BlockSpec((SUBLANES, D), lambda *g: (jnp.maximum(row_block_of(*g) * per - 1, 0), 0)),
            pl.BlockSpec((SUBLANES, D), lambda *g: (jnp.minimum((row_block_of(*g) + 1) * per, last), 0))]


def _proj_kernel(*refs, mode, has_next, want_router):
    i = pl.program_id(0)
    if mode == "plain":
        a_ref, w_ref = refs[:2]
        rest = refs[2:]
        a = a_ref[...]
    elif mode == "rg":
        yf_ref, yb_ref, g_ref, w_ref = refs[:4]
        rest = refs[4:]
        a = ((yf_ref[...] + yb_ref[...]) * g_ref[...]).astype(BF16)
    else:
        bg_ref, m_ref, mp_ref, mn_ref, cw_ref, w_ref = refs[:6]
        rest = refs[6:]
        m = m_ref[...]
        pos, seq_len, r = _seq_pos(i, TM)
        z = (cw_ref[0:1, :] * _shifted(m, mp_ref, mn_ref, -1, pos, seq_len, r)
             + cw_ref[1:2, :] * m
             + cw_ref[2:3, :] * _shifted(m, mp_ref, mn_ref, 1, pos, seq_len, r))
        a = (bg_ref[...] * z).astype(BF16)
    out = _dot(a, w_ref[...])
    _epilogue(out, rest[0], rest[1], rest[2], rest[3:], has_next, want_router)


def _proj(mode, ins, w, x, gv, mv, has_next=True, router=None):
    want_router = router is not None
    k = w.shape[0]
    row = lambda i: (i, 0)
    if mode == "plain":
        in_specs = [pl.BlockSpec((TM, k), row)]
    elif mode == "rg":
        in_specs = [pl.BlockSpec((TM, D), row)] * 3
    else:
        bg, m, cw = ins
        ins = (bg, m, m, m, cw)
        in_specs = ([pl.BlockSpec((TM, D), row)] + _halo_specs(TM, lambda i: i)
                    + [pl.BlockSpec((8, D), lambda i: (0, 0))])
    in_specs.append(pl.BlockSpec((k, D), lambda i: (0, 0)))
    e_in, out_specs, out_shape = _epilogue_specs(has_next, want_router)
    args = list(ins) + [w, x, gv, mv] + (list(router) if want_router else [])
    return pl.pallas_call(
        functools.partial(_proj_kernel, mode=mode, has_next=has_next, want_router=want_router),
        grid=(N_T,),
        in_specs=in_specs + e_in,
        out_specs=out_specs,
        out_shape=out_shape,
        compiler_params=_cparams(("parallel",)),
        name="proj_" + mode,
    )(*args)


MXU_N = 256
FF_SPLIT = (D_FF // MXU_N + 1) // 2 * MXU_N


def _ffn_kernel(h_ref, wg_ref, wu_ref, wd_ref, *refs, has_next):
    h = h_ref[...]
    out = None
    for lo, hi in ((0, FF_SPLIT), (FF_SPLIT, D_FF)):
        a = _silu(_dot(h, wg_ref[:, lo:hi])) * _dot(h, wu_ref[:, lo:hi])
        part = _dot(a.astype(BF16), wd_ref[lo:hi, :])
        out = part if out is None else out + part
    _epilogue(out, refs[0], refs[1], refs[2], refs[3:], has_next, False)


def _ffn(h, wg, wu, wd, x, gv, mv, has_next=True):
    f = wg.shape[1]
    once = pl.Buffered(1)
    in_specs = [pl.BlockSpec((TM, D), lambda i: (i, 0)),
                pl.BlockSpec((D, f), lambda i: (0, 0), pipeline_mode=once),
                pl.BlockSpec((D, f), lambda i: (0, 0), pipeline_mode=once),
                pl.BlockSpec((f, D), lambda i: (0, 0), pipeline_mode=once)]
    e_in, out_specs, out_shape = _epilogue_specs(has_next, False)
    return pl.pallas_call(
        functools.partial(_ffn_kernel, has_next=has_next),
        grid=(N_T,),
        in_specs=in_specs + e_in,
        out_specs=out_specs,
        out_shape=out_shape,
        compiler_params=_cparams(("parallel",)),
        name="ffn_dense",
    )(h, wg, wu, wd, x, gv, mv)


TE = 256
R_ROWS = 2 * T_ROWS + N_EXPERTS * TE
N_TE = R_ROWS // TE


def _route_meta(route):
    e1 = route[:, 0].astype(jnp.int32)
    e2 = route[:, 1].astype(jnp.int32)
    ids = jnp.arange(N_EXPERTS, dtype=jnp.int32)
    hit = ((e1[:, None] == ids) | (e2[:, None] == ids)).astype(jnp.int32)
    csum = jnp.cumsum(hit, axis=0)
    counts = csum[-1]
    padded = (counts + TE - 1) // TE * TE
    ends = jnp.cumsum(padded)
    offs = ends - padded
    rank = csum - 1
    pos1 = offs[e1] + jnp.take_along_axis(rank, e1[:, None], axis=1)[:, 0]
    pos2 = offs[e2] + jnp.take_along_axis(rank, e2[:, None], axis=1)[:, 0]
    n_used = ends[-1] // TE
    tile_row = jnp.minimum(jnp.arange(N_TE, dtype=jnp.int32), n_used - 1) * TE
    tile_e = jnp.minimum(jnp.searchsorted(ends, tile_row, side='right'), N_EXPERTS - 1).astype(jnp.int32)
    pos = jnp.concatenate([pos1, pos2]).astype(jnp.int32)
    pad = jnp.concatenate([offs + counts, padded - counts]).astype(jnp.int32)
    return pos, pad, tile_e, n_used.astype(jnp.int32).reshape(1)


def _row_copy(src, s, dst, d, sem):
    return pltpu.make_async_copy(src.at[pl.ds(s, 1), :], dst.at[pl.ds(d, 1), :], sem)


def _dispatch_kernel(pos_ref, pad_ref, h_ref, xs_ref, zero_scr, sem):
    i = pl.program_id(0)
    base = i * TM

    for r in range(TM):
        _row_copy(h_ref, r, xs_ref, pos_ref[base + r], sem).start(priority=0)
        _row_copy(h_ref, r, xs_ref, pos_ref[T_ROWS + base + r], sem).start(priority=1)
    for _ in range(2):
        pltpu.make_async_copy(h_ref, xs_ref.at[pl.ds(0, TM), :], sem).wait()

    @pl.when(i == N_T - 1)
    def _():
        zero_scr[...] = jnp.zeros(zero_scr.shape, F32)
        for e in range(N_EXPERTS):
            start, n = pad_ref[e], pad_ref[N_EXPERTS + e]

            def fill(r, c):
                _row_copy(zero_scr, 0, xs_ref, start + r, sem).start()
                return c

            def drain(r, c):
                _row_copy(zero_scr, 0, xs_ref, start, sem).wait()
                return c

            lax.fori_loop(0, n, fill, 0)
            lax.fori_loop(0, n, drain, 0)


def _dispatch(pos, pad, h32):
    return pl.pallas_call(
        _dispatch_kernel,
        grid_spec=pltpu.PrefetchScalarGridSpec(
            num_scalar_prefetch=2,
            grid=(N_T,),
            in_specs=[pl.BlockSpec((TM, D), lambda i, *_: (i, 0))],
            out_specs=pl.BlockSpec(memory_space=pl.ANY),
            scratch_shapes=[pltpu.VMEM((SUBLANES, D), F32), pltpu.SemaphoreType.DMA]),
        out_shape=jax.ShapeDtypeStruct((R_ROWS, D), F32),
        compiler_params=_cparams(("arbitrary",)),
        name="moe_dispatch",
    )(pos, pad, h32)


def _moe_ffn_kernel(te_ref, nu_ref, xs_ref, wg_ref, wu_ref, wd_ref, ys_ref, wg_b, wu_b, wd_b):
    i = pl.program_id(0)

    @pl.when((i == 0) | (te_ref[i] != te_ref[jnp.maximum(i - 1, 0)]))
    def _():
        wg_b[...] = wg_ref[...].astype(BF16)
        wu_b[...] = wu_ref[...].astype(BF16)
        wd_b[...] = wd_ref[...].astype(BF16)

    @pl.when(i < nu_ref[0])
    def _():
        x = xs_ref[...].astype(BF16)
        a = _silu(_dot(x, wg_b[...])) * _dot(x, wu_b[...])
        ys_ref[...] = _dot(a.astype(BF16), wd_b[...])


def _moe_ffn(tile_e, n_used, xs, wg, wu, wd, m):
    f = wg.shape[3]
    row = lambda i, te, nu: (jnp.minimum(i, nu[0] - 1), 0)
    expert = lambda i, te, nu: (m, te[i], 0, 0)
    return pl.pallas_call(
        _moe_ffn_kernel,
        grid_spec=pltpu.PrefetchScalarGridSpec(
            num_scalar_prefetch=2,
            grid=(N_TE,),
            in_specs=[pl.BlockSpec((TE, D), row),
                      pl.BlockSpec((None, None, D, f), expert),
                      pl.BlockSpec((None, None, D, f), expert),
                      pl.BlockSpec((None, None, f, D), expert)],
            out_specs=pl.BlockSpec((TE, D), row),
            scratch_shapes=[pltpu.VMEM((D, f), BF16), pltpu.VMEM((D, f), BF16), pltpu.VMEM((f, D), BF16)]),
        out_shape=jax.ShapeDtypeStruct((R_ROWS, D), F32),
        compiler_params=_cparams(("arbitrary",)),
        name="moe_ffn",
    )(tile_e, n_used, xs, wg, wu, wd)


def _combine_kernel(*refs, has_next):
    pos_ref, ys_ref, route_ref = refs[:3]
    ybuf, sem = refs[-2:]
    refs = refs[3:-2]
    i = pl.program_id(0)
    slot = i % 2

    def issue(base, s, r):
        _row_copy(ys_ref, pos_ref[base + r], ybuf.at[s, 0], r, sem.at[s]).start(priority=0)
        _row_copy(ys_ref, pos_ref[T_ROWS + base + r], ybuf.at[s, 1], r, sem.at[s]).start(priority=1)

    @pl.when(i == 0)
    def _():
        lax.fori_loop(0, TM, lambda r, c: issue(0, 0, r), None)

    @pl.when(i + 1 < N_T)
    def _():
        for r in range(TM):
            issue((i + 1) * TM, 1 - slot, r)

    for c in range(2):
        pltpu.make_async_copy(ys_ref.at[pl.ds(0, TM), :], ybuf.at[slot, c], sem.at[slot]).wait()
    route = route_ref[...]
    y = route[:, 2:3] * ybuf[slot, 0] + route[:, 3:4] * ybuf[slot, 1]
    _epilogue(y, refs[0], refs[1], refs[2], refs[3:], has_next, False)


def _combine(pos, ys, route, x, gv, mv, has_next=True):
    e_in, out_specs, out_shape = _epilogue_specs(has_next, False)
    return pl.pallas_call(
        functools.partial(_combine_kernel, has_next=has_next),
        grid_spec=pltpu.PrefetchScalarGridSpec(
            num_scalar_prefetch=1,
            grid=(N_T,),
            in_specs=[pl.BlockSpec(memory_space=pl.ANY),
                      pl.BlockSpec((TM, LANES), lambda i, *_: (i, 0))] + e_in,
            out_specs=out_specs,
            scratch_shapes=[pltpu.VMEM((2, 2, TM, D), F32), pltpu.SemaphoreType.DMA((2,))]),
        out_shape=out_shape,
        compiler_params=_cparams(("arbitrary",)),
        name="moe_combine",
    )(pos, ys, route, x, gv, mv)


def _mla_down_kernel(h_ref, w_ref, gq_ref, gkv_ref, rc_ref, rlo_ref, rhi_ref,
                     cq_ref, ckv_ref, kr_ref, kvb_ref):
    i = pl.program_id(0)
    y = _dot(h_ref[...], w_ref[...])
    cq_ref[...] = _rms(y[:, :Q_RANK], gq_ref[...]).astype(BF16)
    ckv = _rms(y[:, Q_RANK:Q_RANK + KV_RANK], gkv_ref[...])
    kr = y[:, Q_RANK + KV_RANK:]
    kr = jnp.where(i >= NP_T, _rope(kr, rc_ref[...], rlo_ref[...], rhi_ref[...]), kr)
    ckv_ref[...] = ckv
    kr_ref[...] = kr
    kvb_ref[:, :KV_RANK] = ckv.astype(BF16)
    lane = lax.broadcasted_iota(jnp.int32, kr.shape, 1)
    kvb_ref[:, KV_RANK:] = jnp.where(lane == ROPE_DIM, 1.0, kr).astype(BF16)


def _mla_down(h, w, gq, gkv, tabs):
    n = Q_RANK + KVW
    row = lambda i: (i, 0)
    const = lambda i: (0, 0)
    tab = pl.BlockSpec((TM, LANES), lambda i: (_pos_tile(i), 0))
    return pl.pallas_call(
        _mla_down_kernel,
        grid=(N_T,),
        in_specs=[pl.BlockSpec((TM, D), row), pl.BlockSpec((D, n), const),
                  pl.BlockSpec((1, Q_RANK), const), pl.BlockSpec((1, KV_RANK), const),
                  tab, tab, tab],
        out_specs=[pl.BlockSpec((TM, Q_RANK), row), pl.BlockSpec((TM, KV_RANK), row),
                   pl.BlockSpec((TM, LANES), row), pl.BlockSpec((TM, KVW), row)],
        out_shape=[jax.ShapeDtypeStruct((T_ROWS, Q_RANK), BF16),
                   jax.ShapeDtypeStruct((T_ROWS, KV_RANK), F32),
                   jax.ShapeDtypeStruct((T_ROWS, LANES), F32),
                   jax.ShapeDtypeStruct((T_ROWS, KVW), BF16)],
        compiler_params=_cparams(("parallel",)),
        name="mla_down",
    )(h, w, gq, gkv, *tabs)


def _dot_nt(a, b):
    return lax.dot_general(a, b, (((1,), (1,)), ((), ())), preferred_element_type=F32)


def _mla_q_kernel(cq_ref, w_ref, cr_ref, sr_ref, cc_ref, sc_ref, q_ref):
    i = pl.program_id(0)
    y = _dot_nt(w_ref[...], cq_ref[...]) * ATTN_C2
    is_latent = i >= NP_T
    cr = jnp.where(is_latent, cr_ref[...], 1.0)
    sr = jnp.where(is_latent, sr_ref[...], 0.0)
    cc = jnp.where(is_latent, cc_ref[...], 1.0)
    sc = jnp.where(is_latent, sc_ref[...], 0.0)
    for hd in range(N_HEADS):
        r0 = hd * HEAD_PAD
        x0, x1, x2, x3 = (y[r0 + 8 * a:r0 + 8 * (a + 1), :] for a in range(4))
        rot = jnp.concatenate([x0 * cr - x1 * sr, x0 * sr + x1 * cr,
                               x2 * cc - x3 * sc, x2 * sc + x3 * cc], axis=0)
        q_ref[r0:r0 + ROPE_DIM, :] = rot.astype(BF16)
        q_ref[r0 + ROPE_DIM:r0 + HEAD_PAD, :] = y[r0 + ROPE_DIM:r0 + HEAD_PAD, :].astype(BF16)


def _mla_q(cq, w_t, tabs_t):
    tab = pl.BlockSpec((8, TM), lambda i: (0, _pos_tile(i)))
    return pl.pallas_call(
        _mla_q_kernel,
        grid=(N_T,),
        in_specs=[pl.BlockSpec((TM, Q_RANK), lambda i: (i, 0)),
                  pl.BlockSpec((HP, Q_RANK), lambda i: (0, 0)), tab, tab, tab, tab],
        out_specs=pl.BlockSpec((HP, TM), lambda i: (0, i)),
        out_shape=jax.ShapeDtypeStruct((HP, T_ROWS), BF16),
        compiler_params=_cparams(("parallel",)),
        name="mla_q",
    )(cq, w_t, *tabs_t)


def _mla_kv_kernel(c_ref, wk_ref, wvt_ref, k_ref, vt_ref):
    c = c_ref[...]
    k_ref[...] = _dot(c, wk_ref[...]).astype(BF16)
    vt_ref[...] = _dot_nt(wvt_ref[...], c).astype(BF16)


def _mla_kv(ckvkr, w_k, w_vt):
    rows = ckvkr.shape[0]
    hv = HVX
    return pl.pallas_call(
        _mla_kv_kernel,
        grid=(rows // TM,),
        in_specs=[pl.BlockSpec((TM, KVW), lambda i: (i, 0)),
                  pl.BlockSpec((KVW, HP), lambda i: (0, 0)),
                  pl.BlockSpec((hv, KVW), lambda i: (0, 0))],
        out_specs=[pl.BlockSpec((TM, HP), lambda i: (i, 0)), pl.BlockSpec((hv, TM), lambda i: (0, i))],
        out_shape=[jax.ShapeDtypeStruct((rows, HP), BF16), jax.ShapeDtypeStruct((hv, rows), BF16)],
        compiler_params=_cparams(("parallel",)),
        name="mla_kv",
    )(ckvkr, w_k, w_vt)


def _attn_kernel(*refs, nk, aliased):
    if aliased:
        refs = refs[1:]
    qt_ref, k_ref, vt_ref, o_ref, m_scr, acc_scr, ot_scr = refs
    ki = pl.program_id(2)

    @pl.when(ki == 0)
    def _():
        m_scr[...] = jnp.full(m_scr.shape, -jnp.inf, F32)
        acc_scr[...] = jnp.zeros(acc_scr.shape, F32)

    tk, tq = k_ref.shape[0], qt_ref.shape[1]
    vg = V_EXT // SUBLANES

    def across_sublanes(x, op):
        for k in (4, 2, 1):
            x = op(x, pltpu.roll(x, k, 0))
        return x

    def scores(hd):
        qs = slice(hd * HEAD_PAD, (hd + 1) * HEAD_PAD)
        return _dot(k_ref[:, qs], qt_ref[qs, :]).reshape(tk // SUBLANES, SUBLANES, tq)

    s_next = scores(0)
    for hd in range(N_HEADS):
        vs = slice(hd * V_EXT, (hd + 1) * V_EXT)
        s = s_next
        if hd + 1 < N_HEADS:
            s_next = scores(hd + 1)
        m_prev = m_scr[hd]
        m_new = jnp.maximum(m_prev, across_sublanes(jnp.max(s, axis=0), jnp.maximum))
        alpha = jnp.exp2(m_prev - m_new)
        p = jnp.exp2(s - m_new[None])
        pv = _dot(vt_ref[vs, :], p.reshape(tk, tq).astype(BF16))
        acc = acc_scr[vs, :].reshape(vg, SUBLANES, tq)
        acc_scr[vs, :] = (alpha[None] * acc).reshape(V_EXT, tq) + pv
        m_scr[hd] = m_new

    @pl.when(ki == nk - 1)
    def _():
        for hd in range(N_HEADS):
            r0 = hd * V_EXT
            den = across_sublanes(acc_scr[r0 + V_DIM:r0 + V_DIM + SUBLANES, :], jnp.add)
            acc = acc_scr[r0:r0 + V_DIM, :].reshape(V_DIM // SUBLANES, SUBLANES, tq)
            ot_scr[hd * V_DIM:(hd + 1) * V_DIM, :] = (acc / den[None]).reshape(V_DIM, tq)
        o_ref[...] = ot_scr[...].T.astype(BF16)


def _attention(qt, k, vt, n_b, tq, tk, lq, lk, q_row0, prev_out=None):
    nq, nk = lq // tq, lk // tk
    qb0 = q_row0 // tq
    hv = N_HEADS * V_DIM
    aliased = prev_out is not None
    in_specs = [pl.BlockSpec((HP, tq), lambda b, qi, ki: (0, qb0 + b * nq + qi)),
                pl.BlockSpec((tk, HP), lambda b, qi, ki: (b * nk + ki, 0)),
                pl.BlockSpec((HVX, tk), lambda b, qi, ki: (0, b * nk + ki))]
    args = [qt, k, vt]
    if aliased:
        in_specs = [pl.BlockSpec(memory_space=pl.ANY)] + in_specs
        args = [prev_out] + args
    return pl.pallas_call(
        functools.partial(_attn_kernel, nk=nk, aliased=aliased),
        grid=(n_b, nq, nk),
        in_specs=in_specs,
        out_specs=pl.BlockSpec((tq, hv), lambda b, qi, ki: (qb0 + b * nq + qi, 0)),
        out_shape=jax.ShapeDtypeStruct((T_ROWS, hv), BF16),
        scratch_shapes=[pltpu.VMEM((N_HEADS, SUBLANES, tq), F32), pltpu.VMEM((HVX, tq), F32),
                        pltpu.VMEM((hv, tq), F32)],
        input_output_aliases={0: 0} if aliased else {},
        compiler_params=_cparams(("parallel", "parallel", "arbitrary")),
        name="attn_latent" if aliased else "attn_context",
    )(*args)


def _rg_in_kernel(h_ref, w_ref, u_ref, g_ref):
    y = _dot(h_ref[...], w_ref[...])
    u_ref[...] = y[:, :D]
    g_ref[...] = _gelu_tanh(y[:, D:])


def _rg_in(h, w):
    return pl.pallas_call(
        _rg_in_kernel,
        grid=(N_T,),
        in_specs=[pl.BlockSpec((TM, D), lambda i: (i, 0)), pl.BlockSpec((D, 2 * D), lambda i: (0, 0))],
        out_specs=[pl.BlockSpec((TM, D), lambda i: (i, 0))] * 2,
        out_shape=[jax.ShapeDtypeStruct((T_ROWS, D), F32)] * 2,
        compiler_params=_cparams(("parallel",)),
        name="rg_in",
    )(h, w)


RG_TC = 256
RG_GROUP = 256


def _rg_scan_kernel(*refs, nj, seq_len, aliased):
    if aliased:
        refs = refs[2:]
    (uf_ref, ufp_ref, ufn_ref, ub_ref, ubp_ref, ubn_ref, cw_ref, cb_ref, wai_ref, bai_ref,
     lam_ref, h0_ref, yf_ref, yb_ref, fin_ref, carry_ref) = refs
    j = pl.program_id(1)
    tc = RG_TC
    ng = tc // SUBLANES
    sub = lax.broadcasted_iota(jnp.int32, (1, SUBLANES, 1), 1)

    @pl.when(j == 0)
    def _():
        carry_ref[0:2, :] = h0_ref[...]

    def gates(m_ref, p_ref, n_ref, d, chunk):
        m = m_ref[...]
        ext = jnp.concatenate([jnp.where(chunk == 0, 0.0, p_ref[...]), m,
                               jnp.where(chunk == nj - 1, 0.0, n_ref[...])], axis=0)

        def tap(k):
            return pltpu.roll(ext, (-k) % (tc + 2 * SUBLANES), 0)[SUBLANES:SUBLANES + tc]

        u = (cw_ref[0:1, :] * tap(-1) + cw_ref[1:2, :] * m + cw_ref[2:3, :] * tap(1)
             + cw_ref[3:4, :] * tap(2) + cb_ref[...])
        ub = u.astype(BF16)
        ra, ri = [], []
        for q in range(D // RG_GROUP):
            y = _dot(ub[:, q * RG_GROUP:(q + 1) * RG_GROUP], wai_ref[d, q])
            ra.append(y[:, :RG_GROUP])
            ri.append(y[:, RG_GROUP:])
        rr = jax.nn.sigmoid(jnp.concatenate(ra, axis=1) + bai_ref[d, 0:1, :])
        ii = jax.nn.sigmoid(jnp.concatenate(ri, axis=1) + bai_ref[d, 1:2, :])
        nl = -lam_ref[d:d + 1, :]
        softplus = jnp.maximum(nl, 0.0) + jnp.log1p(jnp.exp(-jnp.abs(nl)))
        log_a = (-RG_C * softplus) * rr
        a = jnp.exp(log_a)
        y = jnp.maximum(-jnp.tanh(log_a) * (a * a + 1.0), 0.0)
        bx = jnp.where(y > 0.0, y * lax.rsqrt(y), 0.0) * (ii * u)
        return a.reshape(ng, SUBLANES, D), bx.reshape(ng, SUBLANES, D)

    a, b = gates(uf_ref, ufp_ref, ufn_ref, 0, j)
    for k in (1, 2, 4):
        ok = sub >= k
        b = jnp.where(ok, a * pltpu.roll(b, k, 1) + b, b)
        a = jnp.where(ok, a * pltpu.roll(a, k, 1), a)
    h = carry_ref[0:1, :]
    for g in range(ng):
        hg = a[g] * h + b[g]
        yf_ref[g * SUBLANES:(g + 1) * SUBLANES, :] = hg
        h = hg[SUBLANES - 1:SUBLANES, :]
    carry_ref[0:1, :] = h

    a, b = gates(ub_ref, ubp_ref, ubn_ref, 1, nj - 1 - j)
    for k in (1, 2, 4):
        ok = sub < SUBLANES - k
        b = jnp.where(ok, a * pltpu.roll(b, SUBLANES - k, 1) + b, b)
        a = jnp.where(ok, a * pltpu.roll(a, SUBLANES - k, 1), a)
    h = carry_ref[1:2, :]
    for g in reversed(range(ng)):
        hg = a[g] * h + b[g]
        yb_ref[g * SUBLANES:(g + 1) * SUBLANES, :] = hg
        h = hg[0:1, :]
    carry_ref[1:2, :] = h

    @pl.when(j == nj - 1)
    def _():
        fin_ref[...] = carry_ref[0:2, :]


def _rg_scan(u, cw, cb, wai, bai, lam, h0, n_seq, seq_len, row0, prev=None):
    nj = seq_len // RG_TC
    b0 = row0 // RG_TC
    aliased = prev is not None
    fwd = lambda s, j: b0 + s * nj + j
    bwd = lambda s, j: b0 + s * nj + (nj - 1 - j)
    const2 = lambda s, j: (0, 0)
    in_specs = (_halo_specs(RG_TC, fwd) + _halo_specs(RG_TC, bwd)
                + [pl.BlockSpec((8, D), const2), pl.BlockSpec((1, D), const2),
                   pl.BlockSpec((2, D // RG_GROUP, RG_GROUP, 2 * RG_GROUP), lambda s, j: (0, 0, 0, 0)),
                   pl.BlockSpec((2, 2, D), lambda s, j: (0, 0, 0)),
                   pl.BlockSpec((2, D), const2),
                   pl.BlockSpec((None, 2, D), lambda s, j: (s, 0, 0))])
    args = [u, u, u, u, u, u, cw, cb, wai, bai, lam, h0]
    if aliased:
        in_specs = [pl.BlockSpec(memory_space=pl.ANY)] * 2 + in_specs
        args = list(prev) + args
    return pl.pallas_call(
        functools.partial(_rg_scan_kernel, nj=nj, seq_len=seq_len, aliased=aliased),
        grid=(n_seq, nj),
        in_specs=in_specs,
        out_specs=[pl.BlockSpec((RG_TC, D), lambda s, j: (fwd(s, j), 0)),
                   pl.BlockSpec((RG_TC, D), lambda s, j: (bwd(s, j), 0)),
                   pl.BlockSpec((None, 2, D), lambda s, j: (s, 0, 0))],
        out_shape=[jax.ShapeDtypeStruct((T_ROWS, D), F32), jax.ShapeDtypeStruct((T_ROWS, D), F32),
                   jax.ShapeDtypeStruct((n_seq, 2, D), F32)],
        scratch_shapes=[pltpu.VMEM((8, D), F32)],
        input_output_aliases={0: 0, 1: 1} if aliased else {},
        compiler_params=_cparams(("parallel", "arbitrary")),
        name="rg_scan_latent" if aliased else "rg_scan_context",
    )(*args)


def _sc_in_kernel(h_ref, w_ref, bg_ref, m_ref):
    y = _dot(h_ref[...], w_ref[...])
    bg_ref[...] = y[:, :D]
    m_ref[...] = y[:, D:2 * D] * y[:, 2 * D:]


def _sc_in(h, w):
    return pl.pallas_call(
        _sc_in_kernel,
        grid=(N_T,),
        in_specs=[pl.BlockSpec((TM, D), lambda i: (i, 0)), pl.BlockSpec((D, 3 * D), lambda i: (0, 0))],
        out_specs=[pl.BlockSpec((TM, D), lambda i: (i, 0))] * 2,
        out_shape=[jax.ShapeDtypeStruct((T_ROWS, D), F32)] * 2,
        compiler_params=_cparams(("parallel",)),
        name="sc_in",
    )(h, w)


def _rope_tables():
    t = np.arange(DEC_SEQ)
    inv = ROPE_THETA ** (-jnp.arange(0, AXIS_DIM, 2, dtype=F32) / AXIS_DIM)
    ang_r = jnp.asarray((t // GRID_W).astype(np.float32))[:, None] * inv
    ang_c = jnp.asarray((t % GRID_W).astype(np.float32))[:, None] * inv
    cr, sr, cc, sc = jnp.cos(ang_r), jnp.sin(ang_r), jnp.cos(ang_c), jnp.sin(ang_c)
    z8 = jnp.zeros((DEC_SEQ, 8), F32)
    pad1 = jnp.ones((DEC_SEQ, LANES - ROPE_DIM), F32)
    pad0 = jnp.zeros((DEC_SEQ, LANES - ROPE_DIM), F32)
    c = jnp.concatenate([cr, cr, cc, cc, pad1], axis=1)
    lo = jnp.concatenate([-sr, z8, -sc, z8, pad0], axis=1)
    hi = jnp.concatenate([z8, sr, z8, sc, pad0], axis=1)
    return (c, lo, hi), (cr.T, sr.T, cc.T, sc.T)


def _pad_heads(w, lo):
    r, h, d = w.shape
    return jnp.pad(w, ((0, 0), (0, 0), (lo, HEAD_PAD - lo - d))).reshape(r, h * HEAD_PAD)


def _mla_weights(w_dq, w_uq, w_dkv, w_uk, w_uv, w_o):
    w_down = jnp.concatenate([w_dq, w_dkv, jnp.zeros((D, KVW - KV_RANK - ROPE_DIM), F32)], axis=1)
    w_q = _pad_heads(w_uq[:, :, QK_NOPE:], 0) + _pad_heads(w_uq[:, :, :QK_NOPE], HEAD_PAD - QK_NOPE)
    place = jnp.broadcast_to(jnp.eye(ROPE_DIM, dtype=F32)[:, None, :], (ROPE_DIM, N_HEADS, ROPE_DIM))
    w_k = jnp.concatenate([_pad_heads(w_uk, HEAD_PAD - QK_NOPE), _pad_heads(place, 0),
                           jnp.zeros((KVW - KV_RANK - ROPE_DIM, HP), F32)], axis=0)
    w_vt = jnp.pad(jnp.transpose(w_uv, (1, 2, 0)), ((0, 0), (0, V_EXT - V_DIM), (0, KVW - KV_RANK)))
    w_vt = w_vt.at[:, V_DIM, ONE_COL].set(1.0).reshape(HVX, KVW)
    return (w_down.astype(BF16), w_q.T.astype(BF16), w_k.astype(BF16), w_vt.astype(BF16),
            w_o.astype(BF16))


def _block_diag_groups(w_a, w_i):
    per = RG_GROUP // RG_BLOCK

    def bd(w):
        w = w.reshape(2, D // RG_GROUP, per, RG_BLOCK, RG_BLOCK)
        eye = jnp.eye(per, dtype=F32)
        return jnp.einsum('dgpkj,pq->dgpkqj', w, eye).reshape(2, D // RG_GROUP, RG_GROUP, RG_GROUP)

    return jnp.concatenate([bd(w_a), bd(w_i)], axis=-1).astype(BF16)


def _pad_rows(w, rows=8):
    return jnp.pad(w, ((0, rows - w.shape[0]), (0, 0)))


def kernel(x_prompt, x_sample, cache_mla_ckv, cache_mla_krope, state_rglru, c, c_ctx, mod_w, mod_b, norm_g, mla_w_dq, mla_g_q, mla_w_uq, mla_w_dkv, mla_g_kv, mla_w_uk, mla_w_uv, mla_w_o, rg_w_x, rg_w_y, rg_conv_w, rg_conv_b, rg_w_a, rg_b_a, rg_w_i, rg_b_i, rg_lambda, rg_w_out, sc_w_in, sc_conv_w, sc_w_out, ffn_w_gate, ffn_w_up, ffn_w_down, moe_w_router, moe_b_router, moe_w_gate, moe_w_up, moe_w_down):
    cond8 = jnp.concatenate([c_ctx[None], c, jnp.zeros((N_GROUPS - 1 - DEC_BATCH, D), F32)], axis=0)
    mod = _modulation(cond8, mod_w, mod_b).reshape(DEPTH, N_GROUPS, 6, D)
    tabs, tabs_t = _rope_tables()
    zrow = jnp.zeros((N_GROUPS, 5, D), F32)

    def vecs(l, sub):
        if sub == 0:
            g_next, sh, sc = norm_g[l, 2], mod[l, :, 3], mod[l, :, 4]
        elif l + 1 < DEPTH:
            g_next, sh, sc = norm_g[l + 1, 0], mod[l + 1, :, 0], mod[l + 1, :, 1]
        else:
            g_next, sh, sc = jnp.zeros((D,), F32), zrow[:, 0], zrow[:, 0]
        gv = _pad_rows(jnp.stack([norm_g[l, 1 + 2 * sub], g_next]))
        mv = jnp.concatenate([jnp.stack([mod[l, :, 2 + 3 * sub], sh, sc], axis=1), zrow], axis=1)
        return gv, mv

    gv0 = _pad_rows(jnp.stack([jnp.zeros((D,), F32), norm_g[0, 0]]))
    mv0 = jnp.concatenate([jnp.stack([zrow[:, 0], mod[0, :, 0], mod[0, :, 1]], axis=1), zrow], axis=1)
    x, h = _pre0(x_prompt.reshape(P_ROWS, D), x_sample.reshape(S_ROWS, D), gv0, mv0)

    new_ckv, new_krope, new_rg = [], [], []
    for l in range(DEPTH):
        kind, j = l % 3, l // 3
        gv, mv = vecs(l, 0)
        router = None
        if l % 2 == 1:
            m = l // 2
            w_r = jnp.pad(moe_w_router[m], ((0, 0), (0, LANES - N_EXPERTS)))
            w_r_hi = w_r.astype(BF16)
            router = (w_r_hi, (w_r - w_r_hi.astype(F32)).astype(BF16),
                      jnp.pad(moe_b_router[m], (0, LANES - N_EXPERTS)).reshape(1, LANES))
        if kind == 0:
            w_down, w_qt, w_k, w_vt, w_op = _mla_weights(mla_w_dq[j], mla_w_uq[j], mla_w_dkv[j],
                                                         mla_w_uk[j], mla_w_uv[j], mla_w_o[j])
            cq, ckv, kr, kvb = _mla_down(h, w_down, mla_g_q[j].reshape(1, Q_RANK),
                                         mla_g_kv[j].reshape(1, KV_RANK), tabs)
            q = _mla_q(cq, w_qt, tabs_t)
            new_ckv.append(ckv[:P_ROWS].reshape(BATCH, SEQ, KV_RANK))
            new_krope.append(kr[:P_ROWS, :ROPE_DIM].reshape(BATCH, SEQ, ROPE_DIM))
            cache = jnp.concatenate(
                [cache_mla_ckv[:, j], cache_mla_krope[:, j], jnp.ones((DEC_BATCH, PAST, 1), F32),
                 jnp.zeros((DEC_BATCH, PAST, KVW - ONE_COL - 1), F32)], axis=-1).astype(BF16)
            kv_lat = jnp.concatenate([cache, kvb[P_ROWS:].reshape(DEC_BATCH, DEC_SEQ, KVW)], axis=1)
            k_c, v_c = _mla_kv(kvb[:P_ROWS], w_k, w_vt)
            k_l, v_l = _mla_kv(kv_lat.reshape(DEC_BATCH * (PAST + DEC_SEQ), KVW), w_k, w_vt)
            o = _attention(q, k_c, v_c, BATCH, SEQ, SEQ, SEQ, SEQ, 0)
            o = _attention(q, k_l, v_l, DEC_BATCH, 1024, 512, DEC_SEQ, PAST + DEC_SEQ, P_ROWS, prev_out=o)
            outs = _proj("plain", (o,), w_op, x, gv, mv, router=router)
        elif kind == 1:
            w_xy = jnp.concatenate([rg_w_x[j], rg_w_y[j]], axis=1).astype(BF16)
            u, gate = _rg_in(h, w_xy)
            wai = _block_diag_groups(rg_w_a[j], rg_w_i[j])
            bai = jnp.stack([rg_b_a[j], rg_b_i[j]], axis=1)
            scan_args = (_pad_rows(rg_conv_w[j]), rg_conv_b[j].reshape(1, D), wai, bai, rg_lambda[j])
            yf, yb, fin = _rg_scan(u, *scan_args, jnp.zeros((BATCH, 2, D), F32), BATCH, SEQ, 0)
            yf, yb, _ = _rg_scan(u, *scan_args, state_rglru[:, j], DEC_BATCH, DEC_SEQ, P_ROWS,
                                 prev=(yf, yb))
            new_rg.append(fin)
            outs = _proj("rg", (yf, yb, gate), rg_w_out[j].astype(BF16), x, gv, mv, router=router)
        else:
            bg, mm = _sc_in(h, sc_w_in[j].astype(BF16))
            outs = _proj("sc", (bg, mm, _pad_rows(sc_conv_w[j])), sc_w_out[j].astype(BF16), x, gv, mv,
                         router=router)
        x, h = outs[0], outs[1]

        gv, mv = vecs(l, 1)
        has_next = l + 1 < DEPTH
        m = l // 2
        if l % 2 == 0:
            outs = _ffn(h, ffn_w_gate[m].astype(BF16), ffn_w_up[m].astype(BF16),
                        ffn_w_down[m].astype(BF16), x, gv, mv, has_next=has_next)
        else:
            route = outs[2]
            pos, pad, tile_e, n_used = _route_meta(route)
            xs = _dispatch(pos, pad, h)
            ys = _moe_ffn(tile_e, n_used, xs, moe_w_gate, moe_w_up, moe_w_down, m)
            outs = _combine(pos, ys, route, x, gv, mv, has_next=has_next)
        x, h = outs[0], outs[1]

    y_prompt = x.reshape(BATCH, SEQ, D)
    y_sample = h.reshape(DEC_BATCH, DEC_SEQ, D)
    return (y_prompt, y_sample, jnp.stack(new_ckv, axis=1), jnp.stack(new_krope, axis=1),
            jnp.stack(new_rg, axis=1))
```

```python
import functools
from typing import NamedTuple

import numpy as np
import jax
import jax.numpy as jnp
from jax import lax
from jax.experimental import pallas as pl
from jax.experimental.pallas import tpu as pltpu

F32 = jnp.float32
BF16 = jnp.bfloat16

D = 1024
BATCH = 32
SEQ = 256
DEPTH = 4
DEC_BATCH = 4
DEC_SEQ = 4096
PAST = 512
GRID_W = 64
N_HEADS = 16
QK_NOPE = 64
ROPE_DIM = 32
AXIS_DIM = 16
V_DIM = 64
Q_RANK = 384
KV_RANK = 256
ROPE_THETA = 10000.0
ATTN_SCALE = (QK_NOPE + ROPE_DIM) ** -0.5
RG_BLOCKS = 16
RG_BLOCK = 64
RG_C = 8.0
D_FF = 2816
N_EXPERTS = 8
D_FF_EXPERT = 1408
EPS = 1e-6

P_ROWS = BATCH * SEQ
S_ROWS = DEC_BATCH * DEC_SEQ
T_ROWS = P_ROWS + S_ROWS
N_GROUPS = 8
HEAD_PAD = 128
HP = N_HEADS * HEAD_PAD
KVW = 384
ONE_COL = KV_RANK + ROPE_DIM
V_EXT = 80
HVX = N_HEADS * V_EXT
ATTN_C2 = ATTN_SCALE * float(np.log2(np.e))

TM = 512
NP_T = P_ROWS // TM
NS_T = DEC_SEQ // TM
N_T = T_ROWS // TM
SUBLANES = 8
LANES = 128
VMEM_LIMIT = 56 * 1024 * 1024


def _cparams(sem):
    return pltpu.CompilerParams(dimension_semantics=sem, vmem_limit_bytes=VMEM_LIMIT)


def _group_of_tile(i):
    return jnp.maximum(i - NP_T + NS_T, 0) // NS_T


def _pos_tile(i):
    return jnp.maximum(i - NP_T, 0) % NS_T


def _dot(a, b):
    return jnp.dot(a, b, preferred_element_type=F32)


def _rms(x, g):
    ms = jnp.mean(x * x, axis=-1, keepdims=True)
    return x * lax.rsqrt(ms + EPS) * g


def _silu(x):
    return x * jax.nn.sigmoid(x)


def _gelu_tanh(x):
    return x * (0.5 * (1.0 + jnp.tanh(np.sqrt(2.0 / np.pi).astype(np.float32)
                                      * (x + 0.044715 * (x * x * x)))))


def _rope(x, c, s_lo, s_hi):
    return (x * c + pltpu.roll(x, LANES - 8, 1) * s_lo + pltpu.roll(x, 8, 1) * s_hi)


def _mod_kernel(c_ref, w_ref, b_ref, o_ref):
    c = c_ref[...]
    s = _silu(c).astype(BF16)
    o_ref[...] = _dot(s, w_ref[...].astype(BF16)) + b_ref[...]


def _modulation(cond8, mod_w, mod_b):
    tn = 1536
    return pl.pallas_call(
        _mod_kernel,
        grid=(DEPTH, 6 * D // tn),
        in_specs=[pl.BlockSpec((N_GROUPS, D), lambda l, n: (0, 0)),
                  pl.BlockSpec((None, D, tn), lambda l, n: (l, 0, n)),
                  pl.BlockSpec((None, 1, tn), lambda l, n: (l, 0, n))],
        out_specs=pl.BlockSpec((None, N_GROUPS, tn), lambda l, n: (l, 0, n)),
        out_shape=jax.ShapeDtypeStruct((DEPTH, N_GROUPS, 6 * D), F32),
        compiler_params=_cparams(("parallel", "parallel")),
        name="modulation",
    )(cond8, mod_w, mod_b.reshape(DEPTH, 1, 6 * D))


def _ctx_block(i):
    return jnp.minimum(i, NP_T - 1)


def _lat_block(i):
    return jnp.maximum(i - NP_T, 0)


def _pre0_kernel(xp_ref, xs_ref, gv_ref, mv_ref, x_ref, h_ref):
    x = jnp.where(pl.program_id(0) < NP_T, xp_ref[...], xs_ref[...])
    x_ref[...] = x
    h_ref[...] = _pre(x, gv_ref, mv_ref, 0, 0).astype(BF16)


def _pre0(xp, xs, gv, mv):
    return pl.pallas_call(
        _pre0_kernel,
        grid=(N_T,),
        in_specs=[pl.BlockSpec((TM, D), lambda i: (_ctx_block(i), 0)),
                  pl.BlockSpec((TM, D), lambda i: (_lat_block(i), 0)),
                  pl.BlockSpec((G_ROWS, D), lambda i: (0, 0)),
                  pl.BlockSpec((None, M_ROWS, D), lambda i: (_group_of_tile(i), 0, 0))],
        out_specs=[pl.BlockSpec((TM, D), lambda i: (i, 0))] * 2,
        out_shape=[jax.ShapeDtypeStruct((T_ROWS, D), F32), jax.ShapeDtypeStruct((T_ROWS, D), BF16)],
        compiler_params=_cparams(("arbitrary",)),
        name="pre0",
    )(xp, xs, gv, mv)


def _router(hn, wh_ref, wl_ref, br_ref):
    hh = hn.astype(BF16)
    hl = (hn - hh.astype(F32)).astype(BF16)
    logits = (_dot(hh, wh_ref[...]) + (_dot(hl, wh_ref[...]) + _dot(hh, wl_ref[...]))) + br_ref[...]
    lane = lax.broadcasted_iota(jnp.int32, logits.shape, 1)
    neg = jnp.float32(-jnp.inf)
    logits = jnp.where(lane < N_EXPERTS, logits, neg)
    m1 = jnp.max(logits, axis=-1, keepdims=True)
    i1 = jnp.min(jnp.where(logits == m1, lane, LANES), axis=-1, keepdims=True)
    rest = jnp.where(lane == i1, neg, logits)
    m2 = jnp.max(rest, axis=-1, keepdims=True)
    i2 = jnp.min(jnp.where(rest == m2, lane, LANES), axis=-1, keepdims=True)
    e = jnp.exp(m2 - m1)
    p1 = 1.0 / (1.0 + e)
    p2 = e / (1.0 + e)
    return jnp.where(lane == 0, i1.astype(F32),
                     jnp.where(lane == 1, i2.astype(F32),
                               jnp.where(lane == 2, p1, jnp.where(lane == 3, p2, 0.0))))


class Stage(NamedTuple):
    has_next: bool
    go: int
    mo: int


G_ROWS = 24
M_ROWS = 32


def _pre(x, gv_ref, mv_ref, go, mo):
    return _rms(x, gv_ref[go:go + 1, :]) * (1.0 + mv_ref[mo + 1:mo + 2, :]) + mv_ref[mo:mo + 1, :]


def _post_pre(x, out, gv_ref, mv_ref, st):
    xn = x + mv_ref[st.mo:st.mo + 1, :] * _rms(out, gv_ref[st.go:st.go + 1, :])
    if not st.has_next:
        return xn, None
    return xn, _pre(xn, gv_ref, mv_ref, st.go + 1, st.mo + 1)


def _epilogue(out, x_ref, gv_ref, mv_ref, rest, st, want_router):
    rest = list(rest)
    if want_router:
        router_refs = rest[:3]
        rest = rest[3:]
    xn, hn = _post_pre(x_ref[...], out, gv_ref, mv_ref, st)
    if not st.has_next:
        i = pl.program_id(0)

        @pl.when(i < NP_T)
        def _():
            rest[0][...] = xn

        @pl.when(i >= NP_T)
        def _():
            rest[1][...] = xn
        return
    rest[0][...] = xn
    rest[1][...] = hn.astype(rest[1].dtype)
    if want_router:
        rest[2][...] = _router(hn, *router_refs)


def _epilogue_specs(has_next, want_router):
    in_specs = [pl.BlockSpec((TM, D), lambda i, *_: (i, 0)),
                pl.BlockSpec((G_ROWS, D), lambda i, *_: (0, 0)),
                pl.BlockSpec((None, M_ROWS, D), lambda i, *_: (_group_of_tile(i), 0, 0))]
    if not has_next:
        out_specs = [pl.BlockSpec((TM, D), lambda i, *_: (_ctx_block(i), 0)),
                     pl.BlockSpec((TM, D), lambda i, *_: (_lat_block(i), 0))]
        out_shape = [jax.ShapeDtypeStruct((P_ROWS, D), F32), jax.ShapeDtypeStruct((S_ROWS, D), F32)]
        return in_specs, out_specs, out_shape
    out_specs = [pl.BlockSpec((TM, D), lambda i, *_: (i, 0))]
    out_shape = [jax.ShapeDtypeStruct((T_ROWS, D), F32)]
    if want_router:
        in_specs += [pl.BlockSpec((D, LANES), lambda i, *_: (0, 0)),
                     pl.BlockSpec((D, LANES), lambda i, *_: (0, 0)),
                     pl.BlockSpec((1, LANES), lambda i, *_: (0, 0))]
    if has_next:
        out_specs.append(pl.BlockSpec((TM, D), lambda i, *_: (i, 0)))
        out_shape.append(jax.ShapeDtypeStruct((T_ROWS, D), F32 if want_router else BF16))
    if want_router:
        out_specs.append(pl.BlockSpec((TM, LANES), lambda i, *_: (i, 0)))
        out_shape.append(jax.ShapeDtypeStruct((T_ROWS, LANES), F32))
    return in_specs, out_specs, out_shape


def _seq_pos(i, rows):
    seq_len = jnp.where(i < NP_T, SEQ, DEC_SEQ)
    r = lax.broadcasted_iota(jnp.int32, (rows, 1), 0)
    return (i * rows + r) & (seq_len - 1), seq_len, r


def _shifted(m, prev_ref, next_ref, k, pos, seq_len, r):
    rows = m.shape[0]
    if k < 0:
        y = pltpu.roll(m, -k, 0)
        y = jnp.where(r == 0, prev_ref[SUBLANES - 1:SUBLANES, :], y)
        return jnp.where(pos + k < 0, 0.0, y)
    y = pltpu.roll(m, rows - k, 0)
    for q in range(k):
        y = jnp.where(r == rows - k + q, next_ref[q:q + 1, :], y)
    return jnp.where(pos + k >= seq_len, 0.0, y)


def _halo_specs(rows, row_block_of):
    per = rows // SUBLANES
    last = T_ROWS // SUBLANES - 1
    return [pl.BlockSpec((rows, D), lambda *g: (row_block_of(*g), 0)),
            pl.BlockSpec((SUBLANES, D), lambda *g: (jnp.maximum(row_block_of(*g) * per - 1, 0), 0)),
            pl.BlockSpec((SUBLANES, D), lambda *g: (jnp.minimum((row_block_of(*g) + 1) * per, last), 0))]


def _proj_kernel(*refs, mode, st, want_router):
    i = pl.program_id(0)
    if mode == "plain":
        a_ref, w_ref = refs[:2]
        rest = refs[2:]
        a = a_ref[...]
    elif mode == "rg":
        yf_ref, yb_ref, g_ref, w_ref = refs[:4]
        rest = refs[4:]
        a = ((yf_ref[...] + yb_ref[...]) * g_ref[...]).astype(BF16)
    else:
        bg_ref, m_ref, mp_ref, mn_ref, cw_ref, w_ref = refs[:6]
        rest = refs[6:]
        m = m_ref[...]
        pos, seq_len, r = _seq_pos(i, TM)
        z = (cw_ref[0:1, :] * _shifted(m, mp_ref, mn_ref, -1, pos, seq_len, r)
             + cw_ref[1:2, :] * m
             + cw_ref[2:3, :] * _shifted(m, mp_ref, mn_ref, 1, pos, seq_len, r))
        a = (bg_ref[...] * z).astype(BF16)
    out = _dot(a, w_ref[...])
    _epilogue(out, rest[0], rest[1], rest[2], rest[3:], st, want_router)


def _proj(mode, ins, w, x, gv, mv, st, router=None):
    want_router = router is not None
    k = w.shape[0]
    row = lambda i: (i, 0)
    if mode == "plain":
        in_specs = [pl.BlockSpec((TM, k), row)]
    elif mode == "rg":
        in_specs = [pl.BlockSpec((TM, D), row)] * 3
    else:
        bg, m, cw = ins
        ins = (bg, m, m, m, cw)
        in_specs = ([pl.BlockSpec((TM, D), row)] + _halo_specs(TM, lambda i: i)
                    + [pl.BlockSpec((8, D), lambda i: (0, 0))])
    in_specs.append(pl.BlockSpec((k, D), lambda i: (0, 0)))
    e_in, out_specs, out_shape = _epilogue_specs(st.has_next, want_router)
    args = list(ins) + [w, x, gv, mv] + (list(router) if want_router else [])
    return pl.pallas_call(
        functools.partial(_proj_kernel, mode=mode, st=st, want_router=want_router),
        grid=(N_T,),
        in_specs=in_specs + e_in,
        out_specs=out_specs,
        out_shape=out_shape,
        compiler_params=_cparams(("parallel",)),
        name="proj_" + mode,
    )(*args)


MXU_N = 256
FF_SPLIT = (D_FF // MXU_N + 1) // 2 * MXU_N


def _ffn_kernel(h_ref, wg_ref, wu_ref, wd_ref, *refs, st):
    h = h_ref[...]
    out = None
    for lo, hi in ((0, FF_SPLIT), (FF_SPLIT, D_FF)):
        a = _silu(_dot(h, wg_ref[:, lo:hi])) * _dot(h, wu_ref[:, lo:hi])
        part = _dot(a.astype(BF16), wd_ref[lo:hi, :])
        out = part if out is None else out + part
    _epilogue(out, refs[0], refs[1], refs[2], refs[3:], st, False)


def _ffn(h, wg, wu, wd, x, gv, mv, st):
    f = wg.shape[1]
    once = pl.Buffered(1)
    in_specs = [pl.BlockSpec((TM, D), lambda i: (i, 0)),
                pl.BlockSpec((D, f), lambda i: (0, 0), pipeline_mode=once),
                pl.BlockSpec((D, f), lambda i: (0, 0), pipeline_mode=once),
                pl.BlockSpec((f, D), lambda i: (0, 0), pipeline_mode=once)]
    e_in, out_specs, out_shape = _epilogue_specs(st.has_next, False)
    return pl.pallas_call(
        functools.partial(_ffn_kernel, st=st),
        grid=(N_T,),
        in_specs=in_specs + e_in,
        out_specs=out_specs,
        out_shape=out_shape,
        compiler_params=_cparams(("parallel",)),
        name="ffn_dense",
    )(h, wg, wu, wd, x, gv, mv)


TE = 256
R_ROWS = 2 * T_ROWS + N_EXPERTS * TE
N_TE = R_ROWS // TE


def _route_meta(route):
    e1 = route[:, 0].astype(jnp.int32)
    e2 = route[:, 1].astype(jnp.int32)
    ids = jnp.arange(N_EXPERTS, dtype=jnp.int32)
    hit = ((e1[:, None] == ids) | (e2[:, None] == ids)).astype(jnp.int32)
    csum = jnp.cumsum(hit, axis=0)
    counts = csum[-1]
    padded = (counts + TE - 1) // TE * TE
    ends = jnp.cumsum(padded)
    offs = ends - padded
    rank = csum - 1
    pos1 = offs[e1] + jnp.take_along_axis(rank, e1[:, None], axis=1)[:, 0]
    pos2 = offs[e2] + jnp.take_along_axis(rank, e2[:, None], axis=1)[:, 0]
    n_used = ends[-1] // TE
    tile_row = jnp.minimum(jnp.arange(N_TE, dtype=jnp.int32), n_used - 1) * TE
    tile_e = jnp.minimum(jnp.searchsorted(ends, tile_row, side='right'), N_EXPERTS - 1).astype(jnp.int32)
    pos = jnp.concatenate([pos1, pos2]).astype(jnp.int32)
    pad = jnp.concatenate([offs + counts, padded - counts]).astype(jnp.int32)
    return pos, pad, tile_e, n_used.astype(jnp.int32).reshape(1)


def _row_copy(src, s, dst, d, sem):
    return pltpu.make_async_copy(src.at[pl.ds(s, 1), :], dst.at[pl.ds(d, 1), :], sem)


def _dispatch_kernel(pos_ref, pad_ref, h_ref, xs_ref, zero_scr, sem):
    i = pl.program_id(0)
    base = i * TM

    for r in range(TM):
        _row_copy(h_ref, r, xs_ref, pos_ref[base + r], sem).start(priority=0)
        _row_copy(h_ref, r, xs_ref, pos_ref[T_ROWS + base + r], sem).start(priority=1)
    for _ in range(2):
        pltpu.make_async_copy(h_ref, xs_ref.at[pl.ds(0, TM), :], sem).wait()

    @pl.when(i == N_T - 1)
    def _():
        zero_scr[...] = jnp.zeros(zero_scr.shape, F32)
        for e in range(N_EXPERTS):
            start, n = pad_ref[e], pad_ref[N_EXPERTS + e]

            def fill(r, c):
                _row_copy(zero_scr, 0, xs_ref, start + r, sem).start()
                return c

            def drain(r, c):
                _row_copy(zero_scr, 0, xs_ref, start, sem).wait()
                return c

            lax.fori_loop(0, n, fill, 0)
            lax.fori_loop(0, n, drain, 0)


def _dispatch(pos, pad, h32):
    return pl.pallas_call(
        _dispatch_kernel,
        grid_spec=pltpu.PrefetchScalarGridSpec(
            num_scalar_prefetch=2,
            grid=(N_T,),
            in_specs=[pl.BlockSpec((TM, D), lambda i, *_: (i, 0))],
            out_specs=pl.BlockSpec(memory_space=pl.ANY),
            scratch_shapes=[pltpu.VMEM((SUBLANES, D), F32), pltpu.SemaphoreType.DMA]),
        out_shape=jax.ShapeDtypeStruct((R_ROWS, D), F32),
        compiler_params=_cparams(("arbitrary",)),
        name="moe_dispatch",
    )(pos, pad, h32)


def _moe_ffn_kernel(te_ref, nu_ref, xs_ref, wg_ref, wu_ref, wd_ref, ys_ref, wg_b, wu_b, wd_b):
    i = pl.program_id(0)

    @pl.when((i == 0) | (te_ref[i] != te_ref[jnp.maximum(i - 1, 0)]))
    def _():
        wg_b[...] = wg_ref[...].astype(BF16)
        wu_b[...] = wu_ref[...].astype(BF16)
        wd_b[...] = wd_ref[...].astype(BF16)

    @pl.when(i < nu_ref[0])
    def _():
        x = xs_ref[...].astype(BF16)
        a = _silu(_dot(x, wg_b[...])) * _dot(x, wu_b[...])
        ys_ref[...] = _dot(a.astype(BF16), wd_b[...])


def _moe_ffn(tile_e, n_used, xs, wg, wu, wd, m):
    f = wg.shape[3]
    row = lambda i, te, nu: (jnp.minimum(i, nu[0] - 1), 0)
    expert = lambda i, te, nu: (m, te[i], 0, 0)
    return pl.pallas_call(
        _moe_ffn_kernel,
        grid_spec=pltpu.PrefetchScalarGridSpec(
            num_scalar_prefetch=2,
            grid=(N_TE,),
            in_specs=[pl.BlockSpec((TE, D), row),
                      pl.BlockSpec((None, None, D, f), expert),
                      pl.BlockSpec((None, None, D, f), expert),
                      pl.BlockSpec((None, None, f, D), expert)],
            out_specs=pl.BlockSpec((TE, D), row),
            scratch_shapes=[pltpu.VMEM((D, f), BF16), pltpu.VMEM((D, f), BF16), pltpu.VMEM((f, D), BF16)]),
        out_shape=jax.ShapeDtypeStruct((R_ROWS, D), F32),
        compiler_params=_cparams(("arbitrary",)),
        name="moe_ffn",
    )(tile_e, n_used, xs, wg, wu, wd)


def _combine_kernel(*refs, st):
    pos_ref, ys_ref, route_ref = refs[:3]
    ybuf, sem = refs[-2:]
    refs = refs[3:-2]
    i = pl.program_id(0)
    slot = i % 2

    def issue(base, s, r):
        _row_copy(ys_ref, pos_ref[base + r], ybuf.at[s, 0], r, sem.at[s]).start(priority=0)
        _row_copy(ys_ref, pos_ref[T_ROWS + base + r], ybuf.at[s, 1], r, sem.at[s]).start(priority=1)

    @pl.when(i == 0)
    def _():
        lax.fori_loop(0, TM, lambda r, c: issue(0, 0, r), None)

    @pl.when(i + 1 < N_T)
    def _():
        for r in range(TM):
            issue((i + 1) * TM, 1 - slot, r)

    for c in range(2):
        pltpu.make_async_copy(ys_ref.at[pl.ds(0, TM), :], ybuf.at[slot, c], sem.at[slot]).wait()
    route = route_ref[...]
    y = route[:, 2:3] * ybuf[slot, 0] + route[:, 3:4] * ybuf[slot, 1]
    _epilogue(y, refs[0], refs[1], refs[2], refs[3:], st, False)


def _combine(pos, ys, route, x, gv, mv, st):
    e_in, out_specs, out_shape = _epilogue_specs(st.has_next, False)
    return pl.pallas_call(
        functools.partial(_combine_kernel, st=st),
        grid_spec=pltpu.PrefetchScalarGridSpec(
            num_scalar_prefetch=1,
            grid=(N_T,),
            in_specs=[pl.BlockSpec(memory_space=pl.ANY),
                      pl.BlockSpec((TM, LANES), lambda i, *_: (i, 0))] + e_in,
            out_specs=out_specs,
            scratch_shapes=[pltpu.VMEM((2, 2, TM, D), F32), pltpu.SemaphoreType.DMA((2,))]),
        out_shape=out_shape,
        compiler_params=_cparams(("arbitrary",)),
        name="moe_combine",
    )(pos, ys, route, x, gv, mv)


def _mla_down_kernel(h_ref, w_ref, gq_ref, gkv_ref, rc_ref, rlo_ref, rhi_ref,
                     cq_ref, ckv_ref, kr_ref, kvb_ref):
    i = pl.program_id(0)
    y = _dot(h_ref[...], w_ref[...])
    cq_ref[...] = _rms(y[:, :Q_RANK], gq_ref[...]).astype(BF16)
    ckv = _rms(y[:, Q_RANK:Q_RANK + KV_RANK], gkv_ref[...])
    kr = y[:, Q_RANK + KV_RANK:]
    kr = jnp.where(i >= NP_T, _rope(kr, rc_ref[...], rlo_ref[...], rhi_ref[...]), kr)
    ckv_ref[...] = ckv
    kr_ref[...] = kr
    kvb_ref[:, :KV_RANK] = ckv.astype(BF16)
    lane = lax.broadcasted_iota(jnp.int32, kr.shape, 1)
    kvb_ref[:, KV_RANK:] = jnp.where(lane == ROPE_DIM, 1.0, kr).astype(BF16)


def _mla_down(h, w, gq, gkv, tabs):
    n = Q_RANK + KVW
    row = lambda i: (i, 0)
    const = lambda i: (0, 0)
    tab = pl.BlockSpec((TM, LANES), lambda i: (_pos_tile(i), 0))
    return pl.pallas_call(
        _mla_down_kernel,
        grid=(N_T,),
        in_specs=[pl.BlockSpec((TM, D), row), pl.BlockSpec((D, n), const),
                  pl.BlockSpec((1, Q_RANK), const), pl.BlockSpec((1, KV_RANK), const),
                  tab, tab, tab],
        out_specs=[pl.BlockSpec((TM, Q_RANK), row), pl.BlockSpec((TM, KV_RANK), row),
                   pl.BlockSpec((TM, LANES), row), pl.BlockSpec((TM, KVW), row)],
        out_shape=[jax.ShapeDtypeStruct((T_ROWS, Q_RANK), BF16),
                   jax.ShapeDtypeStruct((T_ROWS, KV_RANK), F32),
                   jax.ShapeDtypeStruct((T_ROWS, LANES), F32),
                   jax.ShapeDtypeStruct((T_ROWS, KVW), BF16)],
        compiler_params=_cparams(("parallel",)),
        name="mla_down",
    )(h, w, gq, gkv, *tabs)


def _dot_nt(a, b):
    return lax.dot_general(a, b, (((1,), (1,)), ((), ())), preferred_element_type=F32)


def _mla_q_kernel(cq_ref, w_ref, cr_ref, sr_ref, cc_ref, sc_ref, q_ref):
    i = pl.program_id(0)
    y = _dot_nt(w_ref[...], cq_ref[...]) * ATTN_C2
    is_latent = i >= NP_T
    cr = jnp.where(is_latent, cr_ref[...], 1.0)
    sr = jnp.where(is_latent, sr_ref[...], 0.0)
    cc = jnp.where(is_latent, cc_ref[...], 1.0)
    sc = jnp.where(is_latent, sc_ref[...], 0.0)
    for hd in range(N_HEADS):
        r0 = hd * HEAD_PAD
        x0, x1, x2, x3 = (y[r0 + 8 * a:r0 + 8 * (a + 1), :] for a in range(4))
        rot = jnp.concatenate([x0 * cr - x1 * sr, x0 * sr + x1 * cr,
                               x2 * cc - x3 * sc, x2 * sc + x3 * cc], axis=0)
        q_ref[r0:r0 + ROPE_DIM, :] = rot.astype(BF16)
        q_ref[r0 + ROPE_DIM:r0 + HEAD_PAD, :] = y[r0 + ROPE_DIM:r0 + HEAD_PAD, :].astype(BF16)


def _mla_q(cq, w_t, tabs_t):
    tab = pl.BlockSpec((8, TM), lambda i: (0, _pos_tile(i)))
    return pl.pallas_call(
        _mla_q_kernel,
        grid=(N_T,),
        in_specs=[pl.BlockSpec((TM, Q_RANK), lambda i: (i, 0)),
                  pl.BlockSpec((HP, Q_RANK), lambda i: (0, 0)), tab, tab, tab, tab],
        out_specs=pl.BlockSpec((HP, TM), lambda i: (0, i)),
        out_shape=jax.ShapeDtypeStruct((HP, T_ROWS), BF16),
        compiler_params=_cparams(("parallel",)),
        name="mla_q",
    )(cq, w_t, *tabs_t)


def _mla_kv_kernel(c_ref, wk_ref, wvt_ref, k_ref, vt_ref):
    c = c_ref[...]
    k_ref[...] = _dot(c, wk_ref[...]).astype(BF16)
    vt_ref[...] = _dot_nt(wvt_ref[...], c).astype(BF16)


def _mla_kv(ckvkr, w_k, w_vt):
    rows = ckvkr.shape[0]
    hv = HVX
    return pl.pallas_call(
        _mla_kv_kernel,
        grid=(rows // TM,),
        in_specs=[pl.BlockSpec((TM, KVW), lambda i: (i, 0)),
                  pl.BlockSpec((KVW, HP), lambda i: (0, 0)),
                  pl.BlockSpec((hv, KVW), lambda i: (0, 0))],
        out_specs=[pl.BlockSpec((TM, HP), lambda i: (i, 0)), pl.BlockSpec((hv, TM), lambda i: (0, i))],
        out_shape=[jax.ShapeDtypeStruct((rows, HP), BF16), jax.ShapeDtypeStruct((hv, rows), BF16)],
        compiler_params=_cparams(("parallel",)),
        name="mla_kv",
    )(ckvkr, w_k, w_vt)


def _attn_kernel(*refs, nk, aliased):
    if aliased:
        refs = refs[1:]
    qt_ref, k_ref, vt_ref, o_ref, m_scr, acc_scr, ot_scr = refs
    ki = pl.program_id(2)

    @pl.when(ki == 0)
    def _():
        m_scr[...] = jnp.full(m_scr.shape, -jnp.inf, F32)
        acc_scr[...] = jnp.zeros(acc_scr.shape, F32)

    tk, tq = k_ref.shape[0], qt_ref.shape[1]
    vg = V_EXT // SUBLANES

    def across_sublanes(x, op):
        for k in (4, 2, 1):
            x = op(x, pltpu.roll(x, k, 0))
        return x

    def scores(hd):
        qs = slice(hd * HEAD_PAD, (hd + 1) * HEAD_PAD)
        return _dot(k_ref[:, qs], qt_ref[qs, :]).reshape(tk // SUBLANES, SUBLANES, tq)

    ahead = 2
    pending = [scores(hd) for hd in range(ahead)]
    for hd in range(N_HEADS):
        vs = slice(hd * V_EXT, (hd + 1) * V_EXT)
        s = pending.pop(0)
        if hd + ahead < N_HEADS:
            pending.append(scores(hd + ahead))
        m_prev = m_scr[hd]
        m_new = jnp.maximum(m_prev, across_sublanes(jnp.max(s, axis=0), jnp.maximum))
        alpha = jnp.exp2(m_prev - m_new)
        p = jnp.exp2(s - m_new[None])
        pv = _dot(vt_ref[vs, :], p.reshape(tk, tq).astype(BF16))
        acc = acc_scr[vs, :].reshape(vg, SUBLANES, tq)
        acc_scr[vs, :] = (alpha[None] * acc).reshape(V_EXT, tq) + pv
        m_scr[hd] = m_new

    @pl.when(ki == nk - 1)
    def _():
        for hd in range(N_HEADS):
            r0 = hd * V_EXT
            den = across_sublanes(acc_scr[r0 + V_DIM:r0 + V_DIM + SUBLANES, :], jnp.add)
            acc = acc_scr[r0:r0 + V_DIM, :].reshape(V_DIM // SUBLANES, SUBLANES, tq)
            ot_scr[hd * V_DIM:(hd + 1) * V_DIM, :] = (acc / den[None]).reshape(V_DIM, tq)
        o_ref[...] = ot_scr[...].T.astype(BF16)


def _attention(qt, k, vt, n_b, tq, tk, lq, lk, q_row0, prev_out=None):
    nq, nk = lq // tq, lk // tk
    qb0 = q_row0 // tq
    hv = N_HEADS * V_DIM
    aliased = prev_out is not None
    in_specs = [pl.BlockSpec((HP, tq), lambda b, qi, ki: (0, qb0 + b * nq + qi)),
                pl.BlockSpec((tk, HP), lambda b, qi, ki: (b * nk + ki, 0)),
                pl.BlockSpec((HVX, tk), lambda b, qi, ki: (0, b * nk + ki))]
    args = [qt, k, vt]
    if aliased:
        in_specs = [pl.BlockSpec(memory_space=pl.ANY)] + in_specs
        args = [prev_out] + args
    return pl.pallas_call(
        functools.partial(_attn_kernel, nk=nk, aliased=aliased),
        grid=(n_b, nq, nk),
        in_specs=in_specs,
        out_specs=pl.BlockSpec((tq, hv), lambda b, qi, ki: (qb0 + b * nq + qi, 0)),
        out_shape=jax.ShapeDtypeStruct((T_ROWS, hv), BF16),
        scratch_shapes=[pltpu.VMEM((N_HEADS, SUBLANES, tq), F32), pltpu.VMEM((HVX, tq), F32),
                        pltpu.VMEM((hv, tq), F32)],
        input_output_aliases={0: 0} if aliased else {},
        compiler_params=_cparams(("parallel", "parallel", "arbitrary")),
        name="attn_latent" if aliased else "attn_context",
    )(*args)


def _rg_in_kernel(h_ref, w_ref, u_ref, g_ref):
    y = _dot(h_ref[...], w_ref[...])
    u_ref[...] = y[:, :D]
    g_ref[...] = _gelu_tanh(y[:, D:])


def _rg_in(h, w):
    return pl.pallas_call(
        _rg_in_kernel,
        grid=(N_T,),
        in_specs=[pl.BlockSpec((TM, D), lambda i: (i, 0)), pl.BlockSpec((D, 2 * D), lambda i: (0, 0))],
        out_specs=[pl.BlockSpec((TM, D), lambda i: (i, 0))] * 2,
        out_shape=[jax.ShapeDtypeStruct((T_ROWS, D), F32)] * 2,
        compiler_params=_cparams(("parallel",)),
        name="rg_in",
    )(h, w)


RG_TC = 256
RG_GROUP = 256


def _rg_scan_kernel(*refs, nj, seq_len, aliased):
    if aliased:
        refs = refs[2:]
    (uf_ref, ufp_ref, ufn_ref, ub_ref, ubp_ref, ubn_ref, cw_ref, cb_ref, wai_ref, bai_ref,
     lam_ref, h0_ref, yf_ref, yb_ref, fin_ref, carry_ref) = refs
    j = pl.program_id(1)
    tc = RG_TC
    ng = tc // SUBLANES
    sub = lax.broadcasted_iota(jnp.int32, (1, SUBLANES, 1), 1)

    @pl.when(j == 0)
    def _():
        carry_ref[0:2, :] = h0_ref[...]

    def gates(m_ref, p_ref, n_ref, d, chunk):
        m = m_ref[...]
        ext = jnp.concatenate([jnp.where(chunk == 0, 0.0, p_ref[...]), m,
                               jnp.where(chunk == nj - 1, 0.0, n_ref[...])], axis=0)

        def tap(k):
            return pltpu.roll(ext, (-k) % (tc + 2 * SUBLANES), 0)[SUBLANES:SUBLANES + tc]

        u = (cw_ref[0:1, :] * tap(-1) + cw_ref[1:2, :] * m + cw_ref[2:3, :] * tap(1)
             + cw_ref[3:4, :] * tap(2) + cb_ref[...])
        ub = u.astype(BF16)
        ra, ri = [], []
        for q in range(D // RG_GROUP):
            y = _dot(ub[:, q * RG_GROUP:(q + 1) * RG_GROUP], wai_ref[d, q])
            ra.append(y[:, :RG_GROUP])
            ri.append(y[:, RG_GROUP:])
        rr = jax.nn.sigmoid(jnp.concatenate(ra, axis=1) + bai_ref[d, 0:1, :])
        ii = jax.nn.sigmoid(jnp.concatenate(ri, axis=1) + bai_ref[d, 1:2, :])
        nl = -lam_ref[d:d + 1, :]
        softplus = jnp.maximum(nl, 0.0) + jnp.log1p(jnp.exp(-jnp.abs(nl)))
        log_a = (-RG_C * softplus) * rr
        a = jnp.exp(log_a)
        y = jnp.maximum(-jnp.tanh(log_a) * (a * a + 1.0), 0.0)
        bx = jnp.where(y > 0.0, y * lax.rsqrt(y), 0.0) * (ii * u)
        return a.reshape(ng, SUBLANES, D), bx.reshape(ng, SUBLANES, D)

    a, b = gates(uf_ref, ufp_ref, ufn_ref, 0, j)
    for k in (1, 2, 4):
        ok = sub >= k
        b = jnp.where(ok, a * pltpu.roll(b, k, 1) + b, b)
        a = jnp.where(ok, a * pltpu.roll(a, k, 1), a)
    h = carry_ref[0:1, :]
    for g in range(ng):
        hg = a[g] * h + b[g]
        yf_ref[g * SUBLANES:(g + 1) * SUBLANES, :] = hg
        h = hg[SUBLANES - 1:SUBLANES, :]
    carry_ref[0:1, :] = h

    a, b = gates(ub_ref, ubp_ref, ubn_ref, 1, nj - 1 - j)
    for k in (1, 2, 4):
        ok = sub < SUBLANES - k
        b = jnp.where(ok, a * pltpu.roll(b, SUBLANES - k, 1) + b, b)
        a = jnp.where(ok, a * pltpu.roll(a, SUBLANES - k, 1), a)
    h = carry_ref[1:2, :]
    for g in reversed(range(ng)):
        hg = a[g] * h + b[g]
        yb_ref[g * SUBLANES:(g + 1) * SUBLANES, :] = hg
        h = hg[0:1, :]
    carry_ref[1:2, :] = h

    @pl.when(j == nj - 1)
    def _():
        fin_ref[...] = carry_ref[0:2, :]


def _rg_scan(u, cw, cb, wai, bai, lam, h0, n_seq, seq_len, row0, prev=None):
    nj = seq_len // RG_TC
    b0 = row0 // RG_TC
    aliased = prev is not None
    fwd = lambda s, j: b0 + s * nj + j
    bwd = lambda s, j: b0 + s * nj + (nj - 1 - j)
    const2 = lambda s, j: (0, 0)
    in_specs = (_halo_specs(RG_TC, fwd) + _halo_specs(RG_TC, bwd)
                + [pl.BlockSpec((8, D), const2), pl.BlockSpec((1, D), const2),
                   pl.BlockSpec((2, D // RG_GROUP, RG_GROUP, 2 * RG_GROUP), lambda s, j: (0, 0, 0, 0)),
                   pl.BlockSpec((2, 2, D), lambda s, j: (0, 0, 0)),
                   pl.BlockSpec((2, D), const2),
                   pl.BlockSpec((None, 2, D), lambda s, j: (s, 0, 0))])
    args = [u, u, u, u, u, u, cw, cb, wai, bai, lam, h0]
    if aliased:
        in_specs = [pl.BlockSpec(memory_space=pl.ANY)] * 2 + in_specs
        args = list(prev) + args
    return pl.pallas_call(
        functools.partial(_rg_scan_kernel, nj=nj, seq_len=seq_len, aliased=aliased),
        grid=(n_seq, nj),
        in_specs=in_specs,
        out_specs=[pl.BlockSpec((RG_TC, D), lambda s, j: (fwd(s, j), 0)),
                   pl.BlockSpec((RG_TC, D), lambda s, j: (bwd(s, j), 0)),
                   pl.BlockSpec((None, 2, D), lambda s, j: (s, 0, 0))],
        out_shape=[jax.ShapeDtypeStruct((T_ROWS, D), F32), jax.ShapeDtypeStruct((T_ROWS, D), F32),
                   jax.ShapeDtypeStruct((n_seq, 2, D), F32)],
        scratch_shapes=[pltpu.VMEM((8, D), F32)],
        input_output_aliases={0: 0, 1: 1} if aliased else {},
        compiler_params=_cparams(("parallel", "arbitrary")),
        name="rg_scan_latent" if aliased else "rg_scan_context",
    )(*args)


def _sc_in_kernel(h_ref, w_ref, bg_ref, m_ref):
    y = _dot(h_ref[...], w_ref[...])
    bg_ref[...] = y[:, :D]
    m_ref[...] = y[:, D:2 * D] * y[:, 2 * D:]


def _sc_in(h, w):
    return pl.pallas_call(
        _sc_in_kernel,
        grid=(N_T,),
        in_specs=[pl.BlockSpec((TM, D), lambda i: (i, 0)), pl.BlockSpec((D, 3 * D), lambda i: (0, 0))],
        out_specs=[pl.BlockSpec((TM, D), lambda i: (i, 0))] * 2,
        out_shape=[jax.ShapeDtypeStruct((T_ROWS, D), F32)] * 2,
        compiler_params=_cparams(("parallel",)),
        name="sc_in",
    )(h, w)


def _rope_tables():
    t = np.arange(DEC_SEQ)
    inv = ROPE_THETA ** (-jnp.arange(0, AXIS_DIM, 2, dtype=F32) / AXIS_DIM)
    ang_r = jnp.asarray((t // GRID_W).astype(np.float32))[:, None] * inv
    ang_c = jnp.asarray((t % GRID_W).astype(np.float32))[:, None] * inv
    cr, sr, cc, sc = jnp.cos(ang_r), jnp.sin(ang_r), jnp.cos(ang_c), jnp.sin(ang_c)
    z8 = jnp.zeros((DEC_SEQ, 8), F32)
    pad1 = jnp.ones((DEC_SEQ, LANES - ROPE_DIM), F32)
    pad0 = jnp.zeros((DEC_SEQ, LANES - ROPE_DIM), F32)
    c = jnp.concatenate([cr, cr, cc, cc, pad1], axis=1)
    lo = jnp.concatenate([-sr, z8, -sc, z8, pad0], axis=1)
    hi = jnp.concatenate([z8, sr, z8, sc, pad0], axis=1)
    return (c, lo, hi), (cr.T, sr.T, cc.T, sc.T)


def _pad_heads(w, lo):
    r, h, d = w.shape
    return jnp.pad(w, ((0, 0), (0, 0), (lo, HEAD_PAD - lo - d))).reshape(r, h * HEAD_PAD)


def _mla_weights(w_dq, w_uq, w_dkv, w_uk, w_uv, w_o):
    w_down = jnp.concatenate([w_dq, w_dkv, jnp.zeros((D, KVW - KV_RANK - ROPE_DIM), F32)], axis=1)
    w_q = _pad_heads(w_uq[:, :, QK_NOPE:], 0) + _pad_heads(w_uq[:, :, :QK_NOPE], HEAD_PAD - QK_NOPE)
    place = jnp.broadcast_to(jnp.eye(ROPE_DIM, dtype=F32)[:, None, :], (ROPE_DIM, N_HEADS, ROPE_DIM))
    w_k = jnp.concatenate([_pad_heads(w_uk, HEAD_PAD - QK_NOPE), _pad_heads(place, 0),
                           jnp.zeros((KVW - KV_RANK - ROPE_DIM, HP), F32)], axis=0)
    w_vt = jnp.pad(jnp.transpose(w_uv, (1, 2, 0)), ((0, 0), (0, V_EXT - V_DIM), (0, KVW - KV_RANK)))
    w_vt = w_vt.at[:, V_DIM, ONE_COL].set(1.0).reshape(HVX, KVW)
    return (w_down.astype(BF16), w_q.T.astype(BF16), w_k.astype(BF16), w_vt.astype(BF16),
            w_o.astype(BF16))


def _block_diag_groups(w_a, w_i):
    per = RG_GROUP // RG_BLOCK

    def bd(w):
        w = w.reshape(2, D // RG_GROUP, per, RG_BLOCK, RG_BLOCK)
        eye = jnp.eye(per, dtype=F32)
        return jnp.einsum('dgpkj,pq->dgpkqj', w, eye).reshape(2, D // RG_GROUP, RG_GROUP, RG_GROUP)

    return jnp.concatenate([bd(w_a), bd(w_i)], axis=-1).astype(BF16)


def _pad_rows(w, rows=8):
    return jnp.pad(w, ((0, rows - w.shape[0]), (0, 0)))


def kernel(x_prompt, x_sample, cache_mla_ckv, cache_mla_krope, state_rglru, c, c_ctx, mod_w, mod_b, norm_g, mla_w_dq, mla_g_q, mla_w_uq, mla_w_dkv, mla_g_kv, mla_w_uk, mla_w_uv, mla_w_o, rg_w_x, rg_w_y, rg_conv_w, rg_conv_b, rg_w_a, rg_b_a, rg_w_i, rg_b_i, rg_lambda, rg_w_out, sc_w_in, sc_conv_w, sc_w_out, ffn_w_gate, ffn_w_up, ffn_w_down, moe_w_router, moe_b_router, moe_w_gate, moe_w_up, moe_w_down):
    cond8 = jnp.concatenate([c_ctx[None], c, jnp.zeros((N_GROUPS - 1 - DEC_BATCH, D), F32)], axis=0)
    mod = _modulation(cond8, mod_w, mod_b).reshape(DEPTH, N_GROUPS, 6, D)
    mv = jnp.pad(jnp.transpose(mod, (1, 0, 2, 3)).reshape(N_GROUPS, DEPTH * 6, D),
                 ((0, 0), (0, M_ROWS - DEPTH * 6), (0, 0)))
    gv = jnp.pad(norm_g.reshape(DEPTH * 4, D), ((0, G_ROWS - DEPTH * 4), (0, 0)))
    tabs, tabs_t = _rope_tables()

    def stage(l, sub):
        has_next = sub == 0 or l + 1 < DEPTH
        return Stage(has_next, 4 * l + 1 + 2 * sub, 6 * l + 2 + 3 * sub)

    x, h = _pre0(x_prompt.reshape(P_ROWS, D), x_sample.reshape(S_ROWS, D), gv, mv)

    new_ckv, new_krope, new_rg = [], [], []
    for l in range(DEPTH):
        kind, j = l % 3, l // 3
        st = stage(l, 0)
        router = None
        if l % 2 == 1:
            m = l // 2
            w_r = jnp.pad(moe_w_router[m], ((0, 0), (0, LANES - N_EXPERTS)))
            w_r_hi = w_r.astype(BF16)
            router = (w_r_hi, (w_r - w_r_hi.astype(F32)).astype(BF16),
                      jnp.pad(moe_b_router[m], (0, LANES - N_EXPERTS)).reshape(1, LANES))
        if kind == 0:
            w_down, w_qt, w_k, w_vt, w_op = _mla_weights(mla_w_dq[j], mla_w_uq[j], mla_w_dkv[j],
                                                         mla_w_uk[j], mla_w_uv[j], mla_w_o[j])
            cq, ckv, kr, kvb = _mla_down(h, w_down, mla_g_q[j].reshape(1, Q_RANK),
                                         mla_g_kv[j].reshape(1, KV_RANK), tabs)
            q = _mla_q(cq, w_qt, tabs_t)
            new_ckv.append(ckv[:P_ROWS].reshape(BATCH, SEQ, KV_RANK))
            new_krope.append(kr[:P_ROWS, :ROPE_DIM].reshape(BATCH, SEQ, ROPE_DIM))
            cache = jnp.concatenate(
                [cache_mla_ckv[:, j], cache_mla_krope[:, j], jnp.ones((DEC_BATCH, PAST, 1), F32),
                 jnp.zeros((DEC_BATCH, PAST, KVW - ONE_COL - 1), F32)], axis=-1).astype(BF16)
            kv_lat = jnp.concatenate([cache, kvb[P_ROWS:].reshape(DEC_BATCH, DEC_SEQ, KVW)], axis=1)
            k_c, v_c = _mla_kv(kvb[:P_ROWS], w_k, w_vt)
            k_l, v_l = _mla_kv(kv_lat.reshape(DEC_BATCH * (PAST + DEC_SEQ), KVW), w_k, w_vt)
            o = _attention(q, k_c, v_c, BATCH, SEQ, SEQ, SEQ, SEQ, 0)
            o = _attention(q, k_l, v_l, DEC_BATCH, 1024, 512, DEC_SEQ, PAST + DEC_SEQ, P_ROWS, prev_out=o)
            outs = _proj("plain", (o,), w_op, x, gv, mv, st, router=router)
        elif kind == 1:
            w_xy = jnp.concatenate([rg_w_x[j], rg_w_y[j]], axis=1).astype(BF16)
            u, gate = _rg_in(h, w_xy)
            wai = _block_diag_groups(rg_w_a[j], rg_w_i[j])
            bai = jnp.stack([rg_b_a[j], rg_b_i[j]], axis=1)
            scan_args = (_pad_rows(rg_conv_w[j]), rg_conv_b[j].reshape(1, D), wai, bai, rg_lambda[j])
            yf, yb, fin = _rg_scan(u, *scan_args, jnp.zeros((BATCH, 2, D), F32), BATCH, SEQ, 0)
            yf, yb, _ = _rg_scan(u, *scan_args, state_rglru[:, j], DEC_BATCH, DEC_SEQ, P_ROWS,
                                 prev=(yf, yb))
            new_rg.append(fin)
            outs = _proj("rg", (yf, yb, gate), rg_w_out[j].astype(BF16), x, gv, mv, st, router=router)
        else:
            bg, mm = _sc_in(h, sc_w_in[j].astype(BF16))
            outs = _proj("sc", (bg, mm, _pad_rows(sc_conv_w[j])), sc_w_out[j].astype(BF16), x, gv, mv,
                         st, router=router)
        x, h = outs[0], outs[1]

        st = stage(l, 1)
        m = l // 2
        if l % 2 == 0:
            outs = _ffn(h, ffn_w_gate[m].astype(BF16), ffn_w_up[m].astype(BF16),
                        ffn_w_down[m].astype(BF16), x, gv, mv, st)
        else:
            route = outs[2]
            pos, pad, tile_e, n_used = _route_meta(route)
            xs = _dispatch(pos, pad, h)
            ys = _moe_ffn(tile_e, n_used, xs, moe_w_gate, moe_w_up, moe_w_down, m)
            outs = _combine(pos, ys, route, x, gv, mv, st)
        x, h = outs[0], outs[1]

    y_prompt = x.reshape(BATCH, SEQ, D)
    y_sample = h.reshape(DEC_BATCH, DEC_SEQ, D)
    return (y_prompt, y_sample, jnp.stack(new_ckv, axis=1), jnp.stack(new_krope, axis=1),
            jnp.stack(new_rg, axis=1))
```

```python
import functools
from typing import NamedTuple

import numpy as np
import jax
import jax.numpy as jnp
from jax import lax
from jax.experimental import pallas as pl
from jax.experimental.pallas import tpu as pltpu

F32 = jnp.float32
BF16 = jnp.bfloat16

D = 1024
BATCH = 32
SEQ = 256
DEPTH = 4
DEC_BATCH = 4
DEC_SEQ = 4096
PAST = 512
GRID_W = 64
N_HEADS = 16
QK_NOPE = 64
ROPE_DIM = 32
AXIS_DIM = 16
V_DIM = 64
Q_RANK = 384
KV_RANK = 256
ROPE_THETA = 10000.0
ATTN_SCALE = (QK_NOPE + ROPE_DIM) ** -0.5
RG_BLOCKS = 16
RG_BLOCK = 64
RG_C = 8.0
D_FF = 2816
N_EXPERTS = 8
D_FF_EXPERT = 1408
EPS = 1e-6

P_ROWS = BATCH * SEQ
S_ROWS = DEC_BATCH * DEC_SEQ
T_ROWS = P_ROWS + S_ROWS
N_GROUPS = 8
HEAD_PAD = 128
HP = N_HEADS * HEAD_PAD
KVW = 384
ONE_COL = KV_RANK + ROPE_DIM
V_EXT = 80
HVX = N_HEADS * V_EXT
ATTN_C2 = ATTN_SCALE * float(np.log2(np.e))

TM = 512
NP_T = P_ROWS // TM
NS_T = DEC_SEQ // TM
N_T = T_ROWS // TM
SUBLANES = 8
LANES = 128
VMEM_LIMIT = 56 * 1024 * 1024


def _cparams(sem):
    return pltpu.CompilerParams(dimension_semantics=sem, vmem_limit_bytes=VMEM_LIMIT)


def _group_of_tile(i):
    return jnp.maximum(i - NP_T + NS_T, 0) // NS_T


def _pos_tile(i):
    return jnp.maximum(i - NP_T, 0) % NS_T


def _dot(a, b):
    return jnp.dot(a, b, preferred_element_type=F32)


def _rms(x, g):
    ms = jnp.mean(x * x, axis=-1, keepdims=True)
    return x * lax.rsqrt(ms + EPS) * g


def _silu(x):
    return x * jax.nn.sigmoid(x)


def _gelu_tanh(x):
    return x * (0.5 * (1.0 + jnp.tanh(np.sqrt(2.0 / np.pi).astype(np.float32)
                                      * (x + 0.044715 * (x * x * x)))))


def _rope(x, c, s_lo, s_hi):
    return (x * c + pltpu.roll(x, LANES - 8, 1) * s_lo + pltpu.roll(x, 8, 1) * s_hi)


def _mod_kernel(c_ref, w_ref, b_ref, o_ref):
    c = c_ref[...]
    s = _silu(c).astype(BF16)
    o_ref[...] = _dot(s, w_ref[...].astype(BF16)) + b_ref[...]


def _modulation(cond8, mod_w, mod_b):
    tn = 1536
    return pl.pallas_call(
        _mod_kernel,
        grid=(DEPTH, 6 * D // tn),
        in_specs=[pl.BlockSpec((N_GROUPS, D), lambda l, n: (0, 0)),
                  pl.BlockSpec((None, D, tn), lambda l, n: (l, 0, n)),
                  pl.BlockSpec((None, 1, tn), lambda l, n: (l, 0, n))],
        out_specs=pl.BlockSpec((None, N_GROUPS, tn), lambda l, n: (l, 0, n)),
        out_shape=jax.ShapeDtypeStruct((DEPTH, N_GROUPS, 6 * D), F32),
        compiler_params=_cparams(("parallel", "parallel")),
        name="modulation",
    )(cond8, mod_w, mod_b.reshape(DEPTH, 1, 6 * D))


def _ctx_block(i):
    return jnp.minimum(i, NP_T - 1)


def _lat_block(i):
    return jnp.maximum(i - NP_T, 0)


def _pre0_kernel(xp_ref, xs_ref, gv_ref, mv_ref, x_ref, h_ref):
    x = jnp.where(pl.program_id(0) < NP_T, xp_ref[...], xs_ref[...])
    x_ref[...] = x
    h_ref[...] = _pre(x, gv_ref, mv_ref, 0, 0).astype(BF16)


def _pre0(xp, xs, gv, mv):
    return pl.pallas_call(
        _pre0_kernel,
        grid=(N_T,),
        in_specs=[pl.BlockSpec((TM, D), lambda i: (_ctx_block(i), 0)),
                  pl.BlockSpec((TM, D), lambda i: (_lat_block(i), 0)),
                  pl.BlockSpec((G_ROWS, D), lambda i: (0, 0)),
                  pl.BlockSpec((None, M_ROWS, D), lambda i: (_group_of_tile(i), 0, 0))],
        out_specs=[pl.BlockSpec((TM, D), lambda i: (i, 0))] * 2,
        out_shape=[jax.ShapeDtypeStruct((T_ROWS, D), F32), jax.ShapeDtypeStruct((T_ROWS, D), BF16)],
        compiler_params=_cparams(("arbitrary",)),
        name="pre0",
    )(xp, xs, gv, mv)


def _router(hn, wh_ref, wl_ref, br_ref):
    hh = hn.astype(BF16)
    hl = (hn - hh.astype(F32)).astype(BF16)
    logits = (_dot(hh, wh_ref[...]) + (_dot(hl, wh_ref[...]) + _dot(hh, wl_ref[...]))) + br_ref[...]
    lane = lax.broadcasted_iota(jnp.int32, logits.shape, 1)
    neg = jnp.float32(-jnp.inf)
    logits = jnp.where(lane < N_EXPERTS, logits, neg)
    m1 = jnp.max(logits, axis=-1, keepdims=True)
    i1 = jnp.min(jnp.where(logits == m1, lane, LANES), axis=-1, keepdims=True)
    rest = jnp.where(lane == i1, neg, logits)
    m2 = jnp.max(rest, axis=-1, keepdims=True)
    i2 = jnp.min(jnp.where(rest == m2, lane, LANES), axis=-1, keepdims=True)
    e = jnp.exp(m2 - m1)
    p1 = 1.0 / (1.0 + e)
    p2 = e / (1.0 + e)
    return jnp.where(lane == 0, i1.astype(F32),
                     jnp.where(lane == 1, i2.astype(F32),
                               jnp.where(lane == 2, p1, jnp.where(lane == 3, p2, 0.0))))


class Stage(NamedTuple):
    has_next: bool
    go: int
    mo: int


G_ROWS = 24
M_ROWS = 32


def _pre(x, gv_ref, mv_ref, go, mo):
    return _rms(x, gv_ref[go:go + 1, :]) * (1.0 + mv_ref[mo + 1:mo + 2, :]) + mv_ref[mo:mo + 1, :]


def _post_pre(x, out, gv_ref, mv_ref, st):
    xn = x + mv_ref[st.mo:st.mo + 1, :] * _rms(out, gv_ref[st.go:st.go + 1, :])
    if not st.has_next:
        return xn, None
    return xn, _pre(xn, gv_ref, mv_ref, st.go + 1, st.mo + 1)


def _epilogue(out, x_ref, gv_ref, mv_ref, rest, st, want_router):
    rest = list(rest)
    if want_router:
        router_refs = rest[:3]
        rest = rest[3:]
    xn, hn = _post_pre(x_ref[...], out, gv_ref, mv_ref, st)
    if not st.has_next:
        i = pl.program_id(0)

        @pl.when(i < NP_T)
        def _():
            rest[0][...] = xn

        @pl.when(i >= NP_T)
        def _():
            rest[1][...] = xn
        return
    rest[0][...] = xn
    if want_router:
        _to_tiles(rest[1], hn)
        rest[2][...] = _router(hn, *router_refs)
    else:
        rest[1][...] = hn.astype(BF16)


def _epilogue_specs(has_next, want_router):
    in_specs = [pl.BlockSpec((TM, D), lambda i, *_: (i, 0)),
                pl.BlockSpec((G_ROWS, D), lambda i, *_: (0, 0)),
                pl.BlockSpec((None, M_ROWS, D), lambda i, *_: (_group_of_tile(i), 0, 0))]
    if not has_next:
        out_specs = [pl.BlockSpec((TM, D), lambda i, *_: (_ctx_block(i), 0)),
                     pl.BlockSpec((TM, D), lambda i, *_: (_lat_block(i), 0))]
        out_shape = [jax.ShapeDtypeStruct((P_ROWS, D), F32), jax.ShapeDtypeStruct((S_ROWS, D), F32)]
        return in_specs, out_specs, out_shape
    out_specs = [pl.BlockSpec((TM, D), lambda i, *_: (i, 0))]
    out_shape = [jax.ShapeDtypeStruct((T_ROWS, D), F32)]
    if want_router:
        in_specs += [pl.BlockSpec((D, LANES), lambda i, *_: (0, 0)),
                     pl.BlockSpec((D, LANES), lambda i, *_: (0, 0)),
                     pl.BlockSpec((1, LANES), lambda i, *_: (0, 0))]
    if want_router:
        out_specs.append(pl.BlockSpec((TM * SUBLANES, LANES), lambda i, *_: (i, 0)))
        out_shape.append(jax.ShapeDtypeStruct((T_ROWS * SUBLANES, LANES), F32))
    else:
        out_specs.append(pl.BlockSpec((TM, D), lambda i, *_: (i, 0)))
        out_shape.append(jax.ShapeDtypeStruct((T_ROWS, D), BF16))
    if want_router:
        out_specs.append(pl.BlockSpec((TM, LANES), lambda i, *_: (i, 0)))
        out_shape.append(jax.ShapeDtypeStruct((T_ROWS, LANES), F32))
    return in_specs, out_specs, out_shape


def _seq_pos(i, rows):
    seq_len = jnp.where(i < NP_T, SEQ, DEC_SEQ)
    r = lax.broadcasted_iota(jnp.int32, (rows, 1), 0)
    return (i * rows + r) & (seq_len - 1), seq_len, r


def _shifted(m, prev_ref, next_ref, k, pos, seq_len, r):
    rows = m.shape[0]
    if k < 0:
        y = pltpu.roll(m, -k, 0)
        y = jnp.where(r == 0, prev_ref[SUBLANES - 1:SUBLANES, :], y)
        return jnp.where(pos + k < 0, 0.0, y)
    y = pltpu.roll(m, rows - k, 0)
    for q in range(k):
        y = jnp.where(r == rows - k + q, next_ref[q:q + 1, :], y)
    return jnp.where(pos + k >= seq_len, 0.0, y)


def _halo_specs(rows, row_block_of):
    per = rows // SUBLANES
    last = T_ROWS // SUBLANES - 1
    return [pl.BlockSpec((rows, D), lambda *g: (row_block_of(*g), 0)),
            pl.BlockSpec((SUBLANES, D), lambda *g: (jnp.maximum(row_block_of(*g) * per - 1, 0), 0)),
            pl.BlockSpec((SUBLANES, D), lambda *g: (jnp.minimum((row_block_of(*g) + 1) * per, last), 0))]


def _proj_kernel(*refs, mode, st, want_router):
    i = pl.program_id(0)
    if mode == "plain":
        a_ref, w_ref = refs[:2]
        rest = refs[2:]
        a = a_ref[...]
    elif mode == "rg":
        yf_ref, yb_ref, g_ref, w_ref = refs[:4]
        rest = refs[4:]
        a = ((yf_ref[...] + yb_ref[...]) * g_ref[...]).astype(BF16)
    else:
        bg_ref, m_ref, mp_ref, mn_ref, cw_ref, w_ref = refs[:6]
        rest = refs[6:]
        m = m_ref[...]
        pos, seq_len, r = _seq_pos(i, TM)
        z = (cw_ref[0:1, :] * _shifted(m, mp_ref, mn_ref, -1, pos, seq_len, r)
             + cw_ref[1:2, :] * m
             + cw_ref[2:3, :] * _shifted(m, mp_ref, mn_ref, 1, pos, seq_len, r))
        a = (bg_ref[...] * z).astype(BF16)
    out = _dot(a, w_ref[...])
    _epilogue(out, rest[0], rest[1], rest[2], rest[3:], st, want_router)


def _proj(mode, ins, w, x, gv, mv, st, router=None):
    want_router = router is not None
    k = w.shape[0]
    row = lambda i: (i, 0)
    if mode == "plain":
        in_specs = [pl.BlockSpec((TM, k), row)]
    elif mode == "rg":
        in_specs = [pl.BlockSpec((TM, D), row)] * 3
    else:
        bg, m, cw = ins
        ins = (bg, m, m, m, cw)
        in_specs = ([pl.BlockSpec((TM, D), row)] + _halo_specs(TM, lambda i: i)
                    + [pl.BlockSpec((8, D), lambda i: (0, 0))])
    in_specs.append(pl.BlockSpec((k, D), lambda i: (0, 0)))
    e_in, out_specs, out_shape = _epilogue_specs(st.has_next, want_router)
    args = list(ins) + [w, x, gv, mv] + (list(router) if want_router else [])
    return pl.pallas_call(
        functools.partial(_proj_kernel, mode=mode, st=st, want_router=want_router),
        grid=(N_T,),
        in_specs=in_specs + e_in,
        out_specs=out_specs,
        out_shape=out_shape,
        compiler_params=_cparams(("parallel",)),
        name="proj_" + mode,
    )(*args)


MXU_N = 256
FF_SPLIT = (D_FF // MXU_N + 1) // 2 * MXU_N


def _ffn_kernel(h_ref, wg_ref, wu_ref, wd_ref, *refs, st):
    h = h_ref[...]
    out = None
    for lo, hi in ((0, FF_SPLIT), (FF_SPLIT, D_FF)):
        a = _silu(_dot(h, wg_ref[:, lo:hi])) * _dot(h, wu_ref[:, lo:hi])
        part = _dot(a.astype(BF16), wd_ref[lo:hi, :])
        out = part if out is None else out + part
    _epilogue(out, refs[0], refs[1], refs[2], refs[3:], st, False)


def _ffn(h, wg, wu, wd, x, gv, mv, st):
    f = wg.shape[1]
    once = pl.Buffered(1)
    in_specs = [pl.BlockSpec((TM, D), lambda i: (i, 0)),
                pl.BlockSpec((D, f), lambda i: (0, 0), pipeline_mode=once),
                pl.BlockSpec((D, f), lambda i: (0, 0), pipeline_mode=once),
                pl.BlockSpec((f, D), lambda i: (0, 0), pipeline_mode=once)]
    e_in, out_specs, out_shape = _epilogue_specs(st.has_next, False)
    return pl.pallas_call(
        functools.partial(_ffn_kernel, st=st),
        grid=(N_T,),
        in_specs=in_specs + e_in,
        out_specs=out_specs,
        out_shape=out_shape,
        compiler_params=_cparams(("parallel",)),
        name="ffn_dense",
    )(h, wg, wu, wd, x, gv, mv)


TE = 256
R_ROWS = 2 * T_ROWS + N_EXPERTS * TE
N_TE = R_ROWS // TE


def _route_meta(route):
    e1 = route[:, 0].astype(jnp.int32)
    e2 = route[:, 1].astype(jnp.int32)
    ids = jnp.arange(N_EXPERTS, dtype=jnp.int32)
    hit = ((e1[:, None] == ids) | (e2[:, None] == ids)).astype(jnp.int32)
    csum = jnp.cumsum(hit, axis=0)
    counts = csum[-1]
    padded = (counts + TE - 1) // TE * TE
    ends = jnp.cumsum(padded)
    offs = ends - padded
    rank = csum - 1
    pos1 = offs[e1] + jnp.take_along_axis(rank, e1[:, None], axis=1)[:, 0]
    pos2 = offs[e2] + jnp.take_along_axis(rank, e2[:, None], axis=1)[:, 0]
    n_used = ends[-1] // TE
    tile_row = jnp.minimum(jnp.arange(N_TE, dtype=jnp.int32), n_used - 1) * TE
    tile_e = jnp.minimum(jnp.searchsorted(ends, tile_row, side='right'), N_EXPERTS - 1).astype(jnp.int32)
    pos8 = (jnp.concatenate([pos1, pos2]) * SUBLANES).astype(jnp.int32)
    pad = jnp.concatenate([(offs + counts) * SUBLANES, padded - counts]).astype(jnp.int32)
    return pos8, pad, tile_e, n_used.astype(jnp.int32).reshape(1)


NCH = D // LANES
assert NCH == SUBLANES


def _to_tiles(ref, x):
    n = x.shape[0]
    for c in range(NCH):
        ref[pl.ds(c, n, stride=NCH), :] = x[:, c * LANES:(c + 1) * LANES]


def _from_tiles(ref, n):
    return jnp.concatenate([ref[pl.ds(c, n, stride=NCH), :] for c in range(NCH)], axis=1)


def _tile_copy(src, s8, dst, d8, sem):
    if not isinstance(s8, int):
        s8 = pl.multiple_of(s8, SUBLANES)
    if not isinstance(d8, int):
        d8 = pl.multiple_of(d8, SUBLANES)
    return pltpu.make_async_copy(src.at[pl.ds(s8, SUBLANES), :], dst.at[pl.ds(d8, SUBLANES), :], sem)


def _dispatch_kernel(pos_ref, pad_ref, h_ref, xs_ref, zero_scr, sem):
    i = pl.program_id(0)
    base = i * TM

    for r in range(TM):
        _tile_copy(h_ref, r * NCH, xs_ref, pos_ref[base + r], sem).start(priority=0)
        _tile_copy(h_ref, r * NCH, xs_ref, pos_ref[T_ROWS + base + r], sem).start(priority=1)
    for _ in range(2):
        pltpu.make_async_copy(h_ref, xs_ref.at[pl.ds(0, TM * NCH), :], sem).wait()

    @pl.when(i == N_T - 1)
    def _():
        zero_scr[...] = jnp.zeros(zero_scr.shape, F32)
        for e in range(N_EXPERTS):
            start, n = pad_ref[e], pad_ref[N_EXPERTS + e]

            def fill(r, c):
                _tile_copy(zero_scr, 0, xs_ref, start + r * NCH, sem).start()
                return c

            def drain(r, c):
                _tile_copy(zero_scr, 0, xs_ref, start, sem).wait()
                return c

            lax.fori_loop(0, n, fill, 0)
            lax.fori_loop(0, n, drain, 0)


def _dispatch(pos8, pad, h_tiles):
    return pl.pallas_call(
        _dispatch_kernel,
        grid_spec=pltpu.PrefetchScalarGridSpec(
            num_scalar_prefetch=2,
            grid=(N_T,),
            in_specs=[pl.BlockSpec((TM * NCH, LANES), lambda i, *_: (i, 0))],
            out_specs=pl.BlockSpec(memory_space=pl.ANY),
            scratch_shapes=[pltpu.VMEM((SUBLANES, LANES), F32), pltpu.SemaphoreType.DMA]),
        out_shape=jax.ShapeDtypeStruct((R_ROWS * NCH, LANES), F32),
        compiler_params=_cparams(("arbitrary",)),
        name="moe_dispatch",
    )(pos8, pad, h_tiles)


def _moe_ffn_kernel(te_ref, nu_ref, xs_ref, wg_ref, wu_ref, wd_ref, ys_ref, wg_b, wu_b, wd_b):
    i = pl.program_id(0)

    @pl.when((i == 0) | (te_ref[i] != te_ref[jnp.maximum(i - 1, 0)]))
    def _():
        wg_b[...] = wg_ref[...].astype(BF16)
        wu_b[...] = wu_ref[...].astype(BF16)
        wd_b[...] = wd_ref[...].astype(BF16)

    @pl.when(i < nu_ref[0])
    def _():
        x = _from_tiles(xs_ref, TE).astype(BF16)
        a = _silu(_dot(x, wg_b[...])) * _dot(x, wu_b[...])
        _to_tiles(ys_ref, _dot(a.astype(BF16), wd_b[...]))


def _moe_ffn(tile_e, n_used, xs, wg, wu, wd, m):
    f = wg.shape[3]
    row = lambda i, te, nu: (jnp.minimum(i, nu[0] - 1), 0)
    expert = lambda i, te, nu: (m, te[i], 0, 0)
    return pl.pallas_call(
        _moe_ffn_kernel,
        grid_spec=pltpu.PrefetchScalarGridSpec(
            num_scalar_prefetch=2,
            grid=(N_TE,),
            in_specs=[pl.BlockSpec((TE * NCH, LANES), row),
                      pl.BlockSpec((None, None, D, f), expert),
                      pl.BlockSpec((None, None, D, f), expert),
                      pl.BlockSpec((None, None, f, D), expert)],
            out_specs=pl.BlockSpec((TE * NCH, LANES), row),
            scratch_shapes=[pltpu.VMEM((D, f), BF16), pltpu.VMEM((D, f), BF16), pltpu.VMEM((f, D), BF16)]),
        out_shape=jax.ShapeDtypeStruct((R_ROWS * NCH, LANES), F32),
        compiler_params=_cparams(("arbitrary",)),
        name="moe_ffn",
    )(tile_e, n_used, xs, wg, wu, wd)


def _combine_kernel(*refs, st):
    pos_ref, ys_ref, route_ref = refs[:3]
    ybuf, sem = refs[-2:]
    refs = refs[3:-2]
    i = pl.program_id(0)
    slot = i % 2

    def issue(base, s, r, r8):
        _tile_copy(ys_ref, pos_ref[base + r], ybuf.at[s, 0], r8, sem.at[s]).start(priority=0)
        _tile_copy(ys_ref, pos_ref[T_ROWS + base + r], ybuf.at[s, 1], r8, sem.at[s]).start(priority=1)

    @pl.when(i == 0)
    def _():
        lax.fori_loop(0, TM, lambda r, c: issue(0, 0, r, r * NCH), None)

    def wait(s):
        for c in range(2):
            pltpu.make_async_copy(ys_ref.at[pl.ds(0, TM * NCH), :], ybuf.at[s, c], sem.at[s]).wait()

    wait(slot)
    route = route_ref[...]
    y = (route[:, 2:3] * _from_tiles(ybuf.at[slot, 0], TM)
         + route[:, 3:4] * _from_tiles(ybuf.at[slot, 1], TM))

    nxt = jnp.minimum(i + 1, N_T - 1)
    for r in range(TM):
        issue(nxt * TM, 1 - slot, r, r * NCH)

    _epilogue(y, refs[0], refs[1], refs[2], refs[3:], st, False)

    @pl.when(i == N_T - 1)
    def _():
        wait(1 - slot)


def _combine(pos, ys, route, x, gv, mv, st):
    e_in, out_specs, out_shape = _epilogue_specs(st.has_next, False)
    return pl.pallas_call(
        functools.partial(_combine_kernel, st=st),
        grid_spec=pltpu.PrefetchScalarGridSpec(
            num_scalar_prefetch=1,
            grid=(N_T,),
            in_specs=[pl.BlockSpec(memory_space=pl.ANY),
                      pl.BlockSpec((TM, LANES), lambda i, *_: (i, 0))] + e_in,
            out_specs=out_specs,
            scratch_shapes=[pltpu.VMEM((2, 2, TM * NCH, LANES), F32), pltpu.SemaphoreType.DMA((2,))]),
        out_shape=out_shape,
        compiler_params=_cparams(("arbitrary",)),
        name="moe_combine",
    )(pos, ys, route, x, gv, mv)


def _mla_down_kernel(h_ref, w_ref, gq_ref, gkv_ref, rc_ref, rlo_ref, rhi_ref,
                     cq_ref, ckv_ref, kr_ref, kvb_ref):
    i = pl.program_id(0)
    y = _dot(h_ref[...], w_ref[...])
    cq_ref[...] = _rms(y[:, :Q_RANK], gq_ref[...]).astype(BF16)
    ckv = _rms(y[:, Q_RANK:Q_RANK + KV_RANK], gkv_ref[...])
    kr = y[:, Q_RANK + KV_RANK:]
    kr = jnp.where(i >= NP_T, _rope(kr, rc_ref[...], rlo_ref[...], rhi_ref[...]), kr)
    ckv_ref[...] = ckv
    kr_ref[...] = kr
    kvb_ref[:, :KV_RANK] = ckv.astype(BF16)
    lane = lax.broadcasted_iota(jnp.int32, kr.shape, 1)
    kvb_ref[:, KV_RANK:] = jnp.where(lane == ROPE_DIM, 1.0, kr).astype(BF16)


def _mla_down(h, w, gq, gkv, tabs):
    n = Q_RANK + KVW
    row = lambda i: (i, 0)
    const = lambda i: (0, 0)
    tab = pl.BlockSpec((TM, LANES), lambda i: (_pos_tile(i), 0))
    return pl.pallas_call(
        _mla_down_kernel,
        grid=(N_T,),
        in_specs=[pl.BlockSpec((TM, D), row), pl.BlockSpec((D, n), const),
                  pl.BlockSpec((1, Q_RANK), const), pl.BlockSpec((1, KV_RANK), const),
                  tab, tab, tab],
        out_specs=[pl.BlockSpec((TM, Q_RANK), row), pl.BlockSpec((TM, KV_RANK), row),
                   pl.BlockSpec((TM, LANES), row), pl.BlockSpec((TM, KVW), row)],
        out_shape=[jax.ShapeDtypeStruct((T_ROWS, Q_RANK), BF16),
                   jax.ShapeDtypeStruct((T_ROWS, KV_RANK), F32),
                   jax.ShapeDtypeStruct((T_ROWS, LANES), F32),
                   jax.ShapeDtypeStruct((T_ROWS, KVW), BF16)],
        compiler_params=_cparams(("parallel",)),
        name="mla_down",
    )(h, w, gq, gkv, *tabs)


def _dot_nt(a, b):
    return lax.dot_general(a, b, (((1,), (1,)), ((), ())), preferred_element_type=F32)


def _mla_q_kernel(cq_ref, w_ref, cr_ref, sr_ref, cc_ref, sc_ref, q_ref):
    i = pl.program_id(0)
    y = _dot_nt(w_ref[...], cq_ref[...]) * ATTN_C2
    is_latent = i >= NP_T
    cr = jnp.where(is_latent, cr_ref[...], 1.0)
    sr = jnp.where(is_latent, sr_ref[...], 0.0)
    cc = jnp.where(is_latent, cc_ref[...], 1.0)
    sc = jnp.where(is_latent, sc_ref[...], 0.0)
    for hd in range(N_HEADS):
        r0 = hd * HEAD_PAD
        x0, x1, x2, x3 = (y[r0 + 8 * a:r0 + 8 * (a + 1), :] for a in range(4))
        rot = jnp.concatenate([x0 * cr - x1 * sr, x0 * sr + x1 * cr,
                               x2 * cc - x3 * sc, x2 * sc + x3 * cc], axis=0)
        q_ref[r0:r0 + ROPE_DIM, :] = rot.astype(BF16)
        q_ref[r0 + ROPE_DIM:r0 + HEAD_PAD, :] = y[r0 + ROPE_DIM:r0 + HEAD_PAD, :].astype(BF16)


def _mla_q(cq, w_t, tabs_t):
    tab = pl.BlockSpec((8, TM), lambda i: (0, _pos_tile(i)))
    return pl.pallas_call(
        _mla_q_kernel,
        grid=(N_T,),
        in_specs=[pl.BlockSpec((TM, Q_RANK), lambda i: (i, 0)),
                  pl.BlockSpec((HP, Q_RANK), lambda i: (0, 0)), tab, tab, tab, tab],
        out_specs=pl.BlockSpec((HP, TM), lambda i: (0, i)),
        out_shape=jax.ShapeDtypeStruct((HP, T_ROWS), BF16),
        compiler_params=_cparams(("parallel",)),
        name="mla_q",
    )(cq, w_t, *tabs_t)


def _mla_kv_kernel(c_ref, wk_ref, wvt_ref, k_ref, vt_ref):
    c = c_ref[...]
    k_ref[...] = _dot(c, wk_ref[...]).astype(BF16)
    vt_ref[...] = _dot_nt(wvt_ref[...], c).astype(BF16)


def _mla_kv(ckvkr, w_k, w_vt):
    rows = ckvkr.shape[0]
    hv = HVX
    return pl.pallas_call(
        _mla_kv_kernel,
        grid=(rows // TM,),
        in_specs=[pl.BlockSpec((TM, KVW), lambda i: (i, 0)),
                  pl.BlockSpec((KVW, HP), lambda i: (0, 0)),
                  pl.BlockSpec((hv, KVW), lambda i: (0, 0))],
        out_specs=[pl.BlockSpec((TM, HP), lambda i: (i, 0)), pl.BlockSpec((hv, TM), lambda i: (0, i))],
        out_shape=[jax.ShapeDtypeStruct((rows, HP), BF16), jax.ShapeDtypeStruct((hv, rows), BF16)],
        compiler_params=_cparams(("parallel",)),
        name="mla_kv",
    )(ckvkr, w_k, w_vt)


def _attn_kernel(*refs, nk, aliased):
    if aliased:
        refs = refs[1:]
    qt_ref, k_ref, vt_ref, o_ref, m_scr, acc_scr, ot_scr = refs
    ki = pl.program_id(2)

    @pl.when(ki == 0)
    def _():
        m_scr[...] = jnp.full(m_scr.shape, -jnp.inf, F32)
        acc_scr[...] = jnp.zeros(acc_scr.shape, F32)

    tk, tq = k_ref.shape[0], qt_ref.shape[1]
    vg = V_EXT // SUBLANES

    def across_sublanes(x, op):
        for k in (4, 2, 1):
            x = op(x, pltpu.roll(x, k, 0))
        return x

    def scores(hd):
        qs = slice(hd * HEAD_PAD, (hd + 1) * HEAD_PAD)
        return _dot(k_ref[:, qs], qt_ref[qs, :]).reshape(tk // SUBLANES, SUBLANES, tq)

    ahead = 2
    pending = [scores(hd) for hd in range(ahead)]
    for hd in range(N_HEADS):
        vs = slice(hd * V_EXT, (hd + 1) * V_EXT)
        s = pending.pop(0)
        if hd + ahead < N_HEADS:
            pending.append(scores(hd + ahead))
        m_prev = m_scr[hd]
        m_new = jnp.maximum(m_prev, across_sublanes(jnp.max(s, axis=0), jnp.maximum))
        alpha = jnp.exp2(m_prev - m_new)
        p = jnp.exp2(s - m_new[None])
        pv = _dot(vt_ref[vs, :], p.reshape(tk, tq).astype(BF16))
        acc = acc_scr[vs, :].reshape(vg, SUBLANES, tq)
        acc_scr[vs, :] = (alpha[None] * acc).reshape(V_EXT, tq) + pv
        m_scr[hd] = m_new

    @pl.when(ki == nk - 1)
    def _():
        for hd in range(N_HEADS):
            r0 = hd * V_EXT
            den = across_sublanes(acc_scr[r0 + V_DIM:r0 + V_DIM + SUBLANES, :], jnp.add)
            acc = acc_scr[r0:r0 + V_DIM, :].reshape(V_DIM // SUBLANES, SUBLANES, tq)
            ot_scr[hd * V_DIM:(hd + 1) * V_DIM, :] = (acc / den[None]).reshape(V_DIM, tq)
        o_ref[...] = ot_scr[...].T.astype(BF16)


def _attention(qt, k, vt, n_b, tq, tk, lq, lk, q_row0, prev_out=None):
    nq, nk = lq // tq, lk // tk
    qb0 = q_row0 // tq
    hv = N_HEADS * V_DIM
    aliased = prev_out is not None
    in_specs = [pl.BlockSpec((HP, tq), lambda b, qi, ki: (0, qb0 + b * nq + qi)),
                pl.BlockSpec((tk, HP), lambda b, qi, ki: (b * nk + ki, 0)),
                pl.BlockSpec((HVX, tk), lambda b, qi, ki: (0, b * nk + ki))]
    args = [qt, k, vt]
    if aliased:
        in_specs = [pl.BlockSpec(memory_space=pl.ANY)] + in_specs
        args = [prev_out] + args
    return pl.pallas_call(
        functools.partial(_attn_kernel, nk=nk, aliased=aliased),
        grid=(n_b, nq, nk),
        in_specs=in_specs,
        out_specs=pl.BlockSpec((tq, hv), lambda b, qi, ki: (qb0 + b * nq + qi, 0)),
        out_shape=jax.ShapeDtypeStruct((T_ROWS, hv), BF16),
        scratch_shapes=[pltpu.VMEM((N_HEADS, SUBLANES, tq), F32), pltpu.VMEM((HVX, tq), F32),
                        pltpu.VMEM((hv, tq), F32)],
        input_output_aliases={0: 0} if aliased else {},
        compiler_params=_cparams(("parallel", "parallel", "arbitrary")),
        name="attn_latent" if aliased else "attn_context",
    )(*args)


def _rg_in_kernel(h_ref, w_ref, u_ref, g_ref):
    y = _dot(h_ref[...], w_ref[...])
    u_ref[...] = y[:, :D]
    g_ref[...] = _gelu_tanh(y[:, D:])


def _rg_in(h, w):
    return pl.pallas_call(
        _rg_in_kernel,
        grid=(N_T,),
        in_specs=[pl.BlockSpec((TM, D), lambda i: (i, 0)), pl.BlockSpec((D, 2 * D), lambda i: (0, 0))],
        out_specs=[pl.BlockSpec((TM, D), lambda i: (i, 0))] * 2,
        out_shape=[jax.ShapeDtypeStruct((T_ROWS, D), F32)] * 2,
        compiler_params=_cparams(("parallel",)),
        name="rg_in",
    )(h, w)


RG_TC = 256
RG_GROUP = 256


def _rg_scan_kernel(*refs, nj, seq_len, aliased):
    if aliased:
        refs = refs[2:]
    (uf_ref, ufp_ref, ufn_ref, ub_ref, ubp_ref, ubn_ref, cw_ref, cb_ref, wai_ref, bai_ref,
     lam_ref, h0_ref, yf_ref, yb_ref, fin_ref, carry_ref) = refs
    j = pl.program_id(1)
    tc = RG_TC
    ng = tc // SUBLANES
    sub = lax.broadcasted_iota(jnp.int32, (1, SUBLANES, 1), 1)

    @pl.when(j == 0)
    def _():
        carry_ref[0:2, :] = h0_ref[...]

    def gates(m_ref, p_ref, n_ref, d, chunk):
        m = m_ref[...]
        ext = jnp.concatenate([jnp.where(chunk == 0, 0.0, p_ref[...]), m,
                               jnp.where(chunk == nj - 1, 0.0, n_ref[...])], axis=0)

        def tap(k):
            return pltpu.roll(ext, (-k) % (tc + 2 * SUBLANES), 0)[SUBLANES:SUBLANES + tc]

        u = (cw_ref[0:1, :] * tap(-1) + cw_ref[1:2, :] * m + cw_ref[2:3, :] * tap(1)
             + cw_ref[3:4, :] * tap(2) + cb_ref[...])
        ub = u.astype(BF16)
        ra, ri = [], []
        for q in range(D // RG_GROUP):
            y = _dot(ub[:, q * RG_GROUP:(q + 1) * RG_GROUP], wai_ref[d, q])
            ra.append(y[:, :RG_GROUP])
            ri.append(y[:, RG_GROUP:])
        rr = jax.nn.sigmoid(jnp.concatenate(ra, axis=1) + bai_ref[d, 0:1, :])
        ii = jax.nn.sigmoid(jnp.concatenate(ri, axis=1) + bai_ref[d, 1:2, :])
        nl = -lam_ref[d:d + 1, :]
        softplus = jnp.maximum(nl, 0.0) + jnp.log1p(jnp.exp(-jnp.abs(nl)))
        log_a = (-RG_C * softplus) * rr
        a = jnp.exp(log_a)
        y = jnp.maximum(-jnp.tanh(log_a) * (a * a + 1.0), 0.0)
        bx = jnp.where(y > 0.0, y * lax.rsqrt(y), 0.0) * (ii * u)
        return a.reshape(ng, SUBLANES, D), bx.reshape(ng, SUBLANES, D)

    a, b = gates(uf_ref, ufp_ref, ufn_ref, 0, j)
    for k in (1, 2, 4):
        ok = sub >= k
        b = jnp.where(ok, a * pltpu.roll(b, k, 1) + b, b)
        a = jnp.where(ok, a * pltpu.roll(a, k, 1), a)
    h = carry_ref[0:1, :]
    for g in range(ng):
        hg = a[g] * h + b[g]
        yf_ref[g * SUBLANES:(g + 1) * SUBLANES, :] = hg
        h = hg[SUBLANES - 1:SUBLANES, :]
    carry_ref[0:1, :] = h

    a, b = gates(ub_ref, ubp_ref, ubn_ref, 1, nj - 1 - j)
    for k in (1, 2, 4):
        ok = sub < SUBLANES - k
        b = jnp.where(ok, a * pltpu.roll(b, SUBLANES - k, 1) + b, b)
        a = jnp.where(ok, a * pltpu.roll(a, SUBLANES - k, 1), a)
    h = carry_ref[1:2, :]
    for g in reversed(range(ng)):
        hg = a[g] * h + b[g]
        yb_ref[g * SUBLANES:(g + 1) * SUBLANES, :] = hg
        h = hg[0:1, :]
    carry_ref[1:2, :] = h

    @pl.when(j == nj - 1)
    def _():
        fin_ref[...] = carry_ref[0:2, :]


def _rg_scan(u, cw, cb, wai, bai, lam, h0, n_seq, seq_len, row0, prev=None):
    nj = seq_len // RG_TC
    b0 = row0 // RG_TC
    aliased = prev is not None
    fwd = lambda s, j: b0 + s * nj + j
    bwd = lambda s, j: b0 + s * nj + (nj - 1 - j)
    const2 = lambda s, j: (0, 0)
    in_specs = (_halo_specs(RG_TC, fwd) + _halo_specs(RG_TC, bwd)
                + [pl.BlockSpec((8, D), const2), pl.BlockSpec((1, D), const2),
                   pl.BlockSpec((2, D // RG_GROUP, RG_GROUP, 2 * RG_GROUP), lambda s, j: (0, 0, 0, 0)),
                   pl.BlockSpec((2, 2, D), lambda s, j: (0, 0, 0)),
                   pl.BlockSpec((2, D), const2),
                   pl.BlockSpec((None, 2, D), lambda s, j: (s, 0, 0))])
    args = [u, u, u, u, u, u, cw, cb, wai, bai, lam, h0]
    if aliased:
        in_specs = [pl.BlockSpec(memory_space=pl.ANY)] * 2 + in_specs
        args = list(prev) + args
    return pl.pallas_call(
        functools.partial(_rg_scan_kernel, nj=nj, seq_len=seq_len, aliased=aliased),
        grid=(n_seq, nj),
        in_specs=in_specs,
        out_specs=[pl.BlockSpec((RG_TC, D), lambda s, j: (fwd(s, j), 0)),
                   pl.BlockSpec((RG_TC, D), lambda s, j: (bwd(s, j), 0)),
                   pl.BlockSpec((None, 2, D), lambda s, j: (s, 0, 0))],
        out_shape=[jax.ShapeDtypeStruct((T_ROWS, D), F32), jax.ShapeDtypeStruct((T_ROWS, D), F32),
                   jax.ShapeDtypeStruct((n_seq, 2, D), F32)],
        scratch_shapes=[pltpu.VMEM((8, D), F32)],
        input_output_aliases={0: 0, 1: 1} if aliased else {},
        compiler_params=_cparams(("parallel", "arbitrary")),
        name="rg_scan_latent" if aliased else "rg_scan_context",
    )(*args)


def _sc_in_kernel(h_ref, w_ref, bg_ref, m_ref):
    y = _dot(h_ref[...], w_ref[...])
    bg_ref[...] = y[:, :D]
    m_ref[...] = y[:, D:2 * D] * y[:, 2 * D:]


def _sc_in(h, w):
    return pl.pallas_call(
        _sc_in_kernel,
        grid=(N_T,),
        in_specs=[pl.BlockSpec((TM, D), lambda i: (i, 0)), pl.BlockSpec((D, 3 * D), lambda i: (0, 0))],
        out_specs=[pl.BlockSpec((TM, D), lambda i: (i, 0))] * 2,
        out_shape=[jax.ShapeDtypeStruct((T_ROWS, D), F32)] * 2,
        compiler_params=_cparams(("parallel",)),
        name="sc_in",
    )(h, w)


def _rope_tables():
    t = np.arange(DEC_SEQ)
    inv = ROPE_THETA ** (-jnp.arange(0, AXIS_DIM, 2, dtype=F32) / AXIS_DIM)
    ang_r = jnp.asarray((t // GRID_W).astype(np.float32))[:, None] * inv
    ang_c = jnp.asarray((t % GRID_W).astype(np.float32))[:, None] * inv
    cr, sr, cc, sc = jnp.cos(ang_r), jnp.sin(ang_r), jnp.cos(ang_c), jnp.sin(ang_c)
    z8 = jnp.zeros((DEC_SEQ, 8), F32)
    pad1 = jnp.ones((DEC_SEQ, LANES - ROPE_DIM), F32)
    pad0 = jnp.zeros((DEC_SEQ, LANES - ROPE_DIM), F32)
    c = jnp.concatenate([cr, cr, cc, cc, pad1], axis=1)
    lo = jnp.concatenate([-sr, z8, -sc, z8, pad0], axis=1)
    hi = jnp.concatenate([z8, sr, z8, sc, pad0], axis=1)
    return (c, lo, hi), (cr.T, sr.T, cc.T, sc.T)


def _pad_heads(w, lo):
    r, h, d = w.shape
    return jnp.pad(w, ((0, 0), (0, 0), (lo, HEAD_PAD - lo - d))).reshape(r, h * HEAD_PAD)


def _mla_weights(w_dq, w_uq, w_dkv, w_uk, w_uv, w_o):
    w_down = jnp.concatenate([w_dq, w_dkv, jnp.zeros((D, KVW - KV_RANK - ROPE_DIM), F32)], axis=1)
    w_q = _pad_heads(w_uq[:, :, QK_NOPE:], 0) + _pad_heads(w_uq[:, :, :QK_NOPE], HEAD_PAD - QK_NOPE)
    place = jnp.broadcast_to(jnp.eye(ROPE_DIM, dtype=F32)[:, None, :], (ROPE_DIM, N_HEADS, ROPE_DIM))
    w_k = jnp.concatenate([_pad_heads(w_uk, HEAD_PAD - QK_NOPE), _pad_heads(place, 0),
                           jnp.zeros((KVW - KV_RANK - ROPE_DIM, HP), F32)], axis=0)
    w_vt = jnp.pad(jnp.transpose(w_uv, (1, 2, 0)), ((0, 0), (0, V_EXT - V_DIM), (0, KVW - KV_RANK)))
    w_vt = w_vt.at[:, V_DIM, ONE_COL].set(1.0).reshape(HVX, KVW)
    return (w_down.astype(BF16), w_q.T.astype(BF16), w_k.astype(BF16), w_vt.astype(BF16),
            w_o.astype(BF16))


def _block_diag_groups(w_a, w_i):
    per = RG_GROUP // RG_BLOCK

    def bd(w):
        w = w.reshape(2, D // RG_GROUP, per, RG_BLOCK, RG_BLOCK)
        eye = jnp.eye(per, dtype=F32)
        return jnp.einsum('dgpkj,pq->dgpkqj', w, eye).reshape(2, D // RG_GROUP, RG_GROUP, RG_GROUP)

    return jnp.concatenate([bd(w_a), bd(w_i)], axis=-1).astype(BF16)


def _pad_rows(w, rows=8):
    return jnp.pad(w, ((0, rows - w.shape[0]), (0, 0)))


def kernel(x_prompt, x_sample, cache_mla_ckv, cache_mla_krope, state_rglru, c, c_ctx, mod_w, mod_b, norm_g, mla_w_dq, mla_g_q, mla_w_uq, mla_w_dkv, mla_g_kv, mla_w_uk, mla_w_uv, mla_w_o, rg_w_x, rg_w_y, rg_conv_w, rg_conv_b, rg_w_a, rg_b_a, rg_w_i, rg_b_i, rg_lambda, rg_w_out, sc_w_in, sc_conv_w, sc_w_out, ffn_w_gate, ffn_w_up, ffn_w_down, moe_w_router, moe_b_router, moe_w_gate, moe_w_up, moe_w_down):
    cond8 = jnp.concatenate([c_ctx[None], c, jnp.zeros((N_GROUPS - 1 - DEC_BATCH, D), F32)], axis=0)
    mod = _modulation(cond8, mod_w, mod_b).reshape(DEPTH, N_GROUPS, 6, D)
    mv = jnp.pad(jnp.transpose(mod, (1, 0, 2, 3)).reshape(N_GROUPS, DEPTH * 6, D),
                 ((0, 0), (0, M_ROWS - DEPTH * 6), (0, 0)))
    gv = jnp.pad(norm_g.reshape(DEPTH * 4, D), ((0, G_ROWS - DEPTH * 4), (0, 0)))
    tabs, tabs_t = _rope_tables()

    def stage(l, sub):
        has_next = sub == 0 or l + 1 < DEPTH
        return Stage(has_next, 4 * l + 1 + 2 * sub, 6 * l + 2 + 3 * sub)

    x, h = _pre0(x_prompt.reshape(P_ROWS, D), x_sample.reshape(S_ROWS, D), gv, mv)

    new_ckv, new_krope, new_rg = [], [], []
    for l in range(DEPTH):
        kind, j = l % 3, l // 3
        st = stage(l, 0)
        router = None
        if l % 2 == 1:
            m = l // 2
            w_r = jnp.pad(moe_w_router[m], ((0, 0), (0, LANES - N_EXPERTS)))
            w_r_hi = w_r.astype(BF16)
            router = (w_r_hi, (w_r - w_r_hi.astype(F32)).astype(BF16),
                      jnp.pad(moe_b_router[m], (0, LANES - N_EXPERTS)).reshape(1, LANES))
        if kind == 0:
            w_down, w_qt, w_k, w_vt, w_op = _mla_weights(mla_w_dq[j], mla_w_uq[j], mla_w_dkv[j],
                                                         mla_w_uk[j], mla_w_uv[j], mla_w_o[j])
            cq, ckv, kr, kvb = _mla_down(h, w_down, mla_g_q[j].reshape(1, Q_RANK),
                                         mla_g_kv[j].reshape(1, KV_RANK), tabs)
            q = _mla_q(cq, w_qt, tabs_t)
            new_ckv.append(ckv[:P_ROWS].reshape(BATCH, SEQ, KV_RANK))
            new_krope.append(kr[:P_ROWS, :ROPE_DIM].reshape(BATCH, SEQ, ROPE_DIM))
            cache = jnp.concatenate(
                [cache_mla_ckv[:, j], cache_mla_krope[:, j], jnp.ones((DEC_BATCH, PAST, 1), F32),
                 jnp.zeros((DEC_BATCH, PAST, KVW - ONE_COL - 1), F32)], axis=-1).astype(BF16)
            kv_lat = jnp.concatenate([cache, kvb[P_ROWS:].reshape(DEC_BATCH, DEC_SEQ, KVW)], axis=1)
            k_c, v_c = _mla_kv(kvb[:P_ROWS], w_k, w_vt)
            k_l, v_l = _mla_kv(kv_lat.reshape(DEC_BATCH * (PAST + DEC_SEQ), KVW), w_k, w_vt)
            o = _attention(q, k_c, v_c, BATCH, SEQ, SEQ, SEQ, SEQ, 0)
            o = _attention(q, k_l, v_l, DEC_BATCH, 1024, 512, DEC_SEQ, PAST + DEC_SEQ, P_ROWS, prev_out=o)
            outs = _proj("plain", (o,), w_op, x, gv, mv, st, router=router)
        elif kind == 1:
            w_xy = jnp.concatenate([rg_w_x[j], rg_w_y[j]], axis=1).astype(BF16)
            u, gate = _rg_in(h, w_xy)
            wai = _block_diag_groups(rg_w_a[j], rg_w_i[j])
            bai = jnp.stack([rg_b_a[j], rg_b_i[j]], axis=1)
            scan_args = (_pad_rows(rg_conv_w[j]), rg_conv_b[j].reshape(1, D), wai, bai, rg_lambda[j])
            yf, yb, fin = _rg_scan(u, *scan_args, jnp.zeros((BATCH, 2, D), F32), BATCH, SEQ, 0)
            yf, yb, _ = _rg_scan(u, *scan_args, state_rglru[:, j], DEC_BATCH, DEC_SEQ, P_ROWS,
                                 prev=(yf, yb))
            new_rg.append(fin)
            outs = _proj("rg", (yf, yb, gate), rg_w_out[j].astype(BF16), x, gv, mv, st, router=router)
        else:
            bg, mm = _sc_in(h, sc_w_in[j].astype(BF16))
            outs = _proj("sc", (bg, mm, _pad_rows(sc_conv_w[j])), sc_w_out[j].astype(BF16), x, gv, mv,
                         st, router=router)
        x, h = outs[0], outs[1]

        st = stage(l, 1)
        m = l // 2
        if l % 2 == 0:
            outs = _ffn(h, ffn_w_gate[m].astype(BF16), ffn_w_up[m].astype(BF16),
                        ffn_w_down[m].astype(BF16), x, gv, mv, st)
        else:
            route = outs[2]
            pos, pad, tile_e, n_used = _route_meta(route)
            xs = _dispatch(pos, pad, h)
            ys = _moe_ffn(tile_e, n_used, xs, moe_w_gate, moe_w_up, moe_w_down, m)
            outs = _combine(pos, ys, route, x, gv, mv, st)
        x, h = outs[0], outs[1]

    y_prompt = x.reshape(BATCH, SEQ, D)
    y_sample = h.reshape(DEC_BATCH, DEC_SEQ, D)
    return (y_prompt, y_sample, jnp.stack(new_ckv, axis=1), jnp.stack(new_krope, axis=1),
            jnp.stack(new_rg, axis=1))
```

```python
import functools
from typing import NamedTuple

import numpy as np
import jax
import jax.numpy as jnp
from jax import lax
from jax.experimental import pallas as pl
from jax.experimental.pallas import tpu as pltpu

F32 = jnp.float32
BF16 = jnp.bfloat16

D = 1024
BATCH = 32
SEQ = 256
DEPTH = 4
DEC_BATCH = 4
DEC_SEQ = 4096
PAST = 512
GRID_W = 64
N_HEADS = 16
QK_NOPE = 64
ROPE_DIM = 32
AXIS_DIM = 16
V_DIM = 64
Q_RANK = 384
KV_RANK = 256
ROPE_THETA = 10000.0
ATTN_SCALE = (QK_NOPE + ROPE_DIM) ** -0.5
RG_BLOCKS = 16
RG_BLOCK = 64
RG_C = 8.0
D_FF = 2816
N_EXPERTS = 8
D_FF_EXPERT = 1408
EPS = 1e-6

P_ROWS = BATCH * SEQ
S_ROWS = DEC_BATCH * DEC_SEQ
T_ROWS = P_ROWS + S_ROWS
N_GROUPS = 8
HEAD_PAD = 128
HP = N_HEADS * HEAD_PAD
KVW = 384
ONE_COL = KV_RANK + ROPE_DIM
V_EXT = 80
HVX = N_HEADS * V_EXT
ATTN_C2 = ATTN_SCALE * float(np.log2(np.e))

TM = 512
NP_T = P_ROWS // TM
NS_T = DEC_SEQ // TM
N_T = T_ROWS // TM
SUBLANES = 8
LANES = 128
VMEM_LIMIT = 56 * 1024 * 1024


def _cparams(sem):
    return pltpu.CompilerParams(dimension_semantics=sem, vmem_limit_bytes=VMEM_LIMIT)


def _group_of_tile(i):
    return jnp.maximum(i - NP_T + NS_T, 0) // NS_T


def _pos_tile(i):
    return jnp.maximum(i - NP_T, 0) % NS_T


def _dot(a, b):
    return jnp.dot(a, b, preferred_element_type=F32)


def _rms(x, g):
    ms = jnp.mean(x * x, axis=-1, keepdims=True)
    return x * lax.rsqrt(ms + EPS) * g


def _silu(x):
    return x * jax.nn.sigmoid(x)


def _gelu_tanh(x):
    return x * (0.5 * (1.0 + jnp.tanh(np.sqrt(2.0 / np.pi).astype(np.float32)
                                      * (x + 0.044715 * (x * x * x)))))


def _rope(x, c, s_lo, s_hi):
    return (x * c + pltpu.roll(x, LANES - 8, 1) * s_lo + pltpu.roll(x, 8, 1) * s_hi)


def _mod_kernel(c_ref, w_ref, b_ref, o_ref):
    c = c_ref[...]
    s = _silu(c).astype(BF16)
    o_ref[...] = _dot(s, w_ref[...].astype(BF16)) + b_ref[...]


def _modulation(cond8, mod_w, mod_b):
    tn = 1536
    return pl.pallas_call(
        _mod_kernel,
        grid=(DEPTH, 6 * D // tn),
        in_specs=[pl.BlockSpec((N_GROUPS, D), lambda l, n: (0, 0)),
                  pl.BlockSpec((None, D, tn), lambda l, n: (l, 0, n)),
                  pl.BlockSpec((None, 1, tn), lambda l, n: (l, 0, n))],
        out_specs=pl.BlockSpec((None, N_GROUPS, tn), lambda l, n: (l, 0, n)),
        out_shape=jax.ShapeDtypeStruct((DEPTH, N_GROUPS, 6 * D), F32),
        compiler_params=_cparams(("parallel", "parallel")),
        name="modulation",
    )(cond8, mod_w, mod_b.reshape(DEPTH, 1, 6 * D))


def _ctx_block(i):
    return jnp.minimum(i, NP_T - 1)


def _lat_block(i):
    return jnp.maximum(i - NP_T, 0)


def _pre0_kernel(xp_ref, xs_ref, gv_ref, mv_ref, x_ref, h_ref):
    x = jnp.where(pl.program_id(0) < NP_T, xp_ref[...], xs_ref[...])
    x_ref[...] = x
    h_ref[...] = _pre(x, gv_ref, mv_ref, 0, 0).astype(BF16)


def _pre0(xp, xs, gv, mv):
    return pl.pallas_call(
        _pre0_kernel,
        grid=(N_T,),
        in_specs=[pl.BlockSpec((TM, D), lambda i: (_ctx_block(i), 0)),
                  pl.BlockSpec((TM, D), lambda i: (_lat_block(i), 0)),
                  pl.BlockSpec((G_ROWS, D), lambda i: (0, 0)),
                  pl.BlockSpec((None, M_ROWS, D), lambda i: (_group_of_tile(i), 0, 0))],
        out_specs=[pl.BlockSpec((TM, D), lambda i: (i, 0))] * 2,
        out_shape=[jax.ShapeDtypeStruct((T_ROWS, D), F32), jax.ShapeDtypeStruct((T_ROWS, D), BF16)],
        compiler_params=_cparams(("arbitrary",)),
        name="pre0",
    )(xp, xs, gv, mv)


def _router(hn, wh_ref, wl_ref, br_ref):
    hh = hn.astype(BF16)
    hl = (hn - hh.astype(F32)).astype(BF16)
    logits = (_dot(hh, wh_ref[...]) + (_dot(hl, wh_ref[...]) + _dot(hh, wl_ref[...]))) + br_ref[...]
    lane = lax.broadcasted_iota(jnp.int32, logits.shape, 1)
    neg = jnp.float32(-jnp.inf)
    logits = jnp.where(lane < N_EXPERTS, logits, neg)
    m1 = jnp.max(logits, axis=-1, keepdims=True)
    i1 = jnp.min(jnp.where(logits == m1, lane, LANES), axis=-1, keepdims=True)
    rest = jnp.where(lane == i1, neg, logits)
    m2 = jnp.max(rest, axis=-1, keepdims=True)
    i2 = jnp.min(jnp.where(rest == m2, lane, LANES), axis=-1, keepdims=True)
    e = jnp.exp(m2 - m1)
    p1 = 1.0 / (1.0 + e)
    p2 = e / (1.0 + e)
    return jnp.where(lane == 0, i1.astype(F32),
                     jnp.where(lane == 1, i2.astype(F32),
                               jnp.where(lane == 2, p1, jnp.where(lane == 3, p2, 0.0))))


class Stage(NamedTuple):
    has_next: bool
    go: int
    mo: int


G_ROWS = 24
M_ROWS = 32


def _pre(x, gv_ref, mv_ref, go, mo):
    return _rms(x, gv_ref[go:go + 1, :]) * (1.0 + mv_ref[mo + 1:mo + 2, :]) + mv_ref[mo:mo + 1, :]


def _post_pre(x, out, gv_ref, mv_ref, st):
    xn = x + mv_ref[st.mo:st.mo + 1, :] * _rms(out, gv_ref[st.go:st.go + 1, :])
    if not st.has_next:
        return xn, None
    return xn, _pre(xn, gv_ref, mv_ref, st.go + 1, st.mo + 1)


def _epilogue(out, x_ref, gv_ref, mv_ref, rest, st, want_router):
    rest = list(rest)
    if want_router:
        router_refs = rest[:3]
        rest = rest[3:]
    xn, hn = _post_pre(x_ref[...], out, gv_ref, mv_ref, st)
    if not st.has_next:
        i = pl.program_id(0)

        @pl.when(i < NP_T)
        def _():
            rest[0][...] = xn

        @pl.when(i >= NP_T)
        def _():
            rest[1][...] = xn
        return
    rest[0][...] = xn
    if want_router:
        _to_tiles(rest[1], hn)
        rest[2][...] = _router(hn, *router_refs)
    else:
        rest[1][...] = hn.astype(BF16)


def _epilogue_specs(has_next, want_router):
    in_specs = [pl.BlockSpec((TM, D), lambda i, *_: (i, 0)),
                pl.BlockSpec((G_ROWS, D), lambda i, *_: (0, 0)),
                pl.BlockSpec((None, M_ROWS, D), lambda i, *_: (_group_of_tile(i), 0, 0))]
    if not has_next:
        out_specs = [pl.BlockSpec((TM, D), lambda i, *_: (_ctx_block(i), 0)),
                     pl.BlockSpec((TM, D), lambda i, *_: (_lat_block(i), 0))]
        out_shape = [jax.ShapeDtypeStruct((P_ROWS, D), F32), jax.ShapeDtypeStruct((S_ROWS, D), F32)]
        return in_specs, out_specs, out_shape
    out_specs = [pl.BlockSpec((TM, D), lambda i, *_: (i, 0))]
    out_shape = [jax.ShapeDtypeStruct((T_ROWS, D), F32)]
    if want_router:
        in_specs += [pl.BlockSpec((D, LANES), lambda i, *_: (0, 0)),
                     pl.BlockSpec((D, LANES), lambda i, *_: (0, 0)),
                     pl.BlockSpec((1, LANES), lambda i, *_: (0, 0))]
    if want_router:
        out_specs.append(pl.BlockSpec((TM * SUBLANES, LANES), lambda i, *_: (i, 0)))
        out_shape.append(jax.ShapeDtypeStruct((T_ROWS * SUBLANES, LANES), F32))
    else:
        out_specs.append(pl.BlockSpec((TM, D), lambda i, *_: (i, 0)))
        out_shape.append(jax.ShapeDtypeStruct((T_ROWS, D), BF16))
    if want_router:
        out_specs.append(pl.BlockSpec((TM, LANES), lambda i, *_: (i, 0)))
        out_shape.append(jax.ShapeDtypeStruct((T_ROWS, LANES), F32))
    return in_specs, out_specs, out_shape


def _seq_pos(i, rows):
    seq_len = jnp.where(i < NP_T, SEQ, DEC_SEQ)
    r = lax.broadcasted_iota(jnp.int32, (rows, 1), 0)
    return (i * rows + r) & (seq_len - 1), seq_len, r


def _shifted(m, prev_ref, next_ref, k, pos, seq_len, r):
    rows = m.shape[0]
    if k < 0:
        y = pltpu.roll(m, -k, 0)
        y = jnp.where(r == 0, prev_ref[SUBLANES - 1:SUBLANES, :], y)
        return jnp.where(pos + k < 0, 0.0, y)
    y = pltpu.roll(m, rows - k, 0)
    for q in range(k):
        y = jnp.where(r == rows - k + q, next_ref[q:q + 1, :], y)
    return jnp.where(pos + k >= seq_len, 0.0, y)


def _halo_specs(rows, row_block_of):
    per = rows // SUBLANES
    last = T_ROWS // SUBLANES - 1
    return [pl.BlockSpec((rows, D), lambda *g: (row_block_of(*g), 0)),
            pl.BlockSpec((SUBLANES, D), lambda *g: (jnp.maximum(row_block_of(*g) * per - 1, 0), 0)),
            pl.BlockSpec((SUBLANES, D), lambda *g: (jnp.minimum((row_block_of(*g) + 1) * per, last), 0))]


def _proj_kernel(*refs, mode, st, want_router):
    i = pl.program_id(0)
    if mode == "plain":
        a_ref, w_ref = refs[:2]
        rest = refs[2:]
        a = a_ref[...]
    elif mode == "rg":
        yf_ref, yb_ref, g_ref, w_ref = refs[:4]
        rest = refs[4:]
        a = ((yf_ref[...] + yb_ref[...]) * g_ref[...]).astype(BF16)
    else:
        bg_ref, m_ref, mp_ref, mn_ref, cw_ref, w_ref = refs[:6]
        rest = refs[6:]
        m = m_ref[...]
        pos, seq_len, r = _seq_pos(i, TM)
        z = (cw_ref[0:1, :] * _shifted(m, mp_ref, mn_ref, -1, pos, seq_len, r)
             + cw_ref[1:2, :] * m
             + cw_ref[2:3, :] * _shifted(m, mp_ref, mn_ref, 1, pos, seq_len, r))
        a = (bg_ref[...] * z).astype(BF16)
    out = _dot(a, w_ref[...])
    _epilogue(out, rest[0], rest[1], rest[2], rest[3:], st, want_router)


def _proj(mode, ins, w, x, gv, mv, st, router=None):
    want_router = router is not None
    k = w.shape[0]
    row = lambda i: (i, 0)
    if mode == "plain":
        in_specs = [pl.BlockSpec((TM, k), row)]
    elif mode == "rg":
        in_specs = [pl.BlockSpec((TM, D), row)] * 3
    else:
        bg, m, cw = ins
        ins = (bg, m, m, m, cw)
        in_specs = ([pl.BlockSpec((TM, D), row)] + _halo_specs(TM, lambda i: i)
                    + [pl.BlockSpec((8, D), lambda i: (0, 0))])
    in_specs.append(pl.BlockSpec((k, D), lambda i: (0, 0)))
    e_in, out_specs, out_shape = _epilogue_specs(st.has_next, want_router)
    args = list(ins) + [w, x, gv, mv] + (list(router) if want_router else [])
    return pl.pallas_call(
        functools.partial(_proj_kernel, mode=mode, st=st, want_router=want_router),
        grid=(N_T,),
        in_specs=in_specs + e_in,
        out_specs=out_specs,
        out_shape=out_shape,
        compiler_params=_cparams(("parallel",)),
        name="proj_" + mode,
    )(*args)


MXU_N = 256
FF_SPLIT = (D_FF // MXU_N + 1) // 2 * MXU_N


def _ffn_kernel(h_ref, wg_ref, wu_ref, wd_ref, *refs, st):
    h = h_ref[...]
    out = None
    for lo, hi in ((0, FF_SPLIT), (FF_SPLIT, D_FF)):
        a = _silu(_dot(h, wg_ref[:, lo:hi])) * _dot(h, wu_ref[:, lo:hi])
        part = _dot(a.astype(BF16), wd_ref[lo:hi, :])
        out = part if out is None else out + part
    _epilogue(out, refs[0], refs[1], refs[2], refs[3:], st, False)


def _ffn(h, wg, wu, wd, x, gv, mv, st):
    f = wg.shape[1]
    once = pl.Buffered(1)
    in_specs = [pl.BlockSpec((TM, D), lambda i: (i, 0)),
                pl.BlockSpec((D, f), lambda i: (0, 0), pipeline_mode=once),
                pl.BlockSpec((D, f), lambda i: (0, 0), pipeline_mode=once),
                pl.BlockSpec((f, D), lambda i: (0, 0), pipeline_mode=once)]
    e_in, out_specs, out_shape = _epilogue_specs(st.has_next, False)
    return pl.pallas_call(
        functools.partial(_ffn_kernel, st=st),
        grid=(N_T,),
        in_specs=in_specs + e_in,
        out_specs=out_specs,
        out_shape=out_shape,
        compiler_params=_cparams(("parallel",)),
        name="ffn_dense",
    )(h, wg, wu, wd, x, gv, mv)


TE = 256
R_ROWS = 2 * T_ROWS + N_EXPERTS * TE
N_TE = R_ROWS // TE


def _route_meta(route):
    blk = 512
    nb = T_ROWS // blk
    e1 = route[:, 0].astype(jnp.int32)[None, :]
    e2 = route[:, 1].astype(jnp.int32)[None, :]
    ids = jnp.arange(N_EXPERTS, dtype=jnp.int32)[:, None]
    is1, is2 = e1 == ids, e2 == ids
    hit = (is1 | is2).astype(F32).reshape(N_EXPERTS, nb, blk)
    tri = jnp.triu(jnp.ones((blk, blk), F32))
    within = jnp.einsum('ebt,ts->ebs', hit, tri)
    totals = within[:, :, -1]
    before = jnp.cumsum(totals, axis=1) - totals
    csum = (within + before[:, :, None]).reshape(N_EXPERTS, T_ROWS).astype(jnp.int32)
    counts = csum[:, -1]
    padded = (counts + TE - 1) // TE * TE
    ends = jnp.cumsum(padded)
    offs = ends - padded
    dest = offs[:, None] + csum - 1
    pos1 = jnp.sum(jnp.where(is1, dest, 0), axis=0)
    pos2 = jnp.sum(jnp.where(is2, dest, 0), axis=0)
    n_used = ends[-1] // TE
    tile_row = jnp.minimum(jnp.arange(N_TE, dtype=jnp.int32), n_used - 1) * TE
    tile_e = jnp.minimum(jnp.sum(tile_row[:, None] >= ends[None, :], axis=1), N_EXPERTS - 1).astype(jnp.int32)
    pos8 = (jnp.concatenate([pos1, pos2]) * SUBLANES).astype(jnp.int32)
    pad = jnp.concatenate([(offs + counts) * SUBLANES, padded - counts]).astype(jnp.int32)
    return pos8, pad, tile_e, n_used.astype(jnp.int32).reshape(1)


NCH = D // LANES
assert NCH == SUBLANES


def _to_tiles(ref, x):
    n = x.shape[0]
    for c in range(NCH):
        ref[pl.ds(c, n, stride=NCH), :] = x[:, c * LANES:(c + 1) * LANES]


def _from_tiles(ref, n):
    return jnp.concatenate([ref[pl.ds(c, n, stride=NCH), :] for c in range(NCH)], axis=1)


def _tile_copy(src, s8, dst, d8, sem):
    if not isinstance(s8, int):
        s8 = pl.multiple_of(s8, SUBLANES)
    if not isinstance(d8, int):
        d8 = pl.multiple_of(d8, SUBLANES)
    return pltpu.make_async_copy(src.at[pl.ds(s8, SUBLANES), :], dst.at[pl.ds(d8, SUBLANES), :], sem)


def _dispatch_kernel(pos_ref, pad_ref, h_ref, xs_ref, zero_scr, sem):
    i = pl.program_id(0)
    base = i * TM

    for r in range(TM):
        _tile_copy(h_ref, r * NCH, xs_ref, pos_ref[base + r], sem).start(priority=0)
        _tile_copy(h_ref, r * NCH, xs_ref, pos_ref[T_ROWS + base + r], sem).start(priority=1)
    for _ in range(2):
        pltpu.make_async_copy(h_ref, xs_ref.at[pl.ds(0, TM * NCH), :], sem).wait()

    @pl.when(i == N_T - 1)
    def _():
        zero_scr[...] = jnp.zeros(zero_scr.shape, F32)
        for e in range(N_EXPERTS):
            start, n = pad_ref[e], pad_ref[N_EXPERTS + e]

            def fill(r, c):
                _tile_copy(zero_scr, 0, xs_ref, start + r * NCH, sem).start()
                return c

            def drain(r, c):
                _tile_copy(zero_scr, 0, xs_ref, start, sem).wait()
                return c

            lax.fori_loop(0, n, fill, 0)
            lax.fori_loop(0, n, drain, 0)


def _dispatch(pos8, pad, h_tiles):
    return pl.pallas_call(
        _dispatch_kernel,
        grid_spec=pltpu.PrefetchScalarGridSpec(
            num_scalar_prefetch=2,
            grid=(N_T,),
            in_specs=[pl.BlockSpec((TM * NCH, LANES), lambda i, *_: (i, 0))],
            out_specs=pl.BlockSpec(memory_space=pl.ANY),
            scratch_shapes=[pltpu.VMEM((SUBLANES, LANES), F32), pltpu.SemaphoreType.DMA]),
        out_shape=jax.ShapeDtypeStruct((R_ROWS * NCH, LANES), F32),
        compiler_params=_cparams(("arbitrary",)),
        name="moe_dispatch",
    )(pos8, pad, h_tiles)


def _moe_ffn_kernel(te_ref, nu_ref, xs_ref, wg_ref, wu_ref, wd_ref, ys_ref, wg_b, wu_b, wd_b):
    i = pl.program_id(0)

    @pl.when((i == 0) | (te_ref[i] != te_ref[jnp.maximum(i - 1, 0)]))
    def _():
        wg_b[...] = wg_ref[...].astype(BF16)
        wu_b[...] = wu_ref[...].astype(BF16)
        wd_b[...] = wd_ref[...].astype(BF16)

    @pl.when(i < nu_ref[0])
    def _():
        x = _from_tiles(xs_ref, TE).astype(BF16)
        a = _silu(_dot(x, wg_b[...])) * _dot(x, wu_b[...])
        _to_tiles(ys_ref, _dot(a.astype(BF16), wd_b[...]))


def _moe_ffn(tile_e, n_used, xs, wg, wu, wd, m):
    f = wg.shape[3]
    row = lambda i, te, nu: (jnp.minimum(i, nu[0] - 1), 0)
    expert = lambda i, te, nu: (m, te[i], 0, 0)
    return pl.pallas_call(
        _moe_ffn_kernel,
        grid_spec=pltpu.PrefetchScalarGridSpec(
            num_scalar_prefetch=2,
            grid=(N_TE,),
            in_specs=[pl.BlockSpec((TE * NCH, LANES), row),
                      pl.BlockSpec((None, None, D, f), expert),
                      pl.BlockSpec((None, None, D, f), expert),
                      pl.BlockSpec((None, None, f, D), expert)],
            out_specs=pl.BlockSpec((TE * NCH, LANES), row),
            scratch_shapes=[pltpu.VMEM((D, f), BF16), pltpu.VMEM((D, f), BF16), pltpu.VMEM((f, D), BF16)]),
        out_shape=jax.ShapeDtypeStruct((R_ROWS * NCH, LANES), F32),
        compiler_params=_cparams(("arbitrary",)),
        name="moe_ffn",
    )(tile_e, n_used, xs, wg, wu, wd)


def _combine_kernel(*refs, st):
    pos_ref, ys_ref, route_ref = refs[:3]
    ybuf, sem = refs[-2:]
    refs = refs[3:-2]
    i = pl.program_id(0)
    slot = i % 2

    def issue(base, s, r, r8):
        _tile_copy(ys_ref, pos_ref[base + r], ybuf.at[s, 0], r8, sem.at[s]).start(priority=0)
        _tile_copy(ys_ref, pos_ref[T_ROWS + base + r], ybuf.at[s, 1], r8, sem.at[s]).start(priority=1)

    @pl.when(i == 0)
    def _():
        lax.fori_loop(0, TM, lambda r, c: issue(0, 0, r, r * NCH), None)

    def wait(s):
        for c in range(2):
            pltpu.make_async_copy(ys_ref.at[pl.ds(0, TM * NCH), :], ybuf.at[s, c], sem.at[s]).wait()

    wait(slot)
    route = route_ref[...]
    y = (route[:, 2:3] * _from_tiles(ybuf.at[slot, 0], TM)
         + route[:, 3:4] * _from_tiles(ybuf.at[slot, 1], TM))

    nxt = jnp.minimum(i + 1, N_T - 1)
    for r in range(TM):
        issue(nxt * TM, 1 - slot, r, r * NCH)

    _epilogue(y, refs[0], refs[1], refs[2], refs[3:], st, False)

    @pl.when(i == N_T - 1)
    def _():
        wait(1 - slot)


def _combine(pos, ys, route, x, gv, mv, st):
    e_in, out_specs, out_shape = _epilogue_specs(st.has_next, False)
    return pl.pallas_call(
        functools.partial(_combine_kernel, st=st),
        grid_spec=pltpu.PrefetchScalarGridSpec(
            num_scalar_prefetch=1,
            grid=(N_T,),
            in_specs=[pl.BlockSpec(memory_space=pl.ANY),
                      pl.BlockSpec((TM, LANES), lambda i, *_: (i, 0))] + e_in,
            out_specs=out_specs,
            scratch_shapes=[pltpu.VMEM((2, 2, TM * NCH, LANES), F32), pltpu.SemaphoreType.DMA((2,))]),
        out_shape=out_shape,
        compiler_params=_cparams(("arbitrary",)),
        name="moe_combine",
    )(pos, ys, route, x, gv, mv)


def _mla_down_kernel(h_ref, w_ref, gq_ref, gkv_ref, rc_ref, rlo_ref, rhi_ref,
                     cq_ref, ckv_ref, kr_ref, kvb_ref):
    i = pl.program_id(0)
    y = _dot(h_ref[...], w_ref[...])
    cq_ref[...] = _rms(y[:, :Q_RANK], gq_ref[...]).astype(BF16)
    ckv = _rms(y[:, Q_RANK:Q_RANK + KV_RANK], gkv_ref[...])
    kr = y[:, Q_RANK + KV_RANK:]
    kr = jnp.where(i >= NP_T, _rope(kr, rc_ref[...], rlo_ref[...], rhi_ref[...]), kr)
    ckv_ref[...] = ckv
    kr_ref[...] = kr
    kvb_ref[:, :KV_RANK] = ckv.astype(BF16)
    lane = lax.broadcasted_iota(jnp.int32, kr.shape, 1)
    kvb_ref[:, KV_RANK:] = jnp.where(lane == ROPE_DIM, 1.0, kr).astype(BF16)


def _mla_down(h, w, gq, gkv, tabs):
    n = Q_RANK + KVW
    row = lambda i: (i, 0)
    const = lambda i: (0, 0)
    tab = pl.BlockSpec((TM, LANES), lambda i: (_pos_tile(i), 0))
    return pl.pallas_call(
        _mla_down_kernel,
        grid=(N_T,),
        in_specs=[pl.BlockSpec((TM, D), row), pl.BlockSpec((D, n), const),
                  pl.BlockSpec((1, Q_RANK), const), pl.BlockSpec((1, KV_RANK), const),
                  tab, tab, tab],
        out_specs=[pl.BlockSpec((TM, Q_RANK), row), pl.BlockSpec((TM, KV_RANK), row),
                   pl.BlockSpec((TM, LANES), row), pl.BlockSpec((TM, KVW), row)],
        out_shape=[jax.ShapeDtypeStruct((T_ROWS, Q_RANK), BF16),
                   jax.ShapeDtypeStruct((T_ROWS, KV_RANK), F32),
                   jax.ShapeDtypeStruct((T_ROWS, LANES), F32),
                   jax.ShapeDtypeStruct((T_ROWS, KVW), BF16)],
        compiler_params=_cparams(("parallel",)),
        name="mla_down",
    )(h, w, gq, gkv, *tabs)


def _dot_nt(a, b):
    return lax.dot_general(a, b, (((1,), (1,)), ((), ())), preferred_element_type=F32)


def _mla_q_kernel(cq_ref, w_ref, cr_ref, sr_ref, cc_ref, sc_ref, q_ref):
    i = pl.program_id(0)
    y = _dot_nt(w_ref[...], cq_ref[...]) * ATTN_C2
    is_latent = i >= NP_T
    cr = jnp.where(is_latent, cr_ref[...], 1.0)
    sr = jnp.where(is_latent, sr_ref[...], 0.0)
    cc = jnp.where(is_latent, cc_ref[...], 1.0)
    sc = jnp.where(is_latent, sc_ref[...], 0.0)
    for hd in range(N_HEADS):
        r0 = hd * HEAD_PAD
        x0, x1, x2, x3 = (y[r0 + 8 * a:r0 + 8 * (a + 1), :] for a in range(4))
        rot = jnp.concatenate([x0 * cr - x1 * sr, x0 * sr + x1 * cr,
                               x2 * cc - x3 * sc, x2 * sc + x3 * cc], axis=0)
        q_ref[r0:r0 + ROPE_DIM, :] = rot.astype(BF16)
        q_ref[r0 + ROPE_DIM:r0 + HEAD_PAD, :] = y[r0 + ROPE_DIM:r0 + HEAD_PAD, :].astype(BF16)


def _mla_q(cq, w_t, tabs_t):
    tab = pl.BlockSpec((8, TM), lambda i: (0, _pos_tile(i)))
    return pl.pallas_call(
        _mla_q_kernel,
        grid=(N_T,),
        in_specs=[pl.BlockSpec((TM, Q_RANK), lambda i: (i, 0)),
                  pl.BlockSpec((HP, Q_RANK), lambda i: (0, 0)), tab, tab, tab, tab],
        out_specs=pl.BlockSpec((HP, TM), lambda i: (0, i)),
        out_shape=jax.ShapeDtypeStruct((HP, T_ROWS), BF16),
        compiler_params=_cparams(("parallel",)),
        name="mla_q",
    )(cq, w_t, *tabs_t)


def _mla_kv_kernel(c_ref, wk_ref, wvt_ref, k_ref, vt_ref):
    c = c_ref[...]
    k_ref[...] = _dot(c, wk_ref[...]).astype(BF16)
    vt_ref[...] = _dot_nt(wvt_ref[...], c).astype(BF16)


def _mla_kv(ckvkr, w_k, w_vt):
    rows = ckvkr.shape[0]
    hv = HVX
    return pl.pallas_call(
        _mla_kv_kernel,
        grid=(rows // TM,),
        in_specs=[pl.BlockSpec((TM, KVW), lambda i: (i, 0)),
                  pl.BlockSpec((KVW, HP), lambda i: (0, 0)),
                  pl.BlockSpec((hv, KVW), lambda i: (0, 0))],
        out_specs=[pl.BlockSpec((TM, HP), lambda i: (i, 0)), pl.BlockSpec((hv, TM), lambda i: (0, i))],
        out_shape=[jax.ShapeDtypeStruct((rows, HP), BF16), jax.ShapeDtypeStruct((hv, rows), BF16)],
        compiler_params=_cparams(("parallel",)),
        name="mla_kv",
    )(ckvkr, w_k, w_vt)


def _attn_kernel(*refs, nk, aliased):
    if aliased:
        refs = refs[1:]
    qt_ref, k_ref, vt_ref, o_ref, m_scr, acc_scr, ot_scr = refs
    ki = pl.program_id(2)

    @pl.when(ki == 0)
    def _():
        m_scr[...] = jnp.full(m_scr.shape, -jnp.inf, F32)
        acc_scr[...] = jnp.zeros(acc_scr.shape, F32)

    tk, tq = k_ref.shape[0], qt_ref.shape[1]
    vg = V_EXT // SUBLANES

    def across_sublanes(x, op):
        for k in (4, 2, 1):
            x = op(x, pltpu.roll(x, k, 0))
        return x

    def scores(hd):
        qs = slice(hd * HEAD_PAD, (hd + 1) * HEAD_PAD)
        return _dot(k_ref[:, qs], qt_ref[qs, :]).reshape(tk // SUBLANES, SUBLANES, tq)

    ahead = 2
    pending = [scores(hd) for hd in range(ahead)]
    for hd in range(N_HEADS):
        vs = slice(hd * V_EXT, (hd + 1) * V_EXT)
        s = pending.pop(0)
        if hd + ahead < N_HEADS:
            pending.append(scores(hd + ahead))
        m_prev = m_scr[hd]
        m_new = jnp.maximum(m_prev, across_sublanes(jnp.max(s, axis=0), jnp.maximum))
        alpha = jnp.exp2(m_prev - m_new)
        p = jnp.exp2(s - m_new[None])
        pv = _dot(vt_ref[vs, :], p.reshape(tk, tq).astype(BF16))
        acc = acc_scr[vs, :].reshape(vg, SUBLANES, tq)
        acc_scr[vs, :] = (alpha[None] * acc).reshape(V_EXT, tq) + pv
        m_scr[hd] = m_new

    @pl.when(ki == nk - 1)
    def _():
        for hd in range(N_HEADS):
            r0 = hd * V_EXT
            den = across_sublanes(acc_scr[r0 + V_DIM:r0 + V_DIM + SUBLANES, :], jnp.add)
            acc = acc_scr[r0:r0 + V_DIM, :].reshape(V_DIM // SUBLANES, SUBLANES, tq)
            ot_scr[hd * V_DIM:(hd + 1) * V_DIM, :] = (acc / den[None]).reshape(V_DIM, tq)
        o_ref[...] = ot_scr[...].T.astype(BF16)


def _attention(qt, k, vt, n_b, tq, tk, lq, lk, q_row0, prev_out=None):
    nq, nk = lq // tq, lk // tk
    qb0 = q_row0 // tq
    hv = N_HEADS * V_DIM
    aliased = prev_out is not None
    in_specs = [pl.BlockSpec((HP, tq), lambda b, qi, ki: (0, qb0 + b * nq + qi)),
                pl.BlockSpec((tk, HP), lambda b, qi, ki: (b * nk + ki, 0)),
                pl.BlockSpec((HVX, tk), lambda b, qi, ki: (0, b * nk + ki))]
    args = [qt, k, vt]
    if aliased:
        in_specs = [pl.BlockSpec(memory_space=pl.ANY)] + in_specs
        args = [prev_out] + args
    return pl.pallas_call(
        functools.partial(_attn_kernel, nk=nk, aliased=aliased),
        grid=(n_b, nq, nk),
        in_specs=in_specs,
        out_specs=pl.BlockSpec((tq, hv), lambda b, qi, ki: (qb0 + b * nq + qi, 0)),
        out_shape=jax.ShapeDtypeStruct((T_ROWS, hv), BF16),
        scratch_shapes=[pltpu.VMEM((N_HEADS, SUBLANES, tq), F32), pltpu.VMEM((HVX, tq), F32),
                        pltpu.VMEM((hv, tq), F32)],
        input_output_aliases={0: 0} if aliased else {},
        compiler_params=_cparams(("parallel", "parallel", "arbitrary")),
        name="attn_latent" if aliased else "attn_context",
    )(*args)


def _rg_in_kernel(h_ref, w_ref, u_ref, g_ref):
    y = _dot(h_ref[...], w_ref[...])
    u_ref[...] = y[:, :D]
    g_ref[...] = _gelu_tanh(y[:, D:]).astype(BF16)


def _rg_in(h, w):
    return pl.pallas_call(
        _rg_in_kernel,
        grid=(N_T,),
        in_specs=[pl.BlockSpec((TM, D), lambda i: (i, 0)), pl.BlockSpec((D, 2 * D), lambda i: (0, 0))],
        out_specs=[pl.BlockSpec((TM, D), lambda i: (i, 0))] * 2,
        out_shape=[jax.ShapeDtypeStruct((T_ROWS, D), F32), jax.ShapeDtypeStruct((T_ROWS, D), BF16)],
        compiler_params=_cparams(("parallel",)),
        name="rg_in",
    )(h, w)


RG_TC = 256
RG_GROUP = 256


def _rg_scan_kernel(*refs, nj, seq_len, aliased):
    if aliased:
        refs = refs[2:]
    (uf_ref, ufp_ref, ufn_ref, ub_ref, ubp_ref, ubn_ref, cw_ref, cb_ref, wai_ref, bai_ref,
     lam_ref, h0_ref, yf_ref, yb_ref, fin_ref, carry_ref) = refs
    j = pl.program_id(1)
    tc = RG_TC
    ng = tc // SUBLANES
    sub = lax.broadcasted_iota(jnp.int32, (1, SUBLANES, 1), 1)

    @pl.when(j == 0)
    def _():
        carry_ref[0:2, :] = h0_ref[...]

    def gates(m_ref, p_ref, n_ref, d, chunk):
        m = m_ref[...]
        ext = jnp.concatenate([jnp.where(chunk == 0, 0.0, p_ref[...]), m,
                               jnp.where(chunk == nj - 1, 0.0, n_ref[...])], axis=0)

        def tap(k):
            return pltpu.roll(ext, (-k) % (tc + 2 * SUBLANES), 0)[SUBLANES:SUBLANES + tc]

        u = (cw_ref[0:1, :] * tap(-1) + cw_ref[1:2, :] * m + cw_ref[2:3, :] * tap(1)
             + cw_ref[3:4, :] * tap(2) + cb_ref[...])
        ub = u.astype(BF16)
        ra, ri = [], []
        for q in range(D // RG_GROUP):
            y = _dot(ub[:, q * RG_GROUP:(q + 1) * RG_GROUP], wai_ref[d, q])
            ra.append(y[:, :RG_GROUP])
            ri.append(y[:, RG_GROUP:])
        rr = jax.nn.sigmoid(jnp.concatenate(ra, axis=1) + bai_ref[d, 0:1, :])
        ii = jax.nn.sigmoid(jnp.concatenate(ri, axis=1) + bai_ref[d, 1:2, :])
        nl = -lam_ref[d:d + 1, :]
        softplus = jnp.maximum(nl, 0.0) + jnp.log1p(jnp.exp(-jnp.abs(nl)))
        log_a = (-RG_C * softplus) * rr
        a = jnp.exp(log_a)
        y = jnp.maximum(-jnp.tanh(log_a) * (a * a + 1.0), 0.0)
        bx = jnp.where(y > 0.0, y * lax.rsqrt(y), 0.0) * (ii * u)
        return a.reshape(ng, SUBLANES, D), bx.reshape(ng, SUBLANES, D)

    a, b = gates(uf_ref, ufp_ref, ufn_ref, 0, j)
    for k in (1, 2, 4):
        ok = sub >= k
        b = jnp.where(ok, a * pltpu.roll(b, k, 1) + b, b)
        a = jnp.where(ok, a * pltpu.roll(a, k, 1), a)
    h = carry_ref[0:1, :]
    for g in range(ng):
        hg = a[g] * h + b[g]
        yf_ref[g * SUBLANES:(g + 1) * SUBLANES, :] = hg
        h = hg[SUBLANES - 1:SUBLANES, :]
    carry_ref[0:1, :] = h

    a, b = gates(ub_ref, ubp_ref, ubn_ref, 1, nj - 1 - j)
    for k in (1, 2, 4):
        ok = sub < SUBLANES - k
        b = jnp.where(ok, a * pltpu.roll(b, SUBLANES - k, 1) + b, b)
        a = jnp.where(ok, a * pltpu.roll(a, SUBLANES - k, 1), a)
    h = carry_ref[1:2, :]
    for g in reversed(range(ng)):
        hg = a[g] * h + b[g]
        yb_ref[g * SUBLANES:(g + 1) * SUBLANES, :] = hg
        h = hg[0:1, :]
    carry_ref[1:2, :] = h

    @pl.when(j == nj - 1)
    def _():
        fin_ref[...] = carry_ref[0:2, :]


def _rg_scan(u, cw, cb, wai, bai, lam, h0, n_seq, seq_len, row0, prev=None):
    nj = seq_len // RG_TC
    b0 = row0 // RG_TC
    aliased = prev is not None
    fwd = lambda s, j: b0 + s * nj + j
    bwd = lambda s, j: b0 + s * nj + (nj - 1 - j)
    const2 = lambda s, j: (0, 0)
    in_specs = (_halo_specs(RG_TC, fwd) + _halo_specs(RG_TC, bwd)
                + [pl.BlockSpec((8, D), const2), pl.BlockSpec((1, D), const2),
                   pl.BlockSpec((2, D // RG_GROUP, RG_GROUP, 2 * RG_GROUP), lambda s, j: (0, 0, 0, 0)),
                   pl.BlockSpec((2, 2, D), lambda s, j: (0, 0, 0)),
                   pl.BlockSpec((2, D), const2),
                   pl.BlockSpec((None, 2, D), lambda s, j: (s, 0, 0))])
    args = [u, u, u, u, u, u, cw, cb, wai, bai, lam, h0]
    if aliased:
        in_specs = [pl.BlockSpec(memory_space=pl.ANY)] * 2 + in_specs
        args = list(prev) + args
    return pl.pallas_call(
        functools.partial(_rg_scan_kernel, nj=nj, seq_len=seq_len, aliased=aliased),
        grid=(n_seq, nj),
        in_specs=in_specs,
        out_specs=[pl.BlockSpec((RG_TC, D), lambda s, j: (fwd(s, j), 0)),
                   pl.BlockSpec((RG_TC, D), lambda s, j: (bwd(s, j), 0)),
                   pl.BlockSpec((None, 2, D), lambda s, j: (s, 0, 0))],
        out_shape=[jax.ShapeDtypeStruct((T_ROWS, D), F32), jax.ShapeDtypeStruct((T_ROWS, D), F32),
                   jax.ShapeDtypeStruct((n_seq, 2, D), F32)],
        scratch_shapes=[pltpu.VMEM((8, D), F32)],
        input_output_aliases={0: 0, 1: 1} if aliased else {},
        compiler_params=_cparams(("parallel", "arbitrary")),
        name="rg_scan_latent" if aliased else "rg_scan_context",
    )(*args)


def _sc_in_kernel(h_ref, w_ref, bg_ref, m_ref):
    y = _dot(h_ref[...], w_ref[...])
    bg_ref[...] = y[:, :D].astype(BF16)
    m_ref[...] = y[:, D:2 * D] * y[:, 2 * D:]


def _sc_in(h, w):
    return pl.pallas_call(
        _sc_in_kernel,
        grid=(N_T,),
        in_specs=[pl.BlockSpec((TM, D), lambda i: (i, 0)), pl.BlockSpec((D, 3 * D), lambda i: (0, 0))],
        out_specs=[pl.BlockSpec((TM, D), lambda i: (i, 0))] * 2,
        out_shape=[jax.ShapeDtypeStruct((T_ROWS, D), BF16), jax.ShapeDtypeStruct((T_ROWS, D), F32)],
        compiler_params=_cparams(("parallel",)),
        name="sc_in",
    )(h, w)


def _rope_tables():
    t = np.arange(DEC_SEQ)
    inv = ROPE_THETA ** (-jnp.arange(0, AXIS_DIM, 2, dtype=F32) / AXIS_DIM)
    ang_r = jnp.asarray((t // GRID_W).astype(np.float32))[:, None] * inv
    ang_c = jnp.asarray((t % GRID_W).astype(np.float32))[:, None] * inv
    cr, sr, cc, sc = jnp.cos(ang_r), jnp.sin(ang_r), jnp.cos(ang_c), jnp.sin(ang_c)
    z8 = jnp.zeros((DEC_SEQ, 8), F32)
    pad1 = jnp.ones((DEC_SEQ, LANES - ROPE_DIM), F32)
    pad0 = jnp.zeros((DEC_SEQ, LANES - ROPE_DIM), F32)
    c = jnp.concatenate([cr, cr, cc, cc, pad1], axis=1)
    lo = jnp.concatenate([-sr, z8, -sc, z8, pad0], axis=1)
    hi = jnp.concatenate([z8, sr, z8, sc, pad0], axis=1)
    return (c, lo, hi), (cr.T, sr.T, cc.T, sc.T)


def _pad_heads(w, lo):
    r, h, d = w.shape
    return jnp.pad(w, ((0, 0), (0, 0), (lo, HEAD_PAD - lo - d))).reshape(r, h * HEAD_PAD)


def _mla_weights(w_dq, w_uq, w_dkv, w_uk, w_uv, w_o):
    w_down = jnp.concatenate([w_dq, w_dkv, jnp.zeros((D, KVW - KV_RANK - ROPE_DIM), F32)], axis=1)
    w_q = _pad_heads(w_uq[:, :, QK_NOPE:], 0) + _pad_heads(w_uq[:, :, :QK_NOPE], HEAD_PAD - QK_NOPE)
    place = jnp.broadcast_to(jnp.eye(ROPE_DIM, dtype=F32)[:, None, :], (ROPE_DIM, N_HEADS, ROPE_DIM))
    w_k = jnp.concatenate([_pad_heads(w_uk, HEAD_PAD - QK_NOPE), _pad_heads(place, 0),
                           jnp.zeros((KVW - KV_RANK - ROPE_DIM, HP), F32)], axis=0)
    w_vt = jnp.pad(jnp.transpose(w_uv, (1, 2, 0)), ((0, 0), (0, V_EXT - V_DIM), (0, KVW - KV_RANK)))
    w_vt = w_vt.at[:, V_DIM, ONE_COL].set(1.0).reshape(HVX, KVW)
    return (w_down.astype(BF16), w_q.T.astype(BF16), w_k.astype(BF16), w_vt.astype(BF16),
            w_o.astype(BF16))


def _block_diag_groups(w_a, w_i):
    per = RG_GROUP // RG_BLOCK

    def bd(w):
        w = w.reshape(2, D // RG_GROUP, per, RG_BLOCK, RG_BLOCK)
        eye = jnp.eye(per, dtype=F32)
        return jnp.einsum('dgpkj,pq->dgpkqj', w, eye).reshape(2, D // RG_GROUP, RG_GROUP, RG_GROUP)

    return jnp.concatenate([bd(w_a), bd(w_i)], axis=-1).astype(BF16)


def _pad_rows(w, rows=8):
    return jnp.pad(w, ((0, rows - w.shape[0]), (0, 0)))


def kernel(x_prompt, x_sample, cache_mla_ckv, cache_mla_krope, state_rglru, c, c_ctx, mod_w, mod_b, norm_g, mla_w_dq, mla_g_q, mla_w_uq, mla_w_dkv, mla_g_kv, mla_w_uk, mla_w_uv, mla_w_o, rg_w_x, rg_w_y, rg_conv_w, rg_conv_b, rg_w_a, rg_b_a, rg_w_i, rg_b_i, rg_lambda, rg_w_out, sc_w_in, sc_conv_w, sc_w_out, ffn_w_gate, ffn_w_up, ffn_w_down, moe_w_router, moe_b_router, moe_w_gate, moe_w_up, moe_w_down):
    cond8 = jnp.concatenate([c_ctx[None], c, jnp.zeros((N_GROUPS - 1 - DEC_BATCH, D), F32)], axis=0)
    mod = _modulation(cond8, mod_w, mod_b).reshape(DEPTH, N_GROUPS, 6, D)
    mv = jnp.pad(jnp.transpose(mod, (1, 0, 2, 3)).reshape(N_GROUPS, DEPTH * 6, D),
                 ((0, 0), (0, M_ROWS - DEPTH * 6), (0, 0)))
    gv = jnp.pad(norm_g.reshape(DEPTH * 4, D), ((0, G_ROWS - DEPTH * 4), (0, 0)))
    tabs, tabs_t = _rope_tables()

    def stage(l, sub):
        has_next = sub == 0 or l + 1 < DEPTH
        return Stage(has_next, 4 * l + 1 + 2 * sub, 6 * l + 2 + 3 * sub)

    x, h = _pre0(x_prompt.reshape(P_ROWS, D), x_sample.reshape(S_ROWS, D), gv, mv)

    new_ckv, new_krope, new_rg = [], [], []
    for l in range(DEPTH):
        kind, j = l % 3, l // 3
        st = stage(l, 0)
        router = None
        if l % 2 == 1:
            m = l // 2
            w_r = jnp.pad(moe_w_router[m], ((0, 0), (0, LANES - N_EXPERTS)))
            w_r_hi = w_r.astype(BF16)
            router = (w_r_hi, (w_r - w_r_hi.astype(F32)).astype(BF16),
                      jnp.pad(moe_b_router[m], (0, LANES - N_EXPERTS)).reshape(1, LANES))
        if kind == 0:
            w_down, w_qt, w_k, w_vt, w_op = _mla_weights(mla_w_dq[j], mla_w_uq[j], mla_w_dkv[j],
                                                         mla_w_uk[j], mla_w_uv[j], mla_w_o[j])
            cq, ckv, kr, kvb = _mla_down(h, w_down, mla_g_q[j].reshape(1, Q_RANK),
                                         mla_g_kv[j].reshape(1, KV_RANK), tabs)
            q = _mla_q(cq, w_qt, tabs_t)
            new_ckv.append(ckv[:P_ROWS].reshape(BATCH, SEQ, KV_RANK))
            new_krope.append(kr[:P_ROWS, :ROPE_DIM].reshape(BATCH, SEQ, ROPE_DIM))
            cache = jnp.concatenate(
                [cache_mla_ckv[:, j], cache_mla_krope[:, j], jnp.ones((DEC_BATCH, PAST, 1), F32),
                 jnp.zeros((DEC_BATCH, PAST, KVW - ONE_COL - 1), F32)], axis=-1).astype(BF16)
            kv_lat = jnp.concatenate([cache, kvb[P_ROWS:].reshape(DEC_BATCH, DEC_SEQ, KVW)], axis=1)
            k_c, v_c = _mla_kv(kvb[:P_ROWS], w_k, w_vt)
            k_l, v_l = _mla_kv(kv_lat.reshape(DEC_BATCH * (PAST + DEC_SEQ), KVW), w_k, w_vt)
            o = _attention(q, k_c, v_c, BATCH, SEQ, SEQ, SEQ, SEQ, 0)
            o = _attention(q, k_l, v_l, DEC_BATCH, 1024, 512, DEC_SEQ, PAST + DEC_SEQ, P_ROWS, prev_out=o)
            outs = _proj("plain", (o,), w_op, x, gv, mv, st, router=router)
        elif kind == 1:
            w_xy = jnp.concatenate([rg_w_x[j], rg_w_y[j]], axis=1).astype(BF16)
            u, gate = _rg_in(h, w_xy)
            wai = _block_diag_groups(rg_w_a[j], rg_w_i[j])
            bai = jnp.stack([rg_b_a[j], rg_b_i[j]], axis=1)
            scan_args = (_pad_rows(rg_conv_w[j]), rg_conv_b[j].reshape(1, D), wai, bai, rg_lambda[j])
            yf, yb, fin = _rg_scan(u, *scan_args, jnp.zeros((BATCH, 2, D), F32), BATCH, SEQ, 0)
            yf, yb, _ = _rg_scan(u, *scan_args, state_rglru[:, j], DEC_BATCH, DEC_SEQ, P_ROWS,
                                 prev=(yf, yb))
            new_rg.append(fin)
            outs = _proj("rg", (yf, yb, gate), rg_w_out[j].astype(BF16), x, gv, mv, st, router=router)
        else:
            bg, mm = _sc_in(h, sc_w_in[j].astype(BF16))
            outs = _proj("sc", (bg, mm, _pad_rows(sc_conv_w[j])), sc_w_out[j].astype(BF16), x, gv, mv,
                         st, router=router)
        x, h = outs[0], outs[1]

        st = stage(l, 1)
        m = l // 2
        if l % 2 == 0:
            outs = _ffn(h, ffn_w_gate[m].astype(BF16), ffn_w_up[m].astype(BF16),
                        ffn_w_down[m].astype(BF16), x, gv, mv, st)
        else:
            route = outs[2]
            pos, pad, tile_e, n_used = _route_meta(route)
            xs = _dispatch(pos, pad, h)
            ys = _moe_ffn(tile_e, n_used, xs, moe_w_gate, moe_w_up, moe_w_down, m)
            outs = _combine(pos, ys, route, x, gv, mv, st)
        x, h = outs[0], outs[1]

    y_prompt = x.reshape(BATCH, SEQ, D)
    y_sample = h.reshape(DEC_BATCH, DEC_SEQ, D)
    return (y_prompt, y_sample, jnp.stack(new_ckv, axis=1), jnp.stack(new_krope, axis=1),
            jnp.stack(new_rg, axis=1))
```

```python
import functools
from typing import NamedTuple

import numpy as np
import jax
import jax.numpy as jnp
from jax import lax
from jax.experimental import pallas as pl
from jax.experimental.pallas import tpu as pltpu

F32 = jnp.float32
BF16 = jnp.bfloat16

D = 1024
BATCH = 32
SEQ = 256
DEPTH = 4
DEC_BATCH = 4
DEC_SEQ = 4096
PAST = 512
GRID_W = 64
N_HEADS = 16
QK_NOPE = 64
ROPE_DIM = 32
AXIS_DIM = 16
V_DIM = 64
Q_RANK = 384
KV_RANK = 256
ROPE_THETA = 10000.0
ATTN_SCALE = (QK_NOPE + ROPE_DIM) ** -0.5
RG_BLOCKS = 16
RG_BLOCK = 64
RG_C = 8.0
D_FF = 2816
N_EXPERTS = 8
D_FF_EXPERT = 1408
EPS = 1e-6

P_ROWS = BATCH * SEQ
S_ROWS = DEC_BATCH * DEC_SEQ
T_ROWS = P_ROWS + S_ROWS
N_GROUPS = 8
HEAD_PAD = 128
HP = N_HEADS * HEAD_PAD
KVW = 384
ONE_COL = KV_RANK + ROPE_DIM
V_EXT = 80
HVX = N_HEADS * V_EXT
ATTN_C2 = ATTN_SCALE * float(np.log2(np.e))

TM = 512
NP_T = P_ROWS // TM
NS_T = DEC_SEQ // TM
N_T = T_ROWS // TM
SUBLANES = 8
LANES = 128
VMEM_LIMIT = 56 * 1024 * 1024


def _cparams(sem):
    return pltpu.CompilerParams(dimension_semantics=sem, vmem_limit_bytes=VMEM_LIMIT)


def _group_of_tile(i):
    return jnp.maximum(i - NP_T + NS_T, 0) // NS_T


def _pos_tile(i):
    return jnp.maximum(i - NP_T, 0) % NS_T


def _dot(a, b):
    return jnp.dot(a, b, preferred_element_type=F32)


def _rms(x, g):
    ms = jnp.mean(x * x, axis=-1, keepdims=True)
    return x * lax.rsqrt(ms + EPS) * g


def _silu(x):
    return x * jax.nn.sigmoid(x)


def _gelu_tanh(x):
    return x * (0.5 * (1.0 + jnp.tanh(np.sqrt(2.0 / np.pi).astype(np.float32)
                                      * (x + 0.044715 * (x * x * x)))))


def _rope(x, c, s_lo, s_hi):
    return (x * c + pltpu.roll(x, LANES - 8, 1) * s_lo + pltpu.roll(x, 8, 1) * s_hi)


def _mod_kernel(c_ref, w_ref, b_ref, o_ref):
    c = c_ref[...]
    s = _silu(c).astype(BF16)
    o_ref[...] = _dot(s, w_ref[...].astype(BF16)) + b_ref[...]


def _modulation(cond8, mod_w, mod_b):
    tn = 1536
    return pl.pallas_call(
        _mod_kernel,
        grid=(DEPTH, 6 * D // tn),
        in_specs=[pl.BlockSpec((N_GROUPS, D), lambda l, n: (0, 0)),
                  pl.BlockSpec((None, D, tn), lambda l, n: (l, 0, n)),
                  pl.BlockSpec((None, 1, tn), lambda l, n: (l, 0, n))],
        out_specs=pl.BlockSpec((None, N_GROUPS, tn), lambda l, n: (l, 0, n)),
        out_shape=jax.ShapeDtypeStruct((DEPTH, N_GROUPS, 6 * D), F32),
        compiler_params=_cparams(("parallel", "parallel")),
        name="modulation",
    )(cond8, mod_w, mod_b.reshape(DEPTH, 1, 6 * D))


def _ctx_block(i):
    return jnp.minimum(i, NP_T - 1)


def _lat_block(i):
    return jnp.maximum(i - NP_T, 0)


def _pre0_kernel(xp_ref, xs_ref, gv_ref, mv_ref, x_ref, h_ref):
    x = jnp.where(pl.program_id(0) < NP_T, xp_ref[...], xs_ref[...])
    x_ref[...] = x
    h_ref[...] = _pre(x, gv_ref, mv_ref, 0, 0).astype(BF16)


def _pre0(xp, xs, gv, mv):
    return pl.pallas_call(
        _pre0_kernel,
        grid=(N_T,),
        in_specs=[pl.BlockSpec((TM, D), lambda i: (_ctx_block(i), 0)),
                  pl.BlockSpec((TM, D), lambda i: (_lat_block(i), 0)),
                  pl.BlockSpec((G_ROWS, D), lambda i: (0, 0)),
                  pl.BlockSpec((None, M_ROWS, D), lambda i: (_group_of_tile(i), 0, 0))],
        out_specs=[pl.BlockSpec((TM, D), lambda i: (i, 0))] * 2,
        out_shape=[jax.ShapeDtypeStruct((T_ROWS, D), F32), jax.ShapeDtypeStruct((T_ROWS, D), BF16)],
        compiler_params=_cparams(("arbitrary",)),
        name="pre0",
    )(xp, xs, gv, mv)


def _router(hn, wh_ref, wl_ref, br_ref):
    hh = hn.astype(BF16)
    hl = (hn - hh.astype(F32)).astype(BF16)
    logits = (_dot(hh, wh_ref[...]) + (_dot(hl, wh_ref[...]) + _dot(hh, wl_ref[...]))) + br_ref[...]
    lane = lax.broadcasted_iota(jnp.int32, logits.shape, 1)
    neg = jnp.float32(-jnp.inf)
    logits = jnp.where(lane < N_EXPERTS, logits, neg)
    m1 = jnp.max(logits, axis=-1, keepdims=True)
    i1 = jnp.min(jnp.where(logits == m1, lane, LANES), axis=-1, keepdims=True)
    rest = jnp.where(lane == i1, neg, logits)
    m2 = jnp.max(rest, axis=-1, keepdims=True)
    i2 = jnp.min(jnp.where(rest == m2, lane, LANES), axis=-1, keepdims=True)
    e = jnp.exp(m2 - m1)
    p1 = 1.0 / (1.0 + e)
    p2 = e / (1.0 + e)
    return jnp.where(lane == 0, i1.astype(F32),
                     jnp.where(lane == 1, i2.astype(F32),
                               jnp.where(lane == 2, p1, jnp.where(lane == 3, p2, 0.0))))


class Stage(NamedTuple):
    has_next: bool
    go: int
    mo: int


G_ROWS = 24
M_ROWS = 32


def _pre(x, gv_ref, mv_ref, go, mo):
    return _rms(x, gv_ref[go:go + 1, :]) * (1.0 + mv_ref[mo + 1:mo + 2, :]) + mv_ref[mo:mo + 1, :]


def _post_pre(x, out, gv_ref, mv_ref, st):
    xn = x + mv_ref[st.mo:st.mo + 1, :] * _rms(out, gv_ref[st.go:st.go + 1, :])
    if not st.has_next:
        return xn, None
    return xn, _pre(xn, gv_ref, mv_ref, st.go + 1, st.mo + 1)


def _epilogue(out, x_ref, gv_ref, mv_ref, rest, st, want_router):
    rest = list(rest)
    if want_router:
        router_refs = rest[:3]
        rest = rest[3:]
    xn, hn = _post_pre(x_ref[...], out, gv_ref, mv_ref, st)
    if not st.has_next:
        i = pl.program_id(0)

        @pl.when(i < NP_T)
        def _():
            rest[0][...] = xn

        @pl.when(i >= NP_T)
        def _():
            rest[1][...] = xn
        return
    rest[0][...] = xn
    if want_router:
        _to_tiles(rest[1], hn)
        rest[2][...] = _router(hn, *router_refs)
    else:
        rest[1][...] = hn.astype(BF16)


def _epilogue_specs(has_next, want_router):
    in_specs = [pl.BlockSpec((TM, D), lambda i, *_: (i, 0)),
                pl.BlockSpec((G_ROWS, D), lambda i, *_: (0, 0)),
                pl.BlockSpec((None, M_ROWS, D), lambda i, *_: (_group_of_tile(i), 0, 0))]
    if not has_next:
        out_specs = [pl.BlockSpec((TM, D), lambda i, *_: (_ctx_block(i), 0)),
                     pl.BlockSpec((TM, D), lambda i, *_: (_lat_block(i), 0))]
        out_shape = [jax.ShapeDtypeStruct((P_ROWS, D), F32), jax.ShapeDtypeStruct((S_ROWS, D), F32)]
        return in_specs, out_specs, out_shape
    out_specs = [pl.BlockSpec((TM, D), lambda i, *_: (i, 0))]
    out_shape = [jax.ShapeDtypeStruct((T_ROWS, D), F32)]
    if want_router:
        in_specs += [pl.BlockSpec((D, LANES), lambda i, *_: (0, 0)),
                     pl.BlockSpec((D, LANES), lambda i, *_: (0, 0)),
                     pl.BlockSpec((1, LANES), lambda i, *_: (0, 0))]
    if want_router:
        out_specs.append(pl.BlockSpec((TM * SUBLANES, LANES), lambda i, *_: (i, 0)))
        out_shape.append(jax.ShapeDtypeStruct((T_ROWS * SUBLANES, LANES), F32))
    else:
        out_specs.append(pl.BlockSpec((TM, D), lambda i, *_: (i, 0)))
        out_shape.append(jax.ShapeDtypeStruct((T_ROWS, D), BF16))
    if want_router:
        out_specs.append(pl.BlockSpec((TM, LANES), lambda i, *_: (i, 0)))
        out_shape.append(jax.ShapeDtypeStruct((T_ROWS, LANES), F32))
    return in_specs, out_specs, out_shape


def _seq_pos(i, rows):
    seq_len = jnp.where(i < NP_T, SEQ, DEC_SEQ)
    r = lax.broadcasted_iota(jnp.int32, (rows, 1), 0)
    return (i * rows + r) & (seq_len - 1), seq_len, r


def _shifted(m, prev_ref, next_ref, k, pos, seq_len, r):
    rows = m.shape[0]
    if k < 0:
        y = pltpu.roll(m, -k, 0)
        y = jnp.where(r == 0, prev_ref[SUBLANES - 1:SUBLANES, :], y)
        return jnp.where(pos + k < 0, 0.0, y)
    y = pltpu.roll(m, rows - k, 0)
    for q in range(k):
        y = jnp.where(r == rows - k + q, next_ref[q:q + 1, :], y)
    return jnp.where(pos + k >= seq_len, 0.0, y)


def _halo_specs(rows, row_block_of):
    per = rows // SUBLANES
    last = T_ROWS // SUBLANES - 1
    return [pl.BlockSpec((rows, D), lambda *g: (row_block_of(*g), 0)),
            pl.BlockSpec((SUBLANES, D), lambda *g: (jnp.maximum(row_block_of(*g) * per - 1, 0), 0)),
            pl.BlockSpec((SUBLANES, D), lambda *g: (jnp.minimum((row_block_of(*g) + 1) * per, last), 0))]


def _proj_kernel(*refs, mode, st, want_router):
    i = pl.program_id(0)
    if mode == "plain":
        a_ref, w_ref = refs[:2]
        rest = refs[2:]
        a = a_ref[...]
    elif mode == "rg":
        yf_ref, yb_ref, g_ref, w_ref = refs[:4]
        rest = refs[4:]
        a = ((yf_ref[...] + yb_ref[...]) * g_ref[...]).astype(BF16)
    else:
        bg_ref, m_ref, mp_ref, mn_ref, cw_ref, w_ref = refs[:6]
        rest = refs[6:]
        m = m_ref[...]
        pos, seq_len, r = _seq_pos(i, TM)
        z = (cw_ref[0:1, :] * _shifted(m, mp_ref, mn_ref, -1, pos, seq_len, r)
             + cw_ref[1:2, :] * m
             + cw_ref[2:3, :] * _shifted(m, mp_ref, mn_ref, 1, pos, seq_len, r))
        a = (bg_ref[...] * z).astype(BF16)
    out = _dot(a, w_ref[...])
    _epilogue(out, rest[0], rest[1], rest[2], rest[3:], st, want_router)


def _proj(mode, ins, w, x, gv, mv, st, router=None):
    want_router = router is not None
    k = w.shape[0]
    row = lambda i: (i, 0)
    if mode == "plain":
        in_specs = [pl.BlockSpec((TM, k), row)]
    elif mode == "rg":
        in_specs = [pl.BlockSpec((TM, D), row)] * 3
    else:
        bg, m, cw = ins
        ins = (bg, m, m, m, cw)
        in_specs = ([pl.BlockSpec((TM, D), row)] + _halo_specs(TM, lambda i: i)
                    + [pl.BlockSpec((8, D), lambda i: (0, 0))])
    in_specs.append(pl.BlockSpec((k, D), lambda i: (0, 0)))
    e_in, out_specs, out_shape = _epilogue_specs(st.has_next, want_router)
    args = list(ins) + [w, x, gv, mv] + (list(router) if want_router else [])
    return pl.pallas_call(
        functools.partial(_proj_kernel, mode=mode, st=st, want_router=want_router),
        grid=(N_T,),
        in_specs=in_specs + e_in,
        out_specs=out_specs,
        out_shape=out_shape,
        compiler_params=_cparams(("parallel",)),
        name="proj_" + mode,
    )(*args)


MXU_N = 256
FF_SPLIT = (D_FF // MXU_N + 1) // 2 * MXU_N


def _ffn_kernel(h_ref, wg_ref, wu_ref, wd_ref, *refs, st):
    h = h_ref[...]
    out = None
    for lo, hi in ((0, FF_SPLIT), (FF_SPLIT, D_FF)):
        a = _silu(_dot(h, wg_ref[:, lo:hi])) * _dot(h, wu_ref[:, lo:hi])
        part = _dot(a.astype(BF16), wd_ref[lo:hi, :])
        out = part if out is None else out + part
    _epilogue(out, refs[0], refs[1], refs[2], refs[3:], st, False)


def _ffn(h, wg, wu, wd, x, gv, mv, st):
    f = wg.shape[1]
    once = pl.Buffered(1)
    in_specs = [pl.BlockSpec((TM, D), lambda i: (i, 0)),
                pl.BlockSpec((D, f), lambda i: (0, 0), pipeline_mode=once),
                pl.BlockSpec((D, f), lambda i: (0, 0), pipeline_mode=once),
                pl.BlockSpec((f, D), lambda i: (0, 0), pipeline_mode=once)]
    e_in, out_specs, out_shape = _epilogue_specs(st.has_next, False)
    return pl.pallas_call(
        functools.partial(_ffn_kernel, st=st),
        grid=(N_T,),
        in_specs=in_specs + e_in,
        out_specs=out_specs,
        out_shape=out_shape,
        compiler_params=_cparams(("parallel",)),
        name="ffn_dense",
    )(h, wg, wu, wd, x, gv, mv)


TE = 256
R_ROWS = 2 * T_ROWS + N_EXPERTS * TE
N_TE = R_ROWS // TE


def _route_meta(route):
    blk = 512
    nb = T_ROWS // blk
    e1 = route[:, 0].astype(jnp.int32)[None, :]
    e2 = route[:, 1].astype(jnp.int32)[None, :]
    ids = jnp.arange(N_EXPERTS, dtype=jnp.int32)[:, None]
    is1, is2 = e1 == ids, e2 == ids
    hit = (is1 | is2).astype(F32).reshape(N_EXPERTS, nb, blk)
    tri = jnp.triu(jnp.ones((blk, blk), F32))
    within = jnp.einsum('ebt,ts->ebs', hit, tri)
    totals = within[:, :, -1]
    before = jnp.cumsum(totals, axis=1) - totals
    csum = (within + before[:, :, None]).reshape(N_EXPERTS, T_ROWS).astype(jnp.int32)
    counts = csum[:, -1]
    padded = (counts + TE - 1) // TE * TE
    ends = jnp.cumsum(padded)
    offs = ends - padded
    dest = offs[:, None] + csum - 1
    pos1 = jnp.sum(jnp.where(is1, dest, 0), axis=0)
    pos2 = jnp.sum(jnp.where(is2, dest, 0), axis=0)
    n_used = ends[-1] // TE
    tile_row = jnp.minimum(jnp.arange(N_TE, dtype=jnp.int32), n_used - 1) * TE
    tile_e = jnp.minimum(jnp.sum(tile_row[:, None] >= ends[None, :], axis=1), N_EXPERTS - 1).astype(jnp.int32)
    pos8 = (jnp.concatenate([pos1, pos2]) * SUBLANES).astype(jnp.int32)
    pad = jnp.concatenate([(offs + counts) * SUBLANES, padded - counts]).astype(jnp.int32)
    return pos8, pad, tile_e, n_used.astype(jnp.int32).reshape(1)


NCH = D // LANES
assert NCH == SUBLANES


def _to_tiles(ref, x):
    n = x.shape[0]
    for c in range(NCH):
        ref[pl.ds(c, n, stride=NCH), :] = x[:, c * LANES:(c + 1) * LANES]


def _from_tiles(ref, n):
    return jnp.concatenate([ref[pl.ds(c, n, stride=NCH), :] for c in range(NCH)], axis=1)


def _tile_copy(src, s8, dst, d8, sem):
    if not isinstance(s8, int):
        s8 = pl.multiple_of(s8, SUBLANES)
    if not isinstance(d8, int):
        d8 = pl.multiple_of(d8, SUBLANES)
    return pltpu.make_async_copy(src.at[pl.ds(s8, SUBLANES), :], dst.at[pl.ds(d8, SUBLANES), :], sem)


def _dispatch_kernel(pos_ref, pad_ref, h_ref, xs_ref, zero_scr, sem):
    i = pl.program_id(0)
    base = i * TM

    for r in range(TM):
        _tile_copy(h_ref, r * NCH, xs_ref, pos_ref[base + r], sem).start(priority=0)
        _tile_copy(h_ref, r * NCH, xs_ref, pos_ref[T_ROWS + base + r], sem).start(priority=1)
    for _ in range(2):
        pltpu.make_async_copy(h_ref, xs_ref.at[pl.ds(0, TM * NCH), :], sem).wait()

    @pl.when(i == N_T - 1)
    def _():
        zero_scr[...] = jnp.zeros(zero_scr.shape, F32)
        for e in range(N_EXPERTS):
            start, n = pad_ref[e], pad_ref[N_EXPERTS + e]

            def fill(r, c):
                _tile_copy(zero_scr, 0, xs_ref, start + r * NCH, sem).start()
                return c

            def drain(r, c):
                _tile_copy(zero_scr, 0, xs_ref, start, sem).wait()
                return c

            lax.fori_loop(0, n, fill, 0)
            lax.fori_loop(0, n, drain, 0)


def _dispatch(pos8, pad, h_tiles):
    return pl.pallas_call(
        _dispatch_kernel,
        grid_spec=pltpu.PrefetchScalarGridSpec(
            num_scalar_prefetch=2,
            grid=(N_T,),
            in_specs=[pl.BlockSpec((TM * NCH, LANES), lambda i, *_: (i, 0))],
            out_specs=pl.BlockSpec(memory_space=pl.ANY),
            scratch_shapes=[pltpu.VMEM((SUBLANES, LANES), F32), pltpu.SemaphoreType.DMA]),
        out_shape=jax.ShapeDtypeStruct((R_ROWS * NCH, LANES), F32),
        compiler_params=_cparams(("arbitrary",)),
        name="moe_dispatch",
    )(pos8, pad, h_tiles)


def _moe_ffn_kernel(te_ref, nu_ref, xs_ref, wg_ref, wu_ref, wd_ref, ys_ref, wg_b, wu_b, wd_b):
    i = pl.program_id(0)

    @pl.when((i == 0) | (te_ref[i] != te_ref[jnp.maximum(i - 1, 0)]))
    def _():
        wg_b[...] = wg_ref[...].astype(BF16)
        wu_b[...] = wu_ref[...].astype(BF16)
        wd_b[...] = wd_ref[...].astype(BF16)

    @pl.when(i < nu_ref[0])
    def _():
        x = _from_tiles(xs_ref, TE).astype(BF16)
        a = _silu(_dot(x, wg_b[...])) * _dot(x, wu_b[...])
        _to_tiles(ys_ref, _dot(a.astype(BF16), wd_b[...]))


def _moe_ffn(tile_e, n_used, xs, wg, wu, wd, m):
    f = wg.shape[3]
    row = lambda i, te, nu: (jnp.minimum(i, nu[0] - 1), 0)
    expert = lambda i, te, nu: (m, te[i], 0, 0)
    return pl.pallas_call(
        _moe_ffn_kernel,
        grid_spec=pltpu.PrefetchScalarGridSpec(
            num_scalar_prefetch=2,
            grid=(N_TE,),
            in_specs=[pl.BlockSpec((TE * NCH, LANES), row),
                      pl.BlockSpec((None, None, D, f), expert),
                      pl.BlockSpec((None, None, D, f), expert),
                      pl.BlockSpec((None, None, f, D), expert)],
            out_specs=pl.BlockSpec((TE * NCH, LANES), row),
            scratch_shapes=[pltpu.VMEM((D, f), BF16), pltpu.VMEM((D, f), BF16), pltpu.VMEM((f, D), BF16)]),
        out_shape=jax.ShapeDtypeStruct((R_ROWS * NCH, LANES), F32),
        compiler_params=_cparams(("arbitrary",)),
        name="moe_ffn",
    )(tile_e, n_used, xs, wg, wu, wd)


def _combine_kernel(*refs, st):
    pos_ref, ys_ref, route_ref = refs[:3]
    ybuf, sem = refs[-2:]
    refs = refs[3:-2]
    i = pl.program_id(0)
    slot = i % 2

    def issue(base, s, r, r8):
        _tile_copy(ys_ref, pos_ref[base + r], ybuf.at[s, 0], r8, sem.at[s]).start(priority=0)
        _tile_copy(ys_ref, pos_ref[T_ROWS + base + r], ybuf.at[s, 1], r8, sem.at[s]).start(priority=1)

    @pl.when(i == 0)
    def _():
        lax.fori_loop(0, TM, lambda r, c: issue(0, 0, r, r * NCH), None)

    def wait(s):
        for c in range(2):
            pltpu.make_async_copy(ys_ref.at[pl.ds(0, TM * NCH), :], ybuf.at[s, c], sem.at[s]).wait()

    wait(slot)
    route = route_ref[...]
    y = (route[:, 2:3] * _from_tiles(ybuf.at[slot, 0], TM)
         + route[:, 3:4] * _from_tiles(ybuf.at[slot, 1], TM))

    nxt = jnp.minimum(i + 1, N_T - 1)
    for r in range(TM):
        issue(nxt * TM, 1 - slot, r, r * NCH)

    _epilogue(y, refs[0], refs[1], refs[2], refs[3:], st, False)

    @pl.when(i == N_T - 1)
    def _():
        wait(1 - slot)


def _combine(pos, ys, route, x, gv, mv, st):
    e_in, out_specs, out_shape = _epilogue_specs(st.has_next, False)
    return pl.pallas_call(
        functools.partial(_combine_kernel, st=st),
        grid_spec=pltpu.PrefetchScalarGridSpec(
            num_scalar_prefetch=1,
            grid=(N_T,),
            in_specs=[pl.BlockSpec(memory_space=pl.ANY),
                      pl.BlockSpec((TM, LANES), lambda i, *_: (i, 0))] + e_in,
            out_specs=out_specs,
            scratch_shapes=[pltpu.VMEM((2, 2, TM * NCH, LANES), F32), pltpu.SemaphoreType.DMA((2,))]),
        out_shape=out_shape,
        compiler_params=_cparams(("arbitrary",)),
        name="moe_combine",
    )(pos, ys, route, x, gv, mv)


def _mla_down_kernel(h_ref, w_ref, gq_ref, gkv_ref, rc_ref, rlo_ref, rhi_ref,
                     cq_ref, ckv_ref, kr_ref, kvb_ref):
    i = pl.program_id(0)
    y = _dot(h_ref[...], w_ref[...])
    cq_ref[...] = _rms(y[:, :Q_RANK], gq_ref[...]).astype(BF16)
    ckv = _rms(y[:, Q_RANK:Q_RANK + KV_RANK], gkv_ref[...])
    kr = y[:, Q_RANK + KV_RANK:]
    kr = jnp.where(i >= NP_T, _rope(kr, rc_ref[...], rlo_ref[...], rhi_ref[...]), kr)
    ckv_ref[...] = ckv
    kr_ref[...] = kr
    kvb_ref[:, :KV_RANK] = ckv.astype(BF16)
    lane = lax.broadcasted_iota(jnp.int32, kr.shape, 1)
    kvb_ref[:, KV_RANK:] = jnp.where(lane == ROPE_DIM, 1.0, kr).astype(BF16)


def _mla_down(h, w, gq, gkv, tabs):
    n = Q_RANK + KVW
    row = lambda i: (i, 0)
    const = lambda i: (0, 0)
    tab = pl.BlockSpec((TM, LANES), lambda i: (_pos_tile(i), 0))
    return pl.pallas_call(
        _mla_down_kernel,
        grid=(N_T,),
        in_specs=[pl.BlockSpec((TM, D), row), pl.BlockSpec((D, n), const),
                  pl.BlockSpec((1, Q_RANK), const), pl.BlockSpec((1, KV_RANK), const),
                  tab, tab, tab],
        out_specs=[pl.BlockSpec((TM, Q_RANK), row), pl.BlockSpec((TM, KV_RANK), row),
                   pl.BlockSpec((TM, LANES), row), pl.BlockSpec((TM, KVW), row)],
        out_shape=[jax.ShapeDtypeStruct((T_ROWS, Q_RANK), BF16),
                   jax.ShapeDtypeStruct((T_ROWS, KV_RANK), F32),
                   jax.ShapeDtypeStruct((T_ROWS, LANES), F32),
                   jax.ShapeDtypeStruct((T_ROWS, KVW), BF16)],
        compiler_params=_cparams(("parallel",)),
        name="mla_down",
    )(h, w, gq, gkv, *tabs)


def _dot_nt(a, b):
    return lax.dot_general(a, b, (((1,), (1,)), ((), ())), preferred_element_type=F32)


def _mla_q_kernel(cq_ref, w_ref, cr_ref, sr_ref, cc_ref, sc_ref, q_ref):
    i = pl.program_id(0)
    y = _dot_nt(w_ref[...], cq_ref[...]) * ATTN_C2
    is_latent = i >= NP_T
    cr = jnp.where(is_latent, cr_ref[...], 1.0)
    sr = jnp.where(is_latent, sr_ref[...], 0.0)
    cc = jnp.where(is_latent, cc_ref[...], 1.0)
    sc = jnp.where(is_latent, sc_ref[...], 0.0)
    for hd in range(N_HEADS):
        r0 = hd * HEAD_PAD
        x0, x1, x2, x3 = (y[r0 + 8 * a:r0 + 8 * (a + 1), :] for a in range(4))
        rot = jnp.concatenate([x0 * cr - x1 * sr, x0 * sr + x1 * cr,
                               x2 * cc - x3 * sc, x2 * sc + x3 * cc], axis=0)
        q_ref[r0:r0 + ROPE_DIM, :] = rot.astype(BF16)
        q_ref[r0 + ROPE_DIM:r0 + HEAD_PAD, :] = y[r0 + ROPE_DIM:r0 + HEAD_PAD, :].astype(BF16)


def _mla_q(cq, w_t, tabs_t):
    tab = pl.BlockSpec((8, TM), lambda i: (0, _pos_tile(i)))
    return pl.pallas_call(
        _mla_q_kernel,
        grid=(N_T,),
        in_specs=[pl.BlockSpec((TM, Q_RANK), lambda i: (i, 0)),
                  pl.BlockSpec((HP, Q_RANK), lambda i: (0, 0)), tab, tab, tab, tab],
        out_specs=pl.BlockSpec((HP, TM), lambda i: (0, i)),
        out_shape=jax.ShapeDtypeStruct((HP, T_ROWS), BF16),
        compiler_params=_cparams(("parallel",)),
        name="mla_q",
    )(cq, w_t, *tabs_t)


def _mla_kv_kernel(c_ref, wk_ref, wvt_ref, k_ref, vt_ref):
    c = c_ref[...]
    k_ref[...] = _dot(c, wk_ref[...]).astype(BF16)
    vt_ref[...] = _dot_nt(wvt_ref[...], c).astype(BF16)


def _mla_kv(ckvkr, w_k, w_vt):
    rows = ckvkr.shape[0]
    hv = HVX
    return pl.pallas_call(
        _mla_kv_kernel,
        grid=(rows // TM,),
        in_specs=[pl.BlockSpec((TM, KVW), lambda i: (i, 0)),
                  pl.BlockSpec((KVW, HP), lambda i: (0, 0)),
                  pl.BlockSpec((hv, KVW), lambda i: (0, 0))],
        out_specs=[pl.BlockSpec((TM, HP), lambda i: (i, 0)), pl.BlockSpec((hv, TM), lambda i: (0, i))],
        out_shape=[jax.ShapeDtypeStruct((rows, HP), BF16), jax.ShapeDtypeStruct((hv, rows), BF16)],
        compiler_params=_cparams(("parallel",)),
        name="mla_kv",
    )(ckvkr, w_k, w_vt)


def _attn_kernel(*refs, nk, aliased):
    if aliased:
        refs = refs[1:]
    qt_ref, k_ref, vt_ref, o_ref, m_scr, acc_scr, ot_scr = refs
    ki = pl.program_id(2)

    @pl.when(ki == 0)
    def _():
        m_scr[...] = jnp.full(m_scr.shape, -jnp.inf, F32)
        acc_scr[...] = jnp.zeros(acc_scr.shape, F32)

    tk, tq = k_ref.shape[0], qt_ref.shape[1]
    vg = V_EXT // SUBLANES

    def across_sublanes(x, op):
        for k in (4, 2, 1):
            x = op(x, pltpu.roll(x, k, 0))
        return x

    def scores(hd):
        qs = slice(hd * HEAD_PAD, (hd + 1) * HEAD_PAD)
        return _dot(k_ref[:, qs], qt_ref[qs, :]).reshape(tk // SUBLANES, SUBLANES, tq)

    ahead = 2
    pending = [scores(hd) for hd in range(ahead)]
    for hd in range(N_HEADS):
        vs = slice(hd * V_EXT, (hd + 1) * V_EXT)
        s = pending.pop(0)
        if hd + ahead < N_HEADS:
            pending.append(scores(hd + ahead))
        m_prev = m_scr[hd]
        m_new = jnp.maximum(m_prev, across_sublanes(jnp.max(s, axis=0), jnp.maximum))
        alpha = jnp.exp2(m_prev - m_new)
        p = jnp.exp2(s - m_new[None])
        pv = _dot(vt_ref[vs, :], p.reshape(tk, tq).astype(BF16))
        acc = acc_scr[vs, :].reshape(vg, SUBLANES, tq)
        acc_scr[vs, :] = (alpha[None] * acc).reshape(V_EXT, tq) + pv
        m_scr[hd] = m_new

    @pl.when(ki == nk - 1)
    def _():
        for hd in range(N_HEADS):
            r0 = hd * V_EXT
            den = across_sublanes(acc_scr[r0 + V_DIM:r0 + V_DIM + SUBLANES, :], jnp.add)
            acc = acc_scr[r0:r0 + V_DIM, :].reshape(V_DIM // SUBLANES, SUBLANES, tq)
            ot_scr[hd * V_DIM:(hd + 1) * V_DIM, :] = (acc / den[None]).reshape(V_DIM, tq)
        o_ref[...] = ot_scr[...].T.astype(BF16)


def _attention(qt, k, vt, n_b, tq, tk, lq, lk, q_row0, prev_out=None):
    nq, nk = lq // tq, lk // tk
    qb0 = q_row0 // tq
    hv = N_HEADS * V_DIM
    aliased = prev_out is not None
    in_specs = [pl.BlockSpec((HP, tq), lambda b, qi, ki: (0, qb0 + b * nq + qi)),
                pl.BlockSpec((tk, HP), lambda b, qi, ki: (b * nk + ki, 0)),
                pl.BlockSpec((HVX, tk), lambda b, qi, ki: (0, b * nk + ki))]
    args = [qt, k, vt]
    if aliased:
        in_specs = [pl.BlockSpec(memory_space=pl.ANY)] + in_specs
        args = [prev_out] + args
    return pl.pallas_call(
        functools.partial(_attn_kernel, nk=nk, aliased=aliased),
        grid=(n_b, nq, nk),
        in_specs=in_specs,
        out_specs=pl.BlockSpec((tq, hv), lambda b, qi, ki: (qb0 + b * nq + qi, 0)),
        out_shape=jax.ShapeDtypeStruct((T_ROWS, hv), BF16),
        scratch_shapes=[pltpu.VMEM((N_HEADS, SUBLANES, tq), F32), pltpu.VMEM((HVX, tq), F32),
                        pltpu.VMEM((hv, tq), F32)],
        input_output_aliases={0: 0} if aliased else {},
        compiler_params=_cparams(("parallel", "parallel", "arbitrary")),
        name="attn_latent" if aliased else "attn_context",
    )(*args)


def _rg_in_kernel(h_ref, w_ref, u_ref, g_ref):
    y = _dot(h_ref[...], w_ref[...])
    u_ref[...] = y[:, :D]
    g_ref[...] = _gelu_tanh(y[:, D:]).astype(BF16)


def _rg_in(h, w):
    return pl.pallas_call(
        _rg_in_kernel,
        grid=(N_T,),
        in_specs=[pl.BlockSpec((TM, D), lambda i: (i, 0)), pl.BlockSpec((D, 2 * D), lambda i: (0, 0))],
        out_specs=[pl.BlockSpec((TM, D), lambda i: (i, 0))] * 2,
        out_shape=[jax.ShapeDtypeStruct((T_ROWS, D), F32), jax.ShapeDtypeStruct((T_ROWS, D), BF16)],
        compiler_params=_cparams(("parallel",)),
        name="rg_in",
    )(h, w)


RG_TC = 256
RG_GROUP = 256


def _rg_scan_kernel(*refs, nj, seq_len, aliased):
    if aliased:
        refs = refs[2:]
    (uf_ref, ufp_ref, ufn_ref, ub_ref, ubp_ref, ubn_ref, cw_ref, cb_ref, wai_ref, bai_ref,
     lam_ref, h0_ref, yf_ref, yb_ref, fin_ref, carry_ref) = refs
    j = pl.program_id(1)
    tc = RG_TC
    ng = tc // SUBLANES
    sub = lax.broadcasted_iota(jnp.int32, (1, SUBLANES, 1), 1)

    @pl.when(j == 0)
    def _():
        carry_ref[0:2, :] = h0_ref[...]

    def gates(m_ref, p_ref, n_ref, d, chunk):
        m = m_ref[...]
        ext = jnp.concatenate([jnp.where(chunk == 0, 0.0, p_ref[...]), m,
                               jnp.where(chunk == nj - 1, 0.0, n_ref[...])], axis=0)

        def tap(k):
            return pltpu.roll(ext, (-k) % (tc + 2 * SUBLANES), 0)[SUBLANES:SUBLANES + tc]

        u = (cw_ref[0:1, :] * tap(-1) + cw_ref[1:2, :] * m + cw_ref[2:3, :] * tap(1)
             + cw_ref[3:4, :] * tap(2) + cb_ref[...])
        ub = u.astype(BF16)
        ra, ri = [], []
        for q in range(D // RG_GROUP):
            y = _dot(ub[:, q * RG_GROUP:(q + 1) * RG_GROUP], wai_ref[d, q])
            ra.append(y[:, :RG_GROUP])
            ri.append(y[:, RG_GROUP:])
        rr = jax.nn.sigmoid(jnp.concatenate(ra, axis=1) + bai_ref[d, 0:1, :])
        ii = jax.nn.sigmoid(jnp.concatenate(ri, axis=1) + bai_ref[d, 1:2, :])
        nl = -lam_ref[d:d + 1, :]
        softplus = jnp.maximum(nl, 0.0) + jnp.log1p(jnp.exp(-jnp.abs(nl)))
        log_a = (-RG_C * softplus) * rr
        a = jnp.exp(log_a)
        y = jnp.maximum(-jnp.tanh(log_a) * (a * a + 1.0), 0.0)
        bx = jnp.where(y > 0.0, y * lax.rsqrt(y), 0.0) * (ii * u)
        return a.reshape(ng, SUBLANES, D), bx.reshape(ng, SUBLANES, D)

    a, b = gates(uf_ref, ufp_ref, ufn_ref, 0, j)
    for k in (1, 2, 4):
        ok = sub >= k
        b = jnp.where(ok, a * pltpu.roll(b, k, 1) + b, b)
        a = jnp.where(ok, a * pltpu.roll(a, k, 1), a)
    h = carry_ref[0:1, :]
    for g in range(ng):
        hg = a[g] * h + b[g]
        yf_ref[g * SUBLANES:(g + 1) * SUBLANES, :] = hg
        h = hg[SUBLANES - 1:SUBLANES, :]
    carry_ref[0:1, :] = h

    a, b = gates(ub_ref, ubp_ref, ubn_ref, 1, nj - 1 - j)
    for k in (1, 2, 4):
        ok = sub < SUBLANES - k
        b = jnp.where(ok, a * pltpu.roll(b, SUBLANES - k, 1) + b, b)
        a = jnp.where(ok, a * pltpu.roll(a, SUBLANES - k, 1), a)
    h = carry_ref[1:2, :]
    for g in reversed(range(ng)):
        hg = a[g] * h + b[g]
        yb_ref[g * SUBLANES:(g + 1) * SUBLANES, :] = hg
        h = hg[0:1, :]
    carry_ref[1:2, :] = h

    @pl.when(j == nj - 1)
    def _():
        fin_ref[...] = carry_ref[0:2, :]


def _rg_scan(u, cw, cb, wai, bai, lam, h0, n_seq, seq_len, row0, prev=None):
    nj = seq_len // RG_TC
    b0 = row0 // RG_TC
    aliased = prev is not None
    fwd = lambda s, j: b0 + s * nj + j
    bwd = lambda s, j: b0 + s * nj + (nj - 1 - j)
    const2 = lambda s, j: (0, 0)
    in_specs = (_halo_specs(RG_TC, fwd) + _halo_specs(RG_TC, bwd)
                + [pl.BlockSpec((8, D), const2), pl.BlockSpec((1, D), const2),
                   pl.BlockSpec((2, D // RG_GROUP, RG_GROUP, 2 * RG_GROUP), lambda s, j: (0, 0, 0, 0)),
                   pl.BlockSpec((2, 2, D), lambda s, j: (0, 0, 0)),
                   pl.BlockSpec((2, D), const2),
                   pl.BlockSpec((None, 2, D), lambda s, j: (s, 0, 0))])
    args = [u, u, u, u, u, u, cw, cb, wai, bai, lam, h0]
    if aliased:
        in_specs = [pl.BlockSpec(memory_space=pl.ANY)] * 2 + in_specs
        args = list(prev) + args
    return pl.pallas_call(
        functools.partial(_rg_scan_kernel, nj=nj, seq_len=seq_len, aliased=aliased),
        grid=(n_seq, nj),
        in_specs=in_specs,
        out_specs=[pl.BlockSpec((RG_TC, D), lambda s, j: (fwd(s, j), 0)),
                   pl.BlockSpec((RG_TC, D), lambda s, j: (bwd(s, j), 0)),
                   pl.BlockSpec((None, 2, D), lambda s, j: (s, 0, 0))],
        out_shape=[jax.ShapeDtypeStruct((T_ROWS, D), F32), jax.ShapeDtypeStruct((T_ROWS, D), F32),
                   jax.ShapeDtypeStruct((n_seq, 2, D), F32)],
        scratch_shapes=[pltpu.VMEM((8, D), F32)],
        input_output_aliases={0: 0, 1: 1} if aliased else {},
        compiler_params=_cparams(("parallel", "arbitrary")),
        name="rg_scan_latent" if aliased else "rg_scan_context",
    )(*args)


def _sc_in_kernel(h_ref, w_ref, bg_ref, m_ref):
    y = _dot(h_ref[...], w_ref[...])
    bg_ref[...] = y[:, :D].astype(BF16)
    m_ref[...] = y[:, D:2 * D] * y[:, 2 * D:]


def _sc_in(h, w):
    return pl.pallas_call(
        _sc_in_kernel,
        grid=(N_T,),
        in_specs=[pl.BlockSpec((TM, D), lambda i: (i, 0)), pl.BlockSpec((D, 3 * D), lambda i: (0, 0))],
        out_specs=[pl.BlockSpec((TM, D), lambda i: (i, 0))] * 2,
        out_shape=[jax.ShapeDtypeStruct((T_ROWS, D), BF16), jax.ShapeDtypeStruct((T_ROWS, D), F32)],
        compiler_params=_cparams(("parallel",)),
        name="sc_in",
    )(h, w)


def _rope_tables():
    t = np.arange(DEC_SEQ)
    inv = ROPE_THETA ** (-jnp.arange(0, AXIS_DIM, 2, dtype=F32) / AXIS_DIM)
    ang_r = jnp.asarray((t // GRID_W).astype(np.float32))[:, None] * inv
    ang_c = jnp.asarray((t % GRID_W).astype(np.float32))[:, None] * inv
    cr, sr, cc, sc = jnp.cos(ang_r), jnp.sin(ang_r), jnp.cos(ang_c), jnp.sin(ang_c)
    z8 = jnp.zeros((DEC_SEQ, 8), F32)
    pad1 = jnp.ones((DEC_SEQ, LANES - ROPE_DIM), F32)
    pad0 = jnp.zeros((DEC_SEQ, LANES - ROPE_DIM), F32)
    c = jnp.concatenate([cr, cr, cc, cc, pad1], axis=1)
    lo = jnp.concatenate([-sr, z8, -sc, z8, pad0], axis=1)
    hi = jnp.concatenate([z8, sr, z8, sc, pad0], axis=1)
    return (c, lo, hi), (cr.T, sr.T, cc.T, sc.T)


def _pad_heads(w, lo):
    r, h, d = w.shape
    return jnp.pad(w, ((0, 0), (0, 0), (lo, HEAD_PAD - lo - d))).reshape(r, h * HEAD_PAD)


def _mla_weights(w_dq, w_uq, w_dkv, w_uk, w_uv, w_o):
    w_down = jnp.concatenate([w_dq, w_dkv, jnp.zeros((D, KVW - KV_RANK - ROPE_DIM), F32)], axis=1)
    w_q = _pad_heads(w_uq[:, :, QK_NOPE:], 0) + _pad_heads(w_uq[:, :, :QK_NOPE], HEAD_PAD - QK_NOPE)
    place = jnp.broadcast_to(jnp.eye(ROPE_DIM, dtype=F32)[:, None, :], (ROPE_DIM, N_HEADS, ROPE_DIM))
    w_k = jnp.concatenate([_pad_heads(w_uk, HEAD_PAD - QK_NOPE), _pad_heads(place, 0),
                           jnp.zeros((KVW - KV_RANK - ROPE_DIM, HP), F32)], axis=0)
    w_vt = jnp.pad(jnp.transpose(w_uv, (1, 2, 0)), ((0, 0), (0, V_EXT - V_DIM), (0, KVW - KV_RANK)))
    w_vt = w_vt.at[:, V_DIM, ONE_COL].set(1.0).reshape(HVX, KVW)
    return (w_down.astype(BF16), w_q.T.astype(BF16), w_k.astype(BF16), w_vt.astype(BF16),
            w_o.astype(BF16))


def _block_diag_groups(w_a, w_i):
    per = RG_GROUP // RG_BLOCK

    def bd(w):
        w = w.reshape(2, D // RG_GROUP, per, RG_BLOCK, RG_BLOCK)
        eye = jnp.eye(per, dtype=F32)
        return jnp.einsum('dgpkj,pq->dgpkqj', w, eye).reshape(2, D // RG_GROUP, RG_GROUP, RG_GROUP)

    return jnp.concatenate([bd(w_a), bd(w_i)], axis=-1).astype(BF16)


def _pad_rows(w, rows=8):
    return jnp.pad(w, ((0, rows - w.shape[0]), (0, 0)))


def kernel(x_prompt, x_sample, cache_mla_ckv, cache_mla_krope, state_rglru, c, c_ctx, mod_w, mod_b, norm_g, mla_w_dq, mla_g_q, mla_w_uq, mla_w_dkv, mla_g_kv, mla_w_uk, mla_w_uv, mla_w_o, rg_w_x, rg_w_y, rg_conv_w, rg_conv_b, rg_w_a, rg_b_a, rg_w_i, rg_b_i, rg_lambda, rg_w_out, sc_w_in, sc_conv_w, sc_w_out, ffn_w_gate, ffn_w_up, ffn_w_down, moe_w_router, moe_b_router, moe_w_gate, moe_w_up, moe_w_down):
    cond8 = jnp.concatenate([c_ctx[None], c, jnp.zeros((N_GROUPS - 1 - DEC_BATCH, D), F32)], axis=0)
    mod = _modulation(cond8, mod_w, mod_b).reshape(DEPTH, N_GROUPS, 6, D)
    mv = jnp.pad(jnp.transpose(mod, (1, 0, 2, 3)).reshape(N_GROUPS, DEPTH * 6, D),
                 ((0, 0), (0, M_ROWS - DEPTH * 6), (0, 0)))
    gv = jnp.pad(norm_g.reshape(DEPTH * 4, D), ((0, G_ROWS - DEPTH * 4), (0, 0)))
    tabs, tabs_t = _rope_tables()

    def stage(l, sub):
        has_next = sub == 0 or l + 1 < DEPTH
        return Stage(has_next, 4 * l + 1 + 2 * sub, 6 * l + 2 + 3 * sub)

    x, h = _pre0(x_prompt.reshape(P_ROWS, D), x_sample.reshape(S_ROWS, D), gv, mv)

    new_ckv, new_krope, new_rg = [], [], []
    for l in range(DEPTH):
        kind, j = l % 3, l // 3
        st = stage(l, 0)
        router = None
        if l % 2 == 1:
            m = l // 2
            w_r = jnp.pad(moe_w_router[m], ((0, 0), (0, LANES - N_EXPERTS)))
            w_r_hi = w_r.astype(BF16)
            router = (w_r_hi, (w_r - w_r_hi.astype(F32)).astype(BF16),
                      jnp.pad(moe_b_router[m], (0, LANES - N_EXPERTS)).reshape(1, LANES))
        if kind == 0:
            w_down, w_qt, w_k, w_vt, w_op = _mla_weights(mla_w_dq[j], mla_w_uq[j], mla_w_dkv[j],
                                                         mla_w_uk[j], mla_w_uv[j], mla_w_o[j])
            cq, ckv, kr, kvb = _mla_down(h, w_down, mla_g_q[j].reshape(1, Q_RANK),
                                         mla_g_kv[j].reshape(1, KV_RANK), tabs)
            q = _mla_q(cq, w_qt, tabs_t)
            new_ckv.append(ckv[:P_ROWS].reshape(BATCH, SEQ, KV_RANK))
            new_krope.append(kr[:P_ROWS, :ROPE_DIM].reshape(BATCH, SEQ, ROPE_DIM))
            cache = jnp.concatenate(
                [cache_mla_ckv[:, j], cache_mla_krope[:, j], jnp.ones((DEC_BATCH, PAST, 1), F32),
                 jnp.zeros((DEC_BATCH, PAST, KVW - ONE_COL - 1), F32)], axis=-1).astype(BF16)
            kv_lat = jnp.concatenate([cache, kvb[P_ROWS:].reshape(DEC_BATCH, DEC_SEQ, KVW)], axis=1)
            k_c, v_c = _mla_kv(kvb[:P_ROWS], w_k, w_vt)
            k_l, v_l = _mla_kv(kv_lat.reshape(DEC_BATCH * (PAST + DEC_SEQ), KVW), w_k, w_vt)
            o = _attention(q, k_c, v_c, BATCH, SEQ, SEQ, SEQ, SEQ, 0)
            o = _attention(q, k_l, v_l, DEC_BATCH, 1024, 768, DEC_SEQ, PAST + DEC_SEQ, P_ROWS, prev_out=o)
            outs = _proj("plain", (o,), w_op, x, gv, mv, st, router=router)
        elif kind == 1:
            w_xy = jnp.concatenate([rg_w_x[j], rg_w_y[j]], axis=1).astype(BF16)
            u, gate = _rg_in(h, w_xy)
            wai = _block_diag_groups(rg_w_a[j], rg_w_i[j])
            bai = jnp.stack([rg_b_a[j], rg_b_i[j]], axis=1)
            scan_args = (_pad_rows(rg_conv_w[j]), rg_conv_b[j].reshape(1, D), wai, bai, rg_lambda[j])
            yf, yb, fin = _rg_scan(u, *scan_args, jnp.zeros((BATCH, 2, D), F32), BATCH, SEQ, 0)
            yf, yb, _ = _rg_scan(u, *scan_args, state_rglru[:, j], DEC_BATCH, DEC_SEQ, P_ROWS,
                                 prev=(yf, yb))
            new_rg.append(fin)
            outs = _proj("rg", (yf, yb, gate), rg_w_out[j].astype(BF16), x, gv, mv, st, router=router)
        else:
            bg, mm = _sc_in(h, sc_w_in[j].astype(BF16))
            outs = _proj("sc", (bg, mm, _pad_rows(sc_conv_w[j])), sc_w_out[j].astype(BF16), x, gv, mv,
                         st, router=router)
        x, h = outs[0], outs[1]

        st = stage(l, 1)
        m = l // 2
        if l % 2 == 0:
            outs = _ffn(h, ffn_w_gate[m].astype(BF16), ffn_w_up[m].astype(BF16),
                        ffn_w_down[m].astype(BF16), x, gv, mv, st)
        else:
            route = outs[2]
            pos, pad, tile_e, n_used = _route_meta(route)
            xs = _dispatch(pos, pad, h)
            ys = _moe_ffn(tile_e, n_used, xs, moe_w_gate, moe_w_up, moe_w_down, m)
            outs = _combine(pos, ys, route, x, gv, mv, st)
        x, h = outs[0], outs[1]

    y_prompt = x.reshape(BATCH, SEQ, D)
    y_sample = h.reshape(DEC_BATCH, DEC_SEQ, D)
    return (y_prompt, y_sample, jnp.stack(new_ckv, axis=1), jnp.stack(new_krope, axis=1),
            jnp.stack(new_rg, axis=1))
```

```python
import functools
from typing import NamedTuple

import numpy as np
import jax
import jax.numpy as jnp
from jax import lax
from jax.experimental import pallas as pl
from jax.experimental.pallas import tpu as pltpu

F32 = jnp.float32
BF16 = jnp.bfloat16

D = 1024
BATCH = 32
SEQ = 256
DEPTH = 4
DEC_BATCH = 4
DEC_SEQ = 4096
PAST = 512
GRID_W = 64
N_HEADS = 16
QK_NOPE = 64
ROPE_DIM = 32
AXIS_DIM = 16
V_DIM = 64
Q_RANK = 384
KV_RANK = 256
ROPE_THETA = 10000.0
ATTN_SCALE = (QK_NOPE + ROPE_DIM) ** -0.5
RG_BLOCKS = 16
RG_BLOCK = 64
RG_C = 8.0
D_FF = 2816
N_EXPERTS = 8
D_FF_EXPERT = 1408
EPS = 1e-6

P_ROWS = BATCH * SEQ
S_ROWS = DEC_BATCH * DEC_SEQ
T_ROWS = P_ROWS + S_ROWS
N_GROUPS = 8
HEAD_PAD = 128
HP = N_HEADS * HEAD_PAD
KVW = 384
ONE_COL = KV_RANK + ROPE_DIM
V_EXT = 80
HVX = N_HEADS * V_EXT
ATTN_C2 = ATTN_SCALE * float(np.log2(np.e))

TM = 512
NP_T = P_ROWS // TM
NS_T = DEC_SEQ // TM
N_T = T_ROWS // TM
SUBLANES = 8
LANES = 128
VMEM_LIMIT = 56 * 1024 * 1024


def _cparams(sem):
    return pltpu.CompilerParams(dimension_semantics=sem, vmem_limit_bytes=VMEM_LIMIT)


def _group_of_tile(i):
    return jnp.maximum(i - NP_T + NS_T, 0) // NS_T


def _pos_tile(i):
    return jnp.maximum(i - NP_T, 0) % NS_T


def _dot(a, b):
    return jnp.dot(a, b, preferred_element_type=F32)


def _rms(x, g):
    ms = jnp.mean(x * x, axis=-1, keepdims=True)
    return x * lax.rsqrt(ms + EPS) * g


def _silu(x):
    return x * jax.nn.sigmoid(x)


def _gelu_tanh(x):
    return x * (0.5 * (1.0 + jnp.tanh(np.sqrt(2.0 / np.pi).astype(np.float32)
                                      * (x + 0.044715 * (x * x * x)))))


def _rope(x, c, s_lo, s_hi):
    return (x * c + pltpu.roll(x, LANES - 8, 1) * s_lo + pltpu.roll(x, 8, 1) * s_hi)


def _mod_kernel(c_ref, w_ref, b_ref, o_ref):
    c = c_ref[...]
    s = _silu(c).astype(BF16)
    o_ref[...] = _dot(s, w_ref[...].astype(BF16)) + b_ref[...]


def _modulation(cond8, mod_w, mod_b):
    tn = 1536
    return pl.pallas_call(
        _mod_kernel,
        grid=(DEPTH, 6 * D // tn),
        in_specs=[pl.BlockSpec((N_GROUPS, D), lambda l, n: (0, 0)),
                  pl.BlockSpec((None, D, tn), lambda l, n: (l, 0, n)),
                  pl.BlockSpec((None, 1, tn), lambda l, n: (l, 0, n))],
        out_specs=pl.BlockSpec((None, N_GROUPS, tn), lambda l, n: (l, 0, n)),
        out_shape=jax.ShapeDtypeStruct((DEPTH, N_GROUPS, 6 * D), F32),
        compiler_params=_cparams(("parallel", "parallel")),
        name="modulation",
    )(cond8, mod_w, mod_b.reshape(DEPTH, 1, 6 * D))


def _ctx_block(i):
    return jnp.minimum(i, NP_T - 1)


def _lat_block(i):
    return jnp.maximum(i - NP_T, 0)


def _pre0_kernel(xp_ref, xs_ref, gv_ref, mv_ref, x_ref, h_ref):
    x = jnp.where(pl.program_id(0) < NP_T, xp_ref[...], xs_ref[...])
    x_ref[...] = x
    h_ref[...] = _pre(x, gv_ref, mv_ref, 0, 0).astype(BF16)


def _pre0(xp, xs, gv, mv):
    return pl.pallas_call(
        _pre0_kernel,
        grid=(N_T,),
        in_specs=[pl.BlockSpec((TM, D), lambda i: (_ctx_block(i), 0)),
                  pl.BlockSpec((TM, D), lambda i: (_lat_block(i), 0)),
                  pl.BlockSpec((G_ROWS, D), lambda i: (0, 0)),
                  pl.BlockSpec((None, M_ROWS, D), lambda i: (_group_of_tile(i), 0, 0))],
        out_specs=[pl.BlockSpec((TM, D), lambda i: (i, 0))] * 2,
        out_shape=[jax.ShapeDtypeStruct((T_ROWS, D), F32), jax.ShapeDtypeStruct((T_ROWS, D), BF16)],
        compiler_params=_cparams(("arbitrary",)),
        name="pre0",
    )(xp, xs, gv, mv)


def _router(hn, wh_ref, wl_ref, br_ref):
    hh = hn.astype(BF16)
    hl = (hn - hh.astype(F32)).astype(BF16)
    logits = (_dot(hh, wh_ref[...]) + (_dot(hl, wh_ref[...]) + _dot(hh, wl_ref[...]))) + br_ref[...]
    lane = lax.broadcasted_iota(jnp.int32, logits.shape, 1)
    neg = jnp.float32(-jnp.inf)
    logits = jnp.where(lane < N_EXPERTS, logits, neg)
    m1 = jnp.max(logits, axis=-1, keepdims=True)
    i1 = jnp.min(jnp.where(logits == m1, lane, LANES), axis=-1, keepdims=True)
    rest = jnp.where(lane == i1, neg, logits)
    m2 = jnp.max(rest, axis=-1, keepdims=True)
    i2 = jnp.min(jnp.where(rest == m2, lane, LANES), axis=-1, keepdims=True)
    e = jnp.exp(m2 - m1)
    p1 = 1.0 / (1.0 + e)
    p2 = e / (1.0 + e)
    return jnp.where(lane == 0, i1.astype(F32),
                     jnp.where(lane == 1, i2.astype(F32),
                               jnp.where(lane == 2, p1, jnp.where(lane == 3, p2, 0.0))))


class Stage(NamedTuple):
    has_next: bool
    go: int
    mo: int


G_ROWS = 24
M_ROWS = 32


def _pre(x, gv_ref, mv_ref, go, mo):
    return _rms(x, gv_ref[go:go + 1, :]) * (1.0 + mv_ref[mo + 1:mo + 2, :]) + mv_ref[mo:mo + 1, :]


def _post_pre(x, out, gv_ref, mv_ref, st):
    xn = x + mv_ref[st.mo:st.mo + 1, :] * _rms(out, gv_ref[st.go:st.go + 1, :])
    if not st.has_next:
        return xn, None
    return xn, _pre(xn, gv_ref, mv_ref, st.go + 1, st.mo + 1)


def _epilogue(out, x_ref, gv_ref, mv_ref, rest, st, want_router):
    rest = list(rest)
    if want_router:
        router_refs = rest[:3]
        rest = rest[3:]
    xn, hn = _post_pre(x_ref[...], out, gv_ref, mv_ref, st)
    if not st.has_next:
        i = pl.program_id(0)

        @pl.when(i < NP_T)
        def _():
            rest[0][...] = xn

        @pl.when(i >= NP_T)
        def _():
            rest[1][...] = xn
        return
    rest[0][...] = xn
    if want_router:
        _to_tiles(rest[1], hn)
        rest[2][...] = _router(hn, *router_refs)
    else:
        rest[1][...] = hn.astype(BF16)


def _epilogue_specs(has_next, want_router):
    in_specs = [pl.BlockSpec((TM, D), lambda i, *_: (i, 0)),
                pl.BlockSpec((G_ROWS, D), lambda i, *_: (0, 0)),
                pl.BlockSpec((None, M_ROWS, D), lambda i, *_: (_group_of_tile(i), 0, 0))]
    if not has_next:
        out_specs = [pl.BlockSpec((TM, D), lambda i, *_: (_ctx_block(i), 0)),
                     pl.BlockSpec((TM, D), lambda i, *_: (_lat_block(i), 0))]
        out_shape = [jax.ShapeDtypeStruct((P_ROWS, D), F32), jax.ShapeDtypeStruct((S_ROWS, D), F32)]
        return in_specs, out_specs, out_shape
    out_specs = [pl.BlockSpec((TM, D), lambda i, *_: (i, 0))]
    out_shape = [jax.ShapeDtypeStruct((T_ROWS, D), F32)]
    if want_router:
        in_specs += [pl.BlockSpec((D, LANES), lambda i, *_: (0, 0)),
                     pl.BlockSpec((D, LANES), lambda i, *_: (0, 0)),
                     pl.BlockSpec((1, LANES), lambda i, *_: (0, 0))]
    if want_router:
        out_specs.append(pl.BlockSpec((TM * SUBLANES, LANES), lambda i, *_: (i, 0)))
        out_shape.append(jax.ShapeDtypeStruct((T_ROWS * SUBLANES, LANES), F32))
    else:
        out_specs.append(pl.BlockSpec((TM, D), lambda i, *_: (i, 0)))
        out_shape.append(jax.ShapeDtypeStruct((T_ROWS, D), BF16))
    if want_router:
        out_specs.append(pl.BlockSpec((TM, LANES), lambda i, *_: (i, 0)))
        out_shape.append(jax.ShapeDtypeStruct((T_ROWS, LANES), F32))
    return in_specs, out_specs, out_shape


def _seq_pos(i, rows):
    seq_len = jnp.where(i < NP_T, SEQ, DEC_SEQ)
    r = lax.broadcasted_iota(jnp.int32, (rows, 1), 0)
    return (i * rows + r) & (seq_len - 1), seq_len, r


def _shifted(m, prev_ref, next_ref, k, pos, seq_len, r):
    rows = m.shape[0]
    if k < 0:
        y = pltpu.roll(m, -k, 0)
        y = jnp.where(r == 0, prev_ref[SUBLANES - 1:SUBLANES, :], y)
        return jnp.where(pos + k < 0, 0.0, y)
    y = pltpu.roll(m, rows - k, 0)
    for q in range(k):
        y = jnp.where(r == rows - k + q, next_ref[q:q + 1, :], y)
    return jnp.where(pos + k >= seq_len, 0.0, y)


def _halo_specs(rows, row_block_of):
    per = rows // SUBLANES
    last = T_ROWS // SUBLANES - 1
    return [pl.BlockSpec((rows, D), lambda *g: (row_block_of(*g), 0)),
            pl.BlockSpec((SUBLANES, D), lambda *g: (jnp.maximum(row_block_of(*g) * per - 1, 0), 0)),
            pl.BlockSpec((SUBLANES, D), lambda *g: (jnp.minimum((row_block_of(*g) + 1) * per, last), 0))]


def _proj_kernel(*refs, mode, st, want_router):
    i = pl.program_id(0)
    if mode == "plain":
        a_ref, w_ref = refs[:2]
        rest = refs[2:]
        a = a_ref[...]
    elif mode == "rg":
        yf_ref, yb_ref, g_ref, w_ref = refs[:4]
        rest = refs[4:]
        a = ((yf_ref[...] + yb_ref[...]) * g_ref[...]).astype(BF16)
    else:
        bg_ref, m_ref, mp_ref, mn_ref, cw_ref, w_ref = refs[:6]
        rest = refs[6:]
        m = m_ref[...]
        pos, seq_len, r = _seq_pos(i, TM)
        z = (cw_ref[0:1, :] * _shifted(m, mp_ref, mn_ref, -1, pos, seq_len, r)
             + cw_ref[1:2, :] * m
             + cw_ref[2:3, :] * _shifted(m, mp_ref, mn_ref, 1, pos, seq_len, r))
        a = (bg_ref[...] * z).astype(BF16)
    out = _dot(a, w_ref[...])
    _epilogue(out, rest[0], rest[1], rest[2], rest[3:], st, want_router)


def _proj(mode, ins, w, x, gv, mv, st, router=None):
    want_router = router is not None
    k = w.shape[0]
    row = lambda i: (i, 0)
    if mode == "plain":
        in_specs = [pl.BlockSpec((TM, k), row)]
    elif mode == "rg":
        in_specs = [pl.BlockSpec((TM, D), row)] * 3
    else:
        bg, m, cw = ins
        ins = (bg, m, m, m, cw)
        in_specs = ([pl.BlockSpec((TM, D), row)] + _halo_specs(TM, lambda i: i)
                    + [pl.BlockSpec((8, D), lambda i: (0, 0))])
    in_specs.append(pl.BlockSpec((k, D), lambda i: (0, 0)))
    e_in, out_specs, out_shape = _epilogue_specs(st.has_next, want_router)
    args = list(ins) + [w, x, gv, mv] + (list(router) if want_router else [])
    return pl.pallas_call(
        functools.partial(_proj_kernel, mode=mode, st=st, want_router=want_router),
        grid=(N_T,),
        in_specs=in_specs + e_in,
        out_specs=out_specs,
        out_shape=out_shape,
        compiler_params=_cparams(("parallel",)),
        name="proj_" + mode,
    )(*args)


MXU_N = 256
FF_SPLIT = (D_FF // MXU_N + 1) // 2 * MXU_N


def _ffn_kernel(h_ref, wg_ref, wu_ref, wd_ref, *refs, st):
    h = h_ref[...]
    out = None
    for lo, hi in ((0, FF_SPLIT), (FF_SPLIT, D_FF)):
        a = _silu(_dot(h, wg_ref[:, lo:hi])) * _dot(h, wu_ref[:, lo:hi])
        part = _dot(a.astype(BF16), wd_ref[lo:hi, :])
        out = part if out is None else out + part
    _epilogue(out, refs[0], refs[1], refs[2], refs[3:], st, False)


def _ffn(h, wg, wu, wd, x, gv, mv, st):
    f = wg.shape[1]
    once = pl.Buffered(1)
    in_specs = [pl.BlockSpec((TM, D), lambda i: (i, 0)),
                pl.BlockSpec((D, f), lambda i: (0, 0), pipeline_mode=once),
                pl.BlockSpec((D, f), lambda i: (0, 0), pipeline_mode=once),
                pl.BlockSpec((f, D), lambda i: (0, 0), pipeline_mode=once)]
    e_in, out_specs, out_shape = _epilogue_specs(st.has_next, False)
    return pl.pallas_call(
        functools.partial(_ffn_kernel, st=st),
        grid=(N_T,),
        in_specs=in_specs + e_in,
        out_specs=out_specs,
        out_shape=out_shape,
        compiler_params=_cparams(("parallel",)),
        name="ffn_dense",
    )(h, wg, wu, wd, x, gv, mv)


TE = 256
R_ROWS = 2 * T_ROWS + N_EXPERTS * TE
N_TE = R_ROWS // TE


def _route_meta(route):
    blk = 512
    nb = T_ROWS // blk
    e1 = route[:, 0].astype(jnp.int32)[None, :]
    e2 = route[:, 1].astype(jnp.int32)[None, :]
    ids = jnp.arange(N_EXPERTS, dtype=jnp.int32)[:, None]
    is1, is2 = e1 == ids, e2 == ids
    hit = (is1 | is2).astype(F32).reshape(N_EXPERTS, nb, blk)
    tri = jnp.triu(jnp.ones((blk, blk), F32))
    within = jnp.einsum('ebt,ts->ebs', hit, tri)
    totals = within[:, :, -1]
    before = jnp.cumsum(totals, axis=1) - totals
    csum = (within + before[:, :, None]).reshape(N_EXPERTS, T_ROWS).astype(jnp.int32)
    counts = csum[:, -1]
    padded = (counts + TE - 1) // TE * TE
    ends = jnp.cumsum(padded)
    offs = ends - padded
    dest = offs[:, None] + csum - 1
    pos1 = jnp.sum(jnp.where(is1, dest, 0), axis=0)
    pos2 = jnp.sum(jnp.where(is2, dest, 0), axis=0)
    n_used = ends[-1] // TE
    tile_row = jnp.minimum(jnp.arange(N_TE, dtype=jnp.int32), n_used - 1) * TE
    tile_e = jnp.minimum(jnp.sum(tile_row[:, None] >= ends[None, :], axis=1), N_EXPERTS - 1).astype(jnp.int32)
    pos8 = (jnp.concatenate([pos1, pos2]) * SUBLANES).astype(jnp.int32)
    pad = jnp.concatenate([(offs + counts) * SUBLANES, padded - counts]).astype(jnp.int32)
    return pos8, pad, tile_e, n_used.astype(jnp.int32).reshape(1)


NCH = D // LANES
assert NCH == SUBLANES


def _to_tiles(ref, x):
    n = x.shape[0]
    for c in range(NCH):
        ref[pl.ds(c, n, stride=NCH), :] = x[:, c * LANES:(c + 1) * LANES]


def _from_tiles(ref, n):
    return jnp.concatenate([ref[pl.ds(c, n, stride=NCH), :] for c in range(NCH)], axis=1)


def _tile_copy(src, s8, dst, d8, sem):
    if not isinstance(s8, int):
        s8 = pl.multiple_of(s8, SUBLANES)
    if not isinstance(d8, int):
        d8 = pl.multiple_of(d8, SUBLANES)
    return pltpu.make_async_copy(src.at[pl.ds(s8, SUBLANES), :], dst.at[pl.ds(d8, SUBLANES), :], sem)


def _dispatch_kernel(pos_ref, pad_ref, h_ref, xs_ref, zero_scr, sem):
    i = pl.program_id(0)
    base = i * TM

    for r in range(TM):
        _tile_copy(h_ref, r * NCH, xs_ref, pos_ref[base + r], sem).start(priority=0)
        _tile_copy(h_ref, r * NCH, xs_ref, pos_ref[T_ROWS + base + r], sem).start(priority=1)
    for _ in range(2):
        pltpu.make_async_copy(h_ref, xs_ref.at[pl.ds(0, TM * NCH), :], sem).wait()

    @pl.when(i == N_T - 1)
    def _():
        zero_scr[...] = jnp.zeros(zero_scr.shape, F32)
        for e in range(N_EXPERTS):
            start, n = pad_ref[e], pad_ref[N_EXPERTS + e]

            def fill(r, c):
                _tile_copy(zero_scr, 0, xs_ref, start + r * NCH, sem).start()
                return c

            def drain(r, c):
                _tile_copy(zero_scr, 0, xs_ref, start, sem).wait()
                return c

            lax.fori_loop(0, n, fill, 0)
            lax.fori_loop(0, n, drain, 0)


def _dispatch(pos8, pad, h_tiles):
    return pl.pallas_call(
        _dispatch_kernel,
        grid_spec=pltpu.PrefetchScalarGridSpec(
            num_scalar_prefetch=2,
            grid=(N_T,),
            in_specs=[pl.BlockSpec((TM * NCH, LANES), lambda i, *_: (i, 0))],
            out_specs=pl.BlockSpec(memory_space=pl.ANY),
            scratch_shapes=[pltpu.VMEM((SUBLANES, LANES), F32), pltpu.SemaphoreType.DMA]),
        out_shape=jax.ShapeDtypeStruct((R_ROWS * NCH, LANES), F32),
        compiler_params=_cparams(("arbitrary",)),
        name="moe_dispatch",
    )(pos8, pad, h_tiles)


def _moe_ffn_kernel(te_ref, nu_ref, xs_ref, wg_ref, wu_ref, wd_ref, ys_ref, wg_b, wu_b, wd_b):
    i = pl.program_id(0)

    @pl.when((i == 0) | (te_ref[i] != te_ref[jnp.maximum(i - 1, 0)]))
    def _():
        wg_b[...] = wg_ref[...].astype(BF16)
        wu_b[...] = wu_ref[...].astype(BF16)
        wd_b[...] = wd_ref[...].astype(BF16)

    @pl.when(i < nu_ref[0])
    def _():
        x = _from_tiles(xs_ref, TE).astype(BF16)
        a = _silu(_dot(x, wg_b[...])) * _dot(x, wu_b[...])
        _to_tiles(ys_ref, _dot(a.astype(BF16), wd_b[...]))


def _moe_ffn(tile_e, n_used, xs, wg, wu, wd, m):
    f = wg.shape[3]
    row = lambda i, te, nu: (jnp.minimum(i, nu[0] - 1), 0)
    expert = lambda i, te, nu: (m, te[i], 0, 0)
    return pl.pallas_call(
        _moe_ffn_kernel,
        grid_spec=pltpu.PrefetchScalarGridSpec(
            num_scalar_prefetch=2,
            grid=(N_TE,),
            in_specs=[pl.BlockSpec((TE * NCH, LANES), row),
                      pl.BlockSpec((None, None, D, f), expert),
                      pl.BlockSpec((None, None, D, f), expert),
                      pl.BlockSpec((None, None, f, D), expert)],
            out_specs=pl.BlockSpec((TE * NCH, LANES), row),
            scratch_shapes=[pltpu.VMEM((D, f), BF16), pltpu.VMEM((D, f), BF16), pltpu.VMEM((f, D), BF16)]),
        out_shape=jax.ShapeDtypeStruct((R_ROWS * NCH, LANES), F32),
        compiler_params=_cparams(("arbitrary",)),
        name="moe_ffn",
    )(tile_e, n_used, xs, wg, wu, wd)


def _combine_kernel(*refs, st):
    pos_ref, ys_ref, route_ref = refs[:3]
    ybuf, sem = refs[-2:]
    refs = refs[3:-2]
    i = pl.program_id(0)
    slot = i % 2

    def issue(base, s, r, r8):
        _tile_copy(ys_ref, pos_ref[base + r], ybuf.at[s, 0], r8, sem.at[s]).start(priority=0)
        _tile_copy(ys_ref, pos_ref[T_ROWS + base + r], ybuf.at[s, 1], r8, sem.at[s]).start(priority=1)

    @pl.when(i == 0)
    def _():
        lax.fori_loop(0, TM, lambda r, c: issue(0, 0, r, r * NCH), None)

    def wait(s):
        for c in range(2):
            pltpu.make_async_copy(ys_ref.at[pl.ds(0, TM * NCH), :], ybuf.at[s, c], sem.at[s]).wait()

    wait(slot)
    route = route_ref[...]
    y = (route[:, 2:3] * _from_tiles(ybuf.at[slot, 0], TM)
         + route[:, 3:4] * _from_tiles(ybuf.at[slot, 1], TM))

    nxt = jnp.minimum(i + 1, N_T - 1)
    for r in range(TM):
        issue(nxt * TM, 1 - slot, r, r * NCH)

    _epilogue(y, refs[0], refs[1], refs[2], refs[3:], st, False)

    @pl.when(i == N_T - 1)
    def _():
        wait(1 - slot)


def _combine(pos, ys, route, x, gv, mv, st):
    e_in, out_specs, out_shape = _epilogue_specs(st.has_next, False)
    return pl.pallas_call(
        functools.partial(_combine_kernel, st=st),
        grid_spec=pltpu.PrefetchScalarGridSpec(
            num_scalar_prefetch=1,
            grid=(N_T,),
            in_specs=[pl.BlockSpec(memory_space=pl.ANY),
                      pl.BlockSpec((TM, LANES), lambda i, *_: (i, 0))] + e_in,
            out_specs=out_specs,
            scratch_shapes=[pltpu.VMEM((2, 2, TM * NCH, LANES), F32), pltpu.SemaphoreType.DMA((2,))]),
        out_shape=out_shape,
        compiler_params=_cparams(("arbitrary",)),
        name="moe_combine",
    )(pos, ys, route, x, gv, mv)


def _mla_down_kernel(h_ref, w_ref, gq_ref, gkv_ref, rc_ref, rlo_ref, rhi_ref,
                     cq_ref, ckv_ref, kr_ref, kvb_ref):
    i = pl.program_id(0)
    y = _dot(h_ref[...], w_ref[...])
    cq_ref[...] = _rms(y[:, :Q_RANK], gq_ref[...]).astype(BF16)
    ckv = _rms(y[:, Q_RANK:Q_RANK + KV_RANK], gkv_ref[...])
    kr = y[:, Q_RANK + KV_RANK:]
    kr = jnp.where(i >= NP_T, _rope(kr, rc_ref[...], rlo_ref[...], rhi_ref[...]), kr)
    ckv_ref[...] = ckv
    kr_ref[...] = kr
    kvb_ref[:, :KV_RANK] = ckv.astype(BF16)
    lane = lax.broadcasted_iota(jnp.int32, kr.shape, 1)
    kvb_ref[:, KV_RANK:] = jnp.where(lane == ROPE_DIM, 1.0, kr).astype(BF16)


def _mla_down(h, w, gq, gkv, tabs):
    n = Q_RANK + KVW
    row = lambda i: (i, 0)
    const = lambda i: (0, 0)
    tab = pl.BlockSpec((TM, LANES), lambda i: (_pos_tile(i), 0))
    return pl.pallas_call(
        _mla_down_kernel,
        grid=(N_T,),
        in_specs=[pl.BlockSpec((TM, D), row), pl.BlockSpec((D, n), const),
                  pl.BlockSpec((1, Q_RANK), const), pl.BlockSpec((1, KV_RANK), const),
                  tab, tab, tab],
        out_specs=[pl.BlockSpec((TM, Q_RANK), row), pl.BlockSpec((TM, KV_RANK), row),
                   pl.BlockSpec((TM, LANES), row), pl.BlockSpec((TM, KVW), row)],
        out_shape=[jax.ShapeDtypeStruct((T_ROWS, Q_RANK), BF16),
                   jax.ShapeDtypeStruct((T_ROWS, KV_RANK), F32),
                   jax.ShapeDtypeStruct((T_ROWS, LANES), F32),
                   jax.ShapeDtypeStruct((T_ROWS, KVW), BF16)],
        compiler_params=_cparams(("parallel",)),
        name="mla_down",
    )(h, w, gq, gkv, *tabs)


def _dot_nt(a, b):
    return lax.dot_general(a, b, (((1,), (1,)), ((), ())), preferred_element_type=F32)


def _mla_q_kernel(cq_ref, w_ref, cr_ref, sr_ref, cc_ref, sc_ref, q_ref):
    i = pl.program_id(0)
    y = _dot_nt(w_ref[...], cq_ref[...]) * ATTN_C2
    is_latent = i >= NP_T
    cr = jnp.where(is_latent, cr_ref[...], 1.0)
    sr = jnp.where(is_latent, sr_ref[...], 0.0)
    cc = jnp.where(is_latent, cc_ref[...], 1.0)
    sc = jnp.where(is_latent, sc_ref[...], 0.0)
    for hd in range(N_HEADS):
        r0 = hd * HEAD_PAD
        x0, x1, x2, x3 = (y[r0 + 8 * a:r0 + 8 * (a + 1), :] for a in range(4))
        rot = jnp.concatenate([x0 * cr - x1 * sr, x0 * sr + x1 * cr,
                               x2 * cc - x3 * sc, x2 * sc + x3 * cc], axis=0)
        q_ref[r0:r0 + ROPE_DIM, :] = rot.astype(BF16)
        q_ref[r0 + ROPE_DIM:r0 + HEAD_PAD, :] = y[r0 + ROPE_DIM:r0 + HEAD_PAD, :].astype(BF16)


def _mla_q(cq, w_t, tabs_t):
    tab = pl.BlockSpec((8, TM), lambda i: (0, _pos_tile(i)))
    return pl.pallas_call(
        _mla_q_kernel,
        grid=(N_T,),
        in_specs=[pl.BlockSpec((TM, Q_RANK), lambda i: (i, 0)),
                  pl.BlockSpec((HP, Q_RANK), lambda i: (0, 0)), tab, tab, tab, tab],
        out_specs=pl.BlockSpec((HP, TM), lambda i: (0, i)),
        out_shape=jax.ShapeDtypeStruct((HP, T_ROWS), BF16),
        compiler_params=_cparams(("parallel",)),
        name="mla_q",
    )(cq, w_t, *tabs_t)


def _mla_down_q_kernel(h_ref, w_ref, gq_ref, gkv_ref, rc_ref, rlo_ref, rhi_ref, wq_ref, cr_ref, sr_ref,
                       cc_ref, sc_ref, ckv_ref, kr_ref, kvb_ref, q_ref, cq_scr):
    _mla_down_kernel(h_ref, w_ref, gq_ref, gkv_ref, rc_ref, rlo_ref, rhi_ref, cq_scr, ckv_ref, kr_ref, kvb_ref)
    _mla_q_kernel(cq_scr, wq_ref, cr_ref, sr_ref, cc_ref, sc_ref, q_ref)


def _mla_down_q(h, w, gq, gkv, tabs, w_qt, tabs_t):
    n = Q_RANK + KVW
    row = lambda i: (i, 0)
    const = lambda i: (0, 0)
    tab = pl.BlockSpec((TM, LANES), lambda i: (_pos_tile(i), 0))
    tab_t = pl.BlockSpec((8, TM), lambda i: (0, _pos_tile(i)))
    return pl.pallas_call(
        _mla_down_q_kernel,
        grid=(N_T,),
        in_specs=[pl.BlockSpec((TM, D), row), pl.BlockSpec((D, n), const),
                  pl.BlockSpec((1, Q_RANK), const), pl.BlockSpec((1, KV_RANK), const),
                  tab, tab, tab, pl.BlockSpec((HP, Q_RANK), const), tab_t, tab_t, tab_t, tab_t],
        out_specs=[pl.BlockSpec((TM, KV_RANK), row), pl.BlockSpec((TM, LANES), row),
                   pl.BlockSpec((TM, KVW), row), pl.BlockSpec((HP, TM), lambda i: (0, i))],
        out_shape=[jax.ShapeDtypeStruct((T_ROWS, KV_RANK), F32),
                   jax.ShapeDtypeStruct((T_ROWS, LANES), F32),
                   jax.ShapeDtypeStruct((T_ROWS, KVW), BF16),
                   jax.ShapeDtypeStruct((HP, T_ROWS), BF16)],
        scratch_shapes=[pltpu.VMEM((TM, Q_RANK), BF16)],
        compiler_params=_cparams(("parallel",)),
        name="mla_down_q",
    )(h, w, gq, gkv, *tabs, w_qt, *tabs_t)


def _mla_kv_kernel(c_ref, wk_ref, wvt_ref, k_ref, vt_ref):
    c = c_ref[...]
    k_ref[...] = _dot(c, wk_ref[...]).astype(BF16)
    vt_ref[...] = _dot_nt(wvt_ref[...], c).astype(BF16)


def _mla_kv(ckvkr, w_k, w_vt):
    rows = ckvkr.shape[0]
    hv = HVX
    return pl.pallas_call(
        _mla_kv_kernel,
        grid=(rows // TM,),
        in_specs=[pl.BlockSpec((TM, KVW), lambda i: (i, 0)),
                  pl.BlockSpec((KVW, HP), lambda i: (0, 0)),
                  pl.BlockSpec((hv, KVW), lambda i: (0, 0))],
        out_specs=[pl.BlockSpec((TM, HP), lambda i: (i, 0)), pl.BlockSpec((hv, TM), lambda i: (0, i))],
        out_shape=[jax.ShapeDtypeStruct((rows, HP), BF16), jax.ShapeDtypeStruct((hv, rows), BF16)],
        compiler_params=_cparams(("parallel",)),
        name="mla_kv",
    )(ckvkr, w_k, w_vt)


def _attn_kernel(*refs, nk, aliased):
    if aliased:
        refs = refs[1:]
    qt_ref, k_ref, vt_ref, o_ref, m_scr, acc_scr, ot_scr = refs
    ki = pl.program_id(2)

    @pl.when(ki == 0)
    def _():
        m_scr[...] = jnp.full(m_scr.shape, -jnp.inf, F32)
        acc_scr[...] = jnp.zeros(acc_scr.shape, F32)

    tk, tq = k_ref.shape[0], qt_ref.shape[1]
    vg = V_EXT // SUBLANES

    def across_sublanes(x, op):
        for k in (4, 2, 1):
            x = op(x, pltpu.roll(x, k, 0))
        return x

    def scores(hd):
        qs = slice(hd * HEAD_PAD, (hd + 1) * HEAD_PAD)
        return _dot(k_ref[:, qs], qt_ref[qs, :]).reshape(tk // SUBLANES, SUBLANES, tq)

    ahead = 2
    pending = [scores(hd) for hd in range(ahead)]
    for hd in range(N_HEADS):
        vs = slice(hd * V_EXT, (hd + 1) * V_EXT)
        s = pending.pop(0)
        if hd + ahead < N_HEADS:
            pending.append(scores(hd + ahead))
        m_prev = m_scr[hd]
        m_new = jnp.maximum(m_prev, across_sublanes(jnp.max(s, axis=0), jnp.maximum))
        alpha = jnp.exp2(m_prev - m_new)
        p = jnp.exp2(s - m_new[None])
        pv = _dot(vt_ref[vs, :], p.reshape(tk, tq).astype(BF16))
        acc = acc_scr[vs, :].reshape(vg, SUBLANES, tq)
        acc_scr[vs, :] = (alpha[None] * acc).reshape(V_EXT, tq) + pv
        m_scr[hd] = m_new

    @pl.when(ki == nk - 1)
    def _():
        for hd in range(N_HEADS):
            r0 = hd * V_EXT
            den = across_sublanes(acc_scr[r0 + V_DIM:r0 + V_DIM + SUBLANES, :], jnp.add)
            acc = acc_scr[r0:r0 + V_DIM, :].reshape(V_DIM // SUBLANES, SUBLANES, tq)
            ot_scr[hd * V_DIM:(hd + 1) * V_DIM, :] = (acc / den[None]).reshape(V_DIM, tq)
        o_ref[...] = ot_scr[...].T.astype(BF16)


def _attention(qt, k, vt, n_b, tq, tk, lq, lk, q_row0, prev_out=None):
    nq, nk = lq // tq, lk // tk
    qb0 = q_row0 // tq
    hv = N_HEADS * V_DIM
    aliased = prev_out is not None
    in_specs = [pl.BlockSpec((HP, tq), lambda b, qi, ki: (0, qb0 + b * nq + qi)),
                pl.BlockSpec((tk, HP), lambda b, qi, ki: (b * nk + ki, 0)),
                pl.BlockSpec((HVX, tk), lambda b, qi, ki: (0, b * nk + ki))]
    args = [qt, k, vt]
    if aliased:
        in_specs = [pl.BlockSpec(memory_space=pl.ANY)] + in_specs
        args = [prev_out] + args
    return pl.pallas_call(
        functools.partial(_attn_kernel, nk=nk, aliased=aliased),
        grid=(n_b, nq, nk),
        in_specs=in_specs,
        out_specs=pl.BlockSpec((tq, hv), lambda b, qi, ki: (qb0 + b * nq + qi, 0)),
        out_shape=jax.ShapeDtypeStruct((T_ROWS, hv), BF16),
        scratch_shapes=[pltpu.VMEM((N_HEADS, SUBLANES, tq), F32), pltpu.VMEM((HVX, tq), F32),
                        pltpu.VMEM((hv, tq), F32)],
        input_output_aliases={0: 0} if aliased else {},
        compiler_params=_cparams(("parallel", "parallel", "arbitrary")),
        name="attn_latent" if aliased else "attn_context",
    )(*args)


def _rg_in_kernel(h_ref, w_ref, u_ref, g_ref):
    y = _dot(h_ref[...], w_ref[...])
    u_ref[...] = y[:, :D]
    g_ref[...] = _gelu_tanh(y[:, D:]).astype(BF16)


def _rg_in(h, w):
    return pl.pallas_call(
        _rg_in_kernel,
        grid=(N_T,),
        in_specs=[pl.BlockSpec((TM, D), lambda i: (i, 0)), pl.BlockSpec((D, 2 * D), lambda i: (0, 0))],
        out_specs=[pl.BlockSpec((TM, D), lambda i: (i, 0))] * 2,
        out_shape=[jax.ShapeDtypeStruct((T_ROWS, D), F32), jax.ShapeDtypeStruct((T_ROWS, D), BF16)],
        compiler_params=_cparams(("parallel",)),
        name="rg_in",
    )(h, w)


RG_TC = 256
RG_GROUP = 256


def _rg_scan_kernel(*refs, nj, seq_len, aliased):
    if aliased:
        refs = refs[2:]
    (uf_ref, ufp_ref, ufn_ref, ub_ref, ubp_ref, ubn_ref, cw_ref, cb_ref, wai_ref, bai_ref,
     lam_ref, h0_ref, yf_ref, yb_ref, fin_ref, carry_ref) = refs
    j = pl.program_id(1)
    tc = RG_TC
    ng = tc // SUBLANES
    sub = lax.broadcasted_iota(jnp.int32, (1, SUBLANES, 1), 1)

    @pl.when(j == 0)
    def _():
        carry_ref[0:2, :] = h0_ref[...]

    def gates(m_ref, p_ref, n_ref, d, chunk):
        m = m_ref[...]
        ext = jnp.concatenate([jnp.where(chunk == 0, 0.0, p_ref[...]), m,
                               jnp.where(chunk == nj - 1, 0.0, n_ref[...])], axis=0)

        def tap(k):
            return pltpu.roll(ext, (-k) % (tc + 2 * SUBLANES), 0)[SUBLANES:SUBLANES + tc]

        u = (cw_ref[0:1, :] * tap(-1) + cw_ref[1:2, :] * m + cw_ref[2:3, :] * tap(1)
             + cw_ref[3:4, :] * tap(2) + cb_ref[...])
        ub = u.astype(BF16)
        ra, ri = [], []
        for q in range(D // RG_GROUP):
            y = _dot(ub[:, q * RG_GROUP:(q + 1) * RG_GROUP], wai_ref[d, q])
            ra.append(y[:, :RG_GROUP])
            ri.append(y[:, RG_GROUP:])
        rr = jax.nn.sigmoid(jnp.concatenate(ra, axis=1) + bai_ref[d, 0:1, :])
        ii = jax.nn.sigmoid(jnp.concatenate(ri, axis=1) + bai_ref[d, 1:2, :])
        nl = -lam_ref[d:d + 1, :]
        softplus = jnp.maximum(nl, 0.0) + jnp.log1p(jnp.exp(-jnp.abs(nl)))
        log_a = (-RG_C * softplus) * rr
        a = jnp.exp(log_a)
        y = jnp.maximum(-jnp.tanh(log_a) * (a * a + 1.0), 0.0)
        bx = jnp.where(y > 0.0, y * lax.rsqrt(y), 0.0) * (ii * u)
        return a.reshape(ng, SUBLANES, D), bx.reshape(ng, SUBLANES, D)

    a, b = gates(uf_ref, ufp_ref, ufn_ref, 0, j)
    for k in (1, 2, 4):
        ok = sub >= k
        b = jnp.where(ok, a * pltpu.roll(b, k, 1) + b, b)
        a = jnp.where(ok, a * pltpu.roll(a, k, 1), a)
    h = carry_ref[0:1, :]
    for g in range(ng):
        hg = a[g] * h + b[g]
        yf_ref[g * SUBLANES:(g + 1) * SUBLANES, :] = hg
        h = hg[SUBLANES - 1:SUBLANES, :]
    carry_ref[0:1, :] = h

    a, b = gates(ub_ref, ubp_ref, ubn_ref, 1, nj - 1 - j)
    for k in (1, 2, 4):
        ok = sub < SUBLANES - k
        b = jnp.where(ok, a * pltpu.roll(b, SUBLANES - k, 1) + b, b)
        a = jnp.where(ok, a * pltpu.roll(a, SUBLANES - k, 1), a)
    h = carry_ref[1:2, :]
    for g in reversed(range(ng)):
        hg = a[g] * h + b[g]
        yb_ref[g * SUBLANES:(g + 1) * SUBLANES, :] = hg
        h = hg[0:1, :]
    carry_ref[1:2, :] = h

    @pl.when(j == nj - 1)
    def _():
        fin_ref[...] = carry_ref[0:2, :]


def _rg_scan(u, cw, cb, wai, bai, lam, h0, n_seq, seq_len, row0, prev=None):
    nj = seq_len // RG_TC
    b0 = row0 // RG_TC
    aliased = prev is not None
    fwd = lambda s, j: b0 + s * nj + j
    bwd = lambda s, j: b0 + s * nj + (nj - 1 - j)
    const2 = lambda s, j: (0, 0)
    in_specs = (_halo_specs(RG_TC, fwd) + _halo_specs(RG_TC, bwd)
                + [pl.BlockSpec((8, D), const2), pl.BlockSpec((1, D), const2),
                   pl.BlockSpec((2, D // RG_GROUP, RG_GROUP, 2 * RG_GROUP), lambda s, j: (0, 0, 0, 0)),
                   pl.BlockSpec((2, 2, D), lambda s, j: (0, 0, 0)),
                   pl.BlockSpec((2, D), const2),
                   pl.BlockSpec((None, 2, D), lambda s, j: (s, 0, 0))])
    args = [u, u, u, u, u, u, cw, cb, wai, bai, lam, h0]
    if aliased:
        in_specs = [pl.BlockSpec(memory_space=pl.ANY)] * 2 + in_specs
        args = list(prev) + args
    return pl.pallas_call(
        functools.partial(_rg_scan_kernel, nj=nj, seq_len=seq_len, aliased=aliased),
        grid=(n_seq, nj),
        in_specs=in_specs,
        out_specs=[pl.BlockSpec((RG_TC, D), lambda s, j: (fwd(s, j), 0)),
                   pl.BlockSpec((RG_TC, D), lambda s, j: (bwd(s, j), 0)),
                   pl.BlockSpec((None, 2, D), lambda s, j: (s, 0, 0))],
        out_shape=[jax.ShapeDtypeStruct((T_ROWS, D), F32), jax.ShapeDtypeStruct((T_ROWS, D), F32),
                   jax.ShapeDtypeStruct((n_seq, 2, D), F32)],
        scratch_shapes=[pltpu.VMEM((8, D), F32)],
        input_output_aliases={0: 0, 1: 1} if aliased else {},
        compiler_params=_cparams(("parallel", "arbitrary")),
        name="rg_scan_latent" if aliased else "rg_scan_context",
    )(*args)


def _sc_in_kernel(h_ref, w_ref, bg_ref, m_ref):
    y = _dot(h_ref[...], w_ref[...])
    bg_ref[...] = y[:, :D].astype(BF16)
    m_ref[...] = y[:, D:2 * D] * y[:, 2 * D:]


def _sc_in(h, w):
    return pl.pallas_call(
        _sc_in_kernel,
        grid=(N_T,),
        in_specs=[pl.BlockSpec((TM, D), lambda i: (i, 0)), pl.BlockSpec((D, 3 * D), lambda i: (0, 0))],
        out_specs=[pl.BlockSpec((TM, D), lambda i: (i, 0))] * 2,
        out_shape=[jax.ShapeDtypeStruct((T_ROWS, D), BF16), jax.ShapeDtypeStruct((T_ROWS, D), F32)],
        compiler_params=_cparams(("parallel",)),
        name="sc_in",
    )(h, w)


def _rope_tables():
    t = np.arange(DEC_SEQ)
    inv = ROPE_THETA ** (-jnp.arange(0, AXIS_DIM, 2, dtype=F32) / AXIS_DIM)
    ang_r = jnp.asarray((t // GRID_W).astype(np.float32))[:, None] * inv
    ang_c = jnp.asarray((t % GRID_W).astype(np.float32))[:, None] * inv
    cr, sr, cc, sc = jnp.cos(ang_r), jnp.sin(ang_r), jnp.cos(ang_c), jnp.sin(ang_c)
    z8 = jnp.zeros((DEC_SEQ, 8), F32)
    pad1 = jnp.ones((DEC_SEQ, LANES - ROPE_DIM), F32)
    pad0 = jnp.zeros((DEC_SEQ, LANES - ROPE_DIM), F32)
    c = jnp.concatenate([cr, cr, cc, cc, pad1], axis=1)
    lo = jnp.concatenate([-sr, z8, -sc, z8, pad0], axis=1)
    hi = jnp.concatenate([z8, sr, z8, sc, pad0], axis=1)
    return (c, lo, hi), (cr.T, sr.T, cc.T, sc.T)


def _pad_heads(w, lo):
    r, h, d = w.shape
    return jnp.pad(w, ((0, 0), (0, 0), (lo, HEAD_PAD - lo - d))).reshape(r, h * HEAD_PAD)


def _mla_weights(w_dq, w_uq, w_dkv, w_uk, w_uv, w_o):
    w_down = jnp.concatenate([w_dq, w_dkv, jnp.zeros((D, KVW - KV_RANK - ROPE_DIM), F32)], axis=1)
    w_q = _pad_heads(w_uq[:, :, QK_NOPE:], 0) + _pad_heads(w_uq[:, :, :QK_NOPE], HEAD_PAD - QK_NOPE)
    place = jnp.broadcast_to(jnp.eye(ROPE_DIM, dtype=F32)[:, None, :], (ROPE_DIM, N_HEADS, ROPE_DIM))
    w_k = jnp.concatenate([_pad_heads(w_uk, HEAD_PAD - QK_NOPE), _pad_heads(place, 0),
                           jnp.zeros((KVW - KV_RANK - ROPE_DIM, HP), F32)], axis=0)
    w_vt = jnp.pad(jnp.transpose(w_uv, (1, 2, 0)), ((0, 0), (0, V_EXT - V_DIM), (0, KVW - KV_RANK)))
    w_vt = w_vt.at[:, V_DIM, ONE_COL].set(1.0).reshape(HVX, KVW)
    return (w_down.astype(BF16), w_q.T.astype(BF16), w_k.astype(BF16), w_vt.astype(BF16),
            w_o.astype(BF16))


def _block_diag_groups(w_a, w_i):
    per = RG_GROUP // RG_BLOCK

    def bd(w):
        w = w.reshape(2, D // RG_GROUP, per, RG_BLOCK, RG_BLOCK)
        eye = jnp.eye(per, dtype=F32)
        return jnp.einsum('dgpkj,pq->dgpkqj', w, eye).reshape(2, D // RG_GROUP, RG_GROUP, RG_GROUP)

    return jnp.concatenate([bd(w_a), bd(w_i)], axis=-1).astype(BF16)


def _pad_rows(w, rows=8):
    return jnp.pad(w, ((0, rows - w.shape[0]), (0, 0)))


def kernel(x_prompt, x_sample, cache_mla_ckv, cache_mla_krope, state_rglru, c, c_ctx, mod_w, mod_b, norm_g, mla_w_dq, mla_g_q, mla_w_uq, mla_w_dkv, mla_g_kv, mla_w_uk, mla_w_uv, mla_w_o, rg_w_x, rg_w_y, rg_conv_w, rg_conv_b, rg_w_a, rg_b_a, rg_w_i, rg_b_i, rg_lambda, rg_w_out, sc_w_in, sc_conv_w, sc_w_out, ffn_w_gate, ffn_w_up, ffn_w_down, moe_w_router, moe_b_router, moe_w_gate, moe_w_up, moe_w_down):
    cond8 = jnp.concatenate([c_ctx[None], c, jnp.zeros((N_GROUPS - 1 - DEC_BATCH, D), F32)], axis=0)
    mod = _modulation(cond8, mod_w, mod_b).reshape(DEPTH, N_GROUPS, 6, D)
    mv = jnp.pad(jnp.transpose(mod, (1, 0, 2, 3)).reshape(N_GROUPS, DEPTH * 6, D),
                 ((0, 0), (0, M_ROWS - DEPTH * 6), (0, 0)))
    gv = jnp.pad(norm_g.reshape(DEPTH * 4, D), ((0, G_ROWS - DEPTH * 4), (0, 0)))
    tabs, tabs_t = _rope_tables()

    def stage(l, sub):
        has_next = sub == 0 or l + 1 < DEPTH
        return Stage(has_next, 4 * l + 1 + 2 * sub, 6 * l + 2 + 3 * sub)

    x, h = _pre0(x_prompt.reshape(P_ROWS, D), x_sample.reshape(S_ROWS, D), gv, mv)

    new_ckv, new_krope, new_rg = [], [], []
    for l in range(DEPTH):
        kind, j = l % 3, l // 3
        st = stage(l, 0)
        router = None
        if l % 2 == 1:
            m = l // 2
            w_r = jnp.pad(moe_w_router[m], ((0, 0), (0, LANES - N_EXPERTS)))
            w_r_hi = w_r.astype(BF16)
            router = (w_r_hi, (w_r - w_r_hi.astype(F32)).astype(BF16),
                      jnp.pad(moe_b_router[m], (0, LANES - N_EXPERTS)).reshape(1, LANES))
        if kind == 0:
            w_down, w_qt, w_k, w_vt, w_op = _mla_weights(mla_w_dq[j], mla_w_uq[j], mla_w_dkv[j],
                                                         mla_w_uk[j], mla_w_uv[j], mla_w_o[j])
            ckv, kr, kvb, q = _mla_down_q(h, w_down, mla_g_q[j].reshape(1, Q_RANK),
                                          mla_g_kv[j].reshape(1, KV_RANK), tabs, w_qt, tabs_t)
            new_ckv.append(ckv[:P_ROWS].reshape(BATCH, SEQ, KV_RANK))
            new_krope.append(kr[:P_ROWS, :ROPE_DIM].reshape(BATCH, SEQ, ROPE_DIM))
            cache = jnp.concatenate(
                [cache_mla_ckv[:, j], cache_mla_krope[:, j], jnp.ones((DEC_BATCH, PAST, 1), F32),
                 jnp.zeros((DEC_BATCH, PAST, KVW - ONE_COL - 1), F32)], axis=-1).astype(BF16)
            kv_lat = jnp.concatenate([cache, kvb[P_ROWS:].reshape(DEC_BATCH, DEC_SEQ, KVW)], axis=1)
            k_c, v_c = _mla_kv(kvb[:P_ROWS], w_k, w_vt)
            k_l, v_l = _mla_kv(kv_lat.reshape(DEC_BATCH * (PAST + DEC_SEQ), KVW), w_k, w_vt)
            o = _attention(q, k_c, v_c, BATCH, SEQ, SEQ, SEQ, SEQ, 0)
            o = _attention(q, k_l, v_l, DEC_BATCH, 1024, 768, DEC_SEQ, PAST + DEC_SEQ, P_ROWS, prev_out=o)
            outs = _proj("plain", (o,), w_op, x, gv, mv, st, router=router)
        elif kind == 1:
            w_xy = jnp.concatenate([rg_w_x[j], rg_w_y[j]], axis=1).astype(BF16)
            u, gate = _rg_in(h, w_xy)
            wai = _block_diag_groups(rg_w_a[j], rg_w_i[j])
            bai = jnp.stack([rg_b_a[j], rg_b_i[j]], axis=1)
            scan_args = (_pad_rows(rg_conv_w[j]), rg_conv_b[j].reshape(1, D), wai, bai, rg_lambda[j])
            yf, yb, fin = _rg_scan(u, *scan_args, jnp.zeros((BATCH, 2, D), F32), BATCH, SEQ, 0)
            yf, yb, _ = _rg_scan(u, *scan_args, state_rglru[:, j], DEC_BATCH, DEC_SEQ, P_ROWS,
                                 prev=(yf, yb))
            new_rg.append(fin)
            outs = _proj("rg", (yf, yb, gate), rg_w_out[j].astype(BF16), x, gv, mv, st, router=router)
        else:
            bg, mm = _sc_in(h, sc_w_in[j].astype(BF16))
            outs = _proj("sc", (bg, mm, _pad_rows(sc_conv_w[j])), sc_w_out[j].astype(BF16), x, gv, mv,
                         st, router=router)
        x, h = outs[0], outs[1]

        st = stage(l, 1)
        m = l // 2
        if l % 2 == 0:
            outs = _ffn(h, ffn_w_gate[m].astype(BF16), ffn_w_up[m].astype(BF16),
                        ffn_w_down[m].astype(BF16), x, gv, mv, st)
        else:
            route = outs[2]
            pos, pad, tile_e, n_used = _route_meta(route)
            xs = _dispatch(pos, pad, h)
            ys = _moe_ffn(tile_e, n_used, xs, moe_w_gate, moe_w_up, moe_w_down, m)
            outs = _combine(pos, ys, route, x, gv, mv, st)
        x, h = outs[0], outs[1]

    y_prompt = x.reshape(BATCH, SEQ, D)
    y_sample = h.reshape(DEC_BATCH, DEC_SEQ, D)
    return (y_prompt, y_sample, jnp.stack(new_ckv, axis=1), jnp.stack(new_krope, axis=1),
            jnp.stack(new_rg, axis=1))
```
